```python
import math, functools
import jax, jax.numpy as jnp
from jax import lax
import numpy as np

D_MODEL = 1024
BATCH = 16
SEQ = 256
DEPTH = 2
DEC_BATCH = 2
DEC_SEQ = 2048
PAST_LEN = 256

GRID_W = 64
HEAD_DIM = 64
HY_CH = 256
NA_HEADS = 6
GQA_Q_HEADS = 6
GQA_KV_HEADS = 2
MIX_WIDTH = HY_CH + NA_HEADS * HEAD_DIM + GQA_Q_HEADS * HEAD_DIM
IN_WIDTH = 3 * HY_CH + 3 * NA_HEADS * HEAD_DIM + (GQA_Q_HEADS + 2 * GQA_KV_HEADS) * HEAD_DIM
D_FF = 2816
N_ADA = 9
SHORT_CONV = 3
HY_BANDS = 16
HY_EMB = 1 + 2 * HY_BANDS
HY_FILT_W = 64
HY_DECAY_TARGET = 1e-2
HY_FAST_PCT = 0.3
HY_SLOW_PCT = 1.5
NA_ROWS = 8
NA_COLS = 16
NA_QCOLS = 16
NA_KCOLS = 32
GQA_WINDOW = 128
BLK = 128
ROPE_BASE = 10000.0
EPS = 1e-6
NEG_INF = -1e30

kernel_name = "hymba_style_hyena_natten_swa_diffusion_step"


def _rmsnorm(x, g):
    xf = x.astype(jnp.float32)
    xf = xf * lax.rsqrt(jnp.mean(xf * xf, axis=-1, keepdims=True) + EPS)
    return xf.astype(x.dtype) * g


def _swiglu(h, w1, w3, w2):
    return (jax.nn.silu(h @ w1) * (h @ w3)) @ w2


def _short_conv(u, w, b):
    n = u.shape[1]
    up = jnp.pad(u, ((0, 0), (1, 1), (0, 0)))
    return up[:, 0:n] * w[0] + up[:, 1:n + 1] * w[1] + up[:, 2:n + 2] * w[2] + b


def _hyena_filter_fft(L, lp):
    f32 = jnp.float32
    idx = jnp.arange(L, dtype=f32)
    t = idx / (L - 1)
    bands = jnp.linspace(1e-4, HY_BANDS - 1, HY_BANDS, dtype=f32)
    ang = (2.0 * math.pi / L) * idx[:, None] * bands[None, :]
    feats = jnp.concatenate([t[:, None], jnp.cos(ang), -jnp.sin(ang)], axis=-1)
    freq = lp["hy_freq"].astype(f32)
    hid = jnp.sin(freq[0] * (feats @ lp["hy_filt_w1"].astype(f32) + lp["hy_filt_b1"].astype(f32)))
    hid = jnp.sin(freq[1] * (hid @ lp["hy_filt_w2"].astype(f32) + lp["hy_filt_b2"].astype(f32)))
    taps = (hid @ lp["hy_filt_w3"].astype(f32)).reshape(L, 2, HY_CH)
    max_decay = math.log(HY_DECAY_TARGET) / HY_FAST_PCT
    min_decay = math.log(HY_DECAY_TARGET) / HY_SLOW_PCT
    deltas = jnp.abs(jnp.linspace(min_decay, max_decay, HY_CH, dtype=f32))
    taps = taps * jnp.exp(-t[:, None, None] * deltas)
    two_sided = jnp.concatenate([taps[:, 0], jnp.zeros((1, HY_CH), f32), taps[:0:-1, 1]], axis=0)
    return jnp.fft.rfft(two_sided, axis=0)


def _hyena(u, lp):
    L = u.shape[1]
    uc = _short_conv(u, lp["hy_conv_w"], lp["hy_conv_b"])
    x0, x1, v = jnp.split(uc, 3, axis=-1)
    kf = _hyena_filter_fft(L, lp)
    z = (x1 * v).astype(jnp.float32)
    zf = jnp.fft.rfft(z, n=2 * L, axis=1)
    conv = jnp.fft.irfft(zf * kf[None], n=2 * L, axis=1)[:, :L]
    y = conv + z * lp["hy_skip"].astype(jnp.float32)
    return x0 * y.astype(u.dtype)


def _axial_rope(x):
    N, Dh = x.shape[1], x.shape[-1]
    pos = jnp.arange(N)
    quarter = Dh // 4
    half = Dh // 2
    inv = ROPE_BASE ** (-jnp.arange(quarter, dtype=jnp.float32) / quarter)

    def rot(xh, coord):
        ang = coord.astype(jnp.float32)[:, None] * inv[None, :]
        cos = jnp.cos(ang)[None, :, None, :].astype(x.dtype)
        sin = jnp.sin(ang)[None, :, None, :].astype(x.dtype)
        a, b = xh[..., :quarter], xh[..., quarter:]
        return jnp.concatenate([a * cos - b * sin, a * sin + b * cos], axis=-1)

    return jnp.concatenate([rot(x[..., :half], pos // GRID_W), rot(x[..., half:], pos % GRID_W)], axis=-1)


def _dense_attention(q, k, v, sink):
    B, L, Hq, Dh = q.shape
    Hkv = k.shape[2]
    G = Hq // Hkv
    nb = L // BLK
    scale = Dh ** -0.5
    qb = jnp.moveaxis(q.reshape(B, nb, BLK, Hkv, G, Dh), 1, 0)

    def block(qi):
        s = jnp.einsum("bqhgd,bkhd->bhgqk", qi, k).astype(jnp.float32) * scale
        if sink is not None:
            s_sink = jnp.broadcast_to(sink.astype(jnp.float32).reshape(1, Hkv, G, 1, 1), s.shape[:-1] + (1,))
            p = jax.nn.softmax(jnp.concatenate([s, s_sink], axis=-1), axis=-1)[..., :L]
        else:
            p = jax.nn.softmax(s, axis=-1)
        return jnp.einsum("bhgqk,bkhd->bqhgd", p.astype(v.dtype), v)

    out = lax.map(block, qb)
    return jnp.moveaxis(out, 0, 1).reshape(B, L, Hq, Dh)


def _neighbourhood_attention(q, k, v, kc, vc, rpb):
    B, N, H, Dh = q.shape
    rows = N // GRID_W
    wr = min(NA_ROWS, rows)
    nb = GRID_W // NA_QCOLS
    scale = Dh ** -0.5
    r = jnp.arange(rows)
    row_idx = jnp.clip(r - wr // 2, 0, rows - wr)[:, None] + jnp.arange(wr)[None, :]
    blk = jnp.arange(nb)
    col_idx = jnp.clip(blk * NA_QCOLS - NA_COLS // 2, 0, GRID_W - NA_KCOLS)[:, None] + jnp.arange(NA_KCOLS)[None, :]
    q_col = blk[:, None] * NA_QCOLS + jnp.arange(NA_QCOLS)[None, :]
    win_lo = jnp.clip(q_col - NA_COLS // 2, 0, GRID_W - NA_COLS)[:, :, None]
    col_ok = (col_idx[:, None, :] >= win_lo) & (col_idx[:, None, :] < win_lo + NA_COLS)
    n_loc = wr * NA_KCOLS
    mask = jnp.broadcast_to(col_ok[:, :, None, :], (nb, NA_QCOLS, wr, NA_KCOLS)).reshape(nb, NA_QCOLS, n_loc)
    dr = row_idx - r[:, None]
    dc = jnp.clip(col_idx[:, None, :] - q_col[:, :, None], 1 - NA_COLS, NA_COLS - 1)
    bias = rpb.astype(jnp.float32)[:, dr[:, None, None, :, None] + NA_ROWS - 1, dc[None, :, :, None, :] + NA_COLS - 1]
    bias = bias.reshape(H, rows, nb, NA_QCOLS, n_loc)
    gr = row_idx[:, None, :, None]
    gc = col_idx[None, :, None, :]
    kg = k.reshape(B, rows, GRID_W, H, Dh)[:, gr, gc].reshape(B, rows, nb, n_loc, H, Dh)
    vg = v.reshape(B, rows, GRID_W, H, Dh)[:, gr, gc].reshape(B, rows, nb, n_loc, H, Dh)
    qg = q.reshape(B, rows, nb, NA_QCOLS, H, Dh)
    s_loc = jnp.einsum("brnqhd,brnkhd->bhrnqk", qg, kg).astype(jnp.float32) * scale
    s_loc = jnp.where(mask[None, None, None], s_loc + bias[None], NEG_INF)
    s_ctx = jnp.einsum("brnqhd,bkhd->bhrnqk", qg, kc).astype(jnp.float32) * scale
    p = jax.nn.softmax(jnp.concatenate([s_loc, s_ctx], axis=-1), axis=-1).astype(v.dtype)
    out = (jnp.einsum("bhrnqk,brnkhd->brnqhd", p[..., :n_loc], vg)
           + jnp.einsum("bhrnqk,bkhd->brnqhd", p[..., n_loc:], vc))
    return out.reshape(B, N, H, Dh)


def _window_attention(q, k, v, kc, vc, sink):
    B, N, Hq, Dh = q.shape
    Hkv = k.shape[2]
    G = Hq // Hkv
    nb = N // BLK
    scale = Dh ** -0.5
    qb = q.reshape(B, nb, BLK, Hkv, G, Dh)
    key_idx = jnp.arange(nb)[:, None] * BLK + jnp.arange(3 * BLK)[None, :]
    pad = ((0, 0), (BLK, BLK), (0, 0), (0, 0))
    kb = jnp.pad(k, pad)[:, key_idx]
    vb = jnp.pad(v, pad)[:, key_idx]
    q_pos = jnp.arange(nb)[:, None] * BLK + jnp.arange(BLK)[None, :]
    k_pos = (key_idx - BLK)[:, None, :]
    ok = (jnp.abs(q_pos[:, :, None] - k_pos) <= GQA_WINDOW) & (k_pos >= 0) & (k_pos < N)
    s_loc = jnp.einsum("bnqhgd,bnkhd->bhgnqk", qb, kb).astype(jnp.float32) * scale
    s_loc = jnp.where(ok[None, None, None], s_loc, NEG_INF)
    s_ctx = jnp.einsum("bnqhgd,bkhd->bhgnqk", qb, kc).astype(jnp.float32) * scale
    s_sink = jnp.broadcast_to(sink.astype(jnp.float32).reshape(1, Hkv, G, 1, 1, 1), s_loc.shape[:-1] + (1,))
    p = jax.nn.softmax(jnp.concatenate([s_loc, s_ctx, s_sink], axis=-1), axis=-1).astype(v.dtype)
    n_loc = 3 * BLK
    n_ctx = kc.shape[1]
    out = (jnp.einsum("bhgnqk,bnkhd->bnqhgd", p[..., :n_loc], vb)
           + jnp.einsum("bhgnqk,bkhd->bnqhgd", p[..., n_loc:n_loc + n_ctx], vc))
    return out.reshape(B, N, Hq, Dh)


def _project(h, lp):
    u = h @ lp["w_in"]
    B, L, _ = u.shape
    s0 = 3 * HY_CH
    s1 = s0 + 3 * NA_HEADS * HEAD_DIM
    nq = GQA_Q_HEADS * HEAD_DIM
    nk = GQA_KV_HEADS * HEAD_DIM
    u_hy = u[..., :s0]
    na = u[..., s0:s1].reshape(B, L, 3, NA_HEADS, HEAD_DIM)
    na_q = _rmsnorm(na[:, :, 0], lp["na_q_g"])
    na_k = _rmsnorm(na[:, :, 1], lp["na_k_g"])
    na_v = na[:, :, 2]
    gq = u[..., s1:]
    gq_q = _rmsnorm(gq[..., :nq].reshape(B, L, GQA_Q_HEADS, HEAD_DIM), lp["gqa_q_g"])
    gq_k = _rmsnorm(gq[..., nq:nq + nk].reshape(B, L, GQA_KV_HEADS, HEAD_DIM), lp["gqa_k_g"])
    gq_v = gq[..., nq + nk:].reshape(B, L, GQA_KV_HEADS, HEAD_DIM)
    return u_hy, na_q, na_k, na_v, gq_q, gq_k, gq_v


def _merge(y_hy, y_na, y_gq, lp):
    B, L = y_hy.shape[:2]
    y = jnp.concatenate([y_hy, y_na.reshape(B, L, -1), y_gq.reshape(B, L, -1)], axis=-1)
    return y @ lp["w_out"]


def _mixer_context(h, lp):
    u_hy, na_q, na_k, na_v, gq_q, gq_k, gq_v = _project(h, lp)
    y_hy = _hyena(u_hy, lp)
    y_na = _dense_attention(na_q, na_k, na_v, None)
    y_gq = _dense_attention(gq_q, gq_k, gq_v, lp["gqa_sink"])
    return _merge(y_hy, y_na, y_gq, lp), (na_k, na_v, gq_k, gq_v)


def _mixer_latent(h, lp, na_kc, na_vc, gq_kc, gq_vc):
    u_hy, na_q, na_k, na_v, gq_q, gq_k, gq_v = _project(h, lp)
    y_hy = _hyena(u_hy, lp)
    y_na = _neighbourhood_attention(na_q, na_k, na_v, na_kc, na_vc, lp["na_rpb"])
    y_gq = _window_attention(_axial_rope(gq_q), _axial_rope(gq_k), gq_v, gq_kc, gq_vc, lp["gqa_sink"])
    return _merge(y_hy, y_na, y_gq, lp), ()


def _layer(x, cvec, lp, mixer):
    mod = jax.nn.silu(cvec) @ lp["ada_w"] + lp["ada_b"]
    sh1, sc1, g1, sh2, sc2, g2, sh3, sc3, g3 = jnp.split(mod[:, None, :], N_ADA, axis=-1)
    h = _rmsnorm(x, lp["norm_g"][0]) * (1.0 + sc1) + sh1
    x = x + 0.5 * g1 * _swiglu(h, lp["ffn_w1"][0], lp["ffn_w3"][0], lp["ffn_w2"][0])
    h = _rmsnorm(x, lp["norm_g"][1]) * (1.0 + sc2) + sh2
    y, ctx = mixer(h)
    x = x + g2 * y
    h = _rmsnorm(x, lp["norm_g"][2]) * (1.0 + sc3) + sh3
    x = x + 0.5 * g3 * _swiglu(h, lp["ffn_w1"][1], lp["ffn_w3"][1], lp["ffn_w2"][1])
    return x, ctx


def setup_inputs(seed: int = 0) -> dict:
    key = jax.random.key(seed)
    ks = jax.random.split(key, 32)
    f32 = jnp.float32

    def nrm(k, shape, s):
        return jax.random.normal(k, shape, f32) * s

    D = D_MODEL
    return {
        "x_prompt": nrm(ks[0], (BATCH, SEQ, D), 1.0),
        "x_sample": nrm(ks[1], (DEC_BATCH, DEC_SEQ, D), 1.0),
        "cache_na_k": nrm(ks[2], (DEC_BATCH, DEPTH, PAST_LEN, NA_HEADS, HEAD_DIM), 1.0),
        "cache_na_v": nrm(ks[3], (DEC_BATCH, DEPTH, PAST_LEN, NA_HEADS, HEAD_DIM), 1.0),
        "cache_gqa_k": nrm(ks[4], (DEC_BATCH, DEPTH, PAST_LEN, GQA_KV_HEADS, HEAD_DIM), 1.0),
        "cache_gqa_v": nrm(ks[5], (DEC_BATCH, DEPTH, PAST_LEN, GQA_KV_HEADS, HEAD_DIM), 1.0),
        "c": nrm(ks[6], (DEC_BATCH, D), 1.0),
        "c_ctx": nrm(ks[7], (D,), 1.0),
        "ada_w": nrm(ks[8], (DEPTH, D, N_ADA * D), 0.5 * D ** -0.5),
        "ada_b": nrm(ks[9], (DEPTH, N_ADA * D), 0.02),
        "norm_g": 1.0 + nrm(ks[10], (DEPTH, 3, D), 0.05),
        "ffn_w1": nrm(ks[11], (DEPTH, 2, D, D_FF), D ** -0.5),
        "ffn_w3": nrm(ks[12], (DEPTH, 2, D, D_FF), D ** -0.5),
        "ffn_w2": nrm(ks[13], (DEPTH, 2, D_FF, D), D_FF ** -0.5),
        "w_in": nrm(ks[14], (DEPTH, D, IN_WIDTH), D ** -0.5),
        "w_out": nrm(ks[15], (DEPTH, MIX_WIDTH, D), MIX_WIDTH ** -0.5),
        "hy_conv_w": nrm(ks[16], (DEPTH, SHORT_CONV, 3 * HY_CH), SHORT_CONV ** -0.5),
        "hy_conv_b": nrm(ks[17], (DEPTH, 3 * HY_CH), 0.02),
        "hy_filt_w1": nrm(ks[18], (DEPTH, HY_EMB, HY_FILT_W), HY_EMB ** -0.5),
        "hy_filt_b1": nrm(ks[19], (DEPTH, HY_FILT_W), 0.1),
        "hy_filt_w2": nrm(ks[20], (DEPTH, HY_FILT_W, HY_FILT_W), HY_FILT_W ** -0.5),
        "hy_filt_b2": nrm(ks[21], (DEPTH, HY_FILT_W), 0.1),
        "hy_filt_w3": nrm(ks[22], (DEPTH, HY_FILT_W, 2 * HY_CH), 0.1 * HY_FILT_W ** -0.5),
        "hy_freq": 1.0 + nrm(ks[23], (DEPTH, 2, HY_FILT_W), 0.1),
        "hy_skip": nrm(ks[24], (DEPTH, HY_CH), 0.5),
        "na_q_g": 1.0 + nrm(ks[25], (DEPTH, HEAD_DIM), 0.05),
        "na_k_g": 1.0 + nrm(ks[26], (DEPTH, HEAD_DIM), 0.05),
        "na_rpb": nrm(ks[27], (DEPTH, NA_HEADS, 2 * NA_ROWS - 1, 2 * NA_COLS - 1), 0.02),
        "gqa_q_g": 1.0 + nrm(ks[28], (DEPTH, HEAD_DIM), 0.05),
        "gqa_k_g": 1.0 + nrm(ks[29], (DEPTH, HEAD_DIM), 0.05),
        "gqa_sink": nrm(ks[30], (DEPTH, GQA_Q_HEADS), 0.5),
    }


def reference(x_prompt, x_sample, cache_na_k, cache_na_v, cache_gqa_k, cache_gqa_v, c, c_ctx,
              ada_w, ada_b, norm_g, ffn_w1, ffn_w3, ffn_w2, w_in, w_out,
              hy_conv_w, hy_conv_b, hy_filt_w1, hy_filt_b1, hy_filt_w2, hy_filt_b2, hy_filt_w3,
              hy_freq, hy_skip, na_q_g, na_k_g, na_rpb, gqa_q_g, gqa_k_g, gqa_sink):
    y_prompt = x_prompt
    y_sample = x_sample
    nk_list, nv_list, gk_list, gv_list = [], [], [], []
    for l in range(DEPTH):
        lp = {
            "ada_w": ada_w[l], "ada_b": ada_b[l], "norm_g": norm_g[l],
            "ffn_w1": ffn_w1[l], "ffn_w3": ffn_w3[l], "ffn_w2": ffn_w2[l],
            "w_in": w_in[l], "w_out": w_out[l],
            "hy_conv_w": hy_conv_w[l], "hy_conv_b": hy_conv_b[l],
            "hy_filt_w1": hy_filt_w1[l], "hy_filt_b1": hy_filt_b1[l],
            "hy_filt_w2": hy_filt_w2[l], "hy_filt_b2": hy_filt_b2[l], "hy_filt_w3": hy_filt_w3[l],
            "hy_freq": hy_freq[l], "hy_skip": hy_skip[l],
            "na_q_g": na_q_g[l], "na_k_g": na_k_g[l], "na_rpb": na_rpb[l],
            "gqa_q_g": gqa_q_g[l], "gqa_k_g": gqa_k_g[l], "gqa_sink": gqa_sink[l],
        }
        y_prompt, (nk, nv, gk, gv) = _layer(y_prompt, c_ctx[None, :], lp, functools.partial(_mixer_context, lp=lp))
        nk_list.append(nk)
        nv_list.append(nv)
        gk_list.append(gk)
        gv_list.append(gv)
        y_sample, _ = _layer(y_sample, c, lp, functools.partial(
            _mixer_latent, lp=lp, na_kc=cache_na_k[:, l], na_vc=cache_na_v[:, l],
            gq_kc=cache_gqa_k[:, l], gq_vc=cache_gqa_v[:, l]))
    new_na_k = jnp.stack(nk_list, axis=1)
    new_na_v = jnp.stack(nv_list, axis=1)
    new_gqa_k = jnp.stack(gk_list, axis=1)
    new_gqa_v = jnp.stack(gv_list, axis=1)
    return (y_prompt, y_sample, new_na_k, new_na_v, new_gqa_k, new_gqa_v)
```

```python
import functools
import math

import numpy as np
import jax
import jax.numpy as jnp
from jax import lax
from jax.experimental import pallas as pl
from jax.experimental.pallas import tpu as pltpu

F32 = jnp.float32
BF16 = jnp.bfloat16

D_MODEL = 1024
BATCH = 16
SEQ = 256
DEPTH = 2
DEC_BATCH = 2
DEC_SEQ = 2048
PAST_LEN = 256
GRID_W = 64
HEAD_DIM = 64
HY_CH = 256
NA_HEADS = 6
GQA_Q_HEADS = 6
GQA_KV_HEADS = 2
NA_W = NA_HEADS * HEAD_DIM
GQ_W = GQA_Q_HEADS * HEAD_DIM
GKV_W = GQA_KV_HEADS * HEAD_DIM
IN_WIDTH = 3 * HY_CH + 3 * NA_W + GQ_W + 2 * GKV_W
D_FF = 2816
N_ADA = 9
HY_BANDS = 16
HY_EMB = 1 + 2 * HY_BANDS
HY_FILT_W = 64
HY_DECAY_TARGET = 1e-2
HY_FAST_PCT = 0.3
HY_SLOW_PCT = 1.5
NA_ROWS = 8
NA_COLS = 16
GQA_WINDOW = 128
BLK = 128
ROPE_BASE = 10000.0
EPS = 1e-6
NEG_INF = -1e30

LANES = 128
V7X_VMEM_BYTES = 64 * 1024 * 1024
VMEM_LIMIT = V7X_VMEM_BYTES * 7 // 8

OFF_NA = 3 * HY_CH
OFF_GQ = OFF_NA + 3 * NA_W

SH1, SC1, G1, SH2, SC2, G2, SH3, SC3, G3 = range(N_ADA)

FFN_TM = 1024
FFN_TF = 256
ROW_TM = 512
HY_TKF = 256

_HIGHEST = lax.Precision.HIGHEST


def _cparams(sem):
    return pltpu.CompilerParams(dimension_semantics=sem, vmem_limit_bytes=VMEM_LIMIT)


def _silu(x):
    return x * (1.0 / (1.0 + jnp.exp(-x)))


def _dot(a, b):
    return jnp.dot(a, b, preferred_element_type=F32)


def _dot_nt(a, b):
    return lax.dot_general(a, b, (((1,), (1,)), ((), ())), preferred_element_type=F32)


def _split(x):
    hi = x.astype(BF16)
    lo = (x - hi.astype(F32)).astype(BF16)
    return hi, lo


def _dot3(a_hi, a_lo, b_hi, b_lo):
    return _dot(a_hi, b_hi) + (_dot(a_lo, b_hi) + _dot(a_hi, b_lo))


def _mod_norm(x, g, shift, scale):
    ms = jnp.mean(x * x, axis=-1, keepdims=True)
    return (x * lax.rsqrt(ms + EPS) * g) * (1.0 + scale) + shift


def _mod_kernel(c_ref, w_ref, b_ref, o_ref):
    a = _silu(c_ref[...]).astype(BF16)
    o_ref[...] = _dot(a, w_ref[...].astype(BF16)) + b_ref[...]


def _adaln_mod(cc, ada_w, ada_b, layer):
    tn = D_MODEL
    n_out = N_ADA * D_MODEL
    out = pl.pallas_call(
        _mod_kernel,
        grid=(n_out // tn,),
        in_specs=[
            pl.BlockSpec((8, D_MODEL), lambda j: (0, 0)),
            pl.BlockSpec((None, D_MODEL, tn), lambda j: (layer, 0, j)),
            pl.BlockSpec((None, 1, tn), lambda j: (layer, 0, j)),
        ],
        out_specs=pl.BlockSpec((8, tn), lambda j: (0, j)),
        out_shape=jax.ShapeDtypeStruct((8, n_out), F32),
        compiler_params=_cparams(("arbitrary",)),
        name="adaln_mod",
    )(cc, ada_w, ada_b.reshape(DEPTH, 1, n_out))
    return out.reshape(8, N_ADA, D_MODEL)


def _cast_kernel(x_ref, o_ref):
    o_ref[...] = x_ref[...].astype(BF16)


def _cast_bf16(w, layer):
    _, rows, cols = w.shape
    tr = 256
    return pl.pallas_call(
        _cast_kernel,
        grid=(rows // tr,),
        in_specs=[pl.BlockSpec((None, tr, cols), lambda i: (layer, i, 0))],
        out_specs=pl.BlockSpec((tr, cols), lambda i: (i, 0)),
        out_shape=jax.ShapeDtypeStruct((rows, cols), BF16),
        compiler_params=_cparams(("arbitrary",)),
        name="cast_bf16",
    )(w)


def _ffn_kernel(x_ref, mod_ref, g_ref, w1_ref, w3_ref, w2_ref, o_ref, h_scr, acc_scr, *, sh, sc, gt):
    j = pl.program_id(1)

    @pl.when(j == 0)
    def _():
        h = _mod_norm(x_ref[...], g_ref[...], mod_ref[0, sh:sh + 1, :], mod_ref[0, sc:sc + 1, :])
        h_scr[...] = h.astype(BF16)
        acc_scr[...] = jnp.zeros_like(acc_scr)

    h = h_scr[...]
    a = _dot(h, w1_ref[...].astype(BF16))
    b = _dot(h, w3_ref[...].astype(BF16))
    act = (_silu(a) * b).astype(BF16)
    acc_scr[...] += _dot(act, w2_ref[...].astype(BF16))

    @pl.when(j == pl.num_programs(1) - 1)
    def _():
        o_ref[...] = x_ref[...] + (0.5 * mod_ref[0, gt:gt + 1, :]) * acc_scr[...]


def _ffn(x, mod, g, w1, w3, w2, layer, which, grp_fn, sh, sc, gt):
    rows = x.shape[0]
    tm, tf = FFN_TM, FFN_TF
    return pl.pallas_call(
        functools.partial(_ffn_kernel, sh=sh, sc=sc, gt=gt),
        grid=(rows // tm, D_FF // tf),
        in_specs=[
            pl.BlockSpec((tm, D_MODEL), lambda i, j: (i, 0)),
            pl.BlockSpec((1, N_ADA, D_MODEL), lambda i, j: (grp_fn(i, tm), 0, 0)),
            pl.BlockSpec((1, D_MODEL), lambda i, j: (0, 0)),
            pl.BlockSpec((None, None, D_MODEL, tf), lambda i, j: (layer, which, 0, j)),
            pl.BlockSpec((None, None, D_MODEL, tf), lambda i, j: (layer, which, 0, j)),
            pl.BlockSpec((None, None, tf, D_MODEL), lambda i, j: (layer, which, j, 0)),
        ],
        out_specs=pl.BlockSpec((tm, D_MODEL), lambda i, j: (i, 0)),
        out_shape=jax.ShapeDtypeStruct((rows, D_MODEL), F32),
        scratch_shapes=[pltpu.VMEM((tm, D_MODEL), BF16), pltpu.VMEM((tm, D_MODEL), F32)],
        compiler_params=_cparams(("arbitrary", "arbitrary")),
        name="ffn_half_step",
    )(x, mod, g, w1, w3, w2)


def _head_norm(xs, bsum, g):
    ms = jnp.dot(xs * xs, bsum, precision=_HIGHEST, preferred_element_type=F32)
    return xs * lax.rsqrt(ms + EPS) * g


def _rope(xs, cos, sin, first_half):
    swapped = jnp.where(first_half, pltpu.roll(xs, LANES - 16, 1), pltpu.roll(xs, 16, 1))
    return xs * cos + swapped * sin


def _dup_heads(x2, low_half):
    sw = pltpu.roll(x2, HEAD_DIM, 1)
    return jnp.where(low_half, x2, sw), jnp.where(low_half, sw, x2)


def _proj_kernel(*refs, rope, emit_kv):
    x_ref, mod_ref, g_ref, w_ref, bsum_ref, gains_ref = refs[:6]
    pos = 6
    if rope:
        cos_ref, sin_ref = refs[pos:pos + 2]
        pos += 2
    uhy_ref, qn_ref, kn_ref, vn_ref, qg_ref, kgd_ref, vgd_ref = refs[pos:pos + 7]
    pos += 7
    if emit_kv:
        kg_ref, vg_ref = refs[pos:pos + 2]
        pos += 2
    u_scr = refs[pos]

    h = _mod_norm(x_ref[...], g_ref[...], mod_ref[0, SH2:SH2 + 1, :], mod_ref[0, SC2:SC2 + 1, :])
    u_scr[...] = _dot(h.astype(BF16), w_ref[...])

    uhy_ref[...] = u_scr[:, 0:OFF_NA]
    bsum = bsum_ref[...]
    qn_ref[...] = _head_norm(u_scr[:, OFF_NA:OFF_NA + NA_W], bsum, gains_ref[0:1, :])
    kn_ref[...] = _head_norm(u_scr[:, OFF_NA + NA_W:OFF_NA + 2 * NA_W], bsum, gains_ref[1:2, :])
    vn_ref[...] = u_scr[:, OFF_NA + 2 * NA_W:OFF_GQ]

    qg = _head_norm(u_scr[:, OFF_GQ:OFF_GQ + GQ_W], bsum, gains_ref[2:3, :])
    kg = _head_norm(u_scr[:, OFF_GQ + GQ_W:OFF_GQ + GQ_W + GKV_W], bsum[0:GKV_W, 0:GKV_W], gains_ref[3:4, 0:GKV_W])
    vg = u_scr[:, OFF_GQ + GQ_W + GKV_W:IN_WIDTH]
    if emit_kv:
        kg_ref[...] = kg
        vg_ref[...] = vg

    lane = lax.broadcasted_iota(jnp.int32, (1, LANES), 1)
    if rope:
        cos = cos_ref[...]
        sin = sin_ref[...]
        first_half = (lane % 32) < 16
        for p in range(GQ_W // LANES):
            qg_ref[:, p * LANES:(p + 1) * LANES] = _rope(qg[:, p * LANES:(p + 1) * LANES], cos, sin, first_half)
        kg = _rope(kg, cos, sin, first_half)
    else:
        qg_ref[...] = qg

    low_half = lane < HEAD_DIM
    k0, k1 = _dup_heads(kg, low_half)
    v0, v1 = _dup_heads(vg, low_half)
    kgd_ref[:, 0:LANES] = k0
    kgd_ref[:, LANES:2 * LANES] = k1
    vgd_ref[:, 0:LANES] = v0
    vgd_ref[:, LANES:2 * LANES] = v1


def _project(x, mod, g, w_bf, bsum, gains, grp_fn, rope_tabs, emit_kv):
    rows = x.shape[0]
    tm = ROW_TM
    rope = rope_tabs is not None
    in_specs = [
        pl.BlockSpec((tm, D_MODEL), lambda i: (i, 0)),
        pl.BlockSpec((1, N_ADA, D_MODEL), lambda i: (grp_fn(i, tm), 0, 0)),
        pl.BlockSpec((1, D_MODEL), lambda i: (0, 0)),
        pl.BlockSpec((D_MODEL, IN_WIDTH), lambda i: (0, 0)),
        pl.BlockSpec((NA_W, NA_W), lambda i: (0, 0)),
        pl.BlockSpec((4, NA_W), lambda i: (0, 0)),
    ]
    args = [x, mod, g, w_bf, bsum, gains]
    if rope:
        seq_tiles = DEC_SEQ // tm
        in_specs += [pl.BlockSpec((tm, LANES), lambda i: (i % seq_tiles, 0))] * 2
        args += list(rope_tabs)
    widths = [OFF_NA, NA_W, NA_W, NA_W, GQ_W, 2 * LANES, 2 * LANES]
    if emit_kv:
        widths += [GKV_W, GKV_W]
    out_specs = [pl.BlockSpec((tm, w), lambda i: (i, 0)) for w in widths]
    out_shape = [jax.ShapeDtypeStruct((rows, w), F32) for w in widths]
    return pl.pallas_call(
        functools.partial(_proj_kernel, rope=rope, emit_kv=emit_kv),
        grid=(rows // tm,),
        in_specs=in_specs,
        out_specs=out_specs,
        out_shape=out_shape,
        scratch_shapes=[pltpu.VMEM((tm, IN_WIDTH), F32)],
        compiler_params=_cparams(("arbitrary",)),
        name="mixer_in_proj",
    )(*args)


def _softmax_pv(scores, values, sink):
    m = scores[0].max(axis=-1, keepdims=True)
    for s in scores[1:]:
        m = jnp.maximum(m, s.max(axis=-1, keepdims=True))
    if sink is not None:
        m = jnp.maximum(m, sink)
    den = None
    acc = None
    for s, v in zip(scores, values):
        p = jnp.exp(s - m)
        d = p.sum(axis=-1, keepdims=True)
        o = _dot(p.astype(BF16), v)
        den = d if den is None else den + d
        acc = o if acc is None else acc + o
    if sink is not None:
        den = den + jnp.exp(sink - m)
    return acc / den


def _half_masks(m):
    lane = lax.broadcasted_iota(jnp.int32, (m, LANES), 1)
    return lane < HEAD_DIM


def _ctx_attn_kernel(sink_ref, qn_ref, kn_ref, vn_ref, qg_ref, kgd_ref, vgd_ref, yna_ref, ygq_ref):
    scale = HEAD_DIM ** -0.5
    m = qn_ref.shape[1]
    low = _half_masks(m)
    for p in range(NA_HEADS // 2):
        sl = slice(p * LANES, (p + 1) * LANES)
        q2 = qn_ref[0, :, sl] * scale
        k2 = kn_ref[0, :, sl].astype(BF16)
        v2 = vn_ref[0, :, sl].astype(BF16)
        halves = []
        for keep in (low, ~low):
            qh = jnp.where(keep, q2, 0.0).astype(BF16)
            halves.append(_softmax_pv([_dot_nt(qh, k2)], [v2], None))
        yna_ref[0, :, sl] = jnp.where(low, halves[0], halves[1])
    group = GQA_Q_HEADS // GQA_KV_HEADS
    for p in range(GQA_Q_HEADS // 2):
        sl = slice(p * LANES, (p + 1) * LANES)
        q2 = qg_ref[0, :, sl] * scale
        halves = []
        for half, keep in enumerate((low, ~low)):
            head = 2 * p + half
            kv = head // group
            kd = kgd_ref[0, :, kv * LANES:(kv + 1) * LANES].astype(BF16)
            vd = vgd_ref[0, :, kv * LANES:(kv + 1) * LANES].astype(BF16)
            qh = jnp.where(keep, q2, 0.0).astype(BF16)
            halves.append(_softmax_pv([_dot_nt(qh, kd)], [vd], sink_ref[head]))
        ygq_ref[0, :, sl] = jnp.where(low, halves[0], halves[1])


def _ctx_attention(sink, qn, kn, vn, qg, kgd, vgd):
    b, l, _ = qn.shape

    def spec(w):
        return pl.BlockSpec((1, l, w), lambda i: (i, 0, 0))

    return pl.pallas_call(
        _ctx_attn_kernel,
        grid=(b,),
        in_specs=[pl.BlockSpec(memory_space=pltpu.SMEM), spec(NA_W), spec(NA_W), spec(NA_W), spec(GQ_W),
                  spec(2 * LANES), spec(2 * LANES)],
        out_specs=[spec(NA_W), spec(GQ_W)],
        out_shape=[jax.ShapeDtypeStruct((b, l, NA_W), F32), jax.ShapeDtypeStruct((b, l, GQ_W), F32)],
        compiler_params=_cparams(("arbitrary",)),
        name="context_attention",
    )(sink, qn, kn, vn, qg, kgd, vgd)


NA_KEYS = NA_ROWS * GRID_W


def _na_bias_kernel(r_ref, oh_ref, mk_ref, o_ref):
    o_ref[...] = jnp.dot(r_ref[...], oh_ref[...], precision=_HIGHEST, preferred_element_type=F32) + mk_ref[...]


def _na_bias_tables():
    q = np.arange(GRID_W)[:, None]
    kc = np.arange(GRID_W)[None, :]
    win_lo = np.clip(q - NA_COLS // 2, 0, GRID_W - NA_COLS)
    ok = (kc >= win_lo) & (kc < win_lo + NA_COLS)
    j = kc - q + NA_COLS - 1
    onehot = np.zeros((LANES, GRID_W * GRID_W), np.float32)
    qq, kk = np.nonzero(ok)
    onehot[j[qq, kk], qq * GRID_W + kk] = 1.0
    mask = np.where(ok, 0.0, NEG_INF).astype(np.float32).reshape(1, GRID_W * GRID_W)
    return onehot, mask


def _na_bias(rpb_l):
    n_dr = 2 * NA_ROWS - 1
    rows = NA_HEADS * n_dr
    rows_pad = -(-rows // 8) * 8
    r = jnp.zeros((rows_pad, LANES), F32).at[:rows, :2 * NA_COLS - 1].set(rpb_l.reshape(rows, 2 * NA_COLS - 1))
    onehot, mask = _na_bias_tables()
    out = pl.pallas_call(
        _na_bias_kernel,
        out_shape=jax.ShapeDtypeStruct((rows_pad, GRID_W * GRID_W), F32),
        compiler_params=pltpu.CompilerParams(vmem_limit_bytes=VMEM_LIMIT),
        name="na_bias_expand",
    )(r, jnp.asarray(onehot), jnp.asarray(mask))
    bm = out[:rows].reshape(NA_HEADS, n_dr, GRID_W, GRID_W)
    return jnp.concatenate([bm[:, :-1], bm[:, 1:]], axis=-1)


def _na_attn_kernel(q_ref, k_ref, v_ref, kc_ref, vc_ref, bm_ref, o_ref):
    scale = HEAD_DIM ** -0.5
    r = pl.program_id(1)
    n_rows = DEC_SEQ // GRID_W
    start = jnp.clip(r - NA_ROWS // 2, 0, n_rows - NA_ROWS)
    shift = r - start
    row0 = pl.multiple_of(start * GRID_W, GRID_W)
    low = _half_masks(GRID_W)
    for p in range(NA_HEADS // 2):
        sl = slice(p * LANES, (p + 1) * LANES)
        q2 = q_ref[0, :, sl] * scale
        k2 = k_ref[0, pl.ds(row0, NA_KEYS), sl].astype(BF16)
        v2 = v_ref[0, pl.ds(row0, NA_KEYS), sl].astype(BF16)
        kc2 = kc_ref[0, 0, :, sl].astype(BF16)
        vc2 = vc_ref[0, 0, :, sl].astype(BF16)
        halves = []
        for half, keep in enumerate((low, ~low)):
            head = 2 * p + half
            qh = jnp.where(keep, q2, 0.0).astype(BF16)
            bias = jnp.concatenate(
                [bm_ref[head, 2 * jj - shift + NA_ROWS - 1] for jj in range(NA_ROWS // 2)], axis=-1)
            s_loc = _dot_nt(qh, k2) + bias
            s_ctx = _dot_nt(qh, kc2)
            halves.append(_softmax_pv([s_loc, s_ctx], [v2, vc2], None))
        o_ref[0, :, sl] = jnp.where(low, halves[0], halves[1])


def _na_attention(q, k, v, kc, vc, bm2, layer):
    b, n, _ = q.shape
    n_rows = n // GRID_W
    return pl.pallas_call(
        _na_attn_kernel,
        grid=(b, n_rows),
        in_specs=[
            pl.BlockSpec((1, GRID_W, NA_W), lambda i, r: (i, r, 0)),
            pl.BlockSpec((1, n, NA_W), lambda i, r: (i, 0, 0)),
            pl.BlockSpec((1, n, NA_W), lambda i, r: (i, 0, 0)),
            pl.BlockSpec((1, 1, PAST_LEN, NA_W), lambda i, r: (i, layer, 0, 0)),
            pl.BlockSpec((1, 1, PAST_LEN, NA_W), lambda i, r: (i, layer, 0, 0)),
            pl.BlockSpec(bm2.shape, lambda i, r: (0, 0, 0, 0)),
        ],
        out_specs=pl.BlockSpec((1, GRID_W, NA_W), lambda i, r: (i, r, 0)),
        out_shape=jax.ShapeDtypeStruct((b, n, NA_W), F32),
        compiler_params=_cparams(("arbitrary", "arbitrary")),
        name="neighbourhood_attention",
    )(q, k, v, kc, vc, bm2)


WIN_KEYS = 3 * BLK


def _win_attn_kernel(sink_ref, q_ref, kgd_ref, vgd_ref, kc_ref, vc_ref, o_ref):
    scale = HEAD_DIM ** -0.5
    nb = pl.program_id(1)
    n = kgd_ref.shape[1]
    start = pl.multiple_of(jnp.clip((nb - 1) * BLK, 0, n - WIN_KEYS), BLK)
    q_pos = nb * BLK + lax.broadcasted_iota(jnp.int32, (BLK, WIN_KEYS), 0)
    k_pos = start + lax.broadcasted_iota(jnp.int32, (BLK, WIN_KEYS), 1)
    ok = jnp.abs(q_pos - k_pos) <= GQA_WINDOW
    low = _half_masks(BLK)
    low_c = _half_masks(PAST_LEN)
    kc0, kc1 = _dup_heads(kc_ref[0, 0], low_c)
    vc0, vc1 = _dup_heads(vc_ref[0, 0], low_c)
    kcd = (kc0.astype(BF16), kc1.astype(BF16))
    vcd = (vc0.astype(BF16), vc1.astype(BF16))
    group = GQA_Q_HEADS // GQA_KV_HEADS
    for p in range(GQA_Q_HEADS // 2):
        sl = slice(p * LANES, (p + 1) * LANES)
        q2 = q_ref[0, :, sl] * scale
        halves = []
        for half, keep in enumerate((low, ~low)):
            head = 2 * p + half
            kv = head // group
            kd = kgd_ref[0, pl.ds(start, WIN_KEYS), kv * LANES:(kv + 1) * LANES].astype(BF16)
            vd = vgd_ref[0, pl.ds(start, WIN_KEYS), kv * LANES:(kv + 1) * LANES].astype(BF16)
            qh = jnp.where(keep, q2, 0.0).astype(BF16)
            s_loc = jnp.where(ok, _dot_nt(qh, kd), NEG_INF)
            s_ctx = _dot_nt(qh, kcd[kv])
            halves.append(_softmax_pv([s_loc, s_ctx], [vd, vcd[kv]], sink_ref[head]))
        o_ref[0, :, sl] = jnp.where(low, halves[0], halves[1])


def _win_attention(sink, q, kgd, vgd, kc, vc, layer):
    b, n, _ = q.shape
    return pl.pallas_call(
        _win_attn_kernel,
        grid=(b, n // BLK),
        in_specs=[
            pl.BlockSpec(memory_space=pltpu.SMEM),
            pl.BlockSpec((1, BLK, GQ_W), lambda i, j: (i, j, 0)),
            pl.BlockSpec((1, n, 2 * LANES), lambda i, j: (i, 0, 0)),
            pl.BlockSpec((1, n, 2 * LANES), lambda i, j: (i, 0, 0)),
            pl.BlockSpec((1, 1, PAST_LEN, GKV_W), lambda i, j: (i, layer, 0, 0)),
            pl.BlockSpec((1, 1, PAST_LEN, GKV_W), lambda i, j: (i, layer, 0, 0)),
        ],
        out_specs=pl.BlockSpec((1, BLK, GQ_W), lambda i, j: (i, j, 0)),
        out_shape=jax.ShapeDtypeStruct((b, n, GQ_W), F32),
        compiler_params=_cparams(("arbitrary", "arbitrary")),
        name="window_attention",
    )(sink, q, kgd, vgd, kc, vc)


@functools.lru_cache(maxsize=None)
def _hyena_consts_np(L):
    n = 2 * L
    k = np.arange(L, dtype=np.int64)
    ang = (2.0 * np.pi / n) * ((k[:, None] * k[None, :]) % n).astype(np.float64)

    def split(a):
        a32 = a.astype(np.float32)
        hi = a32.astype(BF16)
        lo = (a32 - hi.astype(np.float32)).astype(BF16)
        return hi, lo

    cm = split(np.cos(ang))
    sm = split(-np.sin(ang))
    idx = np.arange(L, dtype=np.float32)
    t = idx / np.float32(L - 1)
    bands = np.linspace(1e-4, HY_BANDS - 1, HY_BANDS, dtype=np.float32)
    fang = np.float32(2.0 * math.pi / L) * idx[:, None] * bands[None, :]
    feats = np.zeros((L, LANES), np.float32)
    feats[:, 0] = t
    feats[:, 1:1 + HY_BANDS] = np.cos(fang)
    feats[:, 1 + HY_BANDS:HY_EMB] = -np.sin(fang)
    max_decay = math.log(HY_DECAY_TARGET) / HY_FAST_PCT
    min_decay = math.log(HY_DECAY_TARGET) / HY_SLOW_PCT
    deltas = np.abs(np.linspace(min_decay, max_decay, HY_CH, dtype=np.float32))
    decay = np.exp(-t[:, None] * deltas[None, :]).astype(np.float32)
    decay2 = np.concatenate([decay, decay], axis=1)
    return cm, sm, feats, decay2


def _hyena_consts(L):
    cm, sm, feats, decay2 = _hyena_consts_np(L)
    return ((jnp.asarray(cm[0]), jnp.asarray(cm[1])), (jnp.asarray(sm[0]), jnp.asarray(sm[1])),
            jnp.asarray(feats), jnp.asarray(decay2))


def _hy_filter_kernel(feats_ref, w1_ref, b1_ref, w2_ref, b2_ref, w3_ref, freq_ref, decay_ref, wsum_ref, wdiff_ref):
    def dot_hi(a, b):
        return jnp.dot(a, b, precision=_HIGHEST, preferred_element_type=F32)

    hid = jnp.sin(freq_ref[0:1, :] * (dot_hi(feats_ref[...], w1_ref[...]) + b1_ref[...]))
    hid = jnp.sin(freq_ref[1:2, :] * (dot_hi(hid, w2_ref[...]) + b2_ref[...]))
    taps = dot_hi(hid, w3_ref[...]) * decay_ref[...]
    fwd = taps[:, 0:HY_CH]
    bwd = taps[:, HY_CH:2 * HY_CH]
    row = lax.broadcasted_iota(jnp.int32, bwd.shape, 0)
    bwd = jnp.where(row == 0, 0.0, bwd)
    wsum_ref[...] = fwd + bwd
    wdiff_ref[...] = fwd - bwd


def _hy_filter(L, feats, decay2, w1, b1, w2, b2, w3, freq):
    w1p = jnp.zeros((LANES, HY_FILT_W), F32).at[:HY_EMB].set(w1)
    return pl.pallas_call(
        _hy_filter_kernel,
        out_shape=[jax.ShapeDtypeStruct((L, HY_CH), F32)] * 2,
        compiler_params=pltpu.CompilerParams(vmem_limit_bytes=VMEM_LIMIT),
        name="hyena_filter",
    )(feats, w1p, b1.reshape(1, -1), w2, b2.reshape(1, -1), w3, freq, decay2)


def _hy_pre_kernel(u_ref, cw_ref, cb_ref, x0_ref, z_ref):
    L = u_ref.shape[1]
    row = lax.broadcasted_iota(jnp.int32, (L, 1), 0)

    def conv_chunk(ci):
        sl = slice(ci * HY_CH, (ci + 1) * HY_CH)
        u = u_ref[0, :, sl]
        prev = jnp.where(row == 0, 0.0, pltpu.roll(u, 1, 0))
        nxt = jnp.where(row == L - 1, 0.0, pltpu.roll(u, L - 1, 0))
        return prev * cw_ref[0:1, sl] + u * cw_ref[1:2, sl] + nxt * cw_ref[2:3, sl] + cb_ref[:, sl]

    x0_ref[0] = conv_chunk(0)
    z_ref[0] = conv_chunk(1) * conv_chunk(2)


def _hy_pre(u_hy, conv_w, conv_b):
    b, L, c3 = u_hy.shape
    return pl.pallas_call(
        _hy_pre_kernel,
        grid=(b,),
        in_specs=[
            pl.BlockSpec((1, L, c3), lambda i: (i, 0, 0)),
            pl.BlockSpec((3, c3), lambda i: (0, 0)),
            pl.BlockSpec((1, c3), lambda i: (0, 0)),
        ],
        out_specs=[pl.BlockSpec((1, L, HY_CH), lambda i: (i, 0, 0))] * 2,
        out_shape=[jax.ShapeDtypeStruct((b, L, HY_CH), F32)] * 2,
        compiler_params=_cparams(("arbitrary",)),
        name="hyena_short_conv",
    )(u_hy, conv_w, conv_b.reshape(1, c3))


def _hy_conv_kernel(z_ref, x0_ref, wsum_ref, wdiff_ref, skip_ref,
                    crh_ref, crl_ref, srh_ref, srl_ref, cch_ref, ccl_ref, sch_ref, scl_ref,
                    o_ref, zh_scr, zl_scr, wsh_scr, wsl_scr, wdh_scr, wdl_scr, acc_scr, *, group):
    t = pl.program_id(1)
    L = z_ref.shape[1]
    n = 2 * L

    @pl.when(t == 0)
    def _():
        for b in range(group):
            hi, lo = _split(z_ref[b])
            zh_scr[b] = hi
            zl_scr[b] = lo
        wsh_scr[...], wsl_scr[...] = _split(wsum_ref[...])
        wdh_scr[...], wdl_scr[...] = _split(wdiff_ref[...])
        acc_scr[...] = jnp.zeros_like(acc_scr)

    crh, crl, srh, srl = crh_ref[...], crl_ref[...], srh_ref[...], srl_ref[...]
    cch, ccl, sch, scl = cch_ref[...], ccl_ref[...], sch_ref[...], scl_ref[...]
    k_re = _dot3(crh, crl, wsh_scr[...], wsl_scr[...])
    k_im = _dot3(srh, srl, wdh_scr[...], wdl_scr[...])
    for b in range(group):
        zh, zl = zh_scr[b], zl_scr[b]
        z_re = _dot3(crh, crl, zh, zl)
        z_im = _dot3(srh, srl, zh, zl)
        y_re = z_re * k_re - z_im * k_im
        y_im = z_re * k_im + z_im * k_re
        yrh, yrl = _split(y_re)
        yih, yil = _split(y_im)
        acc_scr[b] += _dot3(cch, ccl, yrh, yrl) + _dot3(sch, scl, yih, yil)

    @pl.when(t == pl.num_programs(1) - 1)
    def _():
        row = lax.broadcasted_iota(jnp.int32, (L, 1), 0)
        sgn = (1 - 2 * (row % 2)).astype(F32)
        w = wsum_ref[...]
        w_dc = w.sum(axis=0, keepdims=True)
        w_ny = (w * sgn).sum(axis=0, keepdims=True)
        for b in range(group):
            z = z_ref[b]
            z_dc = z.sum(axis=0, keepdims=True)
            z_ny = (z * sgn).sum(axis=0, keepdims=True)
            conv = (2.0 / n) * acc_scr[b] - (1.0 / n) * (z_dc * w_dc) + (1.0 / n) * (sgn * (z_ny * w_ny))
            o_ref[b] = x0_ref[b] * (conv + z * skip_ref[...])


def _hy_conv(z, x0, wsum, wdiff, skip, cm, sm, group):
    b, L, c = z.shape
    tkf = min(HY_TKF, L)
    row_spec = pl.BlockSpec((tkf, L), lambda g, t: (t, 0))
    col_spec = pl.BlockSpec((L, tkf), lambda g, t: (0, t))
    once = pl.Buffered(1)
    seq_spec = pl.BlockSpec((group, L, c), lambda g, t: (g, 0, 0), pipeline_mode=once)
    w_spec = pl.BlockSpec((L, c), lambda g, t: (0, 0), pipeline_mode=once)
    return pl.pallas_call(
        functools.partial(_hy_conv_kernel, group=group),
        grid=(b // group, L // tkf),
        in_specs=[seq_spec, seq_spec, w_spec, w_spec, pl.BlockSpec((1, c), lambda g, t: (0, 0)),
                  row_spec, row_spec, row_spec, row_spec, col_spec, col_spec, col_spec, col_spec],
        out_specs=pl.BlockSpec((group, L, c), lambda g, t: (g, 0, 0)),
        out_shape=jax.ShapeDtypeStruct((b, L, c), F32),
        scratch_shapes=[pltpu.VMEM((group, L, c), BF16), pltpu.VMEM((group, L, c), BF16),
                        pltpu.VMEM((L, c), BF16), pltpu.VMEM((L, c), BF16),
                        pltpu.VMEM((L, c), BF16), pltpu.VMEM((L, c), BF16),
                        pltpu.VMEM((group, L, c), F32)],
        compiler_params=_cparams(("arbitrary", "arbitrary")),
        name="hyena_long_conv",
    )(z, x0, wsum, wdiff, skip.reshape(1, c), cm[0], cm[1], sm[0], sm[1], cm[0], cm[1], sm[0], sm[1])


def _merge_kernel(x_ref, yhy_ref, yna_ref, ygq_ref, mod_ref, w_ref, o_ref):
    y = jnp.concatenate([yhy_ref[...], yna_ref[...], ygq_ref[...]], axis=-1).astype(BF16)
    o_ref[...] = x_ref[...] + mod_ref[0, G2:G2 + 1, :] * _dot(y, w_ref[...])


def _merge(x, y_hy, y_na, y_gq, mod, w_bf, grp_fn):
    rows = x.shape[0]
    tm = ROW_TM
    return pl.pallas_call(
        _merge_kernel,
        grid=(rows // tm,),
        in_specs=[
            pl.BlockSpec((tm, D_MODEL), lambda i: (i, 0)),
            pl.BlockSpec((tm, HY_CH), lambda i: (i, 0)),
            pl.BlockSpec((tm, NA_W), lambda i: (i, 0)),
            pl.BlockSpec((tm, GQ_W), lambda i: (i, 0)),
            pl.BlockSpec((1, N_ADA, D_MODEL), lambda i: (grp_fn(i, tm), 0, 0)),
            pl.BlockSpec((D_MODEL, D_MODEL), lambda i: (0, 0)),
        ],
        out_specs=pl.BlockSpec((tm, D_MODEL), lambda i: (i, 0)),
        out_shape=jax.ShapeDtypeStruct((rows, D_MODEL), F32),
        compiler_params=_cparams(("arbitrary",)),
        name="mixer_out_proj",
    )(x, y_hy, y_na, y_gq, mod, w_bf)


def _rope_tables():
    pos = np.arange(DEC_SEQ)
    quarter = HEAD_DIM // 4
    inv = ROPE_BASE ** (-np.arange(quarter, dtype=np.float64) / quarter)
    ang_r = (pos // GRID_W)[:, None] * inv[None, :]
    ang_c = (pos % GRID_W)[:, None] * inv[None, :]
    cos_h = np.concatenate([np.cos(ang_r), np.cos(ang_r), np.cos(ang_c), np.cos(ang_c)], axis=1)
    sin_h = np.concatenate([-np.sin(ang_r), np.sin(ang_r), -np.sin(ang_c), np.sin(ang_c)], axis=1)
    reps = LANES // HEAD_DIM
    return (jnp.asarray(np.tile(cos_h, (1, reps)), F32), jnp.asarray(np.tile(sin_h, (1, reps)), F32))


def _head_sum_matrix():
    idx = np.arange(NA_W) // HEAD_DIM
    return jnp.asarray((idx[:, None] == idx[None, :]).astype(np.float32) / HEAD_DIM)


def _grp_prompt(i, tm):
    return 0


def _grp_sample(i, tm):
    return 1 + (i * tm) // DEC_SEQ


def kernel(x_prompt, x_sample, cache_na_k, cache_na_v, cache_gqa_k, cache_gqa_v, c, c_ctx, ada_w, ada_b, norm_g, ffn_w1, ffn_w3, ffn_w2, w_in, w_out, hy_conv_w, hy_conv_b, hy_filt_w1, hy_filt_b1, hy_filt_w2, hy_filt_b2, hy_filt_w3, hy_freq, hy_skip, na_q_g, na_k_g, na_rpb, gqa_q_g, gqa_k_g, gqa_sink):
    xp = x_prompt.reshape(BATCH * SEQ, D_MODEL)
    xs = x_sample.reshape(DEC_BATCH * DEC_SEQ, D_MODEL)
    cc = jnp.zeros((8, D_MODEL), F32).at[0].set(c_ctx).at[1:1 + DEC_BATCH].set(c)
    kc_na = cache_na_k.reshape(DEC_BATCH, DEPTH, PAST_LEN, NA_W)
    vc_na = cache_na_v.reshape(DEC_BATCH, DEPTH, PAST_LEN, NA_W)
    kc_gq = cache_gqa_k.reshape(DEC_BATCH, DEPTH, PAST_LEN, GKV_W)
    vc_gq = cache_gqa_v.reshape(DEC_BATCH, DEPTH, PAST_LEN, GKV_W)

    rope_tabs = _rope_tables()
    bsum = _head_sum_matrix()
    hy_p = _hyena_consts(SEQ)
    hy_s = _hyena_consts(DEC_SEQ)

    nk_list, nv_list, gk_list, gv_list = [], [], [], []
    for l in range(DEPTH):
        mod = _adaln_mod(cc, ada_w, ada_b, l)
        w_in_bf = _cast_bf16(w_in, l)
        w_out_bf = _cast_bf16(w_out, l)
        gains = jnp.stack([jnp.tile(na_q_g[l], NA_HEADS), jnp.tile(na_k_g[l], NA_HEADS),
                           jnp.tile(gqa_q_g[l], GQA_Q_HEADS), jnp.tile(gqa_k_g[l], GQA_Q_HEADS)])
        bm2 = _na_bias(na_rpb[l])
        ng = [norm_g[l, i].reshape(1, D_MODEL) for i in range(3)]
        filt = {}
        for L, consts in ((SEQ, hy_p), (DEC_SEQ, hy_s)):
            filt[L] = _hy_filter(L, consts[2], consts[3], hy_filt_w1[l], hy_filt_b1[l], hy_filt_w2[l],
                                 hy_filt_b2[l], hy_filt_w3[l], hy_freq[l])

        xp = _ffn(xp, mod, ng[0], ffn_w1, ffn_w3, ffn_w2, l, 0, _grp_prompt, SH1, SC1, G1)
        u_hy, qn, kn, vn, qg, kgd, vgd, kg, vg = _project(xp, mod, ng[1], w_in_bf, bsum, gains, _grp_prompt, None, True)
        x0, z = _hy_pre(u_hy.reshape(BATCH, SEQ, 3 * HY_CH), hy_conv_w[l], hy_conv_b[l])
        y_hy = _hy_conv(z, x0, filt[SEQ][0], filt[SEQ][1], hy_skip[l], hy_p[0], hy_p[1], group=4)
        shp = lambda a: a.reshape(BATCH, SEQ, a.shape[-1])
        y_na, y_gq = _ctx_attention(gqa_sink[l], shp(qn), shp(kn), shp(vn), shp(qg), shp(kgd), shp(vgd))
        xp = _merge(xp, y_hy.reshape(-1, HY_CH), y_na.reshape(-1, NA_W), y_gq.reshape(-1, GQ_W), mod, w_out_bf,
                    _grp_prompt)
        xp = _ffn(xp, mod, ng[2], ffn_w1, ffn_w3, ffn_w2, l, 1, _grp_prompt, SH3, SC3, G3)
        nk_list.append(kn.reshape(BATCH, SEQ, NA_HEADS, HEAD_DIM))
        nv_list.append(vn.reshape(BATCH, SEQ, NA_HEADS, HEAD_DIM))
        gk_list.append(kg.reshape(BATCH, SEQ, GQA_KV_HEADS, HEAD_DIM))
        gv_list.append(vg.reshape(BATCH, SEQ, GQA_KV_HEADS, HEAD_DIM))

        xs = _ffn(xs, mod, ng[0], ffn_w1, ffn_w3, ffn_w2, l, 0, _grp_sample, SH1, SC1, G1)
        u_hy, qn, kn, vn, qg, kgd, vgd = _project(xs, mod, ng[1], w_in_bf, bsum, gains, _grp_sample, rope_tabs, False)
        x0, z = _hy_pre(u_hy.reshape(DEC_BATCH, DEC_SEQ, 3 * HY_CH), hy_conv_w[l], hy_conv_b[l])
        y_hy = _hy_conv(z, x0, filt[DEC_SEQ][0], filt[DEC_SEQ][1], hy_skip[l], hy_s[0], hy_s[1], group=DEC_BATCH)
        shs = lambda a: a.reshape(DEC_BATCH, DEC_SEQ, a.shape[-1])
        y_na = _na_attention(shs(qn), shs(kn), shs(vn), kc_na, vc_na, bm2, l)
        y_gq = _win_attention(gqa_sink[l], shs(qg), shs(kgd), shs(vgd), kc_gq, vc_gq, l)
        xs = _merge(xs, y_hy.reshape(-1, HY_CH), y_na.reshape(-1, NA_W), y_gq.reshape(-1, GQ_W), mod, w_out_bf,
                    _grp_sample)
        xs = _ffn(xs, mod, ng[2], ffn_w1, ffn_w3, ffn_w2, l, 1, _grp_sample, SH3, SC3, G3)

    return (xp.reshape(BATCH, SEQ, D_MODEL), xs.reshape(DEC_BATCH, DEC_SEQ, D_MODEL),
            jnp.stack(nk_list, axis=1), jnp.stack(nv_list, axis=1),
            jnp.stack(gk_list, axis=1), jnp.stack(gv_list, axis=1))
```

```python
import functools
import math

import numpy as np
import jax
import jax.numpy as jnp
from jax import lax
from jax.experimental import pallas as pl
from jax.experimental.pallas import tpu as pltpu

F32 = jnp.float32
BF16 = jnp.bfloat16

D_MODEL = 1024
BATCH = 16
SEQ = 256
DEPTH = 2
DEC_BATCH = 2
DEC_SEQ = 2048
PAST_LEN = 256
GRID_W = 64
HEAD_DIM = 64
HY_CH = 256
NA_HEADS = 6
GQA_Q_HEADS = 6
GQA_KV_HEADS = 2
NA_W = NA_HEADS * HEAD_DIM
GQ_W = GQA_Q_HEADS * HEAD_DIM
GKV_W = GQA_KV_HEADS * HEAD_DIM
IN_WIDTH = 3 * HY_CH + 3 * NA_W + GQ_W + 2 * GKV_W
D_FF = 2816
N_ADA = 9
HY_BANDS = 16
HY_EMB = 1 + 2 * HY_BANDS
HY_FILT_W = 64
HY_DECAY_TARGET = 1e-2
HY_FAST_PCT = 0.3
HY_SLOW_PCT = 1.5
NA_ROWS = 8
NA_COLS = 16
GQA_WINDOW = 128
BLK = 128
ROPE_BASE = 10000.0
EPS = 1e-6
NEG_INF = -1e30

LANES = 128
V7X_VMEM_BYTES = 64 * 1024 * 1024
VMEM_LIMIT = V7X_VMEM_BYTES * 7 // 8

OFF_NA = 3 * HY_CH
OFF_GQ = OFF_NA + 3 * NA_W

SH1, SC1, G1, SH2, SC2, G2, SH3, SC3, G3 = range(N_ADA)

FFN_TM = 1024
FFN_TF = 256
ROW_TM = 512
HY_TKF = 256

_HIGHEST = lax.Precision.HIGHEST


def _cparams(sem):
    return pltpu.CompilerParams(dimension_semantics=sem, vmem_limit_bytes=VMEM_LIMIT)


def _silu(x):
    return x * (1.0 / (1.0 + jnp.exp(-x)))


def _dot(a, b):
    return jnp.dot(a, b, preferred_element_type=F32)


def _dot_nt(a, b):
    return lax.dot_general(a, b, (((1,), (1,)), ((), ())), preferred_element_type=F32)


def _split(x):
    hi = x.astype(BF16)
    lo = (x - hi.astype(F32)).astype(BF16)
    return hi, lo


def _mod_norm(x, g, shift, scale):
    ms = jnp.mean(x * x, axis=-1, keepdims=True)
    return (x * lax.rsqrt(ms + EPS) * g) * (1.0 + scale) + shift


def _mod_kernel(c_ref, w_ref, b_ref, o_ref):
    a = _silu(c_ref[...]).astype(BF16)
    o_ref[...] = _dot(a, w_ref[...].astype(BF16)) + b_ref[...]


def _adaln_mod(cc, ada_w, ada_b, layer):
    tn = D_MODEL
    n_out = N_ADA * D_MODEL
    out = pl.pallas_call(
        _mod_kernel,
        grid=(n_out // tn,),
        in_specs=[
            pl.BlockSpec((8, D_MODEL), lambda j: (0, 0)),
            pl.BlockSpec((None, D_MODEL, tn), lambda j: (layer, 0, j)),
            pl.BlockSpec((None, 1, tn), lambda j: (layer, 0, j)),
        ],
        out_specs=pl.BlockSpec((8, tn), lambda j: (0, j)),
        out_shape=jax.ShapeDtypeStruct((8, n_out), F32),
        compiler_params=_cparams(("arbitrary",)),
        name="adaln_mod",
    )(cc, ada_w, ada_b.reshape(DEPTH, 1, n_out))
    return out.reshape(8, N_ADA, D_MODEL)


def _cast_kernel(x_ref, o_ref):
    o_ref[...] = x_ref[...].astype(BF16)


def _cast_bf16(w, layer):
    _, rows, cols = w.shape
    tr = 256
    return pl.pallas_call(
        _cast_kernel,
        grid=(rows // tr,),
        in_specs=[pl.BlockSpec((None, tr, cols), lambda i: (layer, i, 0))],
        out_specs=pl.BlockSpec((tr, cols), lambda i: (i, 0)),
        out_shape=jax.ShapeDtypeStruct((rows, cols), BF16),
        compiler_params=_cparams(("arbitrary",)),
        name="cast_bf16",
    )(w)


def _ffn_kernel(x_ref, mod_ref, g_ref, w1_ref, w3_ref, w2_ref, o_ref, h_scr, acc_scr, *, sh, sc, gt):
    j = pl.program_id(1)

    @pl.when(j == 0)
    def _():
        h = _mod_norm(x_ref[...], g_ref[...], mod_ref[0, sh:sh + 1, :], mod_ref[0, sc:sc + 1, :])
        h_scr[...] = h.astype(BF16)
        acc_scr[...] = jnp.zeros_like(acc_scr)

    h = h_scr[...]
    a = _dot(h, w1_ref[...].astype(BF16))
    b = _dot(h, w3_ref[...].astype(BF16))
    act = (_silu(a) * b).astype(BF16)
    acc_scr[...] += _dot(act, w2_ref[...].astype(BF16))

    @pl.when(j == pl.num_programs(1) - 1)
    def _():
        o_ref[...] = x_ref[...] + (0.5 * mod_ref[0, gt:gt + 1, :]) * acc_scr[...]


def _ffn(x, mod, g, w1, w3, w2, layer, which, grp_fn, sh, sc, gt):
    rows = x.shape[0]
    tm, tf = FFN_TM, FFN_TF
    return pl.pallas_call(
        functools.partial(_ffn_kernel, sh=sh, sc=sc, gt=gt),
        grid=(rows // tm, D_FF // tf),
        in_specs=[
            pl.BlockSpec((tm, D_MODEL), lambda i, j: (i, 0)),
            pl.BlockSpec((1, N_ADA, D_MODEL), lambda i, j: (grp_fn(i, tm), 0, 0)),
            pl.BlockSpec((1, D_MODEL), lambda i, j: (0, 0)),
            pl.BlockSpec((None, None, D_MODEL, tf), lambda i, j: (layer, which, 0, j)),
            pl.BlockSpec((None, None, D_MODEL, tf), lambda i, j: (layer, which, 0, j)),
            pl.BlockSpec((None, None, tf, D_MODEL), lambda i, j: (layer, which, j, 0)),
        ],
        out_specs=pl.BlockSpec((tm, D_MODEL), lambda i, j: (i, 0)),
        out_shape=jax.ShapeDtypeStruct((rows, D_MODEL), F32),
        scratch_shapes=[pltpu.VMEM((tm, D_MODEL), BF16), pltpu.VMEM((tm, D_MODEL), F32)],
        compiler_params=_cparams(("arbitrary", "arbitrary")),
        name="ffn_half_step",
    )(x, mod, g, w1, w3, w2)


def _head_norm(xs, bsum, g):
    sq_hi, sq_lo = _split(xs * xs)
    ms = _dot(sq_hi, bsum) + _dot(sq_lo, bsum)
    return xs * lax.rsqrt(ms + EPS) * g


def _rope(xs, cos, sin, first_half):
    swapped = jnp.where(first_half, pltpu.roll(xs, LANES - 16, 1), pltpu.roll(xs, 16, 1))
    return xs * cos + swapped * sin


def _dup_heads(x2, low_half):
    sw = pltpu.roll(x2, HEAD_DIM, 1)
    return jnp.where(low_half, x2, sw), jnp.where(low_half, sw, x2)


def _expand_kv(x2, low_half):
    d0, d1 = _dup_heads(x2, low_half)
    return jnp.concatenate([d0, x2, d1], axis=-1)


def _proj_kernel(*refs, rope, emit_kv):
    x_ref, mod_ref, g_ref, w_ref, bsum_ref, gains_ref = refs[:6]
    pos = 6
    if rope:
        cos_ref, sin_ref = refs[pos:pos + 2]
        pos += 2
    uhy_ref, qn_ref, kn_ref, vn_ref, qg_ref, kge_ref, vge_ref = refs[pos:pos + 7]
    pos += 7
    if emit_kv:
        kg_ref, vg_ref = refs[pos:pos + 2]
        pos += 2
    u_scr = refs[pos]
    scale = HEAD_DIM ** -0.5

    h = _mod_norm(x_ref[...], g_ref[...], mod_ref[0, SH2:SH2 + 1, :], mod_ref[0, SC2:SC2 + 1, :])
    u_scr[...] = _dot(h.astype(BF16), w_ref[...])

    uhy_ref[...] = u_scr[:, 0:OFF_NA]
    bsum = bsum_ref[...]
    qn = _head_norm(u_scr[:, OFF_NA:OFF_NA + NA_W], bsum, gains_ref[0:1, :])
    qn_ref[...] = (qn * scale).astype(qn_ref.dtype)
    kn = _head_norm(u_scr[:, OFF_NA + NA_W:OFF_NA + 2 * NA_W], bsum, gains_ref[1:2, :])
    kn_ref[...] = kn.astype(kn_ref.dtype)
    vn_ref[...] = u_scr[:, OFF_NA + 2 * NA_W:OFF_GQ].astype(vn_ref.dtype)

    qg = _head_norm(u_scr[:, OFF_GQ:OFF_GQ + GQ_W], bsum, gains_ref[2:3, :])
    kg = _head_norm(u_scr[:, OFF_GQ + GQ_W:OFF_GQ + GQ_W + GKV_W], bsum[0:GKV_W, 0:GKV_W], gains_ref[3:4, 0:GKV_W])
    vg = u_scr[:, OFF_GQ + GQ_W + GKV_W:IN_WIDTH]
    if emit_kv:
        kg_ref[...] = kg
        vg_ref[...] = vg

    lane = lax.broadcasted_iota(jnp.int32, (1, LANES), 1)
    if rope:
        cos = cos_ref[...]
        sin = sin_ref[...]
        first_half = (lane % 32) < 16
        for p in range(GQ_W // LANES):
            sl = slice(p * LANES, (p + 1) * LANES)
            qg_ref[:, sl] = (_rope(qg[:, sl], cos, sin, first_half) * scale).astype(qg_ref.dtype)
        kg = _rope(kg, cos, sin, first_half)
    else:
        qg_ref[...] = (qg * scale).astype(qg_ref.dtype)

    low_half = lane < HEAD_DIM
    kge_ref[...] = _expand_kv(kg, low_half).astype(kge_ref.dtype)
    vge_ref[...] = _expand_kv(vg, low_half).astype(vge_ref.dtype)


def _project(x, mod, g, w_bf, bsum, gains, grp_fn, rope_tabs, emit_kv, attn_dtype):
    rows = x.shape[0]
    tm = ROW_TM
    rope = rope_tabs is not None
    in_specs = [
        pl.BlockSpec((tm, D_MODEL), lambda i: (i, 0)),
        pl.BlockSpec((1, N_ADA, D_MODEL), lambda i: (grp_fn(i, tm), 0, 0)),
        pl.BlockSpec((1, D_MODEL), lambda i: (0, 0)),
        pl.BlockSpec((D_MODEL, IN_WIDTH), lambda i: (0, 0)),
        pl.BlockSpec((NA_W, NA_W), lambda i: (0, 0)),
        pl.BlockSpec((4, NA_W), lambda i: (0, 0)),
    ]
    args = [x, mod, g, w_bf, bsum, gains]
    if rope:
        seq_tiles = DEC_SEQ // tm
        in_specs += [pl.BlockSpec((tm, LANES), lambda i: (i % seq_tiles, 0))] * 2
        args += list(rope_tabs)
    outs = [(OFF_NA, F32)] + [(NA_W, attn_dtype)] * 3 + [(GQ_W, attn_dtype)] * 3
    if emit_kv:
        outs += [(GKV_W, F32), (GKV_W, F32)]
    out_specs = [pl.BlockSpec((tm, w), lambda i: (i, 0)) for w, _ in outs]
    out_shape = [jax.ShapeDtypeStruct((rows, w), dt) for w, dt in outs]
    return pl.pallas_call(
        functools.partial(_proj_kernel, rope=rope, emit_kv=emit_kv),
        grid=(rows // tm,),
        in_specs=in_specs,
        out_specs=out_specs,
        out_shape=out_shape,
        scratch_shapes=[pltpu.VMEM((tm, IN_WIDTH), F32)],
        compiler_params=_cparams(("arbitrary",)),
        name="mixer_in_proj",
    )(*args)


def _log2(n):
    assert n & (n - 1) == 0
    return n.bit_length() - 1


def _stack_heads(q, n_heads):
    m, w = q.shape
    rows = lax.broadcasted_iota(jnp.int32, (n_heads * m, w), 0)
    lanes = lax.broadcasted_iota(jnp.int32, (n_heads * m, w), 1)
    own = jnp.right_shift(rows, _log2(m)) == jnp.right_shift(lanes, _log2(HEAD_DIM))
    return jnp.where(own, jnp.concatenate([q] * n_heads, axis=0), jnp.zeros((), q.dtype))


def _unstack_heads(o, n_heads):
    m = o.shape[0] // n_heads
    head = jnp.right_shift(lax.broadcasted_iota(jnp.int32, (m, o.shape[1]), 1), _log2(HEAD_DIM))
    out = o[0:m]
    for h in range(1, n_heads):
        out = jnp.where(head == h, o[h * m:(h + 1) * m], out)
    return out


def _sink_column(sink_ref, m, n_heads):
    block = jnp.right_shift(lax.broadcasted_iota(jnp.int32, (n_heads * m, 1), 0), _log2(m))
    col = jnp.full((n_heads * m, 1), sink_ref[0], F32)
    for h in range(1, n_heads):
        col = jnp.where(block == h, sink_ref[h], col)
    return col


def _softmax_pv(scores, values, sink):
    m = scores[0].max(axis=-1, keepdims=True)
    for s in scores[1:]:
        m = jnp.maximum(m, s.max(axis=-1, keepdims=True))
    if sink is not None:
        m = jnp.maximum(m, sink)
    den = None
    acc = None
    for s, v in zip(scores, values):
        p = jnp.exp(s - m)
        d = p.sum(axis=-1, keepdims=True)
        o = _dot(p.astype(BF16), v)
        den = d if den is None else den + d
        acc = o if acc is None else acc + o
    if sink is not None:
        den = den + jnp.exp(sink - m)
    return acc / den


def _ctx_attn_kernel(sink_ref, qn_ref, kn_ref, vn_ref, qg_ref, kge_ref, vge_ref, yna_ref, ygq_ref):
    m = qn_ref.shape[1]
    q = _stack_heads(qn_ref[0].astype(BF16), NA_HEADS)
    s = _dot_nt(q, kn_ref[0].astype(BF16))
    yna_ref[0] = _unstack_heads(_softmax_pv([s], [vn_ref[0].astype(BF16)], None), NA_HEADS)
    q = _stack_heads(qg_ref[0].astype(BF16), GQA_Q_HEADS)
    s = _dot_nt(q, kge_ref[0].astype(BF16))
    sink = _sink_column(sink_ref, m, GQA_Q_HEADS)
    ygq_ref[0] = _unstack_heads(_softmax_pv([s], [vge_ref[0].astype(BF16)], sink), GQA_Q_HEADS)


def _ctx_attention(sink, qn, kn, vn, qg, kge, vge):
    b, l, _ = qn.shape

    def spec(w):
        return pl.BlockSpec((1, l, w), lambda i: (i, 0, 0))

    return pl.pallas_call(
        _ctx_attn_kernel,
        grid=(b,),
        in_specs=[pl.BlockSpec(memory_space=pltpu.SMEM), spec(NA_W), spec(NA_W), spec(NA_W), spec(GQ_W),
                  spec(GQ_W), spec(GQ_W)],
        out_specs=[spec(NA_W), spec(GQ_W)],
        out_shape=[jax.ShapeDtypeStruct((b, l, NA_W), F32), jax.ShapeDtypeStruct((b, l, GQ_W), F32)],
        compiler_params=_cparams(("arbitrary",)),
        name="context_attention",
    )(sink, qn, kn, vn, qg, kge, vge)


NA_KEYS = NA_ROWS * GRID_W


def _na_bias_kernel(r_ref, oh_ref, mk_ref, o_ref):
    o_ref[...] = jnp.dot(r_ref[...], oh_ref[...], precision=_HIGHEST, preferred_element_type=F32) + mk_ref[...]


def _na_bias_tables():
    q = np.arange(GRID_W)[:, None]
    kc = np.arange(GRID_W)[None, :]
    win_lo = np.clip(q - NA_COLS // 2, 0, GRID_W - NA_COLS)
    ok = (kc >= win_lo) & (kc < win_lo + NA_COLS)
    j = kc - q + NA_COLS - 1
    onehot = np.zeros((LANES, GRID_W * GRID_W), np.float32)
    qq, kk = np.nonzero(ok)
    onehot[j[qq, kk], qq * GRID_W + kk] = 1.0
    mask = np.where(ok, 0.0, NEG_INF).astype(np.float32).reshape(1, GRID_W * GRID_W)
    return onehot, mask


def _na_bias(rpb_l):
    n_dr = 2 * NA_ROWS - 1
    rows = NA_HEADS * n_dr
    rows_pad = -(-rows // 8) * 8
    r = jnp.zeros((rows_pad, LANES), F32).at[:rows, :2 * NA_COLS - 1].set(rpb_l.reshape(rows, 2 * NA_COLS - 1))
    onehot, mask = _na_bias_tables()
    out = pl.pallas_call(
        _na_bias_kernel,
        out_shape=jax.ShapeDtypeStruct((rows_pad, GRID_W * GRID_W), F32),
        compiler_params=pltpu.CompilerParams(vmem_limit_bytes=VMEM_LIMIT),
        name="na_bias_expand",
    )(r, jnp.asarray(onehot), jnp.asarray(mask))
    bm = out[:rows].reshape(NA_HEADS, n_dr, GRID_W, GRID_W)
    bm2 = jnp.concatenate([bm[:, :-1], bm[:, 1:]], axis=-1)
    return bm2.transpose(1, 0, 2, 3).reshape(n_dr - 1, NA_HEADS * GRID_W, 2 * GRID_W)


def _na_attn_kernel(q_ref, k_ref, v_ref, kc_ref, vc_ref, bm_ref, o_ref):
    r = pl.program_id(1)
    n_rows = DEC_SEQ // GRID_W
    start = jnp.clip(r - NA_ROWS // 2, 0, n_rows - NA_ROWS)
    shift = r - start
    row0 = pl.multiple_of(start * GRID_W, GRID_W)
    q = _stack_heads(q_ref[0], NA_HEADS)
    k = k_ref[0, pl.ds(row0, NA_KEYS), :]
    v = v_ref[0, pl.ds(row0, NA_KEYS), :]
    bias = jnp.concatenate([bm_ref[2 * jj - shift + NA_ROWS - 1] for jj in range(NA_ROWS // 2)], axis=-1)
    s_loc = _dot_nt(q, k) + bias
    s_ctx = _dot_nt(q, kc_ref[0, 0].astype(BF16))
    o = _softmax_pv([s_loc, s_ctx], [v, vc_ref[0, 0].astype(BF16)], None)
    o_ref[0] = _unstack_heads(o, NA_HEADS)


def _na_attention(q, k, v, kc, vc, bm2, layer):
    b, n, _ = q.shape
    n_rows = n // GRID_W
    return pl.pallas_call(
        _na_attn_kernel,
        grid=(b, n_rows),
        in_specs=[
            pl.BlockSpec((1, GRID_W, NA_W), lambda i, r: (i, r, 0)),
            pl.BlockSpec((1, n, NA_W), lambda i, r: (i, 0, 0)),
            pl.BlockSpec((1, n, NA_W), lambda i, r: (i, 0, 0)),
            pl.BlockSpec((1, 1, PAST_LEN, NA_W), lambda i, r: (i, layer, 0, 0)),
            pl.BlockSpec((1, 1, PAST_LEN, NA_W), lambda i, r: (i, layer, 0, 0)),
            pl.BlockSpec(bm2.shape, lambda i, r: (0, 0, 0)),
        ],
        out_specs=pl.BlockSpec((1, GRID_W, NA_W), lambda i, r: (i, r, 0)),
        out_shape=jax.ShapeDtypeStruct((b, n, NA_W), F32),
        compiler_params=_cparams(("arbitrary", "arbitrary")),
        name="neighbourhood_attention",
    )(q, k, v, kc, vc, bm2)


WIN_KEYS = 3 * BLK


def _win_attn_kernel(sink_ref, q_ref, kge_ref, vge_ref, kc_ref, vc_ref, o_ref):
    nb = pl.program_id(1)
    n = kge_ref.shape[1]
    start = pl.multiple_of(jnp.clip((nb - 1) * BLK, 0, n - WIN_KEYS), BLK)
    rows = GQA_Q_HEADS * BLK
    q_pos = nb * BLK + (lax.broadcasted_iota(jnp.int32, (rows, WIN_KEYS), 0) & (BLK - 1))
    k_pos = start + lax.broadcasted_iota(jnp.int32, (rows, WIN_KEYS), 1)
    ok = jnp.abs(q_pos - k_pos) <= GQA_WINDOW
    low_c = lax.broadcasted_iota(jnp.int32, (1, LANES), 1) < HEAD_DIM
    kce = _expand_kv(kc_ref[0, 0], low_c).astype(BF16)
    vce = _expand_kv(vc_ref[0, 0], low_c).astype(BF16)
    q = _stack_heads(q_ref[0], GQA_Q_HEADS)
    s_loc = jnp.where(ok, _dot_nt(q, kge_ref[0, pl.ds(start, WIN_KEYS), :]), NEG_INF)
    s_ctx = _dot_nt(q, kce)
    sink = _sink_column(sink_ref, BLK, GQA_Q_HEADS)
    o = _softmax_pv([s_loc, s_ctx], [vge_ref[0, pl.ds(start, WIN_KEYS), :], vce], sink)
    o_ref[0] = _unstack_heads(o, GQA_Q_HEADS)


def _win_attention(sink, q, kge, vge, kc, vc, layer):
    b, n, _ = q.shape
    return pl.pallas_call(
        _win_attn_kernel,
        grid=(b, n // BLK),
        in_specs=[
            pl.BlockSpec(memory_space=pltpu.SMEM),
            pl.BlockSpec((1, BLK, GQ_W), lambda i, j: (i, j, 0)),
            pl.BlockSpec((1, n, GQ_W), lambda i, j: (i, 0, 0)),
            pl.BlockSpec((1, n, GQ_W), lambda i, j: (i, 0, 0)),
            pl.BlockSpec((1, 1, PAST_LEN, GKV_W), lambda i, j: (i, layer, 0, 0)),
            pl.BlockSpec((1, 1, PAST_LEN, GKV_W), lambda i, j: (i, layer, 0, 0)),
        ],
        out_specs=pl.BlockSpec((1, BLK, GQ_W), lambda i, j: (i, j, 0)),
        out_shape=jax.ShapeDtypeStruct((b, n, GQ_W), F32),
        compiler_params=_cparams(("arbitrary", "arbitrary")),
        name="window_attention",
    )(sink, q, kge, vge, kc, vc)


@functools.lru_cache(maxsize=None)
def _hyena_consts_np(L):
    n = 2 * L
    k = np.arange(L, dtype=np.int64)
    ang = (2.0 * np.pi / n) * ((k[:, None] * k[None, :]) % n).astype(np.float64)

    cm = np.cos(ang).astype(np.float32)
    sm = (-np.sin(ang)).astype(np.float32)
    idx = np.arange(L, dtype=np.float32)
    t = idx / np.float32(L - 1)
    bands = np.linspace(1e-4, HY_BANDS - 1, HY_BANDS, dtype=np.float32)
    fang = np.float32(2.0 * math.pi / L) * idx[:, None] * bands[None, :]
    feats = np.zeros((L, LANES), np.float32)
    feats[:, 0] = t
    feats[:, 1:1 + HY_BANDS] = np.cos(fang)
    feats[:, 1 + HY_BANDS:HY_EMB] = -np.sin(fang)
    max_decay = math.log(HY_DECAY_TARGET) / HY_FAST_PCT
    min_decay = math.log(HY_DECAY_TARGET) / HY_SLOW_PCT
    deltas = np.abs(np.linspace(min_decay, max_decay, HY_CH, dtype=np.float32))
    decay = np.exp(-t[:, None] * deltas[None, :]).astype(np.float32)
    decay2 = np.concatenate([decay, decay], axis=1)
    return cm, sm, feats, decay2


def _hyena_consts(L):
    cm, sm, feats, decay2 = _hyena_consts_np(L)
    return jnp.asarray(cm), jnp.asarray(sm), jnp.asarray(feats), jnp.asarray(decay2)


def _hy_filter_kernel(feats_ref, w1_ref, b1_ref, w2_ref, b2_ref, w3_ref, freq_ref, decay_ref, wsum_ref, wdiff_ref):
    def dot_hi(a, b):
        return jnp.dot(a, b, precision=_HIGHEST, preferred_element_type=F32)

    hid = jnp.sin(freq_ref[0:1, :] * (dot_hi(feats_ref[...], w1_ref[...]) + b1_ref[...]))
    hid = jnp.sin(freq_ref[1:2, :] * (dot_hi(hid, w2_ref[...]) + b2_ref[...]))
    taps = dot_hi(hid, w3_ref[...]) * decay_ref[...]
    fwd = taps[:, 0:HY_CH]
    bwd = taps[:, HY_CH:2 * HY_CH]
    row = lax.broadcasted_iota(jnp.int32, bwd.shape, 0)
    bwd = jnp.where(row == 0, 0.0, bwd)
    wsum_ref[...] = fwd + bwd
    wdiff_ref[...] = fwd - bwd


def _hy_filter(L, feats, decay2, w1, b1, w2, b2, w3, freq):
    w1p = jnp.zeros((LANES, HY_FILT_W), F32).at[:HY_EMB].set(w1)
    return pl.pallas_call(
        _hy_filter_kernel,
        out_shape=[jax.ShapeDtypeStruct((L, HY_CH), F32)] * 2,
        compiler_params=pltpu.CompilerParams(vmem_limit_bytes=VMEM_LIMIT),
        name="hyena_filter",
    )(feats, w1p, b1.reshape(1, -1), w2, b2.reshape(1, -1), w3, freq, decay2)


def _hy_pre_kernel(u_ref, cw_ref, cb_ref, x0_ref, z_ref):
    L = u_ref.shape[1]
    row = lax.broadcasted_iota(jnp.int32, (L, 1), 0)

    def conv_chunk(ci):
        sl = slice(ci * HY_CH, (ci + 1) * HY_CH)
        u = u_ref[0, :, sl]
        prev = jnp.where(row == 0, 0.0, pltpu.roll(u, 1, 0))
        nxt = jnp.where(row == L - 1, 0.0, pltpu.roll(u, L - 1, 0))
        return prev * cw_ref[0:1, sl] + u * cw_ref[1:2, sl] + nxt * cw_ref[2:3, sl] + cb_ref[:, sl]

    x0_ref[0] = conv_chunk(0)
    z_ref[0] = conv_chunk(1) * conv_chunk(2)


def _hy_pre(u_hy, conv_w, conv_b):
    b, L, c3 = u_hy.shape
    return pl.pallas_call(
        _hy_pre_kernel,
        grid=(b,),
        in_specs=[
            pl.BlockSpec((1, L, c3), lambda i: (i, 0, 0)),
            pl.BlockSpec((3, c3), lambda i: (0, 0)),
            pl.BlockSpec((1, c3), lambda i: (0, 0)),
        ],
        out_specs=[pl.BlockSpec((1, L, HY_CH), lambda i: (i, 0, 0))] * 2,
        out_shape=[jax.ShapeDtypeStruct((b, L, HY_CH), F32)] * 2,
        compiler_params=_cparams(("arbitrary",)),
        name="hyena_short_conv",
    )(u_hy, conv_w, conv_b.reshape(1, c3))


def _hy_conv_kernel(z_ref, x0_ref, wsum_ref, wdiff_ref, skip_ref, cr_ref, sr_ref, cc_ref, sc_ref,
                    o_ref, z_scr, ws_scr, wd_scr, acc_scr, *, group):
    t = pl.program_id(1)
    L = z_ref.shape[1]
    n = 2 * L

    @pl.when(t == 0)
    def _():
        z_scr[...] = z_ref[...].astype(BF16)
        ws_scr[...] = wsum_ref[...].astype(BF16)
        wd_scr[...] = wdiff_ref[...].astype(BF16)
        acc_scr[...] = jnp.zeros_like(acc_scr)

    cr, sr = cr_ref[...].astype(BF16), sr_ref[...].astype(BF16)
    cc, sc = cc_ref[...].astype(BF16), sc_ref[...].astype(BF16)
    k_re = _dot(cr, ws_scr[...])
    k_im = _dot(sr, wd_scr[...])
    for b in range(group):
        zb = z_scr[b]
        z_re = _dot(cr, zb)
        z_im = _dot(sr, zb)
        y_re = (z_re * k_re - z_im * k_im).astype(BF16)
        y_im = (z_re * k_im + z_im * k_re).astype(BF16)
        acc_scr[b] += _dot(cc, y_re) + _dot(sc, y_im)

    @pl.when(t == pl.num_programs(1) - 1)
    def _():
        row = lax.broadcasted_iota(jnp.int32, (L, 1), 0)
        sgn = (1 - 2 * (row % 2)).astype(F32)
        w = wsum_ref[...]
        w_dc = w.sum(axis=0, keepdims=True)
        w_ny = (w * sgn).sum(axis=0, keepdims=True)
        for b in range(group):
            z = z_ref[b]
            z_dc = z.sum(axis=0, keepdims=True)
            z_ny = (z * sgn).sum(axis=0, keepdims=True)
            conv = (2.0 / n) * acc_scr[b] - (1.0 / n) * (z_dc * w_dc) + (1.0 / n) * (sgn * (z_ny * w_ny))
            o_ref[b] = x0_ref[b] * (conv + z * skip_ref[...])


def _hy_conv(z, x0, wsum, wdiff, skip, cm, sm, group):
    b, L, c = z.shape
    tkf = min(HY_TKF, L)
    row_spec = pl.BlockSpec((tkf, L), lambda g, t: (t, 0))
    col_spec = pl.BlockSpec((L, tkf), lambda g, t: (0, t))
    once = pl.Buffered(1)
    seq_spec = pl.BlockSpec((group, L, c), lambda g, t: (g, 0, 0), pipeline_mode=once)
    w_spec = pl.BlockSpec((L, c), lambda g, t: (0, 0), pipeline_mode=once)
    return pl.pallas_call(
        functools.partial(_hy_conv_kernel, group=group),
        grid=(b // group, L // tkf),
        in_specs=[seq_spec, seq_spec, w_spec, w_spec, pl.BlockSpec((1, c), lambda g, t: (0, 0)),
                  row_spec, row_spec, col_spec, col_spec],
        out_specs=pl.BlockSpec((group, L, c), lambda g, t: (g, 0, 0)),
        out_shape=jax.ShapeDtypeStruct((b, L, c), F32),
        scratch_shapes=[pltpu.VMEM((group, L, c), BF16), pltpu.VMEM((L, c), BF16), pltpu.VMEM((L, c), BF16),
                        pltpu.VMEM((group, L, c), F32)],
        compiler_params=_cparams(("arbitrary", "arbitrary")),
        name="hyena_long_conv",
    )(z, x0, wsum, wdiff, skip.reshape(1, c), cm, sm, cm, sm)


def _merge_kernel(x_ref, yhy_ref, yna_ref, ygq_ref, mod_ref, w_ref, o_ref):
    y = jnp.concatenate([yhy_ref[...], yna_ref[...], ygq_ref[...]], axis=-1).astype(BF16)
    o_ref[...] = x_ref[...] + mod_ref[0, G2:G2 + 1, :] * _dot(y, w_ref[...])


def _merge(x, y_hy, y_na, y_gq, mod, w_bf, grp_fn):
    rows = x.shape[0]
    tm = ROW_TM
    return pl.pallas_call(
        _merge_kernel,
        grid=(rows // tm,),
        in_specs=[
            pl.BlockSpec((tm, D_MODEL), lambda i: (i, 0)),
            pl.BlockSpec((tm, HY_CH), lambda i: (i, 0)),
            pl.BlockSpec((tm, NA_W), lambda i: (i, 0)),
            pl.BlockSpec((tm, GQ_W), lambda i: (i, 0)),
            pl.BlockSpec((1, N_ADA, D_MODEL), lambda i: (grp_fn(i, tm), 0, 0)),
            pl.BlockSpec((D_MODEL, D_MODEL), lambda i: (0, 0)),
        ],
        out_specs=pl.BlockSpec((tm, D_MODEL), lambda i: (i, 0)),
        out_shape=jax.ShapeDtypeStruct((rows, D_MODEL), F32),
        compiler_params=_cparams(("arbitrary",)),
        name="mixer_out_proj",
    )(x, y_hy, y_na, y_gq, mod, w_bf)


def _rope_tables():
    pos = np.arange(DEC_SEQ)
    quarter = HEAD_DIM // 4
    inv = ROPE_BASE ** (-np.arange(quarter, dtype=np.float64) / quarter)
    ang_r = (pos // GRID_W)[:, None] * inv[None, :]
    ang_c = (pos % GRID_W)[:, None] * inv[None, :]
    cos_h = np.concatenate([np.cos(ang_r), np.cos(ang_r), np.cos(ang_c), np.cos(ang_c)], axis=1)
    sin_h = np.concatenate([-np.sin(ang_r), np.sin(ang_r), -np.sin(ang_c), np.sin(ang_c)], axis=1)
    reps = LANES // HEAD_DIM
    return (jnp.asarray(np.tile(cos_h, (1, reps)), F32), jnp.asarray(np.tile(sin_h, (1, reps)), F32))


def _head_sum_matrix():
    idx = np.arange(NA_W) // HEAD_DIM
    return jnp.asarray((idx[:, None] == idx[None, :]).astype(np.float32) / HEAD_DIM, BF16)


def _grp_prompt(i, tm):
    return 0


def _grp_sample(i, tm):
    return 1 + (i * tm) // DEC_SEQ


def kernel(x_prompt, x_sample, cache_na_k, cache_na_v, cache_gqa_k, cache_gqa_v, c, c_ctx, ada_w, ada_b, norm_g, ffn_w1, ffn_w3, ffn_w2, w_in, w_out, hy_conv_w, hy_conv_b, hy_filt_w1, hy_filt_b1, hy_filt_w2, hy_filt_b2, hy_filt_w3, hy_freq, hy_skip, na_q_g, na_k_g, na_rpb, gqa_q_g, gqa_k_g, gqa_sink):
    xp = x_prompt.reshape(BATCH * SEQ, D_MODEL)
    xs = x_sample.reshape(DEC_BATCH * DEC_SEQ, D_MODEL)
    cc = jnp.zeros((8, D_MODEL), F32).at[0].set(c_ctx).at[1:1 + DEC_BATCH].set(c)
    kc_na = cache_na_k.reshape(DEC_BATCH, DEPTH, PAST_LEN, NA_W)
    vc_na = cache_na_v.reshape(DEC_BATCH, DEPTH, PAST_LEN, NA_W)
    kc_gq = cache_gqa_k.reshape(DEC_BATCH, DEPTH, PAST_LEN, GKV_W)
    vc_gq = cache_gqa_v.reshape(DEC_BATCH, DEPTH, PAST_LEN, GKV_W)

    rope_tabs = _rope_tables()
    bsum = _head_sum_matrix()
    hy_p = _hyena_consts(SEQ)
    hy_s = _hyena_consts(DEC_SEQ)

    nk_list, nv_list, gk_list, gv_list = [], [], [], []
    for l in range(DEPTH):
        mod = _adaln_mod(cc, ada_w, ada_b, l)
        w_in_bf = _cast_bf16(w_in, l)
        w_out_bf = _cast_bf16(w_out, l)
        gains = jnp.stack([jnp.tile(na_q_g[l], NA_HEADS), jnp.tile(na_k_g[l], NA_HEADS),
                           jnp.tile(gqa_q_g[l], GQA_Q_HEADS), jnp.tile(gqa_k_g[l], GQA_Q_HEADS)])
        bm2 = _na_bias(na_rpb[l])
        ng = [norm_g[l, i].reshape(1, D_MODEL) for i in range(3)]
        filt = {}
        for L, consts in ((SEQ, hy_p), (DEC_SEQ, hy_s)):
            filt[L] = _hy_filter(L, consts[2], consts[3], hy_filt_w1[l], hy_filt_b1[l], hy_filt_w2[l],
                                 hy_filt_b2[l], hy_filt_w3[l], hy_freq[l])

        xp = _ffn(xp, mod, ng[0], ffn_w1, ffn_w3, ffn_w2, l, 0, _grp_prompt, SH1, SC1, G1)
        u_hy, qn, kn, vn, qg, kgd, vgd, kg, vg = _project(xp, mod, ng[1], w_in_bf, bsum, gains, _grp_prompt, None, True,
                                                          F32)
        x0, z = _hy_pre(u_hy.reshape(BATCH, SEQ, 3 * HY_CH), hy_conv_w[l], hy_conv_b[l])
        y_hy = _hy_conv(z, x0, filt[SEQ][0], filt[SEQ][1], hy_skip[l], hy_p[0], hy_p[1], group=4)
        shp = lambda a: a.reshape(BATCH, SEQ, a.shape[-1])
        y_na, y_gq = _ctx_attention(gqa_sink[l], shp(qn), shp(kn), shp(vn), shp(qg), shp(kgd), shp(vgd))
        xp = _merge(xp, y_hy.reshape(-1, HY_CH), y_na.reshape(-1, NA_W), y_gq.reshape(-1, GQ_W), mod, w_out_bf,
                    _grp_prompt)
        xp = _ffn(xp, mod, ng[2], ffn_w1, ffn_w3, ffn_w2, l, 1, _grp_prompt, SH3, SC3, G3)
        nk_list.append(kn.reshape(BATCH, SEQ, NA_HEADS, HEAD_DIM))
        nv_list.append(vn.reshape(BATCH, SEQ, NA_HEADS, HEAD_DIM))
        gk_list.append(kg.reshape(BATCH, SEQ, GQA_KV_HEADS, HEAD_DIM))
        gv_list.append(vg.reshape(BATCH, SEQ, GQA_KV_HEADS, HEAD_DIM))

        xs = _ffn(xs, mod, ng[0], ffn_w1, ffn_w3, ffn_w2, l, 0, _grp_sample, SH1, SC1, G1)
        u_hy, qn, kn, vn, qg, kgd, vgd = _project(xs, mod, ng[1], w_in_bf, bsum, gains, _grp_sample, rope_tabs, False,
                                                  BF16)
        x0, z = _hy_pre(u_hy.reshape(DEC_BATCH, DEC_SEQ, 3 * HY_CH), hy_conv_w[l], hy_conv_b[l])
        y_hy = _hy_conv(z, x0, filt[DEC_SEQ][0], filt[DEC_SEQ][1], hy_skip[l], hy_s[0], hy_s[1], group=DEC_BATCH)
        shs = lambda a: a.reshape(DEC_BATCH, DEC_SEQ, a.shape[-1])
        y_na = _na_attention(shs(qn), shs(kn), shs(vn), kc_na, vc_na, bm2, l)
        y_gq = _win_attention(gqa_sink[l], shs(qg), shs(kgd), shs(vgd), kc_gq, vc_gq, l)
        xs = _merge(xs, y_hy.reshape(-1, HY_CH), y_na.reshape(-1, NA_W), y_gq.reshape(-1, GQ_W), mod, w_out_bf,
                    _grp_sample)
        xs = _ffn(xs, mod, ng[2], ffn_w1, ffn_w3, ffn_w2, l, 1, _grp_sample, SH3, SC3, G3)

    return (xp.reshape(BATCH, SEQ, D_MODEL), xs.reshape(DEC_BATCH, DEC_SEQ, D_MODEL),
            jnp.stack(nk_list, axis=1), jnp.stack(nv_list, axis=1),
            jnp.stack(gk_list, axis=1), jnp.stack(gv_list, axis=1))
```

```python
import functools
import math

import numpy as np
import jax
import jax.numpy as jnp
from jax import lax
from jax.experimental import pallas as pl
from jax.experimental.pallas import tpu as pltpu

F32 = jnp.float32
BF16 = jnp.bfloat16

D_MODEL = 1024
BATCH = 16
SEQ = 256
DEPTH = 2
DEC_BATCH = 2
DEC_SEQ = 2048
PAST_LEN = 256
GRID_W = 64
HEAD_DIM = 64
HY_CH = 256
NA_HEADS = 6
GQA_Q_HEADS = 6
GQA_KV_HEADS = 2
NA_W = NA_HEADS * HEAD_DIM
GQ_W = GQA_Q_HEADS * HEAD_DIM
GKV_W = GQA_KV_HEADS * HEAD_DIM
IN_WIDTH = 3 * HY_CH + 3 * NA_W + GQ_W + 2 * GKV_W
D_FF = 2816
N_ADA = 9
HY_BANDS = 16
HY_EMB = 1 + 2 * HY_BANDS
HY_FILT_W = 64
HY_DECAY_TARGET = 1e-2
HY_FAST_PCT = 0.3
HY_SLOW_PCT = 1.5
NA_ROWS = 8
NA_COLS = 16
GQA_WINDOW = 128
BLK = 128
ROPE_BASE = 10000.0
EPS = 1e-6
NEG_INF = -1e30

LANES = 128
V7X_VMEM_BYTES = 64 * 1024 * 1024
VMEM_LIMIT = V7X_VMEM_BYTES * 7 // 8

OFF_NA = 3 * HY_CH
OFF_GQ = OFF_NA + 3 * NA_W

SH1, SC1, G1, SH2, SC2, G2, SH3, SC3, G3 = range(N_ADA)

FFN_TM = 1024
FFN_TF = 256
ROW_TM = 512
HY_TKF = 256

_HIGHEST = lax.Precision.HIGHEST


def _cparams(sem):
    return pltpu.CompilerParams(dimension_semantics=sem, vmem_limit_bytes=VMEM_LIMIT)


def _silu(x):
    return x * (1.0 / (1.0 + jnp.exp(-x)))


def _dot(a, b):
    return jnp.dot(a, b, preferred_element_type=F32)


def _dot_nt(a, b):
    return lax.dot_general(a, b, (((1,), (1,)), ((), ())), preferred_element_type=F32)


def _split(x):
    hi = x.astype(BF16)
    lo = (x - hi.astype(F32)).astype(BF16)
    return hi, lo


def _mod_norm(x, g, shift, scale):
    ms = jnp.mean(x * x, axis=-1, keepdims=True)
    return (x * lax.rsqrt(ms + EPS) * g) * (1.0 + scale) + shift


def _mod_kernel(c_ref, w_ref, b_ref, o_ref):
    a = _silu(c_ref[...]).astype(BF16)
    o_ref[...] = _dot(a, w_ref[...].astype(BF16)) + b_ref[...]


def _adaln_mod(cc, ada_w, ada_b, layer):
    tn = D_MODEL
    n_out = N_ADA * D_MODEL
    out = pl.pallas_call(
        _mod_kernel,
        grid=(n_out // tn,),
        in_specs=[
            pl.BlockSpec((8, D_MODEL), lambda j: (0, 0)),
            pl.BlockSpec((None, D_MODEL, tn), lambda j: (layer, 0, j)),
            pl.BlockSpec((None, 1, tn), lambda j: (layer, 0, j)),
        ],
        out_specs=pl.BlockSpec((8, tn), lambda j: (0, j)),
        out_shape=jax.ShapeDtypeStruct((8, n_out), F32),
        compiler_params=_cparams(("arbitrary",)),
        name="adaln_mod",
    )(cc, ada_w, ada_b.reshape(DEPTH, 1, n_out))
    return out.reshape(8, N_ADA, D_MODEL)


def _cast_kernel(x_ref, o_ref):
    o_ref[...] = x_ref[...].astype(BF16)


def _cast_bf16(w, layer):
    _, rows, cols = w.shape
    tr = 256
    return pl.pallas_call(
        _cast_kernel,
        grid=(rows // tr,),
        in_specs=[pl.BlockSpec((None, tr, cols), lambda i: (layer, i, 0))],
        out_specs=pl.BlockSpec((tr, cols), lambda i: (i, 0)),
        out_shape=jax.ShapeDtypeStruct((rows, cols), BF16),
        compiler_params=_cparams(("arbitrary",)),
        name="cast_bf16",
    )(w)


def _ffn_kernel(x_ref, mod_ref, g_ref, w1_ref, w3_ref, w2_ref, o_ref, h_scr, acc_scr, *, sh, sc, gt):
    j = pl.program_id(1)

    @pl.when(j == 0)
    def _():
        h = _mod_norm(x_ref[...], g_ref[...], mod_ref[0, sh:sh + 1, :], mod_ref[0, sc:sc + 1, :])
        h_scr[...] = h.astype(BF16)
        acc_scr[...] = jnp.zeros_like(acc_scr)

    h = h_scr[...]
    a = _dot(h, w1_ref[...].astype(BF16))
    b = _dot(h, w3_ref[...].astype(BF16))
    act = (_silu(a) * b).astype(BF16)
    acc_scr[...] += _dot(act, w2_ref[...].astype(BF16))

    @pl.when(j == pl.num_programs(1) - 1)
    def _():
        o_ref[...] = x_ref[...] + (0.5 * mod_ref[0, gt:gt + 1, :]) * acc_scr[...]


def _ffn(x, mod, g, w1, w3, w2, layer, which, grp_fn, sh, sc, gt):
    rows = x.shape[0]
    tm, tf = FFN_TM, FFN_TF
    return pl.pallas_call(
        functools.partial(_ffn_kernel, sh=sh, sc=sc, gt=gt),
        grid=(rows // tm, D_FF // tf),
        in_specs=[
            pl.BlockSpec((tm, D_MODEL), lambda i, j: (i, 0)),
            pl.BlockSpec((1, N_ADA, D_MODEL), lambda i, j: (grp_fn(i, tm), 0, 0)),
            pl.BlockSpec((1, D_MODEL), lambda i, j: (0, 0)),
            pl.BlockSpec((None, None, D_MODEL, tf), lambda i, j: (layer, which, 0, j)),
            pl.BlockSpec((None, None, D_MODEL, tf), lambda i, j: (layer, which, 0, j)),
            pl.BlockSpec((None, None, tf, D_MODEL), lambda i, j: (layer, which, j, 0)),
        ],
        out_specs=pl.BlockSpec((tm, D_MODEL), lambda i, j: (i, 0)),
        out_shape=jax.ShapeDtypeStruct((rows, D_MODEL), F32),
        scratch_shapes=[pltpu.VMEM((tm, D_MODEL), BF16), pltpu.VMEM((tm, D_MODEL), F32)],
        compiler_params=_cparams(("arbitrary", "arbitrary")),
        name="ffn_half_step",
    )(x, mod, g, w1, w3, w2)


def _head_norm(xs, bsum, g):
    sq_hi, sq_lo = _split(xs * xs)
    ms = _dot(sq_hi, bsum) + _dot(sq_lo, bsum)
    return xs * lax.rsqrt(ms + EPS) * g


def _rope(xs, cos, sin, first_half):
    swapped = jnp.where(first_half, pltpu.roll(xs, LANES - 16, 1), pltpu.roll(xs, 16, 1))
    return xs * cos + swapped * sin


def _dup_heads(x2, low_half):
    sw = pltpu.roll(x2, HEAD_DIM, 1)
    return jnp.where(low_half, x2, sw), jnp.where(low_half, sw, x2)


def _expand_kv(x2, low_half):
    d0, d1 = _dup_heads(x2, low_half)
    return jnp.concatenate([d0, x2, d1], axis=-1)


def _store_rows(ref, val, cache_layer):
    if len(ref.shape) == 2:
        ref[...] = val.astype(ref.dtype)
        return
    seq = ref.shape[-2]
    for s in range(ref.shape[0]):
        rows = val[s * seq:(s + 1) * seq].astype(ref.dtype)
        if len(ref.shape) == 3:
            ref[s] = rows
        else:
            for l in range(ref.shape[1]):
                ref[s, l] = rows if l == cache_layer else jnp.zeros_like(rows)


def _proj_kernel(*refs, rope, cache_layer, n_alias):
    x_ref, mod_ref, g_ref, w_ref, bsum_ref, gains_ref = refs[:6]
    pos = 6
    if rope:
        cos_ref, sin_ref = refs[pos:pos + 2]
        pos += 2
    pos += n_alias
    uhy_ref, qn_ref, kn_ref, vn_ref, qg_ref, kge_ref, vge_ref = refs[pos:pos + 7]
    pos += 7
    emit_kv = cache_layer is not None
    if emit_kv:
        kg_ref, vg_ref = refs[pos:pos + 2]
        pos += 2
    u_scr = refs[pos]
    scale = HEAD_DIM ** -0.5

    h = _mod_norm(x_ref[...], g_ref[...], mod_ref[0, SH2:SH2 + 1, :], mod_ref[0, SC2:SC2 + 1, :])
    u_scr[...] = _dot(h.astype(BF16), w_ref[...])

    uhy_ref[...] = u_scr[:, 0:OFF_NA]
    bsum = bsum_ref[...]
    qn = _head_norm(u_scr[:, OFF_NA:OFF_NA + NA_W], bsum, gains_ref[0:1, :])
    qn_ref[...] = (qn * scale).astype(qn_ref.dtype)
    kn = _head_norm(u_scr[:, OFF_NA + NA_W:OFF_NA + 2 * NA_W], bsum, gains_ref[1:2, :])
    _store_rows(kn_ref, kn, cache_layer)
    _store_rows(vn_ref, u_scr[:, OFF_NA + 2 * NA_W:OFF_GQ], cache_layer)

    qg = _head_norm(u_scr[:, OFF_GQ:OFF_GQ + GQ_W], bsum, gains_ref[2:3, :])
    kg = _head_norm(u_scr[:, OFF_GQ + GQ_W:OFF_GQ + GQ_W + GKV_W], bsum[0:GKV_W, 0:GKV_W], gains_ref[3:4, 0:GKV_W])
    vg = u_scr[:, OFF_GQ + GQ_W + GKV_W:IN_WIDTH]
    if emit_kv:
        _store_rows(kg_ref, kg, cache_layer)
        _store_rows(vg_ref, vg, cache_layer)

    lane = lax.broadcasted_iota(jnp.int32, (1, LANES), 1)
    if rope:
        cos = cos_ref[...]
        sin = sin_ref[...]
        first_half = (lane % 32) < 16
        for p in range(GQ_W // LANES):
            sl = slice(p * LANES, (p + 1) * LANES)
            qg_ref[:, sl] = (_rope(qg[:, sl], cos, sin, first_half) * scale).astype(qg_ref.dtype)
        kg = _rope(kg, cos, sin, first_half)
    else:
        qg_ref[...] = (qg * scale).astype(qg_ref.dtype)

    low_half = lane < HEAD_DIM
    kge_ref[...] = _expand_kv(kg, low_half).astype(kge_ref.dtype)
    vge_ref[...] = _expand_kv(vg, low_half).astype(vge_ref.dtype)


def _project(x, mod, g, w_bf, bsum, gains, grp_fn, rope_tabs, attn_dtype, cache_layer=None, prev_cache=None):
    rows = x.shape[0]
    tm = ROW_TM
    rope = rope_tabs is not None
    in_specs = [
        pl.BlockSpec((tm, D_MODEL), lambda i: (i, 0)),
        pl.BlockSpec((1, N_ADA, D_MODEL), lambda i: (grp_fn(i, tm), 0, 0)),
        pl.BlockSpec((1, D_MODEL), lambda i: (0, 0)),
        pl.BlockSpec((D_MODEL, IN_WIDTH), lambda i: (0, 0)),
        pl.BlockSpec((NA_W, NA_W), lambda i: (0, 0)),
        pl.BlockSpec((4, NA_W), lambda i: (0, 0)),
    ]
    args = [x, mod, g, w_bf, bsum, gains]
    if rope:
        seq_tiles = DEC_SEQ // tm
        in_specs += [pl.BlockSpec((tm, LANES), lambda i: (i % seq_tiles, 0))] * 2
        args += list(rope_tabs)
    outs = [(OFF_NA, F32)] + [(NA_W, attn_dtype)] * 3 + [(GQ_W, attn_dtype)] * 3
    if cache_layer is not None:
        outs += [(GKV_W, F32), (GKV_W, F32)]
    out_specs = [pl.BlockSpec((tm, w), lambda i: (i, 0)) for w, _ in outs]
    out_shape = [jax.ShapeDtypeStruct((rows, w), dt) for w, dt in outs]
    aliases = {}
    if cache_layer is not None:
        cache_outs = (2, 3, 7, 8)
        seqs = tm // SEQ
        for o in cache_outs:
            w = outs[o][0]
            out_shape[o] = jax.ShapeDtypeStruct((BATCH, DEPTH, SEQ, w), F32)
            if prev_cache is None:
                out_specs[o] = pl.BlockSpec((seqs, DEPTH, SEQ, w), lambda i: (i, 0, 0, 0))
            else:
                out_specs[o] = pl.BlockSpec((seqs, None, SEQ, w), lambda i: (i, cache_layer, 0, 0))
        if prev_cache is not None:
            aliases = {len(args) + n: o for n, o in enumerate(cache_outs)}
            in_specs += [pl.BlockSpec(memory_space=pl.ANY)] * len(cache_outs)
            args += list(prev_cache)
    return pl.pallas_call(
        functools.partial(_proj_kernel, rope=rope, cache_layer=cache_layer, n_alias=len(aliases)),
        grid=(rows // tm,),
        in_specs=in_specs,
        out_specs=out_specs,
        out_shape=out_shape,
        input_output_aliases=aliases,
        scratch_shapes=[pltpu.VMEM((tm, IN_WIDTH), F32)],
        compiler_params=_cparams(("arbitrary",)),
        name="mixer_in_proj",
    )(*args)


def _log2(n):
    assert n & (n - 1) == 0
    return n.bit_length() - 1


def _stack_heads(q, n_heads):
    m, w = q.shape
    rows = lax.broadcasted_iota(jnp.int32, (n_heads * m, w), 0)
    lanes = lax.broadcasted_iota(jnp.int32, (n_heads * m, w), 1)
    own = jnp.right_shift(rows, _log2(m)) == jnp.right_shift(lanes, _log2(HEAD_DIM))
    return jnp.where(own, jnp.concatenate([q] * n_heads, axis=0), jnp.zeros((), q.dtype))


def _unstack_heads(o, n_heads):
    m = o.shape[0] // n_heads
    head = jnp.right_shift(lax.broadcasted_iota(jnp.int32, (m, o.shape[1]), 1), _log2(HEAD_DIM))
    out = o[0:m]
    for h in range(1, n_heads):
        out = jnp.where(head == h, o[h * m:(h + 1) * m], out)
    return out


def _sink_column(sink_ref, m, n_heads):
    block = jnp.right_shift(lax.broadcasted_iota(jnp.int32, (n_heads * m, 1), 0), _log2(m))
    col = jnp.full((n_heads * m, 1), sink_ref[0], F32)
    for h in range(1, n_heads):
        col = jnp.where(block == h, sink_ref[h], col)
    return col


def _softmax_pv(scores, values, sink):
    m = scores[0].max(axis=-1, keepdims=True)
    for s in scores[1:]:
        m = jnp.maximum(m, s.max(axis=-1, keepdims=True))
    if sink is not None:
        m = jnp.maximum(m, sink)
    den = None
    acc = None
    for s, v in zip(scores, values):
        p = jnp.exp(s - m)
        d = p.sum(axis=-1, keepdims=True)
        o = _dot(p.astype(BF16), v)
        den = d if den is None else den + d
        acc = o if acc is None else acc + o
    if sink is not None:
        den = den + jnp.exp(sink - m)
    return acc / den


def _ctx_attn_kernel(sink_ref, qn_ref, kn_ref, vn_ref, qg_ref, kge_ref, vge_ref, yna_ref, ygq_ref):
    m = qn_ref.shape[1]
    low = lax.broadcasted_iota(jnp.int32, (m, LANES), 1) < HEAD_DIM
    for q_ref, k_ref, v_ref, y_ref, has_sink in ((qn_ref, kn_ref, vn_ref, yna_ref, False),
                                                  (qg_ref, kge_ref, vge_ref, ygq_ref, True)):
        for p in range(NA_W // LANES):
            sl = slice(p * LANES, (p + 1) * LANES)
            q2 = q_ref[0, :, sl].astype(BF16)
            k2 = k_ref[0, :, sl].astype(BF16)
            v2 = v_ref[0, :, sl].astype(BF16)
            halves = []
            for half, keep in enumerate((low, ~low)):
                qh = jnp.where(keep, q2, jnp.zeros((), BF16))
                sink = sink_ref[2 * p + half] if has_sink else None
                halves.append(_softmax_pv([_dot_nt(qh, k2)], [v2], sink))
            y_ref[0, :, sl] = jnp.where(low, halves[0], halves[1])


def _ctx_attention(sink, qn, kn_cache, vn_cache, qg, kge, vge, layer):
    b, l, _ = qn.shape

    def spec(w):
        return pl.BlockSpec((1, l, w), lambda i: (i, 0, 0))

    cache_spec = pl.BlockSpec((1, None, l, NA_W), lambda i: (i, layer, 0, 0))
    return pl.pallas_call(
        _ctx_attn_kernel,
        grid=(b,),
        in_specs=[pl.BlockSpec(memory_space=pltpu.SMEM), spec(NA_W), cache_spec, cache_spec, spec(GQ_W),
                  spec(GQ_W), spec(GQ_W)],
        out_specs=[spec(NA_W), spec(GQ_W)],
        out_shape=[jax.ShapeDtypeStruct((b, l, NA_W), F32), jax.ShapeDtypeStruct((b, l, GQ_W), F32)],
        compiler_params=_cparams(("arbitrary",)),
        name="context_attention",
    )(sink, qn, kn_cache, vn_cache, qg, kge, vge)


NA_KEYS = NA_ROWS * GRID_W
NA_ROWS_PER_STEP = 8


def _na_bias_kernel(r_ref, oh_ref, mk_ref, o_ref):
    o_ref[...] = jnp.dot(r_ref[...], oh_ref[...], precision=_HIGHEST, preferred_element_type=F32) + mk_ref[...]


def _na_bias_tables():
    q = np.arange(GRID_W)[:, None]
    kc = np.arange(GRID_W)[None, :]
    win_lo = np.clip(q - NA_COLS // 2, 0, GRID_W - NA_COLS)
    ok = (kc >= win_lo) & (kc < win_lo + NA_COLS)
    j = kc - q + NA_COLS - 1
    onehot = np.zeros((LANES, GRID_W * GRID_W), np.float32)
    qq, kk = np.nonzero(ok)
    onehot[j[qq, kk], qq * GRID_W + kk] = 1.0
    mask = np.where(ok, 0.0, NEG_INF).astype(np.float32).reshape(1, GRID_W * GRID_W)
    return onehot, mask


def _na_bias(rpb_l):
    n_dr = 2 * NA_ROWS - 1
    rows = NA_HEADS * n_dr
    rows_pad = -(-rows // 8) * 8
    r = jnp.zeros((rows_pad, LANES), F32).at[:rows, :2 * NA_COLS - 1].set(rpb_l.reshape(rows, 2 * NA_COLS - 1))
    onehot, mask = _na_bias_tables()
    out = pl.pallas_call(
        _na_bias_kernel,
        out_shape=jax.ShapeDtypeStruct((rows_pad, GRID_W * GRID_W), F32),
        compiler_params=pltpu.CompilerParams(vmem_limit_bytes=VMEM_LIMIT),
        name="na_bias_expand",
    )(r, jnp.asarray(onehot), jnp.asarray(mask))
    bm = out[:rows].reshape(NA_HEADS, n_dr, GRID_W, GRID_W)
    bm2 = jnp.concatenate([bm[:, :-1], bm[:, 1:]], axis=-1)
    return bm2.transpose(1, 0, 2, 3).reshape(n_dr - 1, NA_HEADS * GRID_W, 2 * GRID_W)


def _na_attn_kernel(q_ref, k_ref, v_ref, kc_ref, vc_ref, bm_ref, o_ref):
    n_rows = DEC_SEQ // GRID_W
    kc = kc_ref[0, 0].astype(BF16)
    vc = vc_ref[0, 0].astype(BF16)
    for rr in range(NA_ROWS_PER_STEP):
        r = pl.program_id(1) * NA_ROWS_PER_STEP + rr
        start = jnp.clip(r - NA_ROWS // 2, 0, n_rows - NA_ROWS)
        shift = r - start
        row0 = pl.multiple_of(start * GRID_W, GRID_W)
        q = _stack_heads(q_ref[0, rr * GRID_W:(rr + 1) * GRID_W, :], NA_HEADS)
        k = k_ref[0, pl.ds(row0, NA_KEYS), :]
        v = v_ref[0, pl.ds(row0, NA_KEYS), :]
        bias = jnp.concatenate([bm_ref[2 * jj - shift + NA_ROWS - 1] for jj in range(NA_ROWS // 2)], axis=-1)
        s_loc = _dot_nt(q, k) + bias
        s_ctx = _dot_nt(q, kc)
        o = _softmax_pv([s_loc, s_ctx], [v, vc], None)
        o_ref[0, rr * GRID_W:(rr + 1) * GRID_W, :] = _unstack_heads(o, NA_HEADS)


def _na_attention(q, k, v, kc, vc, bm2, layer):
    b, n, _ = q.shape
    tq = NA_ROWS_PER_STEP * GRID_W
    return pl.pallas_call(
        _na_attn_kernel,
        grid=(b, n // tq),
        in_specs=[
            pl.BlockSpec((1, tq, NA_W), lambda i, r: (i, r, 0)),
            pl.BlockSpec((1, n, NA_W), lambda i, r: (i, 0, 0)),
            pl.BlockSpec((1, n, NA_W), lambda i, r: (i, 0, 0)),
            pl.BlockSpec((1, 1, PAST_LEN, NA_W), lambda i, r: (i, layer, 0, 0)),
            pl.BlockSpec((1, 1, PAST_LEN, NA_W), lambda i, r: (i, layer, 0, 0)),
            pl.BlockSpec(bm2.shape, lambda i, r: (0, 0, 0)),
        ],
        out_specs=pl.BlockSpec((1, tq, NA_W), lambda i, r: (i, r, 0)),
        out_shape=jax.ShapeDtypeStruct((b, n, NA_W), F32),
        compiler_params=_cparams(("arbitrary", "arbitrary")),
        name="neighbourhood_attention",
    )(q, k, v, kc, vc, bm2)


WIN_KEYS = 3 * BLK
WIN_BLOCKS_PER_STEP = 4


def _win_attn_kernel(sink_ref, mask_ref, q_ref, kge_ref, vge_ref, kc_ref, vc_ref, o_ref):
    n = kge_ref.shape[1]
    n_blocks = n // BLK
    low_c = lax.broadcasted_iota(jnp.int32, (1, LANES), 1) < HEAD_DIM
    kce = _expand_kv(kc_ref[0, 0], low_c).astype(BF16)
    vce = _expand_kv(vc_ref[0, 0], low_c).astype(BF16)
    sink = _sink_column(sink_ref, BLK, GQA_Q_HEADS)
    for bb in range(WIN_BLOCKS_PER_STEP):
        nb = pl.program_id(1) * WIN_BLOCKS_PER_STEP + bb
        start = pl.multiple_of(jnp.clip((nb - 1) * BLK, 0, n - WIN_KEYS), BLK)
        variant = jnp.where(nb == 0, 0, jnp.where(nb == n_blocks - 1, 2, 1))
        q = _stack_heads(q_ref[0, bb * BLK:(bb + 1) * BLK, :], GQA_Q_HEADS)
        s_loc = _dot_nt(q, kge_ref[0, pl.ds(start, WIN_KEYS), :]) + mask_ref[variant]
        s_ctx = _dot_nt(q, kce)
        o = _softmax_pv([s_loc, s_ctx], [vge_ref[0, pl.ds(start, WIN_KEYS), :], vce], sink)
        o_ref[0, bb * BLK:(bb + 1) * BLK, :] = _unstack_heads(o, GQA_Q_HEADS)


def _win_mask_table():
    qi = np.arange(BLK)[:, None]
    kj = np.arange(WIN_KEYS)[None, :]
    tabs = [np.where(np.abs(qi + off - kj) <= GQA_WINDOW, 0.0, NEG_INF) for off in (0, BLK, 2 * BLK)]
    return jnp.asarray(np.stack([np.tile(t, (GQA_Q_HEADS, 1)) for t in tabs]).astype(np.float32))


def _win_attention(sink, q, kge, vge, kc, vc, layer):
    b, n, _ = q.shape
    mask = _win_mask_table()
    tq = WIN_BLOCKS_PER_STEP * BLK
    return pl.pallas_call(
        _win_attn_kernel,
        grid=(b, n // tq),
        in_specs=[
            pl.BlockSpec(memory_space=pltpu.SMEM),
            pl.BlockSpec(mask.shape, lambda i, j: (0, 0, 0)),
            pl.BlockSpec((1, tq, GQ_W), lambda i, j: (i, j, 0)),
            pl.BlockSpec((1, n, GQ_W), lambda i, j: (i, 0, 0)),
            pl.BlockSpec((1, n, GQ_W), lambda i, j: (i, 0, 0)),
            pl.BlockSpec((1, 1, PAST_LEN, GKV_W), lambda i, j: (i, layer, 0, 0)),
            pl.BlockSpec((1, 1, PAST_LEN, GKV_W), lambda i, j: (i, layer, 0, 0)),
        ],
        out_specs=pl.BlockSpec((1, tq, GQ_W), lambda i, j: (i, j, 0)),
        out_shape=jax.ShapeDtypeStruct((b, n, GQ_W), F32),
        compiler_params=_cparams(("arbitrary", "arbitrary")),
        name="window_attention",
    )(sink, mask, q, kge, vge, kc, vc)


@functools.lru_cache(maxsize=None)
def _hyena_consts_np(L):
    n = 2 * L
    k = np.arange(L, dtype=np.int64)
    ang = (2.0 * np.pi / n) * ((k[:, None] * k[None, :]) % n).astype(np.float64)

    cm = np.cos(ang).astype(np.float32)
    sm = (-np.sin(ang)).astype(np.float32)
    idx = np.arange(L, dtype=np.float32)
    t = idx / np.float32(L - 1)
    bands = np.linspace(1e-4, HY_BANDS - 1, HY_BANDS, dtype=np.float32)
    fang = np.float32(2.0 * math.pi / L) * idx[:, None] * bands[None, :]
    feats = np.zeros((L, LANES), np.float32)
    feats[:, 0] = t
    feats[:, 1:1 + HY_BANDS] = np.cos(fang)
    feats[:, 1 + HY_BANDS:HY_EMB] = -np.sin(fang)
    max_decay = math.log(HY_DECAY_TARGET) / HY_FAST_PCT
    min_decay = math.log(HY_DECAY_TARGET) / HY_SLOW_PCT
    deltas = np.abs(np.linspace(min_decay, max_decay, HY_CH, dtype=np.float32))
    decay = np.exp(-t[:, None] * deltas[None, :]).astype(np.float32)
    decay2 = np.concatenate([decay, decay], axis=1)
    return cm, sm, feats, decay2


def _hyena_consts(L):
    cm, sm, feats, decay2 = _hyena_consts_np(L)
    return jnp.asarray(cm), jnp.asarray(sm), jnp.asarray(feats), jnp.asarray(decay2)


def _hy_filter_kernel(feats_ref, w1_ref, b1_ref, w2_ref, b2_ref, w3_ref, freq_ref, decay_ref, wsum_ref, wdiff_ref):
    def dot_hi(a, b):
        return jnp.dot(a, b, precision=_HIGHEST, preferred_element_type=F32)

    hid = jnp.sin(freq_ref[0:1, :] * (dot_hi(feats_ref[...], w1_ref[...]) + b1_ref[...]))
    hid = jnp.sin(freq_ref[1:2, :] * (dot_hi(hid, w2_ref[...]) + b2_ref[...]))
    taps = dot_hi(hid, w3_ref[...]) * decay_ref[...]
    fwd = taps[:, 0:HY_CH]
    bwd = taps[:, HY_CH:2 * HY_CH]
    row = lax.broadcasted_iota(jnp.int32, bwd.shape, 0)
    bwd = jnp.where(row == 0, 0.0, bwd)
    wsum_ref[...] = fwd + bwd
    wdiff_ref[...] = fwd - bwd


def _hy_filter(L, feats, decay2, w1, b1, w2, b2, w3, freq):
    w1p = jnp.zeros((LANES, HY_FILT_W), F32).at[:HY_EMB].set(w1)
    return pl.pallas_call(
        _hy_filter_kernel,
        out_shape=[jax.ShapeDtypeStruct((L, HY_CH), F32)] * 2,
        compiler_params=pltpu.CompilerParams(vmem_limit_bytes=VMEM_LIMIT),
        name="hyena_filter",
    )(feats, w1p, b1.reshape(1, -1), w2, b2.reshape(1, -1), w3, freq, decay2)


def _hy_pre_kernel(u_ref, cw_ref, cb_ref, x0_ref, z_ref):
    L = u_ref.shape[1]
    row = lax.broadcasted_iota(jnp.int32, (L, 1), 0)

    def conv_chunk(ci):
        sl = slice(ci * HY_CH, (ci + 1) * HY_CH)
        u = u_ref[0, :, sl]
        prev = jnp.where(row == 0, 0.0, pltpu.roll(u, 1, 0))
        nxt = jnp.where(row == L - 1, 0.0, pltpu.roll(u, L - 1, 0))
        return prev * cw_ref[0:1, sl] + u * cw_ref[1:2, sl] + nxt * cw_ref[2:3, sl] + cb_ref[:, sl]

    x0_ref[0] = conv_chunk(0)
    z_ref[0] = conv_chunk(1) * conv_chunk(2)


def _hy_pre(u_hy, conv_w, conv_b):
    b, L, c3 = u_hy.shape
    return pl.pallas_call(
        _hy_pre_kernel,
        grid=(b,),
        in_specs=[
            pl.BlockSpec((1, L, c3), lambda i: (i, 0, 0)),
            pl.BlockSpec((3, c3), lambda i: (0, 0)),
            pl.BlockSpec((1, c3), lambda i: (0, 0)),
        ],
        out_specs=[pl.BlockSpec((1, L, HY_CH), lambda i: (i, 0, 0))] * 2,
        out_shape=[jax.ShapeDtypeStruct((b, L, HY_CH), F32)] * 2,
        compiler_params=_cparams(("arbitrary",)),
        name="hyena_short_conv",
    )(u_hy, conv_w, conv_b.reshape(1, c3))


def _hy_conv_kernel(z_ref, x0_ref, wsum_ref, wdiff_ref, skip_ref, cr_ref, sr_ref, cc_ref, sc_ref,
                    o_ref, z_scr, ws_scr, wd_scr, acc_scr, *, group):
    t = pl.program_id(1)
    L = z_ref.shape[1]
    n = 2 * L

    @pl.when(t == 0)
    def _():
        z_scr[...] = z_ref[...].astype(BF16)
        ws_scr[...] = wsum_ref[...].astype(BF16)
        wd_scr[...] = wdiff_ref[...].astype(BF16)
        acc_scr[...] = jnp.zeros_like(acc_scr)

    cr, sr = cr_ref[...].astype(BF16), sr_ref[...].astype(BF16)
    cc, sc = cc_ref[...].astype(BF16), sc_ref[...].astype(BF16)
    k_re = _dot(cr, ws_scr[...])
    k_im = _dot(sr, wd_scr[...])
    for b in range(group):
        zb = z_scr[b]
        z_re = _dot(cr, zb)
        z_im = _dot(sr, zb)
        y_re = (z_re * k_re - z_im * k_im).astype(BF16)
        y_im = (z_re * k_im + z_im * k_re).astype(BF16)
        acc_scr[b] += _dot(cc, y_re) + _dot(sc, y_im)

    @pl.when(t == pl.num_programs(1) - 1)
    def _():
        row = lax.broadcasted_iota(jnp.int32, (L, 1), 0)
        sgn = (1 - 2 * (row % 2)).astype(F32)
        w = wsum_ref[...]
        w_dc = w.sum(axis=0, keepdims=True)
        w_ny = (w * sgn).sum(axis=0, keepdims=True)
        for b in range(group):
            z = z_ref[b]
            z_dc = z.sum(axis=0, keepdims=True)
            z_ny = (z * sgn).sum(axis=0, keepdims=True)
            conv = (2.0 / n) * acc_scr[b] - (1.0 / n) * (z_dc * w_dc) + (1.0 / n) * (sgn * (z_ny * w_ny))
            o_ref[b] = x0_ref[b] * (conv + z * skip_ref[...])


def _hy_conv(z, x0, wsum, wdiff, skip, cm, sm, group):
    b, L, c = z.shape
    tkf = min(HY_TKF, L)
    row_spec = pl.BlockSpec((tkf, L), lambda g, t: (t, 0))
    col_spec = pl.BlockSpec((L, tkf), lambda g, t: (0, t))
    once = pl.Buffered(1)
    seq_spec = pl.BlockSpec((group, L, c), lambda g, t: (g, 0, 0), pipeline_mode=once)
    w_spec = pl.BlockSpec((L, c), lambda g, t: (0, 0), pipeline_mode=once)
    return pl.pallas_call(
        functools.partial(_hy_conv_kernel, group=group),
        grid=(b // group, L // tkf),
        in_specs=[seq_spec, seq_spec, w_spec, w_spec, pl.BlockSpec((1, c), lambda g, t: (0, 0)),
                  row_spec, row_spec, col_spec, col_spec],
        out_specs=pl.BlockSpec((group, L, c), lambda g, t: (g, 0, 0)),
        out_shape=jax.ShapeDtypeStruct((b, L, c), F32),
        scratch_shapes=[pltpu.VMEM((group, L, c), BF16), pltpu.VMEM((L, c), BF16), pltpu.VMEM((L, c), BF16),
                        pltpu.VMEM((group, L, c), F32)],
        compiler_params=_cparams(("arbitrary", "arbitrary")),
        name="hyena_long_conv",
    )(z, x0, wsum, wdiff, skip.reshape(1, c), cm, sm, cm, sm)


def _merge_kernel(x_ref, yhy_ref, yna_ref, ygq_ref, mod_ref, w_ref, o_ref):
    y = jnp.concatenate([yhy_ref[...], yna_ref[...], ygq_ref[...]], axis=-1).astype(BF16)
    o_ref[...] = x_ref[...] + mod_ref[0, G2:G2 + 1, :] * _dot(y, w_ref[...])


def _merge(x, y_hy, y_na, y_gq, mod, w_bf, grp_fn):
    rows = x.shape[0]
    tm = ROW_TM
    return pl.pallas_call(
        _merge_kernel,
        grid=(rows // tm,),
        in_specs=[
            pl.BlockSpec((tm, D_MODEL), lambda i: (i, 0)),
            pl.BlockSpec((tm, HY_CH), lambda i: (i, 0)),
            pl.BlockSpec((tm, NA_W), lambda i: (i, 0)),
            pl.BlockSpec((tm, GQ_W), lambda i: (i, 0)),
            pl.BlockSpec((1, N_ADA, D_MODEL), lambda i: (grp_fn(i, tm), 0, 0)),
            pl.BlockSpec((D_MODEL, D_MODEL), lambda i: (0, 0)),
        ],
        out_specs=pl.BlockSpec((tm, D_MODEL), lambda i: (i, 0)),
        out_shape=jax.ShapeDtypeStruct((rows, D_MODEL), F32),
        compiler_params=_cparams(("arbitrary",)),
        name="mixer_out_proj",
    )(x, y_hy, y_na, y_gq, mod, w_bf)


def _rope_tables():
    pos = np.arange(DEC_SEQ)
    quarter = HEAD_DIM // 4
    inv = ROPE_BASE ** (-np.arange(quarter, dtype=np.float64) / quarter)
    ang_r = (pos // GRID_W)[:, None] * inv[None, :]
    ang_c = (pos % GRID_W)[:, None] * inv[None, :]
    cos_h = np.concatenate([np.cos(ang_r), np.cos(ang_r), np.cos(ang_c), np.cos(ang_c)], axis=1)
    sin_h = np.concatenate([-np.sin(ang_r), np.sin(ang_r), -np.sin(ang_c), np.sin(ang_c)], axis=1)
    reps = LANES // HEAD_DIM
    return (jnp.asarray(np.tile(cos_h, (1, reps)), F32), jnp.asarray(np.tile(sin_h, (1, reps)), F32))


def _head_sum_matrix():
    idx = np.arange(NA_W) // HEAD_DIM
    return jnp.asarray((idx[:, None] == idx[None, :]).astype(np.float32) / HEAD_DIM, BF16)


def _grp_prompt(i, tm):
    return 0


def _grp_sample(i, tm):
    return 1 + (i * tm) // DEC_SEQ


def kernel(x_prompt, x_sample, cache_na_k, cache_na_v, cache_gqa_k, cache_gqa_v, c, c_ctx, ada_w, ada_b, norm_g, ffn_w1, ffn_w3, ffn_w2, w_in, w_out, hy_conv_w, hy_conv_b, hy_filt_w1, hy_filt_b1, hy_filt_w2, hy_filt_b2, hy_filt_w3, hy_freq, hy_skip, na_q_g, na_k_g, na_rpb, gqa_q_g, gqa_k_g, gqa_sink):
    xp = x_prompt.reshape(BATCH * SEQ, D_MODEL)
    xs = x_sample.reshape(DEC_BATCH * DEC_SEQ, D_MODEL)
    cc = jnp.zeros((8, D_MODEL), F32).at[0].set(c_ctx).at[1:1 + DEC_BATCH].set(c)
    kc_na = cache_na_k.reshape(DEC_BATCH, DEPTH, PAST_LEN, NA_W)
    vc_na = cache_na_v.reshape(DEC_BATCH, DEPTH, PAST_LEN, NA_W)
    kc_gq = cache_gqa_k.reshape(DEC_BATCH, DEPTH, PAST_LEN, GKV_W)
    vc_gq = cache_gqa_v.reshape(DEC_BATCH, DEPTH, PAST_LEN, GKV_W)

    rope_tabs = _rope_tables()
    bsum = _head_sum_matrix()
    hy_p = _hyena_consts(SEQ)
    hy_s = _hyena_consts(DEC_SEQ)

    cache = None
    for l in range(DEPTH):
        mod = _adaln_mod(cc, ada_w, ada_b, l)
        w_in_bf = _cast_bf16(w_in, l)
        w_out_bf = _cast_bf16(w_out, l)
        gains = jnp.stack([jnp.tile(na_q_g[l], NA_HEADS), jnp.tile(na_k_g[l], NA_HEADS),
                           jnp.tile(gqa_q_g[l], GQA_Q_HEADS), jnp.tile(gqa_k_g[l], GQA_Q_HEADS)])
        bm2 = _na_bias(na_rpb[l])
        ng = [norm_g[l, i].reshape(1, D_MODEL) for i in range(3)]
        filt = {}
        for L, consts in ((SEQ, hy_p), (DEC_SEQ, hy_s)):
            filt[L] = _hy_filter(L, consts[2], consts[3], hy_filt_w1[l], hy_filt_b1[l], hy_filt_w2[l],
                                 hy_filt_b2[l], hy_filt_w3[l], hy_freq[l])

        xp = _ffn(xp, mod, ng[0], ffn_w1, ffn_w3, ffn_w2, l, 0, _grp_prompt, SH1, SC1, G1)
        u_hy, qn, kn, vn, qg, kgd, vgd, kg, vg = _project(xp, mod, ng[1], w_in_bf, bsum, gains, _grp_prompt, None, F32,
                                                          cache_layer=l, prev_cache=cache)
        cache = (kn, vn, kg, vg)
        x0, z = _hy_pre(u_hy.reshape(BATCH, SEQ, 3 * HY_CH), hy_conv_w[l], hy_conv_b[l])
        y_hy = _hy_conv(z, x0, filt[SEQ][0], filt[SEQ][1], hy_skip[l], hy_p[0], hy_p[1], group=4)
        shp = lambda a: a.reshape(BATCH, SEQ, a.shape[-1])
        y_na, y_gq = _ctx_attention(gqa_sink[l], shp(qn), kn, vn, shp(qg), shp(kgd), shp(vgd), l)
        xp = _merge(xp, y_hy.reshape(-1, HY_CH), y_na.reshape(-1, NA_W), y_gq.reshape(-1, GQ_W), mod, w_out_bf,
                    _grp_prompt)
        xp = _ffn(xp, mod, ng[2], ffn_w1, ffn_w3, ffn_w2, l, 1, _grp_prompt, SH3, SC3, G3)

        xs = _ffn(xs, mod, ng[0], ffn_w1, ffn_w3, ffn_w2, l, 0, _grp_sample, SH1, SC1, G1)
        u_hy, qn, kn, vn, qg, kgd, vgd = _project(xs, mod, ng[1], w_in_bf, bsum, gains, _grp_sample, rope_tabs, BF16)
        x0, z = _hy_pre(u_hy.reshape(DEC_BATCH, DEC_SEQ, 3 * HY_CH), hy_conv_w[l], hy_conv_b[l])
        y_hy = _hy_conv(z, x0, filt[DEC_SEQ][0], filt[DEC_SEQ][1], hy_skip[l], hy_s[0], hy_s[1], group=DEC_BATCH)
        shs = lambda a: a.reshape(DEC_BATCH, DEC_SEQ, a.shape[-1])
        y_na = _na_attention(shs(qn), shs(kn), shs(vn), kc_na, vc_na, bm2, l)
        y_gq = _win_attention(gqa_sink[l], shs(qg), shs(kgd), shs(vgd), kc_gq, vc_gq, l)
        xs = _merge(xs, y_hy.reshape(-1, HY_CH), y_na.reshape(-1, NA_W), y_gq.reshape(-1, GQ_W), mod, w_out_bf,
                    _grp_sample)
        xs = _ffn(xs, mod, ng[2], ffn_w1, ffn_w3, ffn_w2, l, 1, _grp_sample, SH3, SC3, G3)

    kn, vn, kg, vg = cache
    return (xp.reshape(BATCH, SEQ, D_MODEL), xs.reshape(DEC_BATCH, DEC_SEQ, D_MODEL),
            kn.reshape(BATCH, DEPTH, SEQ, NA_HEADS, HEAD_DIM), vn.reshape(BATCH, DEPTH, SEQ, NA_HEADS, HEAD_DIM),
            kg.reshape(BATCH, DEPTH, SEQ, GQA_KV_HEADS, HEAD_DIM), vg.reshape(BATCH, DEPTH, SEQ, GQA_KV_HEADS, HEAD_DIM))
```

```python
import functools
import math

import numpy as np
import jax
import jax.numpy as jnp
from jax import lax
from jax.experimental import pallas as pl
from jax.experimental.pallas import tpu as pltpu

F32 = jnp.float32
BF16 = jnp.bfloat16

D_MODEL = 1024
BATCH = 16
SEQ = 256
DEPTH = 2
DEC_BATCH = 2
DEC_SEQ = 2048
PAST_LEN = 256
GRID_W = 64
HEAD_DIM = 64
HY_CH = 256
NA_HEADS = 6
GQA_Q_HEADS = 6
GQA_KV_HEADS = 2
NA_W = NA_HEADS * HEAD_DIM
GQ_W = GQA_Q_HEADS * HEAD_DIM
GKV_W = GQA_KV_HEADS * HEAD_DIM
IN_WIDTH = 3 * HY_CH + 3 * NA_W + GQ_W + 2 * GKV_W
D_FF = 2816
N_ADA = 9
HY_BANDS = 16
HY_EMB = 1 + 2 * HY_BANDS
HY_FILT_W = 64
HY_DECAY_TARGET = 1e-2
HY_FAST_PCT = 0.3
HY_SLOW_PCT = 1.5
NA_ROWS = 8
NA_COLS = 16
GQA_WINDOW = 128
BLK = 128
ROPE_BASE = 10000.0
EPS = 1e-6
NEG_INF = -1e30

LANES = 128
V7X_VMEM_BYTES = 64 * 1024 * 1024
VMEM_LIMIT = V7X_VMEM_BYTES * 15 // 16

OFF_NA = 3 * HY_CH
OFF_GQ = OFF_NA + 3 * NA_W

SH1, SC1, G1, SH2, SC2, G2, SH3, SC3, G3 = range(N_ADA)

FFN_TM = 2048
FFN_TF = 256
FFN_ROW_GROUPS = 2
ROW_TM = 512
HY_TKF = 256

_HIGHEST = lax.Precision.HIGHEST


def _cparams(sem):
    return pltpu.CompilerParams(dimension_semantics=sem, vmem_limit_bytes=VMEM_LIMIT)


def _silu(x):
    return x * (1.0 / (1.0 + jnp.exp(-x)))


def _dot(a, b):
    return jnp.dot(a, b, preferred_element_type=F32)


def _dot_nt(a, b):
    return lax.dot_general(a, b, (((1,), (1,)), ((), ())), preferred_element_type=F32)


def _split(x):
    hi = x.astype(BF16)
    lo = (x - hi.astype(F32)).astype(BF16)
    return hi, lo


def _mod_norm(x, g, shift, scale):
    ms = jnp.mean(x * x, axis=-1, keepdims=True)
    return (x * lax.rsqrt(ms + EPS) * g) * (1.0 + scale) + shift


def _mod_kernel(c_ref, w_ref, b_ref, o_ref):
    a = _silu(c_ref[...]).astype(BF16)
    o_ref[...] = _dot(a, w_ref[...].astype(BF16)) + b_ref[...]


def _adaln_mod(cc, ada_w, ada_b, layer):
    tn = D_MODEL
    n_out = N_ADA * D_MODEL
    out = pl.pallas_call(
        _mod_kernel,
        grid=(n_out // tn,),
        in_specs=[
            pl.BlockSpec((8, D_MODEL), lambda j: (0, 0)),
            pl.BlockSpec((None, D_MODEL, tn), lambda j: (layer, 0, j)),
            pl.BlockSpec((None, 1, tn), lambda j: (layer, 0, j)),
        ],
        out_specs=pl.BlockSpec((8, tn), lambda j: (0, j)),
        out_shape=jax.ShapeDtypeStruct((8, n_out), F32),
        compiler_params=_cparams(("arbitrary",)),
        name="adaln_mod",
    )(cc, ada_w, ada_b.reshape(DEPTH, 1, n_out))
    return out.reshape(8, N_ADA, D_MODEL)


def _cast_kernel(x_ref, o_ref):
    o_ref[...] = x_ref[...].astype(BF16)


def _cast_bf16(w, layer):
    _, rows, cols = w.shape
    tr = 128
    return pl.pallas_call(
        _cast_kernel,
        grid=(rows // tr,),
        in_specs=[pl.BlockSpec((None, tr, cols), lambda i: (layer, i, 0))],
        out_specs=pl.BlockSpec((tr, cols), lambda i: (i, 0)),
        out_shape=jax.ShapeDtypeStruct((rows, cols), BF16),
        compiler_params=_cparams(("arbitrary",)),
        name="cast_bf16",
    )(w)


def _ffn_kernel(x_ref, mod_ref, g_ref, w1_ref, w3_ref, w2_ref, o_ref, h_scr, acc_scr, *, sh, sc, gt):
    j = pl.program_id(1)

    @pl.when(j == 0)
    def _():
        h = _mod_norm(x_ref[...], g_ref[...], mod_ref[0, sh:sh + 1, :], mod_ref[0, sc:sc + 1, :])
        h_scr[...] = h.astype(BF16)
        acc_scr[...] = jnp.zeros_like(acc_scr)

    w1 = w1_ref[...].astype(BF16)
    w3 = w3_ref[...].astype(BF16)
    w2 = w2_ref[...].astype(BF16)
    rows = h_scr.shape[0] // FFN_ROW_GROUPS
    for s in range(FFN_ROW_GROUPS):
        sl = slice(s * rows, (s + 1) * rows)
        h = h_scr[sl]
        a = _dot(h, w1)
        b = _dot(h, w3)
        act = (_silu(a) * b).astype(BF16)
        acc_scr[sl] += _dot(act, w2)

    @pl.when(j == pl.num_programs(1) - 1)
    def _():
        o_ref[...] = x_ref[...] + (0.5 * mod_ref[0, gt:gt + 1, :]) * acc_scr[...]


def _ffn(x, mod, g, w1, w3, w2, layer, which, grp_fn, sh, sc, gt):
    rows = x.shape[0]
    tm, tf = FFN_TM, FFN_TF
    return pl.pallas_call(
        functools.partial(_ffn_kernel, sh=sh, sc=sc, gt=gt),
        grid=(rows // tm, D_FF // tf),
        in_specs=[
            pl.BlockSpec((tm, D_MODEL), lambda i, j: (i, 0)),
            pl.BlockSpec((1, N_ADA, D_MODEL), lambda i, j: (grp_fn(i, tm), 0, 0)),
            pl.BlockSpec((1, D_MODEL), lambda i, j: (0, 0)),
            pl.BlockSpec((None, None, D_MODEL, tf), lambda i, j: (layer, which, 0, j)),
            pl.BlockSpec((None, None, D_MODEL, tf), lambda i, j: (layer, which, 0, j)),
            pl.BlockSpec((None, None, tf, D_MODEL), lambda i, j: (layer, which, j, 0)),
        ],
        out_specs=pl.BlockSpec((tm, D_MODEL), lambda i, j: (i, 0)),
        out_shape=jax.ShapeDtypeStruct((rows, D_MODEL), F32),
        scratch_shapes=[pltpu.VMEM((tm, D_MODEL), BF16), pltpu.VMEM((tm, D_MODEL), F32)],
        compiler_params=_cparams(("arbitrary", "arbitrary")),
        name="ffn_half_step",
    )(x, mod, g, w1, w3, w2)


def _head_norm(xs, bsum, g):
    sq_hi, sq_lo = _split(xs * xs)
    ms = _dot(sq_hi, bsum) + _dot(sq_lo, bsum)
    return xs * lax.rsqrt(ms + EPS) * g


def _rope(xs, cos, sin, first_half):
    swapped = jnp.where(first_half, pltpu.roll(xs, LANES - 16, 1), pltpu.roll(xs, 16, 1))
    return xs * cos + swapped * sin


def _dup_heads(x2, low_half):
    sw = pltpu.roll(x2, HEAD_DIM, 1)
    return jnp.where(low_half, x2, sw), jnp.where(low_half, sw, x2)


def _expand_kv(x2, low_half):
    d0, d1 = _dup_heads(x2, low_half)
    return jnp.concatenate([d0, x2, d1], axis=-1)


def _store_rows(ref, val, cache_layer):
    if len(ref.shape) == 2:
        ref[...] = val.astype(ref.dtype)
        return
    seq = ref.shape[-2]
    for s in range(ref.shape[0]):
        rows = val[s * seq:(s + 1) * seq].astype(ref.dtype)
        if len(ref.shape) == 3:
            ref[s] = rows
        else:
            for l in range(ref.shape[1]):
                ref[s, l] = rows if l == cache_layer else jnp.zeros_like(rows)


def _proj_kernel(*refs, rope, cache_layer, n_alias, seq_len):
    x_ref, mod_ref, g_ref, w_ref, bsum_ref, gains_ref, xprev_ref, xnext_ref, cw_ref, cb_ref = refs[:10]
    pos = 10
    if rope:
        cos_ref, sin_ref = refs[pos:pos + 2]
        pos += 2
    pos += n_alias
    x0_ref, z_ref, qn_ref, kn_ref, vn_ref, qg_ref, kge_ref, vge_ref = refs[pos:pos + 8]
    pos += 8
    emit_kv = cache_layer is not None
    if emit_kv:
        kg_ref, vg_ref = refs[pos:pos + 2]
        pos += 2
    u_scr = refs[pos]
    scale = HEAD_DIM ** -0.5
    tm = x_ref.shape[0]

    def projected(x, cols):
        h = _mod_norm(x, g_ref[...], mod_ref[0, SH2:SH2 + 1, :], mod_ref[0, SC2:SC2 + 1, :])
        return _dot(h.astype(BF16), w_ref[:, cols])

    u_scr[...] = projected(x_ref[...], slice(0, IN_WIDTH))

    hy_cols = slice(0, OFF_NA)
    halo_prev = projected(xprev_ref[...], hy_cols)[7:8]
    halo_next = projected(xnext_ref[...], hy_cols)[0:1]
    row = lax.broadcasted_iota(jnp.int32, (tm, 1), 0)
    t_pos = (pl.program_id(0) * tm + row) & (seq_len - 1)

    def conv_chunk(ci):
        sl = slice(ci * HY_CH, (ci + 1) * HY_CH)
        u = u_scr[:, sl]
        prev = jnp.where(row == 0, halo_prev[:, sl], pltpu.roll(u, 1, 0))
        prev = jnp.where(t_pos == 0, 0.0, prev)
        nxt = jnp.where(row == tm - 1, halo_next[:, sl], pltpu.roll(u, tm - 1, 0))
        nxt = jnp.where(t_pos == seq_len - 1, 0.0, nxt)
        return prev * cw_ref[0:1, sl] + u * cw_ref[1:2, sl] + nxt * cw_ref[2:3, sl] + cb_ref[:, sl]

    x0_ref[...] = conv_chunk(0)
    z_ref[...] = conv_chunk(1) * conv_chunk(2)

    bsum = bsum_ref[...]
    qn = _head_norm(u_scr[:, OFF_NA:OFF_NA + NA_W], bsum, gains_ref[0:1, :])
    qn_ref[...] = (qn * scale).astype(qn_ref.dtype)
    kn = _head_norm(u_scr[:, OFF_NA + NA_W:OFF_NA + 2 * NA_W], bsum, gains_ref[1:2, :])
    _store_rows(kn_ref, kn, cache_layer)
    _store_rows(vn_ref, u_scr[:, OFF_NA + 2 * NA_W:OFF_GQ], cache_layer)

    qg = _head_norm(u_scr[:, OFF_GQ:OFF_GQ + GQ_W], bsum, gains_ref[2:3, :])
    kg = _head_norm(u_scr[:, OFF_GQ + GQ_W:OFF_GQ + GQ_W + GKV_W], bsum[0:GKV_W, 0:GKV_W], gains_ref[3:4, 0:GKV_W])
    vg = u_scr[:, OFF_GQ + GQ_W + GKV_W:IN_WIDTH]
    if emit_kv:
        _store_rows(kg_ref, kg, cache_layer)
        _store_rows(vg_ref, vg, cache_layer)

    lane = lax.broadcasted_iota(jnp.int32, (1, LANES), 1)
    if rope:
        cos = cos_ref[...]
        sin = sin_ref[...]
        first_half = (lane % 32) < 16
        for p in range(GQ_W // LANES):
            sl = slice(p * LANES, (p + 1) * LANES)
            qg_ref[:, sl] = (_rope(qg[:, sl], cos, sin, first_half) * scale).astype(qg_ref.dtype)
        kg = _rope(kg, cos, sin, first_half)
    else:
        qg_ref[...] = (qg * scale).astype(qg_ref.dtype)

    low_half = lane < HEAD_DIM
    kge_ref[...] = _expand_kv(kg, low_half).astype(kge_ref.dtype)
    vge_ref[...] = _expand_kv(vg, low_half).astype(vge_ref.dtype)


def _project(x, mod, g, w_bf, bsum, gains, conv_w, conv_b, seq_len, grp_fn, rope_tabs, attn_dtype,
             cache_layer=None, prev_cache=None):
    rows = x.shape[0]
    tm = ROW_TM
    rope = rope_tabs is not None
    assert seq_len & (seq_len - 1) == 0 and (seq_len % tm == 0 or tm % seq_len == 0)
    sub = 8
    in_specs = [
        pl.BlockSpec((tm, D_MODEL), lambda i: (i, 0)),
        pl.BlockSpec((1, N_ADA, D_MODEL), lambda i: (grp_fn(i, tm), 0, 0)),
        pl.BlockSpec((1, D_MODEL), lambda i: (0, 0)),
        pl.BlockSpec((D_MODEL, IN_WIDTH), lambda i: (0, 0)),
        pl.BlockSpec((NA_W, NA_W), lambda i: (0, 0)),
        pl.BlockSpec((4, NA_W), lambda i: (0, 0)),
        pl.BlockSpec((sub, D_MODEL), lambda i: (jnp.maximum(i * (tm // sub) - 1, 0), 0)),
        pl.BlockSpec((sub, D_MODEL), lambda i: (jnp.minimum((i + 1) * (tm // sub), rows // sub - 1), 0)),
        pl.BlockSpec((3, OFF_NA), lambda i: (0, 0)),
        pl.BlockSpec((1, OFF_NA), lambda i: (0, 0)),
    ]
    args = [x, mod, g, w_bf, bsum, gains, x, x, conv_w, conv_b.reshape(1, OFF_NA)]
    if rope:
        seq_tiles = DEC_SEQ // tm
        in_specs += [pl.BlockSpec((tm, LANES), lambda i: (i % seq_tiles, 0))] * 2
        args += list(rope_tabs)
    outs = [(HY_CH, F32)] * 2 + [(NA_W, attn_dtype)] * 3 + [(GQ_W, attn_dtype)] * 3
    if cache_layer is not None:
        outs += [(GKV_W, F32), (GKV_W, F32)]
    out_specs = [pl.BlockSpec((tm, w), lambda i: (i, 0)) for w, _ in outs]
    out_shape = [jax.ShapeDtypeStruct((rows, w), dt) for w, dt in outs]
    aliases = {}
    if cache_layer is not None:
        cache_outs = (3, 4, 8, 9)
        seqs = tm // SEQ
        for o in cache_outs:
            w = outs[o][0]
            out_shape[o] = jax.ShapeDtypeStruct((BATCH, DEPTH, SEQ, w), F32)
            if prev_cache is None:
                out_specs[o] = pl.BlockSpec((seqs, DEPTH, SEQ, w), lambda i: (i, 0, 0, 0))
            else:
                out_specs[o] = pl.BlockSpec((seqs, None, SEQ, w), lambda i: (i, cache_layer, 0, 0))
        if prev_cache is not None:
            aliases = {len(args) + n: o for n, o in enumerate(cache_outs)}
            in_specs += [pl.BlockSpec(memory_space=pl.ANY)] * len(cache_outs)
            args += list(prev_cache)
    return pl.pallas_call(
        functools.partial(_proj_kernel, rope=rope, cache_layer=cache_layer, n_alias=len(aliases), seq_len=seq_len),
        grid=(rows // tm,),
        in_specs=in_specs,
        out_specs=out_specs,
        out_shape=out_shape,
        input_output_aliases=aliases,
        scratch_shapes=[pltpu.VMEM((tm, IN_WIDTH), F32)],
        compiler_params=_cparams(("arbitrary",)),
        name="mixer_in_proj",
    )(*args)


def _log2(n):
    assert n & (n - 1) == 0
    return n.bit_length() - 1


def _stack_heads(q, n_heads):
    m, w = q.shape
    rows = lax.broadcasted_iota(jnp.int32, (n_heads * m, w), 0)
    lanes = lax.broadcasted_iota(jnp.int32, (n_heads * m, w), 1)
    own = jnp.right_shift(rows, _log2(m)) == jnp.right_shift(lanes, _log2(HEAD_DIM))
    return jnp.where(own, jnp.concatenate([q] * n_heads, axis=0), jnp.zeros((), q.dtype))


def _unstack_heads(o, n_heads):
    m = o.shape[0] // n_heads
    head = jnp.right_shift(lax.broadcasted_iota(jnp.int32, (m, o.shape[1]), 1), _log2(HEAD_DIM))
    out = o[0:m]
    for h in range(1, n_heads):
        out = jnp.where(head == h, o[h * m:(h + 1) * m], out)
    return out


def _sink_column(sink_ref, m, n_heads):
    block = jnp.right_shift(lax.broadcasted_iota(jnp.int32, (n_heads * m, 1), 0), _log2(m))
    col = jnp.full((n_heads * m, 1), sink_ref[0], F32)
    for h in range(1, n_heads):
        col = jnp.where(block == h, sink_ref[h], col)
    return col


def _softmax_pv(scores, values, sink):
    m = scores[0].max(axis=-1, keepdims=True)
    for s in scores[1:]:
        m = jnp.maximum(m, s.max(axis=-1, keepdims=True))
    if sink is not None:
        m = jnp.maximum(m, sink)
    den = None
    acc = None
    for s, v in zip(scores, values):
        p = jnp.exp(s - m)
        d = p.sum(axis=-1, keepdims=True)
        o = _dot(p.astype(BF16), v)
        den = d if den is None else den + d
        acc = o if acc is None else acc + o
    if sink is not None:
        den = den + jnp.exp(sink - m)
    return acc / den


def _ctx_attn_kernel(sink_ref, qn_ref, kn_ref, vn_ref, qg_ref, kge_ref, vge_ref, yna_ref, ygq_ref):
    m = qn_ref.shape[1]
    low = lax.broadcasted_iota(jnp.int32, (m, LANES), 1) < HEAD_DIM
    for q_ref, k_ref, v_ref, y_ref, has_sink in ((qn_ref, kn_ref, vn_ref, yna_ref, False),
                                                  (qg_ref, kge_ref, vge_ref, ygq_ref, True)):
        for p in range(NA_W // LANES):
            sl = slice(p * LANES, (p + 1) * LANES)
            q2 = q_ref[0, :, sl].astype(BF16)
            k2 = k_ref[0, :, sl].astype(BF16)
            v2 = v_ref[0, :, sl].astype(BF16)
            halves = []
            for half, keep in enumerate((low, ~low)):
                qh = jnp.where(keep, q2, jnp.zeros((), BF16))
                sink = sink_ref[2 * p + half] if has_sink else None
                halves.append(_softmax_pv([_dot_nt(qh, k2)], [v2], sink))
            y_ref[0, :, sl] = jnp.where(low, halves[0], halves[1]).astype(y_ref.dtype)


def _ctx_attention(sink, qn, kn_cache, vn_cache, qg, kge, vge, layer):
    b, l, _ = qn.shape

    def spec(w):
        return pl.BlockSpec((1, l, w), lambda i: (i, 0, 0))

    cache_spec = pl.BlockSpec((1, None, l, NA_W), lambda i: (i, layer, 0, 0))
    return pl.pallas_call(
        _ctx_attn_kernel,
        grid=(b,),
        in_specs=[pl.BlockSpec(memory_space=pltpu.SMEM), spec(NA_W), cache_spec, cache_spec, spec(GQ_W),
                  spec(GQ_W), spec(GQ_W)],
        out_specs=[spec(NA_W), spec(GQ_W)],
        out_shape=[jax.ShapeDtypeStruct((b, l, NA_W), BF16), jax.ShapeDtypeStruct((b, l, GQ_W), BF16)],
        compiler_params=_cparams(("arbitrary",)),
        name="context_attention",
    )(sink, qn, kn_cache, vn_cache, qg, kge, vge)


NA_KEYS = NA_ROWS * GRID_W
NA_ROWS_PER_STEP = 8


def _na_bias_kernel(r_ref, oh_ref, mk_ref, o_ref):
    o_ref[...] = jnp.dot(r_ref[...], oh_ref[...], precision=_HIGHEST, preferred_element_type=F32) + mk_ref[...]


def _na_bias_tables():
    q = np.arange(GRID_W)[:, None]
    kc = np.arange(GRID_W)[None, :]
    win_lo = np.clip(q - NA_COLS // 2, 0, GRID_W - NA_COLS)
    ok = (kc >= win_lo) & (kc < win_lo + NA_COLS)
    j = kc - q + NA_COLS - 1
    onehot = np.zeros((LANES, GRID_W * GRID_W), np.float32)
    qq, kk = np.nonzero(ok)
    onehot[j[qq, kk], qq * GRID_W + kk] = 1.0
    mask = np.where(ok, 0.0, NEG_INF).astype(np.float32).reshape(1, GRID_W * GRID_W)
    return onehot, mask


def _na_bias(rpb_l):
    n_dr = 2 * NA_ROWS - 1
    rows = NA_HEADS * n_dr
    rows_pad = -(-rows // 8) * 8
    r = jnp.zeros((rows_pad, LANES), F32).at[:rows, :2 * NA_COLS - 1].set(rpb_l.reshape(rows, 2 * NA_COLS - 1))
    onehot, mask = _na_bias_tables()
    out = pl.pallas_call(
        _na_bias_kernel,
        out_shape=jax.ShapeDtypeStruct((rows_pad, GRID_W * GRID_W), F32),
        compiler_params=pltpu.CompilerParams(vmem_limit_bytes=VMEM_LIMIT),
        name="na_bias_expand",
    )(r, jnp.asarray(onehot), jnp.asarray(mask))
    bm = out[:rows].reshape(NA_HEADS, n_dr, GRID_W, GRID_W)
    bm2 = jnp.concatenate([bm[:, :-1], bm[:, 1:]], axis=-1)
    return bm2.transpose(1, 0, 2, 3).reshape(n_dr - 1, NA_HEADS * GRID_W, 2 * GRID_W)


def _na_attn_kernel(q_ref, k_ref, v_ref, kc_ref, vc_ref, bm_ref, o_ref):
    n_rows = DEC_SEQ // GRID_W
    kc = kc_ref[0, 0].astype(BF16)
    vc = vc_ref[0, 0].astype(BF16)
    for rr in range(NA_ROWS_PER_STEP):
        r = pl.program_id(1) * NA_ROWS_PER_STEP + rr
        start = jnp.clip(r - NA_ROWS // 2, 0, n_rows - NA_ROWS)
        shift = r - start
        row0 = pl.multiple_of(start * GRID_W, GRID_W)
        q = _stack_heads(q_ref[0, rr * GRID_W:(rr + 1) * GRID_W, :], NA_HEADS)
        k = k_ref[0, pl.ds(row0, NA_KEYS), :]
        v = v_ref[0, pl.ds(row0, NA_KEYS), :]
        bias = jnp.concatenate([bm_ref[2 * jj - shift + NA_ROWS - 1] for jj in range(NA_ROWS // 2)], axis=-1)
        s_loc = _dot_nt(q, k) + bias
        s_ctx = _dot_nt(q, kc)
        o = _softmax_pv([s_loc, s_ctx], [v, vc], None)
        o_ref[0, rr * GRID_W:(rr + 1) * GRID_W, :] = _unstack_heads(o, NA_HEADS).astype(o_ref.dtype)


def _na_attention(q, k, v, kc, vc, bm2, layer):
    b, n, _ = q.shape
    tq = NA_ROWS_PER_STEP * GRID_W
    return pl.pallas_call(
        _na_attn_kernel,
        grid=(b, n // tq),
        in_specs=[
            pl.BlockSpec((1, tq, NA_W), lambda i, r: (i, r, 0)),
            pl.BlockSpec((1, n, NA_W), lambda i, r: (i, 0, 0)),
            pl.BlockSpec((1, n, NA_W), lambda i, r: (i, 0, 0)),
            pl.BlockSpec((1, 1, PAST_LEN, NA_W), lambda i, r: (i, layer, 0, 0)),
            pl.BlockSpec((1, 1, PAST_LEN, NA_W), lambda i, r: (i, layer, 0, 0)),
            pl.BlockSpec(bm2.shape, lambda i, r: (0, 0, 0)),
        ],
        out_specs=pl.BlockSpec((1, tq, NA_W), lambda i, r: (i, r, 0)),
        out_shape=jax.ShapeDtypeStruct((b, n, NA_W), BF16),
        compiler_params=_cparams(("arbitrary", "arbitrary")),
        name="neighbourhood_attention",
    )(q, k, v, kc, vc, bm2)


WIN_KEYS = 3 * BLK
WIN_BLOCKS_PER_STEP = 4


def _win_attn_kernel(sink_ref, mask_ref, q_ref, kge_ref, vge_ref, kc_ref, vc_ref, o_ref):
    n = kge_ref.shape[1]
    n_blocks = n // BLK
    low_c = lax.broadcasted_iota(jnp.int32, (1, LANES), 1) < HEAD_DIM
    kce = _expand_kv(kc_ref[0, 0], low_c).astype(BF16)
    vce = _expand_kv(vc_ref[0, 0], low_c).astype(BF16)
    sink = _sink_column(sink_ref, BLK, GQA_Q_HEADS)
    for bb in range(WIN_BLOCKS_PER_STEP):
        nb = pl.program_id(1) * WIN_BLOCKS_PER_STEP + bb
        start = pl.multiple_of(jnp.clip((nb - 1) * BLK, 0, n - WIN_KEYS), BLK)
        variant = jnp.where(nb == 0, 0, jnp.where(nb == n_blocks - 1, 2, 1))
        q = _stack_heads(q_ref[0, bb * BLK:(bb + 1) * BLK, :], GQA_Q_HEADS)
        s_loc = _dot_nt(q, kge_ref[0, pl.ds(start, WIN_KEYS), :]) + mask_ref[variant]
        s_ctx = _dot_nt(q, kce)
        o = _softmax_pv([s_loc, s_ctx], [vge_ref[0, pl.ds(start, WIN_KEYS), :], vce], sink)
        o_ref[0, bb * BLK:(bb + 1) * BLK, :] = _unstack_heads(o, GQA_Q_HEADS).astype(o_ref.dtype)


def _win_mask_table():
    qi = np.arange(BLK)[:, None]
    kj = np.arange(WIN_KEYS)[None, :]
    tabs = [np.where(np.abs(qi + off - kj) <= GQA_WINDOW, 0.0, NEG_INF) for off in (0, BLK, 2 * BLK)]
    return jnp.asarray(np.stack([np.tile(t, (GQA_Q_HEADS, 1)) for t in tabs]).astype(np.float32))


def _win_attention(sink, q, kge, vge, kc, vc, layer):
    b, n, _ = q.shape
    mask = _win_mask_table()
    tq = WIN_BLOCKS_PER_STEP * BLK
    return pl.pallas_call(
        _win_attn_kernel,
        grid=(b, n // tq),
        in_specs=[
            pl.BlockSpec(memory_space=pltpu.SMEM),
            pl.BlockSpec(mask.shape, lambda i, j: (0, 0, 0)),
            pl.BlockSpec((1, tq, GQ_W), lambda i, j: (i, j, 0)),
            pl.BlockSpec((1, n, GQ_W), lambda i, j: (i, 0, 0)),
            pl.BlockSpec((1, n, GQ_W), lambda i, j: (i, 0, 0)),
            pl.BlockSpec((1, 1, PAST_LEN, GKV_W), lambda i, j: (i, layer, 0, 0)),
            pl.BlockSpec((1, 1, PAST_LEN, GKV_W), lambda i, j: (i, layer, 0, 0)),
        ],
        out_specs=pl.BlockSpec((1, tq, GQ_W), lambda i, j: (i, j, 0)),
        out_shape=jax.ShapeDtypeStruct((b, n, GQ_W), BF16),
        compiler_params=_cparams(("arbitrary", "arbitrary")),
        name="window_attention",
    )(sink, mask, q, kge, vge, kc, vc)


@functools.lru_cache(maxsize=None)
def _hyena_consts_np(L):
    n = 2 * L
    k = np.arange(L, dtype=np.int64)
    ang = (2.0 * np.pi / n) * ((k[:, None] * k[None, :]) % n).astype(np.float64)

    cm = np.cos(ang).astype(np.float32)
    sm = (-np.sin(ang)).astype(np.float32)
    idx = np.arange(L, dtype=np.float32)
    t = idx / np.float32(L - 1)
    bands = np.linspace(1e-4, HY_BANDS - 1, HY_BANDS, dtype=np.float32)
    fang = np.float32(2.0 * math.pi / L) * idx[:, None] * bands[None, :]
    feats = np.zeros((L, LANES), np.float32)
    feats[:, 0] = t
    feats[:, 1:1 + HY_BANDS] = np.cos(fang)
    feats[:, 1 + HY_BANDS:HY_EMB] = -np.sin(fang)
    max_decay = math.log(HY_DECAY_TARGET) / HY_FAST_PCT
    min_decay = math.log(HY_DECAY_TARGET) / HY_SLOW_PCT
    deltas = np.abs(np.linspace(min_decay, max_decay, HY_CH, dtype=np.float32))
    decay = np.exp(-t[:, None] * deltas[None, :]).astype(np.float32)
    decay2 = np.concatenate([decay, decay], axis=1)
    return cm, sm, feats, decay2


def _hyena_consts(L):
    cm, sm, feats, decay2 = _hyena_consts_np(L)
    return jnp.asarray(cm), jnp.asarray(sm), jnp.asarray(feats), jnp.asarray(decay2)


def _hy_filter_kernel(feats_ref, w1_ref, b1_ref, w2_ref, b2_ref, w3_ref, freq_ref, decay_ref, wsum_ref, wdiff_ref):
    def dot_hi(a, b):
        a_hi, a_lo = _split(a)
        b_hi, b_lo = _split(b)
        return _dot(a_hi, b_hi) + (_dot(a_lo, b_hi) + _dot(a_hi, b_lo))

    hid = jnp.sin(freq_ref[0:1, :] * (dot_hi(feats_ref[...], w1_ref[...]) + b1_ref[...]))
    hid = jnp.sin(freq_ref[1:2, :] * (dot_hi(hid, w2_ref[...]) + b2_ref[...]))
    taps = dot_hi(hid, w3_ref[...]) * decay_ref[...]
    fwd = taps[:, 0:HY_CH]
    bwd = taps[:, HY_CH:2 * HY_CH]
    row = lax.broadcasted_iota(jnp.int32, bwd.shape, 0)
    bwd = jnp.where(row == 0, 0.0, bwd)
    wsum_ref[...] = fwd + bwd
    wdiff_ref[...] = fwd - bwd


def _hy_filter(L, feats, decay2, w1, b1, w2, b2, w3, freq):
    w1p = jnp.zeros((LANES, HY_FILT_W), F32).at[:HY_EMB].set(w1)
    return pl.pallas_call(
        _hy_filter_kernel,
        out_shape=[jax.ShapeDtypeStruct((L, HY_CH), F32)] * 2,
        compiler_params=pltpu.CompilerParams(vmem_limit_bytes=VMEM_LIMIT),
        name="hyena_filter",
    )(feats, w1p, b1.reshape(1, -1), w2, b2.reshape(1, -1), w3, freq, decay2)


def _hy_conv_kernel(z_ref, x0_ref, wsum_ref, wdiff_ref, skip_ref, cr_ref, sr_ref, cc_ref, sc_ref,
                    o_ref, z_scr, ws_scr, wd_scr, acc_scr, *, group):
    t = pl.program_id(1)
    L = z_ref.shape[1]
    n = 2 * L

    @pl.when(t == 0)
    def _():
        z_scr[...] = z_ref[...].astype(BF16)
        ws_scr[...] = wsum_ref[...].astype(BF16)
        wd_scr[...] = wdiff_ref[...].astype(BF16)
        acc_scr[...] = jnp.zeros_like(acc_scr)

    cr, sr = cr_ref[...].astype(BF16), sr_ref[...].astype(BF16)
    cc, sc = cc_ref[...].astype(BF16), sc_ref[...].astype(BF16)
    k_re = _dot(cr, ws_scr[...])
    k_im = _dot(sr, wd_scr[...])
    for b in range(group):
        zb = z_scr[b]
        z_re = _dot(cr, zb)
        z_im = _dot(sr, zb)
        y_re = (z_re * k_re - z_im * k_im).astype(BF16)
        y_im = (z_re * k_im + z_im * k_re).astype(BF16)
        acc_scr[b] += _dot(cc, y_re) + _dot(sc, y_im)

    @pl.when(t == pl.num_programs(1) - 1)
    def _():
        row = lax.broadcasted_iota(jnp.int32, (L, 1), 0)
        sgn = (1 - 2 * (row % 2)).astype(F32)
        w = wsum_ref[...]
        w_dc = w.sum(axis=0, keepdims=True)
        w_ny = (w * sgn).sum(axis=0, keepdims=True)
        for b in range(group):
            z = z_ref[b]
            z_dc = z.sum(axis=0, keepdims=True)
            z_ny = (z * sgn).sum(axis=0, keepdims=True)
            conv = (2.0 / n) * acc_scr[b] - (1.0 / n) * (z_dc * w_dc) + (1.0 / n) * (sgn * (z_ny * w_ny))
            o_ref[b] = (x0_ref[b] * (conv + z * skip_ref[...])).astype(o_ref.dtype)


def _hy_conv(z, x0, wsum, wdiff, skip, cm, sm, group):
    b, L, c = z.shape
    tkf = min(HY_TKF, L)
    row_spec = pl.BlockSpec((tkf, L), lambda g, t: (t, 0))
    col_spec = pl.BlockSpec((L, tkf), lambda g, t: (0, t))
    once = pl.Buffered(1)
    seq_spec = pl.BlockSpec((group, L, c), lambda g, t: (g, 0, 0), pipeline_mode=once)
    w_spec = pl.BlockSpec((L, c), lambda g, t: (0, 0), pipeline_mode=once)
    return pl.pallas_call(
        functools.partial(_hy_conv_kernel, group=group),
        grid=(b // group, L // tkf),
        in_specs=[seq_spec, seq_spec, w_spec, w_spec, pl.BlockSpec((1, c), lambda g, t: (0, 0)),
                  row_spec, row_spec, col_spec, col_spec],
        out_specs=pl.BlockSpec((group, L, c), lambda g, t: (g, 0, 0)),
        out_shape=jax.ShapeDtypeStruct((b, L, c), BF16),
        scratch_shapes=[pltpu.VMEM((group, L, c), BF16), pltpu.VMEM((L, c), BF16), pltpu.VMEM((L, c), BF16),
                        pltpu.VMEM((group, L, c), F32)],
        compiler_params=_cparams(("arbitrary", "arbitrary")),
        name="hyena_long_conv",
    )(z, x0, wsum, wdiff, skip.reshape(1, c), cm, sm, cm, sm)


def _merge_kernel(x_ref, yhy_ref, yna_ref, ygq_ref, mod_ref, w_ref, o_ref):
    y = jnp.concatenate([yhy_ref[...], yna_ref[...], ygq_ref[...]], axis=-1)
    o_ref[...] = x_ref[...] + mod_ref[0, G2:G2 + 1, :] * _dot(y, w_ref[...])


def _merge(x, y_hy, y_na, y_gq, mod, w_bf, grp_fn):
    rows = x.shape[0]
    tm = ROW_TM
    return pl.pallas_call(
        _merge_kernel,
        grid=(rows // tm,),
        in_specs=[
            pl.BlockSpec((tm, D_MODEL), lambda i: (i, 0)),
            pl.BlockSpec((tm, HY_CH), lambda i: (i, 0)),
            pl.BlockSpec((tm, NA_W), lambda i: (i, 0)),
            pl.BlockSpec((tm, GQ_W), lambda i: (i, 0)),
            pl.BlockSpec((1, N_ADA, D_MODEL), lambda i: (grp_fn(i, tm), 0, 0)),
            pl.BlockSpec((D_MODEL, D_MODEL), lambda i: (0, 0)),
        ],
        out_specs=pl.BlockSpec((tm, D_MODEL), lambda i: (i, 0)),
        out_shape=jax.ShapeDtypeStruct((rows, D_MODEL), F32),
        compiler_params=_cparams(("arbitrary",)),
        name="mixer_out_proj",
    )(x, y_hy, y_na, y_gq, mod, w_bf)


def _rope_tables():
    pos = np.arange(DEC_SEQ)
    quarter = HEAD_DIM // 4
    inv = ROPE_BASE ** (-np.arange(quarter, dtype=np.float64) / quarter)
    ang_r = (pos // GRID_W)[:, None] * inv[None, :]
    ang_c = (pos % GRID_W)[:, None] * inv[None, :]
    cos_h = np.concatenate([np.cos(ang_r), np.cos(ang_r), np.cos(ang_c), np.cos(ang_c)], axis=1)
    sin_h = np.concatenate([-np.sin(ang_r), np.sin(ang_r), -np.sin(ang_c), np.sin(ang_c)], axis=1)
    reps = LANES // HEAD_DIM
    return (jnp.asarray(np.tile(cos_h, (1, reps)), F32), jnp.asarray(np.tile(sin_h, (1, reps)), F32))


def _head_sum_matrix():
    idx = np.arange(NA_W) // HEAD_DIM
    return jnp.asarray((idx[:, None] == idx[None, :]).astype(np.float32) / HEAD_DIM, BF16)


def _grp_prompt(i, tm):
    return 0


def _grp_sample(i, tm):
    return 1 + (i * tm) // DEC_SEQ


def kernel(x_prompt, x_sample, cache_na_k, cache_na_v, cache_gqa_k, cache_gqa_v, c, c_ctx, ada_w, ada_b, norm_g, ffn_w1, ffn_w3, ffn_w2, w_in, w_out, hy_conv_w, hy_conv_b, hy_filt_w1, hy_filt_b1, hy_filt_w2, hy_filt_b2, hy_filt_w3, hy_freq, hy_skip, na_q_g, na_k_g, na_rpb, gqa_q_g, gqa_k_g, gqa_sink):
    xp = x_prompt.reshape(BATCH * SEQ, D_MODEL)
    xs = x_sample.reshape(DEC_BATCH * DEC_SEQ, D_MODEL)
    cc = jnp.zeros((8, D_MODEL), F32).at[0].set(c_ctx).at[1:1 + DEC_BATCH].set(c)
    kc_na = cache_na_k.reshape(DEC_BATCH, DEPTH, PAST_LEN, NA_W)
    vc_na = cache_na_v.reshape(DEC_BATCH, DEPTH, PAST_LEN, NA_W)
    kc_gq = cache_gqa_k.reshape(DEC_BATCH, DEPTH, PAST_LEN, GKV_W)
    vc_gq = cache_gqa_v.reshape(DEC_BATCH, DEPTH, PAST_LEN, GKV_W)

    rope_tabs = _rope_tables()
    bsum = _head_sum_matrix()
    hy_p = _hyena_consts(SEQ)
    hy_s = _hyena_consts(DEC_SEQ)

    cache = None
    for l in range(DEPTH):
        mod = _adaln_mod(cc, ada_w, ada_b, l)
        w_in_bf = _cast_bf16(w_in, l)
        w_out_bf = _cast_bf16(w_out, l)
        gains = jnp.stack([jnp.tile(na_q_g[l], NA_HEADS), jnp.tile(na_k_g[l], NA_HEADS),
                           jnp.tile(gqa_q_g[l], GQA_Q_HEADS), jnp.tile(gqa_k_g[l], GQA_Q_HEADS)])
        bm2 = _na_bias(na_rpb[l])
        ng = [norm_g[l, i].reshape(1, D_MODEL) for i in range(3)]
        filt = {}
        for L, consts in ((SEQ, hy_p), (DEC_SEQ, hy_s)):
            filt[L] = _hy_filter(L, consts[2], consts[3], hy_filt_w1[l], hy_filt_b1[l], hy_filt_w2[l],
                                 hy_filt_b2[l], hy_filt_w3[l], hy_freq[l])

        xp = _ffn(xp, mod, ng[0], ffn_w1, ffn_w3, ffn_w2, l, 0, _grp_prompt, SH1, SC1, G1)
        x0, z, qn, kn, vn, qg, kgd, vgd, kg, vg = _project(
            xp, mod, ng[1], w_in_bf, bsum, gains, hy_conv_w[l], hy_conv_b[l], SEQ, _grp_prompt, None, F32,
            cache_layer=l, prev_cache=cache)
        cache = (kn, vn, kg, vg)
        shp = lambda a: a.reshape(BATCH, SEQ, a.shape[-1])
        y_hy = _hy_conv(shp(z), shp(x0), filt[SEQ][0], filt[SEQ][1], hy_skip[l], hy_p[0], hy_p[1], group=4)
        y_na, y_gq = _ctx_attention(gqa_sink[l], shp(qn), kn, vn, shp(qg), shp(kgd), shp(vgd), l)
        xp = _merge(xp, y_hy.reshape(-1, HY_CH), y_na.reshape(-1, NA_W), y_gq.reshape(-1, GQ_W), mod, w_out_bf,
                    _grp_prompt)
        xp = _ffn(xp, mod, ng[2], ffn_w1, ffn_w3, ffn_w2, l, 1, _grp_prompt, SH3, SC3, G3)

        xs = _ffn(xs, mod, ng[0], ffn_w1, ffn_w3, ffn_w2, l, 0, _grp_sample, SH1, SC1, G1)
        x0, z, qn, kn, vn, qg, kgd, vgd = _project(
            xs, mod, ng[1], w_in_bf, bsum, gains, hy_conv_w[l], hy_conv_b[l], DEC_SEQ, _grp_sample, rope_tabs, BF16)
        shs = lambda a: a.reshape(DEC_BATCH, DEC_SEQ, a.shape[-1])
        y_hy = _hy_conv(shs(z), shs(x0), filt[DEC_SEQ][0], filt[DEC_SEQ][1], hy_skip[l], hy_s[0], hy_s[1],
                        group=DEC_BATCH)
        y_na = _na_attention(shs(qn), shs(kn), shs(vn), kc_na, vc_na, bm2, l)
        y_gq = _win_attention(gqa_sink[l], shs(qg), shs(kgd), shs(vgd), kc_gq, vc_gq, l)
        xs = _merge(xs, y_hy.reshape(-1, HY_CH), y_na.reshape(-1, NA_W), y_gq.reshape(-1, GQ_W), mod, w_out_bf,
                    _grp_sample)
        xs = _ffn(xs, mod, ng[2], ffn_w1, ffn_w3, ffn_w2, l, 1, _grp_sample, SH3, SC3, G3)

    kn, vn, kg, vg = cache
    return (xp.reshape(BATCH, SEQ, D_MODEL), xs.reshape(DEC_BATCH, DEC_SEQ, D_MODEL),
            kn.reshape(BATCH, DEPTH, SEQ, NA_HEADS, HEAD_DIM), vn.reshape(BATCH, DEPTH, SEQ, NA_HEADS, HEAD_DIM),
            kg.reshape(BATCH, DEPTH, SEQ, GQA_KV_HEADS, HEAD_DIM), vg.reshape(BATCH, DEPTH, SEQ, GQA_KV_HEADS, HEAD_DIM))
```

```python
import functools
import math

import numpy as np
import jax
import jax.numpy as jnp
from jax import lax
from jax.experimental import pallas as pl
from jax.experimental.pallas import tpu as pltpu

F32 = jnp.float32
BF16 = jnp.bfloat16

D_MODEL = 1024
BATCH = 16
SEQ = 256
DEPTH = 2
DEC_BATCH = 2
DEC_SEQ = 2048
PAST_LEN = 256
GRID_W = 64
HEAD_DIM = 64
HY_CH = 256
NA_HEADS = 6
GQA_Q_HEADS = 6
GQA_KV_HEADS = 2
NA_W = NA_HEADS * HEAD_DIM
GQ_W = GQA_Q_HEADS * HEAD_DIM
GKV_W = GQA_KV_HEADS * HEAD_DIM
IN_WIDTH = 3 * HY_CH + 3 * NA_W + GQ_W + 2 * GKV_W
D_FF = 2816
N_ADA = 9
HY_BANDS = 16
HY_EMB = 1 + 2 * HY_BANDS
HY_FILT_W = 64
HY_DECAY_TARGET = 1e-2
HY_FAST_PCT = 0.3
HY_SLOW_PCT = 1.5
NA_ROWS = 8
NA_COLS = 16
GQA_WINDOW = 128
BLK = 128
ROPE_BASE = 10000.0
EPS = 1e-6
NEG_INF = -1e30

LANES = 128
V7X_VMEM_BYTES = 64 * 1024 * 1024
VMEM_LIMIT = V7X_VMEM_BYTES * 15 // 16

OFF_NA = 3 * HY_CH
OFF_GQ = OFF_NA + 3 * NA_W

SH1, SC1, G1, SH2, SC2, G2, SH3, SC3, G3 = range(N_ADA)

FFN_TM = 2048
FFN_TF = 256
FFN_ROW_GROUPS = 4
ROW_TM = 512
HY_TKF = 256

_HIGHEST = lax.Precision.HIGHEST


def _cparams(sem):
    return pltpu.CompilerParams(dimension_semantics=sem, vmem_limit_bytes=VMEM_LIMIT)


def _silu(x):
    return x * (1.0 / (1.0 + jnp.exp(-x)))


def _dot(a, b):
    return jnp.dot(a, b, preferred_element_type=F32)


def _dot_nt(a, b):
    return lax.dot_general(a, b, (((1,), (1,)), ((), ())), preferred_element_type=F32)


def _split(x):
    hi = x.astype(BF16)
    lo = (x - hi.astype(F32)).astype(BF16)
    return hi, lo


def _mod_norm(x, g, shift, scale):
    ms = jnp.mean(x * x, axis=-1, keepdims=True)
    return (x * lax.rsqrt(ms + EPS) * g) * (1.0 + scale) + shift


def _mod_kernel(c_ref, w_ref, b_ref, o_ref):
    a = _silu(c_ref[...]).astype(BF16)
    o_ref[...] = _dot(a, w_ref[...].astype(BF16)) + b_ref[...]


def _adaln_mod(cc, ada_w, ada_b, layer):
    tn = D_MODEL
    n_out = N_ADA * D_MODEL
    out = pl.pallas_call(
        _mod_kernel,
        grid=(n_out // tn,),
        in_specs=[
            pl.BlockSpec((8, D_MODEL), lambda j: (0, 0)),
            pl.BlockSpec((None, D_MODEL, tn), lambda j: (layer, 0, j)),
            pl.BlockSpec((None, 1, tn), lambda j: (layer, 0, j)),
        ],
        out_specs=pl.BlockSpec((8, tn), lambda j: (0, j)),
        out_shape=jax.ShapeDtypeStruct((8, n_out), F32),
        compiler_params=_cparams(("arbitrary",)),
        name="adaln_mod",
    )(cc, ada_w, ada_b.reshape(DEPTH, 1, n_out))
    return out.reshape(8, N_ADA, D_MODEL)


def _cast_kernel(x_ref, o_ref):
    o_ref[...] = x_ref[...].astype(BF16)


def _cast_bf16(w, layer):
    _, rows, cols = w.shape
    tr = 128
    return pl.pallas_call(
        _cast_kernel,
        grid=(rows // tr,),
        in_specs=[pl.BlockSpec((None, tr, cols), lambda i: (layer, i, 0))],
        out_specs=pl.BlockSpec((tr, cols), lambda i: (i, 0)),
        out_shape=jax.ShapeDtypeStruct((rows, cols), BF16),
        compiler_params=_cparams(("arbitrary",)),
        name="cast_bf16",
    )(w)


def _ffn_kernel(x_ref, mod_ref, g_ref, w1_ref, w3_ref, w2_ref, o_ref, h_scr, acc_scr, *, sh, sc, gt):
    j = pl.program_id(1)
    last_j = pl.num_programs(1) - 1
    rows = h_scr.shape[0] // FFN_ROW_GROUPS

    def step(first, last):
        w1 = w1_ref[...].astype(BF16)
        w3 = w3_ref[...].astype(BF16)
        w2 = w2_ref[...].astype(BF16)
        for s in range(FFN_ROW_GROUPS):
            sl = slice(s * rows, (s + 1) * rows)
            if first:
                h = _mod_norm(x_ref[sl, :], g_ref[...], mod_ref[0, sh:sh + 1, :], mod_ref[0, sc:sc + 1, :]).astype(BF16)
                h_scr[sl] = h
            else:
                h = h_scr[sl]
            act = (_silu(_dot(h, w1)) * _dot(h, w3)).astype(BF16)
            part = _dot(act, w2)
            if first:
                acc_scr[sl] = part
            elif last:
                o_ref[sl, :] = x_ref[sl, :] + (0.5 * mod_ref[0, gt:gt + 1, :]) * (acc_scr[sl] + part)
            else:
                acc_scr[sl] += part

    pl.when(j == 0)(lambda: step(True, False))
    pl.when(jnp.logical_and(j > 0, j < last_j))(lambda: step(False, False))
    pl.when(j == last_j)(lambda: step(False, True))


def _ffn(x, mod, g, w1, w3, w2, layer, which, grp_fn, sh, sc, gt):
    rows = x.shape[0]
    tm, tf = FFN_TM, FFN_TF
    return pl.pallas_call(
        functools.partial(_ffn_kernel, sh=sh, sc=sc, gt=gt),
        grid=(rows // tm, D_FF // tf),
        in_specs=[
            pl.BlockSpec((tm, D_MODEL), lambda i, j: (i, 0)),
            pl.BlockSpec((1, N_ADA, D_MODEL), lambda i, j: (grp_fn(i, tm), 0, 0)),
            pl.BlockSpec((1, D_MODEL), lambda i, j: (0, 0)),
            pl.BlockSpec((None, None, D_MODEL, tf), lambda i, j: (layer, which, 0, j)),
            pl.BlockSpec((None, None, D_MODEL, tf), lambda i, j: (layer, which, 0, j)),
            pl.BlockSpec((None, None, tf, D_MODEL), lambda i, j: (layer, which, j, 0)),
        ],
        out_specs=pl.BlockSpec((tm, D_MODEL), lambda i, j: (i, 0)),
        out_shape=jax.ShapeDtypeStruct((rows, D_MODEL), F32),
        scratch_shapes=[pltpu.VMEM((tm, D_MODEL), BF16), pltpu.VMEM((tm, D_MODEL), F32)],
        compiler_params=_cparams(("arbitrary", "arbitrary")),
        name="ffn_half_step",
    )(x, mod, g, w1, w3, w2)


def _head_norm(xs, bsum, g):
    normed = []
    for p in range(xs.shape[1] // LANES):
        x = xs[:, p * LANES:(p + 1) * LANES]
        sq_hi, sq_lo = _split(x * x)
        ms = _dot(sq_hi, bsum) + _dot(sq_lo, bsum)
        normed.append(x * lax.rsqrt(ms + EPS))
    return jnp.concatenate(normed, axis=-1) * g


def _rope(xs, cos, sin, first_half):
    swapped = jnp.where(first_half, pltpu.roll(xs, LANES - 16, 1), pltpu.roll(xs, 16, 1))
    return xs * cos + swapped * sin


def _dup_heads(x2, low_half):
    sw = pltpu.roll(x2, HEAD_DIM, 1)
    return jnp.where(low_half, x2, sw), jnp.where(low_half, sw, x2)


def _expand_kv(x2, low_half):
    d0, d1 = _dup_heads(x2, low_half)
    return jnp.concatenate([d0, x2, d1], axis=-1)


def _store_rows(ref, val, cache_layer):
    if len(ref.shape) == 2:
        ref[...] = val.astype(ref.dtype)
        return
    seq = ref.shape[-2]
    for s in range(ref.shape[0]):
        rows = val[s * seq:(s + 1) * seq].astype(ref.dtype)
        if len(ref.shape) == 3:
            ref[s] = rows
        else:
            for l in range(ref.shape[1]):
                ref[s, l] = rows if l == cache_layer else jnp.zeros_like(rows)


def _proj_kernel(*refs, rope, cache_layer, n_alias, seq_len):
    x_ref, mod_ref, g_ref, w_ref, bsum_ref, gains_ref, xprev_ref, xnext_ref, cw_ref, cb_ref = refs[:10]
    pos = 10
    if rope:
        cos_ref, sin_ref = refs[pos:pos + 2]
        pos += 2
    pos += n_alias
    x0_ref, z_ref, qn_ref, kn_ref, vn_ref, qg_ref, kge_ref, vge_ref = refs[pos:pos + 8]
    pos += 8
    emit_kv = cache_layer is not None
    if emit_kv:
        kg_ref, vg_ref = refs[pos:pos + 2]
        pos += 2
    u_scr = refs[pos]
    scale = HEAD_DIM ** -0.5
    tm = x_ref.shape[0]

    def projected(x, cols):
        h = _mod_norm(x, g_ref[...], mod_ref[0, SH2:SH2 + 1, :], mod_ref[0, SC2:SC2 + 1, :])
        return _dot(h.astype(BF16), w_ref[:, cols])

    u_scr[...] = projected(x_ref[...], slice(0, IN_WIDTH))

    hy_cols = slice(0, OFF_NA)
    halo_prev = projected(xprev_ref[...], hy_cols)[7:8]
    halo_next = projected(xnext_ref[...], hy_cols)[0:1]
    row = lax.broadcasted_iota(jnp.int32, (tm, 1), 0)
    t_pos = (pl.program_id(0) * tm + row) & (seq_len - 1)

    def conv_chunk(ci):
        sl = slice(ci * HY_CH, (ci + 1) * HY_CH)
        u = u_scr[:, sl]
        prev = jnp.where(row == 0, halo_prev[:, sl], pltpu.roll(u, 1, 0))
        prev = jnp.where(t_pos == 0, 0.0, prev)
        nxt = jnp.where(row == tm - 1, halo_next[:, sl], pltpu.roll(u, tm - 1, 0))
        nxt = jnp.where(t_pos == seq_len - 1, 0.0, nxt)
        return prev * cw_ref[0:1, sl] + u * cw_ref[1:2, sl] + nxt * cw_ref[2:3, sl] + cb_ref[:, sl]

    x0_ref[...] = conv_chunk(0)
    z_ref[...] = conv_chunk(1) * conv_chunk(2)

    bsum = bsum_ref[...]
    qn = _head_norm(u_scr[:, OFF_NA:OFF_NA + NA_W], bsum, gains_ref[0:1, :])
    qn_ref[...] = (qn * scale).astype(qn_ref.dtype)
    kn = _head_norm(u_scr[:, OFF_NA + NA_W:OFF_NA + 2 * NA_W], bsum, gains_ref[1:2, :])
    _store_rows(kn_ref, kn, cache_layer)
    _store_rows(vn_ref, u_scr[:, OFF_NA + 2 * NA_W:OFF_GQ], cache_layer)

    qg = _head_norm(u_scr[:, OFF_GQ:OFF_GQ + GQ_W], bsum, gains_ref[2:3, :])
    kg = _head_norm(u_scr[:, OFF_GQ + GQ_W:OFF_GQ + GQ_W + GKV_W], bsum, gains_ref[3:4, 0:GKV_W])
    vg = u_scr[:, OFF_GQ + GQ_W + GKV_W:IN_WIDTH]
    if emit_kv:
        _store_rows(kg_ref, kg, cache_layer)
        _store_rows(vg_ref, vg, cache_layer)

    lane = lax.broadcasted_iota(jnp.int32, (1, LANES), 1)
    if rope:
        cos = cos_ref[...]
        sin = sin_ref[...]
        first_half = (lane % 32) < 16
        for p in range(GQ_W // LANES):
            sl = slice(p * LANES, (p + 1) * LANES)
            qg_ref[:, sl] = (_rope(qg[:, sl], cos, sin, first_half) * scale).astype(qg_ref.dtype)
        kg = _rope(kg, cos, sin, first_half)
    else:
        qg_ref[...] = (qg * scale).astype(qg_ref.dtype)

    low_half = lane < HEAD_DIM
    kge_ref[...] = _expand_kv(kg, low_half).astype(kge_ref.dtype)
    vge_ref[...] = _expand_kv(vg, low_half).astype(vge_ref.dtype)


def _project(x, mod, g, w_bf, bsum, gains, conv_w, conv_b, seq_len, grp_fn, rope_tabs, attn_dtype,
             cache_layer=None, prev_cache=None):
    rows = x.shape[0]
    tm = ROW_TM
    rope = rope_tabs is not None
    assert seq_len & (seq_len - 1) == 0 and (seq_len % tm == 0 or tm % seq_len == 0)
    sub = 8
    in_specs = [
        pl.BlockSpec((tm, D_MODEL), lambda i: (i, 0)),
        pl.BlockSpec((1, N_ADA, D_MODEL), lambda i: (grp_fn(i, tm), 0, 0)),
        pl.BlockSpec((1, D_MODEL), lambda i: (0, 0)),
        pl.BlockSpec((D_MODEL, IN_WIDTH), lambda i: (0, 0)),
        pl.BlockSpec((LANES, LANES), lambda i: (0, 0)),
        pl.BlockSpec((4, NA_W), lambda i: (0, 0)),
        pl.BlockSpec((sub, D_MODEL), lambda i: (jnp.maximum(i * (tm // sub) - 1, 0), 0)),
        pl.BlockSpec((sub, D_MODEL), lambda i: (jnp.minimum((i + 1) * (tm // sub), rows // sub - 1), 0)),
        pl.BlockSpec((3, OFF_NA), lambda i: (0, 0)),
        pl.BlockSpec((1, OFF_NA), lambda i: (0, 0)),
    ]
    args = [x, mod, g, w_bf, bsum, gains, x, x, conv_w, conv_b.reshape(1, OFF_NA)]
    if rope:
        seq_tiles = DEC_SEQ // tm
        in_specs += [pl.BlockSpec((tm, LANES), lambda i: (i % seq_tiles, 0))] * 2
        args += list(rope_tabs)
    outs = [(HY_CH, F32)] * 2 + [(NA_W, attn_dtype)] * 3 + [(GQ_W, attn_dtype)] * 3
    if cache_layer is not None:
        outs += [(GKV_W, F32), (GKV_W, F32)]
    out_specs = [pl.BlockSpec((tm, w), lambda i: (i, 0)) for w, _ in outs]
    out_shape = [jax.ShapeDtypeStruct((rows, w), dt) for w, dt in outs]
    aliases = {}
    if cache_layer is not None:
        cache_outs = (3, 4, 8, 9)
        seqs = tm // SEQ
        for o in cache_outs:
            w = outs[o][0]
            out_shape[o] = jax.ShapeDtypeStruct((BATCH, DEPTH, SEQ, w), F32)
            if prev_cache is None:
                out_specs[o] = pl.BlockSpec((seqs, DEPTH, SEQ, w), lambda i: (i, 0, 0, 0))
            else:
                out_specs[o] = pl.BlockSpec((seqs, None, SEQ, w), lambda i: (i, cache_layer, 0, 0))
        if prev_cache is not None:
            aliases = {len(args) + n: o for n, o in enumerate(cache_outs)}
            in_specs += [pl.BlockSpec(memory_space=pl.ANY)] * len(cache_outs)
            args += list(prev_cache)
    return pl.pallas_call(
        functools.partial(_proj_kernel, rope=rope, cache_layer=cache_layer, n_alias=len(aliases), seq_len=seq_len),
        grid=(rows // tm,),
        in_specs=in_specs,
        out_specs=out_specs,
        out_shape=out_shape,
        input_output_aliases=aliases,
        scratch_shapes=[pltpu.VMEM((tm, IN_WIDTH), F32)],
        compiler_params=_cparams(("arbitrary",)),
        name="mixer_in_proj",
    )(*args)


HEADS_PER_CHAIN = 2
CHAIN_W = HEADS_PER_CHAIN * HEAD_DIM


def _log2(n):
    assert n & (n - 1) == 0
    return n.bit_length() - 1


def _stack_heads(q, n_heads):
    m, w = q.shape
    rows = lax.broadcasted_iota(jnp.int32, (n_heads * m, w), 0)
    lanes = lax.broadcasted_iota(jnp.int32, (n_heads * m, w), 1)
    own = jnp.right_shift(rows, _log2(m)) == jnp.right_shift(lanes, _log2(HEAD_DIM))
    return jnp.where(own, jnp.concatenate([q] * n_heads, axis=0), jnp.zeros((), q.dtype))


def _unstack_heads(o, n_heads):
    m = o.shape[0] // n_heads
    head = jnp.right_shift(lax.broadcasted_iota(jnp.int32, (m, o.shape[1]), 1), _log2(HEAD_DIM))
    out = o[0:m]
    for h in range(1, n_heads):
        out = jnp.where(head == h, o[h * m:(h + 1) * m], out)
    return out


def _sink_column(sink_ref, m, first_head, n_heads):
    block = jnp.right_shift(lax.broadcasted_iota(jnp.int32, (n_heads * m, 1), 0), _log2(m))
    col = jnp.full((n_heads * m, 1), sink_ref[first_head], F32)
    for h in range(1, n_heads):
        col = jnp.where(block == h, sink_ref[first_head + h], col)
    return col


def _softmax_pv(scores, values, sink):
    m = scores[0].max(axis=-1, keepdims=True)
    for s in scores[1:]:
        m = jnp.maximum(m, s.max(axis=-1, keepdims=True))
    if sink is not None:
        m = jnp.maximum(m, sink)
    den = None
    acc = None
    for s, v in zip(scores, values):
        p = jnp.exp(s - m)
        d = p.sum(axis=-1, keepdims=True)
        o = _dot(p.astype(BF16), v)
        den = d if den is None else den + d
        acc = o if acc is None else acc + o
    if sink is not None:
        den = den + jnp.exp(sink - m)
    return acc / den


def _ctx_attn_kernel(sink_ref, qn_ref, kn_ref, vn_ref, qg_ref, kge_ref, vge_ref, yna_ref, ygq_ref):
    m = qn_ref.shape[1]
    low = lax.broadcasted_iota(jnp.int32, (m, LANES), 1) < HEAD_DIM
    for q_ref, k_ref, v_ref, y_ref, has_sink in ((qn_ref, kn_ref, vn_ref, yna_ref, False),
                                                  (qg_ref, kge_ref, vge_ref, ygq_ref, True)):
        for p in range(NA_W // LANES):
            sl = slice(p * LANES, (p + 1) * LANES)
            q2 = q_ref[0, :, sl].astype(BF16)
            k2 = k_ref[0, :, sl].astype(BF16)
            v2 = v_ref[0, :, sl].astype(BF16)
            halves = []
            for half, keep in enumerate((low, ~low)):
                qh = jnp.where(keep, q2, jnp.zeros((), BF16))
                sink = sink_ref[2 * p + half] if has_sink else None
                halves.append(_softmax_pv([_dot_nt(qh, k2)], [v2], sink))
            y_ref[0, :, sl] = jnp.where(low, halves[0], halves[1]).astype(y_ref.dtype)


def _ctx_attention(sink, qn, kn_cache, vn_cache, qg, kge, vge, layer):
    b, l, _ = qn.shape

    def spec(w):
        return pl.BlockSpec((1, l, w), lambda i: (i, 0, 0))

    cache_spec = pl.BlockSpec((1, None, l, NA_W), lambda i: (i, layer, 0, 0))
    return pl.pallas_call(
        _ctx_attn_kernel,
        grid=(b,),
        in_specs=[pl.BlockSpec(memory_space=pltpu.SMEM), spec(NA_W), cache_spec, cache_spec, spec(GQ_W),
                  spec(GQ_W), spec(GQ_W)],
        out_specs=[spec(NA_W), spec(GQ_W)],
        out_shape=[jax.ShapeDtypeStruct((b, l, NA_W), BF16), jax.ShapeDtypeStruct((b, l, GQ_W), BF16)],
        compiler_params=_cparams(("arbitrary",)),
        name="context_attention",
    )(sink, qn, kn_cache, vn_cache, qg, kge, vge)


NA_KEYS = NA_ROWS * GRID_W
NA_ROWS_PER_STEP = 8


def _na_bias_kernel(r_ref, oh_ref, mk_ref, o_ref):
    o_ref[...] = jnp.dot(r_ref[...], oh_ref[...], precision=_HIGHEST, preferred_element_type=F32) + mk_ref[...]


def _na_bias_tables():
    q = np.arange(GRID_W)[:, None]
    kc = np.arange(GRID_W)[None, :]
    win_lo = np.clip(q - NA_COLS // 2, 0, GRID_W - NA_COLS)
    ok = (kc >= win_lo) & (kc < win_lo + NA_COLS)
    j = kc - q + NA_COLS - 1
    onehot = np.zeros((LANES, GRID_W * GRID_W), np.float32)
    qq, kk = np.nonzero(ok)
    onehot[j[qq, kk], qq * GRID_W + kk] = 1.0
    mask = np.where(ok, 0.0, NEG_INF).astype(np.float32).reshape(1, GRID_W * GRID_W)
    return onehot, mask


def _na_bias(rpb_l):
    n_dr = 2 * NA_ROWS - 1
    rows = NA_HEADS * n_dr
    rows_pad = -(-rows // 8) * 8
    r = jnp.zeros((rows_pad, LANES), F32).at[:rows, :2 * NA_COLS - 1].set(rpb_l.reshape(rows, 2 * NA_COLS - 1))
    onehot, mask = _na_bias_tables()
    out = pl.pallas_call(
        _na_bias_kernel,
        out_shape=jax.ShapeDtypeStruct((rows_pad, GRID_W * GRID_W), F32),
        compiler_params=pltpu.CompilerParams(vmem_limit_bytes=VMEM_LIMIT),
        name="na_bias_expand",
    )(r, jnp.asarray(onehot), jnp.asarray(mask))
    bm = out[:rows].reshape(NA_HEADS, n_dr, GRID_W, GRID_W)
    bm2 = jnp.concatenate([bm[:, :-1], bm[:, 1:]], axis=-1)
    return bm2.transpose(1, 0, 2, 3).reshape(n_dr - 1, NA_HEADS * GRID_W, 2 * GRID_W)


def _na_attn_kernel(q_ref, k_ref, v_ref, kc_ref, vc_ref, bm_ref, o_ref):
    n_rows = DEC_SEQ // GRID_W
    kc = kc_ref[0, 0].astype(BF16)
    vc = vc_ref[0, 0].astype(BF16)
    for rr in range(NA_ROWS_PER_STEP):
        r = pl.program_id(1) * NA_ROWS_PER_STEP + rr
        start = jnp.clip(r - NA_ROWS // 2, 0, n_rows - NA_ROWS)
        shift = r - start
        row0 = pl.multiple_of(start * GRID_W, GRID_W)
        rows = slice(rr * GRID_W, (rr + 1) * GRID_W)
        q = _stack_heads(q_ref[0, rows, :], NA_HEADS)
        k = k_ref[0, pl.ds(row0, NA_KEYS), :]
        v = v_ref[0, pl.ds(row0, NA_KEYS), :]
        bias = jnp.concatenate([bm_ref[2 * jj - shift + NA_ROWS - 1] for jj in range(NA_ROWS // 2)], axis=-1)
        s_loc = _dot_nt(q, k) + bias
        s_ctx = _dot_nt(q, kc)
        o = _softmax_pv([s_loc, s_ctx], [v, vc], None)
        o_ref[0, rows, :] = _unstack_heads(o, NA_HEADS).astype(o_ref.dtype)


def _na_attention(q, k, v, kc, vc, bm2, layer):
    b, n, _ = q.shape
    tq = NA_ROWS_PER_STEP * GRID_W
    return pl.pallas_call(
        _na_attn_kernel,
        grid=(b, n // tq),
        in_specs=[
            pl.BlockSpec((1, tq, NA_W), lambda i, r: (i, r, 0)),
            pl.BlockSpec((1, n, NA_W), lambda i, r: (i, 0, 0)),
            pl.BlockSpec((1, n, NA_W), lambda i, r: (i, 0, 0)),
            pl.BlockSpec((1, 1, PAST_LEN, NA_W), lambda i, r: (i, layer, 0, 0)),
            pl.BlockSpec((1, 1, PAST_LEN, NA_W), lambda i, r: (i, layer, 0, 0)),
            pl.BlockSpec(bm2.shape, lambda i, r: (0, 0, 0)),
        ],
        out_specs=pl.BlockSpec((1, tq, NA_W), lambda i, r: (i, r, 0)),
        out_shape=jax.ShapeDtypeStruct((b, n, NA_W), BF16),
        compiler_params=_cparams(("arbitrary", "arbitrary")),
        name="neighbourhood_attention",
    )(q, k, v, kc, vc, bm2)


WIN_KEYS = 3 * BLK
WIN_BLOCKS_PER_STEP = 4


def _win_attn_kernel(sink_ref, mask_ref, q_ref, kge_ref, vge_ref, kc_ref, vc_ref, o_ref):
    n = kge_ref.shape[1]
    n_blocks = n // BLK
    low_c = lax.broadcasted_iota(jnp.int32, (1, LANES), 1) < HEAD_DIM
    kce = _expand_kv(kc_ref[0, 0], low_c).astype(BF16)
    vce = _expand_kv(vc_ref[0, 0], low_c).astype(BF16)
    n_chains = GQA_Q_HEADS // HEADS_PER_CHAIN
    sinks = [_sink_column(sink_ref, BLK, p * HEADS_PER_CHAIN, HEADS_PER_CHAIN) for p in range(n_chains)]
    for bb in range(WIN_BLOCKS_PER_STEP):
        nb = pl.program_id(1) * WIN_BLOCKS_PER_STEP + bb
        start = pl.multiple_of(jnp.clip((nb - 1) * BLK, 0, n - WIN_KEYS), BLK)
        variant = jnp.where(nb == 0, 0, jnp.where(nb == n_blocks - 1, 2, 1))
        rows = slice(bb * BLK, (bb + 1) * BLK)
        for p in range(n_chains):
            sl = slice(p * CHAIN_W, (p + 1) * CHAIN_W)
            q = _stack_heads(q_ref[0, rows, sl], HEADS_PER_CHAIN)
            s_loc = _dot_nt(q, kge_ref[0, pl.ds(start, WIN_KEYS), sl]) + mask_ref[variant]
            s_ctx = _dot_nt(q, kce[:, sl])
            o = _softmax_pv([s_loc, s_ctx], [vge_ref[0, pl.ds(start, WIN_KEYS), sl], vce[:, sl]], sinks[p])
            o_ref[0, rows, sl] = _unstack_heads(o, HEADS_PER_CHAIN).astype(o_ref.dtype)


def _win_mask_table():
    qi = np.arange(BLK)[:, None]
    kj = np.arange(WIN_KEYS)[None, :]
    tabs = [np.where(np.abs(qi + off - kj) <= GQA_WINDOW, 0.0, NEG_INF) for off in (0, BLK, 2 * BLK)]
    return jnp.asarray(np.stack([np.tile(t, (HEADS_PER_CHAIN, 1)) for t in tabs]).astype(np.float32))


def _win_attention(sink, q, kge, vge, kc, vc, layer):
    b, n, _ = q.shape
    mask = _win_mask_table()
    tq = WIN_BLOCKS_PER_STEP * BLK
    return pl.pallas_call(
        _win_attn_kernel,
        grid=(b, n // tq),
        in_specs=[
            pl.BlockSpec(memory_space=pltpu.SMEM),
            pl.BlockSpec(mask.shape, lambda i, j: (0, 0, 0)),
            pl.BlockSpec((1, tq, GQ_W), lambda i, j: (i, j, 0)),
            pl.BlockSpec((1, n, GQ_W), lambda i, j: (i, 0, 0)),
            pl.BlockSpec((1, n, GQ_W), lambda i, j: (i, 0, 0)),
            pl.BlockSpec((1, 1, PAST_LEN, GKV_W), lambda i, j: (i, layer, 0, 0)),
            pl.BlockSpec((1, 1, PAST_LEN, GKV_W), lambda i, j: (i, layer, 0, 0)),
        ],
        out_specs=pl.BlockSpec((1, tq, GQ_W), lambda i, j: (i, j, 0)),
        out_shape=jax.ShapeDtypeStruct((b, n, GQ_W), BF16),
        compiler_params=_cparams(("arbitrary", "arbitrary")),
        name="window_attention",
    )(sink, mask, q, kge, vge, kc, vc)


@functools.lru_cache(maxsize=None)
def _hyena_consts_np(L):
    n = 2 * L
    k = np.arange(L, dtype=np.int64)
    ang = (2.0 * np.pi / n) * ((k[:, None] * k[None, :]) % n).astype(np.float64)

    cm = np.cos(ang).astype(np.float32)
    sm = (-np.sin(ang)).astype(np.float32)
    idx = np.arange(L, dtype=np.float32)
    t = idx / np.float32(L - 1)
    bands = np.linspace(1e-4, HY_BANDS - 1, HY_BANDS, dtype=np.float32)
    fang = np.float32(2.0 * math.pi / L) * idx[:, None] * bands[None, :]
    feats = np.zeros((L, LANES), np.float32)
    feats[:, 0] = t
    feats[:, 1:1 + HY_BANDS] = np.cos(fang)
    feats[:, 1 + HY_BANDS:HY_EMB] = -np.sin(fang)
    max_decay = math.log(HY_DECAY_TARGET) / HY_FAST_PCT
    min_decay = math.log(HY_DECAY_TARGET) / HY_SLOW_PCT
    deltas = np.abs(np.linspace(min_decay, max_decay, HY_CH, dtype=np.float32))
    decay = np.exp(-t[:, None] * deltas[None, :]).astype(np.float32)
    decay2 = np.concatenate([decay, decay], axis=1)
    return cm, sm, feats, decay2


def _hyena_consts(L):
    cm, sm, feats, decay2 = _hyena_consts_np(L)
    return jnp.asarray(cm), jnp.asarray(sm), jnp.asarray(feats), jnp.asarray(decay2)


def _hy_filter_kernel(feats_ref, w1_ref, b1_ref, w2_ref, b2_ref, w3_ref, freq_ref, decay_ref, wsum_ref, wdiff_ref):
    def dot_hi(a, b):
        a_hi, a_lo = _split(a)
        b_hi, b_lo = _split(b)
        return _dot(a_hi, b_hi) + (_dot(a_lo, b_hi) + _dot(a_hi, b_lo))

    hid = jnp.sin(freq_ref[0:1, :] * (dot_hi(feats_ref[...], w1_ref[...]) + b1_ref[...]))
    hid = jnp.sin(freq_ref[1:2, :] * (dot_hi(hid, w2_ref[...]) + b2_ref[...]))
    taps = dot_hi(hid, w3_ref[...]) * decay_ref[...]
    fwd = taps[:, 0:HY_CH]
    bwd = taps[:, HY_CH:2 * HY_CH]
    row = lax.broadcasted_iota(jnp.int32, bwd.shape, 0)
    bwd = jnp.where(row == 0, 0.0, bwd)
    wsum_ref[...] = fwd + bwd
    wdiff_ref[...] = fwd - bwd


def _hy_filter(L, feats, decay2, w1, b1, w2, b2, w3, freq):
    w1p = jnp.zeros((LANES, HY_FILT_W), F32).at[:HY_EMB].set(w1)
    return pl.pallas_call(
        _hy_filter_kernel,
        out_shape=[jax.ShapeDtypeStruct((L, HY_CH), F32)] * 2,
        compiler_params=pltpu.CompilerParams(vmem_limit_bytes=VMEM_LIMIT),
        name="hyena_filter",
    )(feats, w1p, b1.reshape(1, -1), w2, b2.reshape(1, -1), w3, freq, decay2)


def _hy_conv_kernel(z_ref, x0_ref, wsum_ref, wdiff_ref, skip_ref, cr_ref, sr_ref, cc_ref, sc_ref,
                    o_ref, z_scr, ws_scr, wd_scr, acc_scr, *, group):
    t = pl.program_id(1)
    L = z_ref.shape[1]
    n = 2 * L

    @pl.when(t == 0)
    def _():
        z_scr[...] = z_ref[...].astype(BF16)
        ws_scr[...] = wsum_ref[...].astype(BF16)
        wd_scr[...] = wdiff_ref[...].astype(BF16)
        acc_scr[...] = jnp.zeros_like(acc_scr)

    cr, sr = cr_ref[...].astype(BF16), sr_ref[...].astype(BF16)
    cc, sc = cc_ref[...].astype(BF16), sc_ref[...].astype(BF16)
    k_re = _dot(cr, ws_scr[...])
    k_im = _dot(sr, wd_scr[...])
    for b in range(group):
        zb = z_scr[b]
        z_re = _dot(cr, zb)
        z_im = _dot(sr, zb)
        y_re = (z_re * k_re - z_im * k_im).astype(BF16)
        y_im = (z_re * k_im + z_im * k_re).astype(BF16)
        acc_scr[b] += _dot(cc, y_re) + _dot(sc, y_im)

    @pl.when(t == pl.num_programs(1) - 1)
    def _():
        row = lax.broadcasted_iota(jnp.int32, (L, 1), 0)
        sgn = (1 - 2 * (row % 2)).astype(F32)
        w = wsum_ref[...]
        w_dc = w.sum(axis=0, keepdims=True)
        w_ny = (w * sgn).sum(axis=0, keepdims=True)
        for b in range(group):
            z = z_ref[b]
            z_dc = z.sum(axis=0, keepdims=True)
            z_ny = (z * sgn).sum(axis=0, keepdims=True)
            conv = (2.0 / n) * acc_scr[b] - (1.0 / n) * (z_dc * w_dc) + (1.0 / n) * (sgn * (z_ny * w_ny))
            o_ref[b] = (x0_ref[b] * (conv + z * skip_ref[...])).astype(o_ref.dtype)


def _hy_conv(z, x0, wsum, wdiff, skip, cm, sm, group):
    b, L, c = z.shape
    tkf = min(HY_TKF, L)
    row_spec = pl.BlockSpec((tkf, L), lambda g, t: (t, 0))
    col_spec = pl.BlockSpec((L, tkf), lambda g, t: (0, t))
    once = pl.Buffered(1)
    seq_spec = pl.BlockSpec((group, L, c), lambda g, t: (g, 0, 0), pipeline_mode=once)
    w_spec = pl.BlockSpec((L, c), lambda g, t: (0, 0), pipeline_mode=once)
    return pl.pallas_call(
        functools.partial(_hy_conv_kernel, group=group),
        grid=(b // group, L // tkf),
        in_specs=[seq_spec, seq_spec, w_spec, w_spec, pl.BlockSpec((1, c), lambda g, t: (0, 0)),
                  row_spec, row_spec, col_spec, col_spec],
        out_specs=pl.BlockSpec((group, L, c), lambda g, t: (g, 0, 0)),
        out_shape=jax.ShapeDtypeStruct((b, L, c), BF16),
        scratch_shapes=[pltpu.VMEM((group, L, c), BF16), pltpu.VMEM((L, c), BF16), pltpu.VMEM((L, c), BF16),
                        pltpu.VMEM((group, L, c), F32)],
        compiler_params=_cparams(("arbitrary", "arbitrary")),
        name="hyena_long_conv",
    )(z, x0, wsum, wdiff, skip.reshape(1, c), cm, sm, cm, sm)


def _merge_kernel(x_ref, yhy_ref, yna_ref, ygq_ref, mod_ref, w_ref, o_ref):
    y = jnp.concatenate([yhy_ref[...], yna_ref[...], ygq_ref[...]], axis=-1)
    o_ref[...] = x_ref[...] + mod_ref[0, G2:G2 + 1, :] * _dot(y, w_ref[...])


def _merge(x, y_hy, y_na, y_gq, mod, w_bf, grp_fn):
    rows = x.shape[0]
    tm = ROW_TM
    return pl.pallas_call(
        _merge_kernel,
        grid=(rows // tm,),
        in_specs=[
            pl.BlockSpec((tm, D_MODEL), lambda i: (i, 0)),
            pl.BlockSpec((tm, HY_CH), lambda i: (i, 0)),
            pl.BlockSpec((tm, NA_W), lambda i: (i, 0)),
            pl.BlockSpec((tm, GQ_W), lambda i: (i, 0)),
            pl.BlockSpec((1, N_ADA, D_MODEL), lambda i: (grp_fn(i, tm), 0, 0)),
            pl.BlockSpec((D_MODEL, D_MODEL), lambda i: (0, 0)),
        ],
        out_specs=pl.BlockSpec((tm, D_MODEL), lambda i: (i, 0)),
        out_shape=jax.ShapeDtypeStruct((rows, D_MODEL), F32),
        compiler_params=_cparams(("arbitrary",)),
        name="mixer_out_proj",
    )(x, y_hy, y_na, y_gq, mod, w_bf)


def _rope_tables():
    pos = np.arange(DEC_SEQ)
    quarter = HEAD_DIM // 4
    inv = ROPE_BASE ** (-np.arange(quarter, dtype=np.float64) / quarter)
    ang_r = (pos // GRID_W)[:, None] * inv[None, :]
    ang_c = (pos % GRID_W)[:, None] * inv[None, :]
    cos_h = np.concatenate([np.cos(ang_r), np.cos(ang_r), np.cos(ang_c), np.cos(ang_c)], axis=1)
    sin_h = np.concatenate([-np.sin(ang_r), np.sin(ang_r), -np.sin(ang_c), np.sin(ang_c)], axis=1)
    reps = LANES // HEAD_DIM
    return (jnp.asarray(np.tile(cos_h, (1, reps)), F32), jnp.asarray(np.tile(sin_h, (1, reps)), F32))


def _head_sum_matrix():
    idx = np.arange(LANES) // HEAD_DIM
    return jnp.asarray((idx[:, None] == idx[None, :]).astype(np.float32) / HEAD_DIM, BF16)


def _grp_prompt(i, tm):
    return 0


def _grp_sample(i, tm):
    return 1 + (i * tm) // DEC_SEQ


def kernel(x_prompt, x_sample, cache_na_k, cache_na_v, cache_gqa_k, cache_gqa_v, c, c_ctx, ada_w, ada_b, norm_g, ffn_w1, ffn_w3, ffn_w2, w_in, w_out, hy_conv_w, hy_conv_b, hy_filt_w1, hy_filt_b1, hy_filt_w2, hy_filt_b2, hy_filt_w3, hy_freq, hy_skip, na_q_g, na_k_g, na_rpb, gqa_q_g, gqa_k_g, gqa_sink):
    xp = x_prompt.reshape(BATCH * SEQ, D_MODEL)
    xs = x_sample.reshape(DEC_BATCH * DEC_SEQ, D_MODEL)
    cc = jnp.zeros((8, D_MODEL), F32).at[0].set(c_ctx).at[1:1 + DEC_BATCH].set(c)
    kc_na = cache_na_k.reshape(DEC_BATCH, DEPTH, PAST_LEN, NA_W)
    vc_na = cache_na_v.reshape(DEC_BATCH, DEPTH, PAST_LEN, NA_W)
    kc_gq = cache_gqa_k.reshape(DEC_BATCH, DEPTH, PAST_LEN, GKV_W)
    vc_gq = cache_gqa_v.reshape(DEC_BATCH, DEPTH, PAST_LEN, GKV_W)

    rope_tabs = _rope_tables()
    bsum = _head_sum_matrix()
    hy_p = _hyena_consts(SEQ)
    hy_s = _hyena_consts(DEC_SEQ)

    cache = None
    for l in range(DEPTH):
        mod = _adaln_mod(cc, ada_w, ada_b, l)
        w_in_bf = _cast_bf16(w_in, l)
        w_out_bf = _cast_bf16(w_out, l)
        gains = jnp.stack([jnp.tile(na_q_g[l], NA_HEADS), jnp.tile(na_k_g[l], NA_HEADS),
                           jnp.tile(gqa_q_g[l], GQA_Q_HEADS), jnp.tile(gqa_k_g[l], GQA_Q_HEADS)])
        bm2 = _na_bias(na_rpb[l])
        ng = [norm_g[l, i].reshape(1, D_MODEL) for i in range(3)]
        filt = {}
        for L, consts in ((SEQ, hy_p), (DEC_SEQ, hy_s)):
            filt[L] = _hy_filter(L, consts[2], consts[3], hy_filt_w1[l], hy_filt_b1[l], hy_filt_w2[l],
                                 hy_filt_b2[l], hy_filt_w3[l], hy_freq[l])

        xp = _ffn(xp, mod, ng[0], ffn_w1, ffn_w3, ffn_w2, l, 0, _grp_prompt, SH1, SC1, G1)
        x0, z, qn, kn, vn, qg, kgd, vgd, kg, vg = _project(
            xp, mod, ng[1], w_in_bf, bsum, gains, hy_conv_w[l], hy_conv_b[l], SEQ, _grp_prompt, None, F32,
            cache_layer=l, prev_cache=cache)
        cache = (kn, vn, kg, vg)
        shp = lambda a: a.reshape(BATCH, SEQ, a.shape[-1])
        y_hy = _hy_conv(shp(z), shp(x0), filt[SEQ][0], filt[SEQ][1], hy_skip[l], hy_p[0], hy_p[1], group=4)
        y_na, y_gq = _ctx_attention(gqa_sink[l], shp(qn), kn, vn, shp(qg), shp(kgd), shp(vgd), l)
        xp = _merge(xp, y_hy.reshape(-1, HY_CH), y_na.reshape(-1, NA_W), y_gq.reshape(-1, GQ_W), mod, w_out_bf,
                    _grp_prompt)
        xp = _ffn(xp, mod, ng[2], ffn_w1, ffn_w3, ffn_w2, l, 1, _grp_prompt, SH3, SC3, G3)

        xs = _ffn(xs, mod, ng[0], ffn_w1, ffn_w3, ffn_w2, l, 0, _grp_sample, SH1, SC1, G1)
        x0, z, qn, kn, vn, qg, kgd, vgd = _project(
            xs, mod, ng[1], w_in_bf, bsum, gains, hy_conv_w[l], hy_conv_b[l], DEC_SEQ, _grp_sample, rope_tabs, BF16)
        shs = lambda a: a.reshape(DEC_BATCH, DEC_SEQ, a.shape[-1])
        y_hy = _hy_conv(shs(z), shs(x0), filt[DEC_SEQ][0], filt[DEC_SEQ][1], hy_skip[l], hy_s[0], hy_s[1],
                        group=DEC_BATCH)
        y_na = _na_attention(shs(qn), shs(kn), shs(vn), kc_na, vc_na, bm2, l)
        y_gq = _win_attention(gqa_sink[l], shs(qg), shs(kgd), shs(vgd), kc_gq, vc_gq, l)
        xs = _merge(xs, y_hy.reshape(-1, HY_CH), y_na.reshape(-1, NA_W), y_gq.reshape(-1, GQ_W), mod, w_out_bf,
                    _grp_sample)
        xs = _ffn(xs, mod, ng[2], ffn_w1, ffn_w3, ffn_w2, l, 1, _grp_sample, SH3, SC3, G3)

    kn, vn, kg, vg = cache
    return (xp.reshape(BATCH, SEQ, D_MODEL), xs.reshape(DEC_BATCH, DEC_SEQ, D_MODEL),
            kn.reshape(BATCH, DEPTH, SEQ, NA_HEADS, HEAD_DIM), vn.reshape(BATCH, DEPTH, SEQ, NA_HEADS, HEAD_DIM),
            kg.reshape(BATCH, DEPTH, SEQ, GQA_KV_HEADS, HEAD_DIM), vg.reshape(BATCH, DEPTH, SEQ, GQA_KV_HEADS, HEAD_DIM))
```

```python
import functools
import math

import numpy as np
import jax
import jax.numpy as jnp
from jax import lax
from jax.experimental import pallas as pl
from jax.experimental.pallas import tpu as pltpu

F32 = jnp.float32
BF16 = jnp.bfloat16

D_MODEL = 1024
BATCH = 16
SEQ = 256
DEPTH = 2
DEC_BATCH = 2
DEC_SEQ = 2048
PAST_LEN = 256
GRID_W = 64
HEAD_DIM = 64
HY_CH = 256
NA_HEADS = 6
GQA_Q_HEADS = 6
GQA_KV_HEADS = 2
NA_W = NA_HEADS * HEAD_DIM
GQ_W = GQA_Q_HEADS * HEAD_DIM
GKV_W = GQA_KV_HEADS * HEAD_DIM
IN_WIDTH = 3 * HY_CH + 3 * NA_W + GQ_W + 2 * GKV_W
D_FF = 2816
N_ADA = 9
HY_BANDS = 16
HY_EMB = 1 + 2 * HY_BANDS
HY_FILT_W = 64
HY_DECAY_TARGET = 1e-2
HY_FAST_PCT = 0.3
HY_SLOW_PCT = 1.5
NA_ROWS = 8
NA_COLS = 16
GQA_WINDOW = 128
BLK = 128
ROPE_BASE = 10000.0
EPS = 1e-6
NEG_INF = -1e30

LANES = 128
V7X_VMEM_BYTES = 64 * 1024 * 1024
VMEM_LIMIT = V7X_VMEM_BYTES * 15 // 16

OFF_NA = 3 * HY_CH
OFF_GQ = OFF_NA + 3 * NA_W

SH1, SC1, G1, SH2, SC2, G2, SH3, SC3, G3 = range(N_ADA)

FFN_TM = 2048
FFN_TF = 256
FFN_ROW_GROUPS = 4
ROW_TM = 512
MERGE_TM = 1024
HY_TKF = 256

_HIGHEST = lax.Precision.HIGHEST


def _cparams(sem):
    return pltpu.CompilerParams(dimension_semantics=sem, vmem_limit_bytes=VMEM_LIMIT)


def _silu(x):
    return x * (1.0 / (1.0 + jnp.exp(-x)))


def _dot(a, b):
    return jnp.dot(a, b, preferred_element_type=F32)


def _dot_nt(a, b):
    return lax.dot_general(a, b, (((1,), (1,)), ((), ())), preferred_element_type=F32)


def _split(x):
    hi = x.astype(BF16)
    lo = (x - hi.astype(F32)).astype(BF16)
    return hi, lo


def _mod_norm(x, g, shift, scale):
    ms = jnp.mean(x * x, axis=-1, keepdims=True)
    return (x * lax.rsqrt(ms + EPS) * g) * (1.0 + scale) + shift


def _mod_kernel(c_ref, w_ref, b_ref, o_ref):
    a = _silu(c_ref[...]).astype(BF16)
    o_ref[...] = _dot(a, w_ref[...].astype(BF16)) + b_ref[...]


def _adaln_mod(cc, ada_w, ada_b, layer):
    tn = D_MODEL
    n_out = N_ADA * D_MODEL
    out = pl.pallas_call(
        _mod_kernel,
        grid=(n_out // tn,),
        in_specs=[
            pl.BlockSpec((8, D_MODEL), lambda j: (0, 0)),
            pl.BlockSpec((None, D_MODEL, tn), lambda j: (layer, 0, j)),
            pl.BlockSpec((None, 1, tn), lambda j: (layer, 0, j)),
        ],
        out_specs=pl.BlockSpec((8, tn), lambda j: (0, j)),
        out_shape=jax.ShapeDtypeStruct((8, n_out), F32),
        compiler_params=_cparams(("arbitrary",)),
        name="adaln_mod",
    )(cc, ada_w, ada_b.reshape(DEPTH, 1, n_out))
    return out.reshape(8, N_ADA, D_MODEL)


def _ffn_kernel(x_ref, mod_ref, g_ref, w1_ref, w3_ref, w2_ref, o_ref, h_scr, acc_scr, *, sh, sc, gt):
    j = pl.program_id(1)
    last_j = pl.num_programs(1) - 1
    rows = h_scr.shape[0] // FFN_ROW_GROUPS

    def step(first, last):
        w1 = w1_ref[...].astype(BF16)
        w3 = w3_ref[...].astype(BF16)
        w2 = w2_ref[...].astype(BF16)
        for s in range(FFN_ROW_GROUPS):
            sl = slice(s * rows, (s + 1) * rows)
            if first:
                h = _mod_norm(x_ref[sl, :], g_ref[...], mod_ref[0, sh:sh + 1, :], mod_ref[0, sc:sc + 1, :]).astype(BF16)
                h_scr[sl] = h
            else:
                h = h_scr[sl]
            act = (_silu(_dot(h, w1)) * _dot(h, w3)).astype(BF16)
            part = _dot(act, w2)
            if first:
                acc_scr[sl] = part
            elif last:
                o_ref[sl, :] = x_ref[sl, :] + (0.5 * mod_ref[0, gt:gt + 1, :]) * (acc_scr[sl] + part)
            else:
                acc_scr[sl] += part

    pl.when(j == 0)(lambda: step(True, False))
    pl.when(jnp.logical_and(j > 0, j < last_j))(lambda: step(False, False))
    pl.when(j == last_j)(lambda: step(False, True))


def _ffn(x, mod, g, w1, w3, w2, layer, which, grp_fn, sh, sc, gt):
    rows = x.shape[0]
    tm, tf = FFN_TM, FFN_TF
    return pl.pallas_call(
        functools.partial(_ffn_kernel, sh=sh, sc=sc, gt=gt),
        grid=(rows // tm, D_FF // tf),
        in_specs=[
            pl.BlockSpec((tm, D_MODEL), lambda i, j: (i, 0)),
            pl.BlockSpec((1, N_ADA, D_MODEL), lambda i, j: (grp_fn(i, tm), 0, 0)),
            pl.BlockSpec((1, D_MODEL), lambda i, j: (0, 0)),
            pl.BlockSpec((None, None, D_MODEL, tf), lambda i, j: (layer, which, 0, j)),
            pl.BlockSpec((None, None, D_MODEL, tf), lambda i, j: (layer, which, 0, j)),
            pl.BlockSpec((None, None, tf, D_MODEL), lambda i, j: (layer, which, j, 0)),
        ],
        out_specs=pl.BlockSpec((tm, D_MODEL), lambda i, j: (i, 0)),
        out_shape=jax.ShapeDtypeStruct((rows, D_MODEL), F32),
        scratch_shapes=[pltpu.VMEM((tm, D_MODEL), BF16), pltpu.VMEM((tm, D_MODEL), F32)],
        compiler_params=_cparams(("arbitrary", "arbitrary")),
        name="ffn_half_step",
    )(x, mod, g, w1, w3, w2)


def _head_norm(xs, bsum, g):
    normed = []
    for p in range(xs.shape[1] // LANES):
        x = xs[:, p * LANES:(p + 1) * LANES]
        sq_hi, sq_lo = _split(x * x)
        ms = _dot(sq_hi, bsum) + _dot(sq_lo, bsum)
        normed.append(x * lax.rsqrt(ms + EPS))
    return jnp.concatenate(normed, axis=-1) * g


def _rope(xs, cos, sin, first_half):
    swapped = jnp.where(first_half, pltpu.roll(xs, LANES - 16, 1), pltpu.roll(xs, 16, 1))
    return xs * cos + swapped * sin


def _dup_heads(x2, low_half):
    sw = pltpu.roll(x2, HEAD_DIM, 1)
    return jnp.where(low_half, x2, sw), jnp.where(low_half, sw, x2)


def _expand_kv(x2, low_half):
    d0, d1 = _dup_heads(x2, low_half)
    return jnp.concatenate([d0, x2, d1], axis=-1)


def _store_rows(ref, val, cache_layer):
    if len(ref.shape) == 2:
        ref[...] = val.astype(ref.dtype)
        return
    seq = ref.shape[-2]
    for s in range(ref.shape[0]):
        rows = val[s * seq:(s + 1) * seq].astype(ref.dtype)
        if len(ref.shape) == 3:
            ref[s] = rows
        else:
            for l in range(ref.shape[1]):
                ref[s, l] = rows if l == cache_layer else jnp.zeros_like(rows)


def _proj_kernel(*refs, rope, cache_layer, n_alias, seq_len):
    x_ref, mod_ref, g_ref, w_ref, bsum_ref, gains_ref, xprev_ref, xnext_ref, cw_ref, cb_ref = refs[:10]
    pos = 10
    if rope:
        cos_ref, sin_ref = refs[pos:pos + 2]
        pos += 2
    pos += n_alias
    x0_ref, z_ref, qn_ref, kn_ref, vn_ref, qg_ref, kge_ref, vge_ref = refs[pos:pos + 8]
    pos += 8
    emit_kv = cache_layer is not None
    if emit_kv:
        kg_ref, vg_ref = refs[pos:pos + 2]
        pos += 2
    u_scr, w_scr = refs[pos:pos + 2]
    scale = HEAD_DIM ** -0.5
    tm = x_ref.shape[0]

    @pl.when(pl.program_id(0) == 0)
    def _():
        w_scr[...] = w_ref[...].astype(BF16)

    def projected(x, cols):
        h = _mod_norm(x, g_ref[...], mod_ref[0, SH2:SH2 + 1, :], mod_ref[0, SC2:SC2 + 1, :])
        return _dot(h.astype(BF16), w_scr[:, cols])

    u_scr[...] = projected(x_ref[...], slice(0, IN_WIDTH))

    hy_cols = slice(0, OFF_NA)
    halo_prev = projected(xprev_ref[...], hy_cols)[7:8]
    halo_next = projected(xnext_ref[...], hy_cols)[0:1]
    row = lax.broadcasted_iota(jnp.int32, (tm, 1), 0)
    t_pos = (pl.program_id(0) * tm + row) & (seq_len - 1)

    def conv_chunk(ci):
        sl = slice(ci * HY_CH, (ci + 1) * HY_CH)
        u = u_scr[:, sl]
        prev = jnp.where(row == 0, halo_prev[:, sl], pltpu.roll(u, 1, 0))
        prev = jnp.where(t_pos == 0, 0.0, prev)
        nxt = jnp.where(row == tm - 1, halo_next[:, sl], pltpu.roll(u, tm - 1, 0))
        nxt = jnp.where(t_pos == seq_len - 1, 0.0, nxt)
        return prev * cw_ref[0:1, sl] + u * cw_ref[1:2, sl] + nxt * cw_ref[2:3, sl] + cb_ref[:, sl]

    x0_ref[...] = conv_chunk(0)
    z_ref[...] = conv_chunk(1) * conv_chunk(2)

    bsum = bsum_ref[...]
    qn = _head_norm(u_scr[:, OFF_NA:OFF_NA + NA_W], bsum, gains_ref[0:1, :])
    qn_ref[...] = (qn * scale).astype(qn_ref.dtype)
    kn = _head_norm(u_scr[:, OFF_NA + NA_W:OFF_NA + 2 * NA_W], bsum, gains_ref[1:2, :])
    _store_rows(kn_ref, kn, cache_layer)
    _store_rows(vn_ref, u_scr[:, OFF_NA + 2 * NA_W:OFF_GQ], cache_layer)

    qg = _head_norm(u_scr[:, OFF_GQ:OFF_GQ + GQ_W], bsum, gains_ref[2:3, :])
    kg = _head_norm(u_scr[:, OFF_GQ + GQ_W:OFF_GQ + GQ_W + GKV_W], bsum, gains_ref[3:4, 0:GKV_W])
    vg = u_scr[:, OFF_GQ + GQ_W + GKV_W:IN_WIDTH]
    if emit_kv:
        _store_rows(kg_ref, kg, cache_layer)
        _store_rows(vg_ref, vg, cache_layer)

    lane = lax.broadcasted_iota(jnp.int32, (1, LANES), 1)
    if rope:
        cos = cos_ref[...]
        sin = sin_ref[...]
        first_half = (lane % 32) < 16
        for p in range(GQ_W // LANES):
            sl = slice(p * LANES, (p + 1) * LANES)
            qg_ref[:, sl] = (_rope(qg[:, sl], cos, sin, first_half) * scale).astype(qg_ref.dtype)
        kg = _rope(kg, cos, sin, first_half)
    else:
        qg_ref[...] = (qg * scale).astype(qg_ref.dtype)

    low_half = lane < HEAD_DIM
    kge_ref[...] = _expand_kv(kg, low_half).astype(kge_ref.dtype)
    vge_ref[...] = _expand_kv(vg, low_half).astype(vge_ref.dtype)


def _project(x, mod, g, w_in, layer, bsum, gains, conv_w, conv_b, seq_len, grp_fn, rope_tabs, attn_dtype,
             cache_layer=None, prev_cache=None):
    rows = x.shape[0]
    tm = ROW_TM
    rope = rope_tabs is not None
    assert seq_len & (seq_len - 1) == 0 and (seq_len % tm == 0 or tm % seq_len == 0)
    sub = 8
    in_specs = [
        pl.BlockSpec((tm, D_MODEL), lambda i: (i, 0)),
        pl.BlockSpec((1, N_ADA, D_MODEL), lambda i: (grp_fn(i, tm), 0, 0)),
        pl.BlockSpec((1, D_MODEL), lambda i: (0, 0)),
        pl.BlockSpec((None, D_MODEL, IN_WIDTH), lambda i: (layer, 0, 0), pipeline_mode=pl.Buffered(1)),
        pl.BlockSpec((LANES, LANES), lambda i: (0, 0)),
        pl.BlockSpec((4, NA_W), lambda i: (0, 0)),
        pl.BlockSpec((sub, D_MODEL), lambda i: (jnp.maximum(i * (tm // sub) - 1, 0), 0)),
        pl.BlockSpec((sub, D_MODEL), lambda i: (jnp.minimum((i + 1) * (tm // sub), rows // sub - 1), 0)),
        pl.BlockSpec((3, OFF_NA), lambda i: (0, 0)),
        pl.BlockSpec((1, OFF_NA), lambda i: (0, 0)),
    ]
    args = [x, mod, g, w_in, bsum, gains, x, x, conv_w, conv_b.reshape(1, OFF_NA)]
    if rope:
        seq_tiles = DEC_SEQ // tm
        in_specs += [pl.BlockSpec((tm, LANES), lambda i: (i % seq_tiles, 0))] * 2
        args += list(rope_tabs)
    outs = [(HY_CH, F32)] * 2 + [(NA_W, attn_dtype)] * 3 + [(GQ_W, attn_dtype)] * 3
    if cache_layer is not None:
        outs += [(GKV_W, F32), (GKV_W, F32)]
    out_specs = [pl.BlockSpec((tm, w), lambda i: (i, 0)) for w, _ in outs]
    out_shape = [jax.ShapeDtypeStruct((rows, w), dt) for w, dt in outs]
    aliases = {}
    if cache_layer is not None:
        cache_outs = (3, 4, 8, 9)
        seqs = tm // SEQ
        for o in cache_outs:
            w = outs[o][0]
            out_shape[o] = jax.ShapeDtypeStruct((BATCH, DEPTH, SEQ, w), F32)
            if prev_cache is None:
                out_specs[o] = pl.BlockSpec((seqs, DEPTH, SEQ, w), lambda i: (i, 0, 0, 0))
            else:
                out_specs[o] = pl.BlockSpec((seqs, None, SEQ, w), lambda i: (i, cache_layer, 0, 0))
        if prev_cache is not None:
            aliases = {len(args) + n: o for n, o in enumerate(cache_outs)}
            in_specs += [pl.BlockSpec(memory_space=pl.ANY)] * len(cache_outs)
            args += list(prev_cache)
    return pl.pallas_call(
        functools.partial(_proj_kernel, rope=rope, cache_layer=cache_layer, n_alias=len(aliases), seq_len=seq_len),
        grid=(rows // tm,),
        in_specs=in_specs,
        out_specs=out_specs,
        out_shape=out_shape,
        input_output_aliases=aliases,
        scratch_shapes=[pltpu.VMEM((tm, IN_WIDTH), F32), pltpu.VMEM((D_MODEL, IN_WIDTH), BF16)],
        compiler_params=_cparams(("arbitrary",)),
        name="mixer_in_proj",
    )(*args)


HEADS_PER_CHAIN = 2
CHAIN_W = HEADS_PER_CHAIN * HEAD_DIM


def _log2(n):
    assert n & (n - 1) == 0
    return n.bit_length() - 1


def _stack_heads(q, n_heads):
    m, w = q.shape
    rows = lax.broadcasted_iota(jnp.int32, (n_heads * m, w), 0)
    lanes = lax.broadcasted_iota(jnp.int32, (n_heads * m, w), 1)
    own = jnp.right_shift(rows, _log2(m)) == jnp.right_shift(lanes, _log2(HEAD_DIM))
    return jnp.where(own, jnp.concatenate([q] * n_heads, axis=0), jnp.zeros((), q.dtype))


def _unstack_heads(o, n_heads):
    m = o.shape[0] // n_heads
    head = jnp.right_shift(lax.broadcasted_iota(jnp.int32, (m, o.shape[1]), 1), _log2(HEAD_DIM))
    out = o[0:m]
    for h in range(1, n_heads):
        out = jnp.where(head == h, o[h * m:(h + 1) * m], out)
    return out


def _sink_column(sink_ref, m, first_head, n_heads):
    block = jnp.right_shift(lax.broadcasted_iota(jnp.int32, (n_heads * m, 1), 0), _log2(m))
    col = jnp.full((n_heads * m, 1), sink_ref[first_head], F32)
    for h in range(1, n_heads):
        col = jnp.where(block == h, sink_ref[first_head + h], col)
    return col


def _softmax_pv(scores, values, sink):
    m = scores[0].max(axis=-1, keepdims=True)
    for s in scores[1:]:
        m = jnp.maximum(m, s.max(axis=-1, keepdims=True))
    if sink is not None:
        m = jnp.maximum(m, sink)
    den = None
    acc = None
    for s, v in zip(scores, values):
        p = jnp.exp(s - m)
        d = p.sum(axis=-1, keepdims=True)
        o = _dot(p.astype(BF16), v)
        den = d if den is None else den + d
        acc = o if acc is None else acc + o
    if sink is not None:
        den = den + jnp.exp(sink - m)
    return acc / den


def _ctx_attn_kernel(sink_ref, qn_ref, kn_ref, vn_ref, qg_ref, kge_ref, vge_ref, yna_ref, ygq_ref):
    m = qn_ref.shape[1]
    low = lax.broadcasted_iota(jnp.int32, (m, LANES), 1) < HEAD_DIM
    for q_ref, k_ref, v_ref, y_ref, has_sink in ((qn_ref, kn_ref, vn_ref, yna_ref, False),
                                                  (qg_ref, kge_ref, vge_ref, ygq_ref, True)):
        for p in range(NA_W // LANES):
            sl = slice(p * LANES, (p + 1) * LANES)
            q2 = q_ref[0, :, sl].astype(BF16)
            k2 = k_ref[0, :, sl].astype(BF16)
            v2 = v_ref[0, :, sl].astype(BF16)
            halves = []
            for half, keep in enumerate((low, ~low)):
                qh = jnp.where(keep, q2, jnp.zeros((), BF16))
                sink = sink_ref[2 * p + half] if has_sink else None
                halves.append(_softmax_pv([_dot_nt(qh, k2)], [v2], sink))
            y_ref[0, :, sl] = jnp.where(low, halves[0], halves[1]).astype(y_ref.dtype)


def _ctx_attention(sink, qn, kn_cache, vn_cache, qg, kge, vge, layer):
    b, l, _ = qn.shape

    def spec(w):
        return pl.BlockSpec((1, l, w), lambda i: (i, 0, 0))

    cache_spec = pl.BlockSpec((1, None, l, NA_W), lambda i: (i, layer, 0, 0))
    return pl.pallas_call(
        _ctx_attn_kernel,
        grid=(b,),
        in_specs=[pl.BlockSpec(memory_space=pltpu.SMEM), spec(NA_W), cache_spec, cache_spec, spec(GQ_W),
                  spec(GQ_W), spec(GQ_W)],
        out_specs=[spec(NA_W), spec(GQ_W)],
        out_shape=[jax.ShapeDtypeStruct((b, l, NA_W), BF16), jax.ShapeDtypeStruct((b, l, GQ_W), BF16)],
        compiler_params=_cparams(("arbitrary",)),
        name="context_attention",
    )(sink, qn, kn_cache, vn_cache, qg, kge, vge)


NA_KEYS = NA_ROWS * GRID_W
NA_ROWS_PER_STEP = 8


def _na_bias_kernel(r_ref, oh_ref, mk_ref, o_ref):
    o_ref[...] = jnp.dot(r_ref[...], oh_ref[...], precision=_HIGHEST, preferred_element_type=F32) + mk_ref[...]


def _na_bias_tables():
    q = np.arange(GRID_W)[:, None]
    kc = np.arange(GRID_W)[None, :]
    win_lo = np.clip(q - NA_COLS // 2, 0, GRID_W - NA_COLS)
    ok = (kc >= win_lo) & (kc < win_lo + NA_COLS)
    j = kc - q + NA_COLS - 1
    onehot = np.zeros((LANES, GRID_W * GRID_W), np.float32)
    qq, kk = np.nonzero(ok)
    onehot[j[qq, kk], qq * GRID_W + kk] = 1.0
    mask = np.where(ok, 0.0, NEG_INF).astype(np.float32).reshape(1, GRID_W * GRID_W)
    return onehot, mask


def _na_bias(rpb_l):
    n_dr = 2 * NA_ROWS - 1
    rows = NA_HEADS * n_dr
    rows_pad = -(-rows // 8) * 8
    r = jnp.zeros((rows_pad, LANES), F32).at[:rows, :2 * NA_COLS - 1].set(rpb_l.reshape(rows, 2 * NA_COLS - 1))
    onehot, mask = _na_bias_tables()
    out = pl.pallas_call(
        _na_bias_kernel,
        out_shape=jax.ShapeDtypeStruct((rows_pad, GRID_W * GRID_W), F32),
        compiler_params=pltpu.CompilerParams(vmem_limit_bytes=VMEM_LIMIT),
        name="na_bias_expand",
    )(r, jnp.asarray(onehot), jnp.asarray(mask))
    bm = out[:rows].reshape(NA_HEADS, n_dr, GRID_W, GRID_W)
    bm2 = jnp.concatenate([bm[:, :-1], bm[:, 1:]], axis=-1)
    return bm2.transpose(1, 0, 2, 3).reshape(n_dr - 1, NA_HEADS * GRID_W, 2 * GRID_W)


def _na_attn_kernel(q_ref, k_ref, v_ref, kc_ref, vc_ref, bm_ref, o_ref):
    n_rows = DEC_SEQ // GRID_W
    kc = kc_ref[0, 0].astype(BF16)
    vc = vc_ref[0, 0].astype(BF16)
    for rr in range(NA_ROWS_PER_STEP):
        r = pl.program_id(1) * NA_ROWS_PER_STEP + rr
        start = jnp.clip(r - NA_ROWS // 2, 0, n_rows - NA_ROWS)
        shift = r - start
        row0 = pl.multiple_of(start * GRID_W, GRID_W)
        rows = slice(rr * GRID_W, (rr + 1) * GRID_W)
        q = _stack_heads(q_ref[0, rows, :], NA_HEADS)
        k = k_ref[0, pl.ds(row0, NA_KEYS), :]
        v = v_ref[0, pl.ds(row0, NA_KEYS), :]
        bias = jnp.concatenate([bm_ref[2 * jj - shift + NA_ROWS - 1] for jj in range(NA_ROWS // 2)], axis=-1)
        s_loc = _dot_nt(q, k) + bias
        s_ctx = _dot_nt(q, kc)
        o = _softmax_pv([s_loc, s_ctx], [v, vc], None)
        o_ref[0, rows, :] = _unstack_heads(o, NA_HEADS).astype(o_ref.dtype)


def _na_attention(q, k, v, kc, vc, bm2, layer):
    b, n, _ = q.shape
    tq = NA_ROWS_PER_STEP * GRID_W
    return pl.pallas_call(
        _na_attn_kernel,
        grid=(b, n // tq),
        in_specs=[
            pl.BlockSpec((1, tq, NA_W), lambda i, r: (i, r, 0)),
            pl.BlockSpec((1, n, NA_W), lambda i, r: (i, 0, 0)),
            pl.BlockSpec((1, n, NA_W), lambda i, r: (i, 0, 0)),
            pl.BlockSpec((1, 1, PAST_LEN, NA_W), lambda i, r: (i, layer, 0, 0)),
            pl.BlockSpec((1, 1, PAST_LEN, NA_W), lambda i, r: (i, layer, 0, 0)),
            pl.BlockSpec(bm2.shape, lambda i, r: (0, 0, 0)),
        ],
        out_specs=pl.BlockSpec((1, tq, NA_W), lambda i, r: (i, r, 0)),
        out_shape=jax.ShapeDtypeStruct((b, n, NA_W), BF16),
        compiler_params=_cparams(("arbitrary", "arbitrary")),
        name="neighbourhood_attention",
    )(q, k, v, kc, vc, bm2)


WIN_KEYS = 3 * BLK
WIN_BLOCKS_PER_STEP = 4


def _win_attn_kernel(sink_ref, mask_ref, q_ref, kge_ref, vge_ref, kc_ref, vc_ref, o_ref):
    n = kge_ref.shape[1]
    n_blocks = n // BLK
    low_c = lax.broadcasted_iota(jnp.int32, (1, LANES), 1) < HEAD_DIM
    kce = _expand_kv(kc_ref[0, 0], low_c).astype(BF16)
    vce = _expand_kv(vc_ref[0, 0], low_c).astype(BF16)
    n_chains = GQA_Q_HEADS // HEADS_PER_CHAIN
    sinks = [_sink_column(sink_ref, BLK, p * HEADS_PER_CHAIN, HEADS_PER_CHAIN) for p in range(n_chains)]
    for bb in range(WIN_BLOCKS_PER_STEP):
        nb = pl.program_id(1) * WIN_BLOCKS_PER_STEP + bb
        start = pl.multiple_of(jnp.clip((nb - 1) * BLK, 0, n - WIN_KEYS), BLK)
        variant = jnp.where(nb == 0, 0, jnp.where(nb == n_blocks - 1, 2, 1))
        rows = slice(bb * BLK, (bb + 1) * BLK)
        for p in range(n_chains):
            sl = slice(p * CHAIN_W, (p + 1) * CHAIN_W)
            q = _stack_heads(q_ref[0, rows, sl], HEADS_PER_CHAIN)
            s_loc = _dot_nt(q, kge_ref[0, pl.ds(start, WIN_KEYS), sl]) + mask_ref[variant]
            s_ctx = _dot_nt(q, kce[:, sl])
            o = _softmax_pv([s_loc, s_ctx], [vge_ref[0, pl.ds(start, WIN_KEYS), sl], vce[:, sl]], sinks[p])
            o_ref[0, rows, sl] = _unstack_heads(o, HEADS_PER_CHAIN).astype(o_ref.dtype)


def _win_mask_table():
    qi = np.arange(BLK)[:, None]
    kj = np.arange(WIN_KEYS)[None, :]
    tabs = [np.where(np.abs(qi + off - kj) <= GQA_WINDOW, 0.0, NEG_INF) for off in (0, BLK, 2 * BLK)]
    return jnp.asarray(np.stack([np.tile(t, (HEADS_PER_CHAIN, 1)) for t in tabs]).astype(np.float32))


def _win_attention(sink, q, kge, vge, kc, vc, layer):
    b, n, _ = q.shape
    mask = _win_mask_table()
    tq = WIN_BLOCKS_PER_STEP * BLK
    return pl.pallas_call(
        _win_attn_kernel,
        grid=(b, n // tq),
        in_specs=[
            pl.BlockSpec(memory_space=pltpu.SMEM),
            pl.BlockSpec(mask.shape, lambda i, j: (0, 0, 0)),
            pl.BlockSpec((1, tq, GQ_W), lambda i, j: (i, j, 0)),
            pl.BlockSpec((1, n, GQ_W), lambda i, j: (i, 0, 0)),
            pl.BlockSpec((1, n, GQ_W), lambda i, j: (i, 0, 0)),
            pl.BlockSpec((1, 1, PAST_LEN, GKV_W), lambda i, j: (i, layer, 0, 0)),
            pl.BlockSpec((1, 1, PAST_LEN, GKV_W), lambda i, j: (i, layer, 0, 0)),
        ],
        out_specs=pl.BlockSpec((1, tq, GQ_W), lambda i, j: (i, j, 0)),
        out_shape=jax.ShapeDtypeStruct((b, n, GQ_W), BF16),
        compiler_params=_cparams(("arbitrary", "arbitrary")),
        name="window_attention",
    )(sink, mask, q, kge, vge, kc, vc)


@functools.lru_cache(maxsize=None)
def _hyena_consts_np(L):
    n = 2 * L
    k = np.arange(L, dtype=np.int64)
    ang = (2.0 * np.pi / n) * ((k[:, None] * k[None, :]) % n).astype(np.float64)

    cm = np.cos(ang).astype(np.float32)
    sm = (-np.sin(ang)).astype(np.float32)
    idx = np.arange(L, dtype=np.float32)
    t = idx / np.float32(L - 1)
    bands = np.linspace(1e-4, HY_BANDS - 1, HY_BANDS, dtype=np.float32)
    fang = np.float32(2.0 * math.pi / L) * idx[:, None] * bands[None, :]
    feats = np.zeros((L, LANES), np.float32)
    feats[:, 0] = t
    feats[:, 1:1 + HY_BANDS] = np.cos(fang)
    feats[:, 1 + HY_BANDS:HY_EMB] = -np.sin(fang)
    max_decay = math.log(HY_DECAY_TARGET) / HY_FAST_PCT
    min_decay = math.log(HY_DECAY_TARGET) / HY_SLOW_PCT
    deltas = np.abs(np.linspace(min_decay, max_decay, HY_CH, dtype=np.float32))
    decay = np.exp(-t[:, None] * deltas[None, :]).astype(np.float32)
    decay2 = np.concatenate([decay, decay], axis=1)
    return cm, sm, feats, decay2


def _hyena_consts(L):
    cm, sm, feats, decay2 = _hyena_consts_np(L)
    return jnp.asarray(cm), jnp.asarray(sm), jnp.asarray(feats), jnp.asarray(decay2)


def _hy_filter_kernel(feats_ref, w1_ref, b1_ref, w2_ref, b2_ref, w3_ref, freq_ref, decay_ref, wsum_ref, wdiff_ref):
    def dot_hi(a, b):
        a_hi, a_lo = _split(a)
        b_hi, b_lo = _split(b)
        return _dot(a_hi, b_hi) + (_dot(a_lo, b_hi) + _dot(a_hi, b_lo))

    hid = jnp.sin(freq_ref[0:1, :] * (dot_hi(feats_ref[...], w1_ref[...]) + b1_ref[...]))
    hid = jnp.sin(freq_ref[1:2, :] * (dot_hi(hid, w2_ref[...]) + b2_ref[...]))
    taps = dot_hi(hid, w3_ref[...]) * decay_ref[...]
    fwd = taps[:, 0:HY_CH]
    bwd = taps[:, HY_CH:2 * HY_CH]
    row = lax.broadcasted_iota(jnp.int32, bwd.shape, 0)
    bwd = jnp.where(row == 0, 0.0, bwd)
    wsum_ref[...] = fwd + bwd
    wdiff_ref[...] = fwd - bwd


def _hy_filter(L, feats, decay2, w1, b1, w2, b2, w3, freq):
    w1p = jnp.zeros((LANES, HY_FILT_W), F32).at[:HY_EMB].set(w1)
    return pl.pallas_call(
        _hy_filter_kernel,
        out_shape=[jax.ShapeDtypeStruct((L, HY_CH), F32)] * 2,
        compiler_params=pltpu.CompilerParams(vmem_limit_bytes=VMEM_LIMIT),
        name="hyena_filter",
    )(feats, w1p, b1.reshape(1, -1), w2, b2.reshape(1, -1), w3, freq, decay2)


def _hy_conv_kernel(z_ref, x0_ref, wsum_ref, wdiff_ref, skip_ref, cr_ref, sr_ref, cc_ref, sc_ref,
                    o_ref, z_scr, ws_scr, wd_scr, acc_scr, *, group):
    t = pl.program_id(1)
    L = z_ref.shape[1]
    n = 2 * L

    @pl.when(t == 0)
    def _():
        z_scr[...] = z_ref[...].astype(BF16)
        ws_scr[...] = wsum_ref[...].astype(BF16)
        wd_scr[...] = wdiff_ref[...].astype(BF16)
        acc_scr[...] = jnp.zeros_like(acc_scr)

    cr, sr = cr_ref[...].astype(BF16), sr_ref[...].astype(BF16)
    cc, sc = cc_ref[...].astype(BF16), sc_ref[...].astype(BF16)
    k_re = _dot(cr, ws_scr[...])
    k_im = _dot(sr, wd_scr[...])
    for b in range(group):
        zb = z_scr[b]
        z_re = _dot(cr, zb)
        z_im = _dot(sr, zb)
        y_re = (z_re * k_re - z_im * k_im).astype(BF16)
        y_im = (z_re * k_im + z_im * k_re).astype(BF16)
        acc_scr[b] += _dot(cc, y_re) + _dot(sc, y_im)

    @pl.when(t == pl.num_programs(1) - 1)
    def _():
        row = lax.broadcasted_iota(jnp.int32, (L, 1), 0)
        sgn = (1 - 2 * (row % 2)).astype(F32)
        w = wsum_ref[...]
        w_dc = w.sum(axis=0, keepdims=True)
        w_ny = (w * sgn).sum(axis=0, keepdims=True)
        for b in range(group):
            z = z_ref[b]
            z_dc = z.sum(axis=0, keepdims=True)
            z_ny = (z * sgn).sum(axis=0, keepdims=True)
            conv = (2.0 / n) * acc_scr[b] - (1.0 / n) * (z_dc * w_dc) + (1.0 / n) * (sgn * (z_ny * w_ny))
            o_ref[b] = (x0_ref[b] * (conv + z * skip_ref[...])).astype(o_ref.dtype)


def _hy_conv(z, x0, wsum, wdiff, skip, cm, sm, group):
    b, L, c = z.shape
    tkf = min(HY_TKF, L)
    row_spec = pl.BlockSpec((tkf, L), lambda g, t: (t, 0))
    col_spec = pl.BlockSpec((L, tkf), lambda g, t: (0, t))
    once = pl.Buffered(1)
    seq_spec = pl.BlockSpec((group, L, c), lambda g, t: (g, 0, 0), pipeline_mode=once)
    w_spec = pl.BlockSpec((L, c), lambda g, t: (0, 0), pipeline_mode=once)
    return pl.pallas_call(
        functools.partial(_hy_conv_kernel, group=group),
        grid=(b // group, L // tkf),
        in_specs=[seq_spec, seq_spec, w_spec, w_spec, pl.BlockSpec((1, c), lambda g, t: (0, 0)),
                  row_spec, row_spec, col_spec, col_spec],
        out_specs=pl.BlockSpec((group, L, c), lambda g, t: (g, 0, 0)),
        out_shape=jax.ShapeDtypeStruct((b, L, c), BF16),
        scratch_shapes=[pltpu.VMEM((group, L, c), BF16), pltpu.VMEM((L, c), BF16), pltpu.VMEM((L, c), BF16),
                        pltpu.VMEM((group, L, c), F32)],
        compiler_params=_cparams(("arbitrary", "arbitrary")),
        name="hyena_long_conv",
    )(z, x0, wsum, wdiff, skip.reshape(1, c), cm, sm, cm, sm)


def _merge_kernel(x_ref, yhy_ref, yna_ref, ygq_ref, mod_ref, w_ref, o_ref, w_scr):
    @pl.when(pl.program_id(0) == 0)
    def _():
        w_scr[...] = w_ref[...].astype(BF16)

    y = jnp.concatenate([yhy_ref[...], yna_ref[...], ygq_ref[...]], axis=-1)
    o_ref[...] = x_ref[...] + mod_ref[0, G2:G2 + 1, :] * _dot(y, w_scr[...])


def _merge(x, y_hy, y_na, y_gq, mod, w_out, layer, grp_fn):
    rows = x.shape[0]
    tm = MERGE_TM
    return pl.pallas_call(
        _merge_kernel,
        grid=(rows // tm,),
        in_specs=[
            pl.BlockSpec((tm, D_MODEL), lambda i: (i, 0)),
            pl.BlockSpec((tm, HY_CH), lambda i: (i, 0)),
            pl.BlockSpec((tm, NA_W), lambda i: (i, 0)),
            pl.BlockSpec((tm, GQ_W), lambda i: (i, 0)),
            pl.BlockSpec((1, N_ADA, D_MODEL), lambda i: (grp_fn(i, tm), 0, 0)),
            pl.BlockSpec((None, D_MODEL, D_MODEL), lambda i: (layer, 0, 0), pipeline_mode=pl.Buffered(1)),
        ],
        out_specs=pl.BlockSpec((tm, D_MODEL), lambda i: (i, 0)),
        out_shape=jax.ShapeDtypeStruct((rows, D_MODEL), F32),
        scratch_shapes=[pltpu.VMEM((D_MODEL, D_MODEL), BF16)],
        compiler_params=_cparams(("arbitrary",)),
        name="mixer_out_proj",
    )(x, y_hy, y_na, y_gq, mod, w_out)


def _rope_tables():
    pos = np.arange(DEC_SEQ)
    quarter = HEAD_DIM // 4
    inv = ROPE_BASE ** (-np.arange(quarter, dtype=np.float64) / quarter)
    ang_r = (pos // GRID_W)[:, None] * inv[None, :]
    ang_c = (pos % GRID_W)[:, None] * inv[None, :]
    cos_h = np.concatenate([np.cos(ang_r), np.cos(ang_r), np.cos(ang_c), np.cos(ang_c)], axis=1)
    sin_h = np.concatenate([-np.sin(ang_r), np.sin(ang_r), -np.sin(ang_c), np.sin(ang_c)], axis=1)
    reps = LANES // HEAD_DIM
    return (jnp.asarray(np.tile(cos_h, (1, reps)), F32), jnp.asarray(np.tile(sin_h, (1, reps)), F32))


def _head_sum_matrix():
    idx = np.arange(LANES) // HEAD_DIM
    return jnp.asarray((idx[:, None] == idx[None, :]).astype(np.float32) / HEAD_DIM, BF16)


def _grp_prompt(i, tm):
    return 0


def _grp_sample(i, tm):
    return 1 + (i * tm) // DEC_SEQ


def kernel(x_prompt, x_sample, cache_na_k, cache_na_v, cache_gqa_k, cache_gqa_v, c, c_ctx, ada_w, ada_b, norm_g, ffn_w1, ffn_w3, ffn_w2, w_in, w_out, hy_conv_w, hy_conv_b, hy_filt_w1, hy_filt_b1, hy_filt_w2, hy_filt_b2, hy_filt_w3, hy_freq, hy_skip, na_q_g, na_k_g, na_rpb, gqa_q_g, gqa_k_g, gqa_sink):
    xp = x_prompt.reshape(BATCH * SEQ, D_MODEL)
    xs = x_sample.reshape(DEC_BATCH * DEC_SEQ, D_MODEL)
    cc = jnp.zeros((8, D_MODEL), F32).at[0].set(c_ctx).at[1:1 + DEC_BATCH].set(c)
    kc_na = cache_na_k.reshape(DEC_BATCH, DEPTH, PAST_LEN, NA_W)
    vc_na = cache_na_v.reshape(DEC_BATCH, DEPTH, PAST_LEN, NA_W)
    kc_gq = cache_gqa_k.reshape(DEC_BATCH, DEPTH, PAST_LEN, GKV_W)
    vc_gq = cache_gqa_v.reshape(DEC_BATCH, DEPTH, PAST_LEN, GKV_W)

    rope_tabs = _rope_tables()
    bsum = _head_sum_matrix()
    hy_p = _hyena_consts(SEQ)
    hy_s = _hyena_consts(DEC_SEQ)

    cache = None
    for l in range(DEPTH):
        mod = _adaln_mod(cc, ada_w, ada_b, l)
        gains = jnp.stack([jnp.tile(na_q_g[l], NA_HEADS), jnp.tile(na_k_g[l], NA_HEADS),
                           jnp.tile(gqa_q_g[l], GQA_Q_HEADS), jnp.tile(gqa_k_g[l], GQA_Q_HEADS)])
        bm2 = _na_bias(na_rpb[l])
        ng = [norm_g[l, i].reshape(1, D_MODEL) for i in range(3)]
        filt = {}
        for L, consts in ((SEQ, hy_p), (DEC_SEQ, hy_s)):
            filt[L] = _hy_filter(L, consts[2], consts[3], hy_filt_w1[l], hy_filt_b1[l], hy_filt_w2[l],
                                 hy_filt_b2[l], hy_filt_w3[l], hy_freq[l])

        xp = _ffn(xp, mod, ng[0], ffn_w1, ffn_w3, ffn_w2, l, 0, _grp_prompt, SH1, SC1, G1)
        x0, z, qn, kn, vn, qg, kgd, vgd, kg, vg = _project(
            xp, mod, ng[1], w_in, l, bsum, gains, hy_conv_w[l], hy_conv_b[l], SEQ, _grp_prompt, None, F32,
            cache_layer=l, prev_cache=cache)
        cache = (kn, vn, kg, vg)
        shp = lambda a: a.reshape(BATCH, SEQ, a.shape[-1])
        y_hy = _hy_conv(shp(z), shp(x0), filt[SEQ][0], filt[SEQ][1], hy_skip[l], hy_p[0], hy_p[1], group=BATCH)
        y_na, y_gq = _ctx_attention(gqa_sink[l], shp(qn), kn, vn, shp(qg), shp(kgd), shp(vgd), l)
        xp = _merge(xp, y_hy.reshape(-1, HY_CH), y_na.reshape(-1, NA_W), y_gq.reshape(-1, GQ_W), mod, w_out, l,
                    _grp_prompt)
        xp = _ffn(xp, mod, ng[2], ffn_w1, ffn_w3, ffn_w2, l, 1, _grp_prompt, SH3, SC3, G3)

        xs = _ffn(xs, mod, ng[0], ffn_w1, ffn_w3, ffn_w2, l, 0, _grp_sample, SH1, SC1, G1)
        x0, z, qn, kn, vn, qg, kgd, vgd = _project(
            xs, mod, ng[1], w_in, l, bsum, gains, hy_conv_w[l], hy_conv_b[l], DEC_SEQ, _grp_sample, rope_tabs, BF16)
        shs = lambda a: a.reshape(DEC_BATCH, DEC_SEQ, a.shape[-1])
        y_hy = _hy_conv(shs(z), shs(x0), filt[DEC_SEQ][0], filt[DEC_SEQ][1], hy_skip[l], hy_s[0], hy_s[1],
                        group=DEC_BATCH)
        y_na = _na_attention(shs(qn), shs(kn), shs(vn), kc_na, vc_na, bm2, l)
        y_gq = _win_attention(gqa_sink[l], shs(qg), shs(kgd), shs(vgd), kc_gq, vc_gq, l)
        xs = _merge(xs, y_hy.reshape(-1, HY_CH), y_na.reshape(-1, NA_W), y_gq.reshape(-1, GQ_W), mod, w_out, l,
                    _grp_sample)
        xs = _ffn(xs, mod, ng[2], ffn_w1, ffn_w3, ffn_w2, l, 1, _grp_sample, SH3, SC3, G3)

    kn, vn, kg, vg = cache
    return (xp.reshape(BATCH, SEQ, D_MODEL), xs.reshape(DEC_BATCH, DEC_SEQ, D_MODEL),
            kn.reshape(BATCH, DEPTH, SEQ, NA_HEADS, HEAD_DIM), vn.reshape(BATCH, DEPTH, SEQ, NA_HEADS, HEAD_DIM),
            kg.reshape(BATCH, DEPTH, SEQ, GQA_KV_HEADS, HEAD_DIM), vg.reshape(BATCH, DEPTH, SEQ, GQA_KV_HEADS, HEAD_DIM))
```

```python
import functools
import math

import numpy as np
import jax
import jax.numpy as jnp
from jax import lax
from jax.experimental import pallas as pl
from jax.experimental.pallas import tpu as pltpu

F32 = jnp.float32
BF16 = jnp.bfloat16

D_MODEL = 1024
BATCH = 16
SEQ = 256
DEPTH = 2
DEC_BATCH = 2
DEC_SEQ = 2048
PAST_LEN = 256
GRID_W = 64
HEAD_DIM = 64
HY_CH = 256
NA_HEADS = 6
GQA_Q_HEADS = 6
GQA_KV_HEADS = 2
NA_W = NA_HEADS * HEAD_DIM
GQ_W = GQA_Q_HEADS * HEAD_DIM
GKV_W = GQA_KV_HEADS * HEAD_DIM
IN_WIDTH = 3 * HY_CH + 3 * NA_W + GQ_W + 2 * GKV_W
D_FF = 2816
N_ADA = 9
HY_BANDS = 16
HY_EMB = 1 + 2 * HY_BANDS
HY_FILT_W = 64
HY_DECAY_TARGET = 1e-2
HY_FAST_PCT = 0.3
HY_SLOW_PCT = 1.5
NA_ROWS = 8
NA_COLS = 16
GQA_WINDOW = 128
BLK = 128
ROPE_BASE = 10000.0
EPS = 1e-6
NEG_INF = -1e30

LANES = 128
V7X_VMEM_BYTES = 64 * 1024 * 1024
VMEM_LIMIT = V7X_VMEM_BYTES * 15 // 16

OFF_NA = 3 * HY_CH
OFF_GQ = OFF_NA + 3 * NA_W

SH1, SC1, G1, SH2, SC2, G2, SH3, SC3, G3 = range(N_ADA)

FFN_TM = 2048
FFN_TF = 256
FFN_ROW_GROUPS = 4
ROW_TM = 512
MERGE_TM = 1024
HY_TKF = 256

_HIGHEST = lax.Precision.HIGHEST


def _cparams(sem):
    return pltpu.CompilerParams(dimension_semantics=sem, vmem_limit_bytes=VMEM_LIMIT)


def _silu(x):
    return x * (1.0 / (1.0 + jnp.exp(-x)))


def _dot(a, b):
    return jnp.dot(a, b, preferred_element_type=F32)


def _dot_nt(a, b):
    return lax.dot_general(a, b, (((1,), (1,)), ((), ())), preferred_element_type=F32)


def _split(x):
    hi = x.astype(BF16)
    lo = (x - hi.astype(F32)).astype(BF16)
    return hi, lo


def _mod_norm(x, g, shift, scale):
    ms = jnp.mean(x * x, axis=-1, keepdims=True)
    return (x * lax.rsqrt(ms + EPS) * g) * (1.0 + scale) + shift


def _mod_kernel(c_ref, w_ref, b_ref, o_ref):
    a = _silu(c_ref[...]).astype(BF16)
    o_ref[...] = _dot(a, w_ref[...].astype(BF16)) + b_ref[...]


def _adaln_mod(cc, ada_w, ada_b, layer):
    n_out = N_ADA * D_MODEL
    tn = n_out // 4
    out = pl.pallas_call(
        _mod_kernel,
        grid=(n_out // tn,),
        in_specs=[
            pl.BlockSpec((8, D_MODEL), lambda j: (0, 0)),
            pl.BlockSpec((None, D_MODEL, tn), lambda j: (layer, 0, j)),
            pl.BlockSpec((None, 1, tn), lambda j: (layer, 0, j)),
        ],
        out_specs=pl.BlockSpec((8, tn), lambda j: (0, j)),
        out_shape=jax.ShapeDtypeStruct((8, n_out), F32),
        compiler_params=_cparams(("arbitrary",)),
        name="adaln_mod",
    )(cc, ada_w, ada_b.reshape(DEPTH, 1, n_out))
    return out.reshape(8, N_ADA, D_MODEL)


def _ffn_kernel(x_ref, mod_ref, g_ref, w1_ref, w3_ref, w2_ref, o_ref, h_scr, acc_scr, *, sh, sc, gt):
    j = pl.program_id(1)
    last_j = pl.num_programs(1) - 1
    rows = h_scr.shape[0] // FFN_ROW_GROUPS

    def step(first, last):
        w1 = w1_ref[...].astype(BF16)
        w3 = w3_ref[...].astype(BF16)
        w2 = w2_ref[...].astype(BF16)
        for s in range(FFN_ROW_GROUPS):
            sl = slice(s * rows, (s + 1) * rows)
            if first:
                h = _mod_norm(x_ref[sl, :], g_ref[...], mod_ref[0, sh:sh + 1, :], mod_ref[0, sc:sc + 1, :]).astype(BF16)
                h_scr[sl] = h
            else:
                h = h_scr[sl]
            act = (_silu(_dot(h, w1)) * _dot(h, w3)).astype(BF16)
            part = _dot(act, w2)
            if first:
                acc_scr[sl] = part
            elif last:
                o_ref[sl, :] = x_ref[sl, :] + (0.5 * mod_ref[0, gt:gt + 1, :]) * (acc_scr[sl] + part)
            else:
                acc_scr[sl] += part

    pl.when(j == 0)(lambda: step(True, False))
    pl.when(jnp.logical_and(j > 0, j < last_j))(lambda: step(False, False))
    pl.when(j == last_j)(lambda: step(False, True))


def _ffn(x, mod, g, w1, w3, w2, layer, which, grp_fn, sh, sc, gt):
    rows = x.shape[0]
    tm, tf = FFN_TM, FFN_TF
    return pl.pallas_call(
        functools.partial(_ffn_kernel, sh=sh, sc=sc, gt=gt),
        grid=(rows // tm, D_FF // tf),
        in_specs=[
            pl.BlockSpec((tm, D_MODEL), lambda i, j: (i, 0)),
            pl.BlockSpec((1, N_ADA, D_MODEL), lambda i, j: (grp_fn(i, tm), 0, 0)),
            pl.BlockSpec((1, D_MODEL), lambda i, j: (0, 0)),
            pl.BlockSpec((None, None, D_MODEL, tf), lambda i, j: (layer, which, 0, j)),
            pl.BlockSpec((None, None, D_MODEL, tf), lambda i, j: (layer, which, 0, j)),
            pl.BlockSpec((None, None, tf, D_MODEL), lambda i, j: (layer, which, j, 0)),
        ],
        out_specs=pl.BlockSpec((tm, D_MODEL), lambda i, j: (i, 0)),
        out_shape=jax.ShapeDtypeStruct((rows, D_MODEL), F32),
        scratch_shapes=[pltpu.VMEM((tm, D_MODEL), BF16), pltpu.VMEM((tm, D_MODEL), F32)],
        compiler_params=_cparams(("arbitrary", "arbitrary")),
        name="ffn_half_step",
    )(x, mod, g, w1, w3, w2)


def _head_norm(xs, bsum, g):
    normed = []
    for p in range(xs.shape[1] // LANES):
        x = xs[:, p * LANES:(p + 1) * LANES]
        sq_hi, sq_lo = _split(x * x)
        ms = _dot(sq_hi, bsum) + _dot(sq_lo, bsum)
        normed.append(x * lax.rsqrt(ms + EPS))
    return jnp.concatenate(normed, axis=-1) * g


def _rope(xs, cos, sin, first_half):
    swapped = jnp.where(first_half, pltpu.roll(xs, LANES - 16, 1), pltpu.roll(xs, 16, 1))
    return xs * cos + swapped * sin


def _dup_heads(x2, low_half):
    sw = pltpu.roll(x2, HEAD_DIM, 1)
    return jnp.where(low_half, x2, sw), jnp.where(low_half, sw, x2)


def _expand_kv(x2, low_half):
    d0, d1 = _dup_heads(x2, low_half)
    return jnp.concatenate([d0, x2, d1], axis=-1)


def _store_rows(ref, val, cache_layer):
    if len(ref.shape) == 2:
        ref[...] = val.astype(ref.dtype)
        return
    seq = ref.shape[-2]
    for s in range(ref.shape[0]):
        rows = val[s * seq:(s + 1) * seq].astype(ref.dtype)
        if len(ref.shape) == 3:
            ref[s] = rows
        else:
            for l in range(ref.shape[1]):
                ref[s, l] = rows if l == cache_layer else jnp.zeros_like(rows)


def _proj_kernel(*refs, rope, cache_layer, n_alias, seq_len):
    x_ref, mod_ref, g_ref, w_ref, bsum_ref, gains_ref, xprev_ref, xnext_ref, cw_ref, cb_ref = refs[:10]
    pos = 10
    if rope:
        cos_ref, sin_ref = refs[pos:pos + 2]
        pos += 2
    pos += n_alias
    x0_ref, z_ref, qn_ref, kn_ref, vn_ref, qg_ref, kge_ref, vge_ref = refs[pos:pos + 8]
    pos += 8
    emit_kv = cache_layer is not None
    if emit_kv:
        kg_ref, vg_ref = refs[pos:pos + 2]
        pos += 2
    u_scr, w_scr = refs[pos:pos + 2]
    scale = HEAD_DIM ** -0.5
    tm = x_ref.shape[0]

    @pl.when(pl.program_id(0) == 0)
    def _():
        w_scr[...] = w_ref[...].astype(BF16)

    def projected(x, cols):
        h = _mod_norm(x, g_ref[...], mod_ref[0, SH2:SH2 + 1, :], mod_ref[0, SC2:SC2 + 1, :])
        return _dot(h.astype(BF16), w_scr[:, cols])

    u_scr[...] = projected(x_ref[...], slice(0, IN_WIDTH))

    hy_cols = slice(0, OFF_NA)
    halo_prev = projected(xprev_ref[...], hy_cols)[7:8]
    halo_next = projected(xnext_ref[...], hy_cols)[0:1]
    row = lax.broadcasted_iota(jnp.int32, (tm, 1), 0)
    t_pos = (pl.program_id(0) * tm + row) & (seq_len - 1)

    def conv_chunk(ci):
        sl = slice(ci * HY_CH, (ci + 1) * HY_CH)
        u = u_scr[:, sl]
        prev = jnp.where(row == 0, halo_prev[:, sl], pltpu.roll(u, 1, 0))
        prev = jnp.where(t_pos == 0, 0.0, prev)
        nxt = jnp.where(row == tm - 1, halo_next[:, sl], pltpu.roll(u, tm - 1, 0))
        nxt = jnp.where(t_pos == seq_len - 1, 0.0, nxt)
        return prev * cw_ref[0:1, sl] + u * cw_ref[1:2, sl] + nxt * cw_ref[2:3, sl] + cb_ref[:, sl]

    x0_ref[...] = conv_chunk(0)
    z_ref[...] = conv_chunk(1) * conv_chunk(2)

    bsum = bsum_ref[...]
    qn = _head_norm(u_scr[:, OFF_NA:OFF_NA + NA_W], bsum, gains_ref[0:1, :])
    qn_ref[...] = (qn * scale).astype(qn_ref.dtype)
    kn = _head_norm(u_scr[:, OFF_NA + NA_W:OFF_NA + 2 * NA_W], bsum, gains_ref[1:2, :])
    _store_rows(kn_ref, kn, cache_layer)
    _store_rows(vn_ref, u_scr[:, OFF_NA + 2 * NA_W:OFF_GQ], cache_layer)

    qg = _head_norm(u_scr[:, OFF_GQ:OFF_GQ + GQ_W], bsum, gains_ref[2:3, :])
    kg = _head_norm(u_scr[:, OFF_GQ + GQ_W:OFF_GQ + GQ_W + GKV_W], bsum, gains_ref[3:4, 0:GKV_W])
    vg = u_scr[:, OFF_GQ + GQ_W + GKV_W:IN_WIDTH]
    if emit_kv:
        _store_rows(kg_ref, kg, cache_layer)
        _store_rows(vg_ref, vg, cache_layer)

    lane = lax.broadcasted_iota(jnp.int32, (1, LANES), 1)
    if rope:
        cos = cos_ref[...]
        sin = sin_ref[...]
        first_half = (lane % 32) < 16
        for p in range(GQ_W // LANES):
            sl = slice(p * LANES, (p + 1) * LANES)
            qg_ref[:, sl] = (_rope(qg[:, sl], cos, sin, first_half) * scale).astype(qg_ref.dtype)
        kg = _rope(kg, cos, sin, first_half)
    else:
        qg_ref[...] = (qg * scale).astype(qg_ref.dtype)

    low_half = lane < HEAD_DIM
    kge_ref[...] = _expand_kv(kg, low_half).astype(kge_ref.dtype)
    vge_ref[...] = _expand_kv(vg, low_half).astype(vge_ref.dtype)


def _project(x, mod, g, w_in, layer, bsum, gains, conv_w, conv_b, seq_len, grp_fn, rope_tabs, attn_dtype,
             cache_layer=None, prev_cache=None):
    rows = x.shape[0]
    tm = ROW_TM
    rope = rope_tabs is not None
    assert seq_len & (seq_len - 1) == 0 and (seq_len % tm == 0 or tm % seq_len == 0)
    sub = 8
    in_specs = [
        pl.BlockSpec((tm, D_MODEL), lambda i: (i, 0)),
        pl.BlockSpec((1, N_ADA, D_MODEL), lambda i: (grp_fn(i, tm), 0, 0)),
        pl.BlockSpec((1, D_MODEL), lambda i: (0, 0)),
        pl.BlockSpec((None, D_MODEL, IN_WIDTH), lambda i: (layer, 0, 0), pipeline_mode=pl.Buffered(1)),
        pl.BlockSpec((LANES, LANES), lambda i: (0, 0)),
        pl.BlockSpec((4, NA_W), lambda i: (0, 0)),
        pl.BlockSpec((sub, D_MODEL), lambda i: (jnp.maximum(i * (tm // sub) - 1, 0), 0)),
        pl.BlockSpec((sub, D_MODEL), lambda i: (jnp.minimum((i + 1) * (tm // sub), rows // sub - 1), 0)),
        pl.BlockSpec((3, OFF_NA), lambda i: (0, 0)),
        pl.BlockSpec((1, OFF_NA), lambda i: (0, 0)),
    ]
    args = [x, mod, g, w_in, bsum, gains, x, x, conv_w, conv_b.reshape(1, OFF_NA)]
    if rope:
        seq_tiles = DEC_SEQ // tm
        in_specs += [pl.BlockSpec((tm, LANES), lambda i: (i % seq_tiles, 0))] * 2
        args += list(rope_tabs)
    outs = [(HY_CH, F32)] * 2 + [(NA_W, attn_dtype)] * 3 + [(GQ_W, attn_dtype)] * 3
    if cache_layer is not None:
        outs += [(GKV_W, F32), (GKV_W, F32)]
    out_specs = [pl.BlockSpec((tm, w), lambda i: (i, 0)) for w, _ in outs]
    out_shape = [jax.ShapeDtypeStruct((rows, w), dt) for w, dt in outs]
    aliases = {}
    if cache_layer is not None:
        cache_outs = (3, 4, 8, 9)
        seqs = tm // SEQ
        for o in cache_outs:
            w = outs[o][0]
            out_shape[o] = jax.ShapeDtypeStruct((BATCH, DEPTH, SEQ, w), F32)
            if prev_cache is None:
                out_specs[o] = pl.BlockSpec((seqs, DEPTH, SEQ, w), lambda i: (i, 0, 0, 0))
            else:
                out_specs[o] = pl.BlockSpec((seqs, None, SEQ, w), lambda i: (i, cache_layer, 0, 0))
        if prev_cache is not None:
            aliases = {len(args) + n: o for n, o in enumerate(cache_outs)}
            in_specs += [pl.BlockSpec(memory_space=pl.ANY)] * len(cache_outs)
            args += list(prev_cache)
    return pl.pallas_call(
        functools.partial(_proj_kernel, rope=rope, cache_layer=cache_layer, n_alias=len(aliases), seq_len=seq_len),
        grid=(rows // tm,),
        in_specs=in_specs,
        out_specs=out_specs,
        out_shape=out_shape,
        input_output_aliases=aliases,
        scratch_shapes=[pltpu.VMEM((tm, IN_WIDTH), F32), pltpu.VMEM((D_MODEL, IN_WIDTH), BF16)],
        compiler_params=_cparams(("arbitrary",)),
        name="mixer_in_proj",
    )(*args)


HEADS_PER_CHAIN = 2
CHAIN_W = HEADS_PER_CHAIN * HEAD_DIM


def _log2(n):
    assert n & (n - 1) == 0
    return n.bit_length() - 1


def _stack_heads(q, n_heads):
    m, w = q.shape
    rows = lax.broadcasted_iota(jnp.int32, (n_heads * m, w), 0)
    lanes = lax.broadcasted_iota(jnp.int32, (n_heads * m, w), 1)
    own = jnp.right_shift(rows, _log2(m)) == jnp.right_shift(lanes, _log2(HEAD_DIM))
    return jnp.where(own, jnp.concatenate([q] * n_heads, axis=0), jnp.zeros((), q.dtype))


def _unstack_heads(o, n_heads):
    m = o.shape[0] // n_heads
    head = jnp.right_shift(lax.broadcasted_iota(jnp.int32, (m, o.shape[1]), 1), _log2(HEAD_DIM))
    out = o[0:m]
    for h in range(1, n_heads):
        out = jnp.where(head == h, o[h * m:(h + 1) * m], out)
    return out


def _sink_column(sink_ref, m, first_head, n_heads):
    block = jnp.right_shift(lax.broadcasted_iota(jnp.int32, (n_heads * m, 1), 0), _log2(m))
    col = jnp.full((n_heads * m, 1), sink_ref[first_head], F32)
    for h in range(1, n_heads):
        col = jnp.where(block == h, sink_ref[first_head + h], col)
    return col


def _softmax_pv(scores, values, sink):
    m = scores[0].max(axis=-1, keepdims=True)
    for s in scores[1:]:
        m = jnp.maximum(m, s.max(axis=-1, keepdims=True))
    if sink is not None:
        m = jnp.maximum(m, sink)
    den = None
    acc = None
    for s, v in zip(scores, values):
        p = jnp.exp(s - m)
        d = p.sum(axis=-1, keepdims=True)
        o = _dot(p.astype(BF16), v)
        den = d if den is None else den + d
        acc = o if acc is None else acc + o
    if sink is not None:
        den = den + jnp.exp(sink - m)
    return acc / den


CTX_SEQS_PER_STEP = 2


def _ctx_attn_kernel(sink_ref, qn_ref, kn_ref, vn_ref, qg_ref, kge_ref, vge_ref, yna_ref, ygq_ref):
    m = qn_ref.shape[1]
    low = lax.broadcasted_iota(jnp.int32, (m, LANES), 1) < HEAD_DIM
    for q_ref, k_ref, v_ref, y_ref, has_sink in ((qn_ref, kn_ref, vn_ref, yna_ref, False),
                                                  (qg_ref, kge_ref, vge_ref, ygq_ref, True)):
        for bi in range(CTX_SEQS_PER_STEP):
            for p in range(NA_W // LANES):
                sl = slice(p * LANES, (p + 1) * LANES)
                q2 = q_ref[bi, :, sl].astype(BF16)
                k2 = k_ref[bi, :, sl].astype(BF16)
                v2 = v_ref[bi, :, sl].astype(BF16)
                halves = []
                for half, keep in enumerate((low, ~low)):
                    qh = jnp.where(keep, q2, jnp.zeros((), BF16))
                    sink = sink_ref[2 * p + half] if has_sink else None
                    halves.append(_softmax_pv([_dot_nt(qh, k2)], [v2], sink))
                y_ref[bi, :, sl] = jnp.where(low, halves[0], halves[1]).astype(y_ref.dtype)


def _ctx_attention(sink, qn, kn_cache, vn_cache, qg, kge, vge, layer):
    b, l, _ = qn.shape

    nb = CTX_SEQS_PER_STEP

    def spec(w):
        return pl.BlockSpec((nb, l, w), lambda i: (i, 0, 0))

    cache_spec = pl.BlockSpec((nb, None, l, NA_W), lambda i: (i, layer, 0, 0))
    return pl.pallas_call(
        _ctx_attn_kernel,
        grid=(b // nb,),
        in_specs=[pl.BlockSpec(memory_space=pltpu.SMEM), spec(NA_W), cache_spec, cache_spec, spec(GQ_W),
                  spec(GQ_W), spec(GQ_W)],
        out_specs=[spec(NA_W), spec(GQ_W)],
        out_shape=[jax.ShapeDtypeStruct((b, l, NA_W), BF16), jax.ShapeDtypeStruct((b, l, GQ_W), BF16)],
        compiler_params=_cparams(("arbitrary",)),
        name="context_attention",
    )(sink, qn, kn_cache, vn_cache, qg, kge, vge)


NA_KEYS = NA_ROWS * GRID_W
NA_ROWS_PER_STEP = 8


def _na_bias_kernel(r_ref, oh_ref, mk_ref, o_ref):
    o_ref[...] = jnp.dot(r_ref[...], oh_ref[...], precision=_HIGHEST, preferred_element_type=F32) + mk_ref[...]


def _na_bias_tables():
    q = np.arange(GRID_W)[:, None]
    kc = np.arange(GRID_W)[None, :]
    win_lo = np.clip(q - NA_COLS // 2, 0, GRID_W - NA_COLS)
    ok = (kc >= win_lo) & (kc < win_lo + NA_COLS)
    j = kc - q + NA_COLS - 1
    onehot = np.zeros((LANES, GRID_W * GRID_W), np.float32)
    qq, kk = np.nonzero(ok)
    onehot[j[qq, kk], qq * GRID_W + kk] = 1.0
    mask = np.where(ok, 0.0, NEG_INF).astype(np.float32).reshape(1, GRID_W * GRID_W)
    return onehot, mask


def _na_bias(rpb_l):
    n_dr = 2 * NA_ROWS - 1
    rows = NA_HEADS * n_dr
    rows_pad = -(-rows // 8) * 8
    r = jnp.zeros((rows_pad, LANES), F32).at[:rows, :2 * NA_COLS - 1].set(rpb_l.reshape(rows, 2 * NA_COLS - 1))
    onehot, mask = _na_bias_tables()
    out = pl.pallas_call(
        _na_bias_kernel,
        out_shape=jax.ShapeDtypeStruct((rows_pad, GRID_W * GRID_W), F32),
        compiler_params=pltpu.CompilerParams(vmem_limit_bytes=VMEM_LIMIT),
        name="na_bias_expand",
    )(r, jnp.asarray(onehot), jnp.asarray(mask))
    bm = out[:rows].reshape(NA_HEADS, n_dr, GRID_W, GRID_W)
    bm2 = jnp.concatenate([bm[:, :-1], bm[:, 1:]], axis=-1)
    return bm2.transpose(1, 0, 2, 3).reshape(n_dr - 1, NA_HEADS * GRID_W, 2 * GRID_W)


def _na_attn_kernel(q_ref, k_ref, v_ref, kc_ref, vc_ref, bm_ref, o_ref):
    n_rows = DEC_SEQ // GRID_W
    kc = kc_ref[0, 0].astype(BF16)
    vc = vc_ref[0, 0].astype(BF16)
    for rr in range(NA_ROWS_PER_STEP):
        r = pl.program_id(1) * NA_ROWS_PER_STEP + rr
        start = jnp.clip(r - NA_ROWS // 2, 0, n_rows - NA_ROWS)
        shift = r - start
        row0 = pl.multiple_of(start * GRID_W, GRID_W)
        rows = slice(rr * GRID_W, (rr + 1) * GRID_W)
        q = _stack_heads(q_ref[0, rows, :], NA_HEADS)
        k = k_ref[0, pl.ds(row0, NA_KEYS), :]
        v = v_ref[0, pl.ds(row0, NA_KEYS), :]
        bias = jnp.concatenate([bm_ref[2 * jj - shift + NA_ROWS - 1] for jj in range(NA_ROWS // 2)], axis=-1)
        s_loc = _dot_nt(q, k) + bias
        s_ctx = _dot_nt(q, kc)
        o = _softmax_pv([s_loc, s_ctx], [v, vc], None)
        o_ref[0, rows, :] = _unstack_heads(o, NA_HEADS).astype(o_ref.dtype)


WIN_KEYS = 3 * BLK
WIN_BLOCKS_PER_STEP = 4


def _win_attn_kernel(sink_ref, mask_ref, q_ref, kge_ref, vge_ref, kc_ref, vc_ref, o_ref):
    n = kge_ref.shape[1]
    n_blocks = n // BLK
    low_c = lax.broadcasted_iota(jnp.int32, (1, LANES), 1) < HEAD_DIM
    kce = _expand_kv(kc_ref[0, 0], low_c).astype(BF16)
    vce = _expand_kv(vc_ref[0, 0], low_c).astype(BF16)
    n_chains = GQA_Q_HEADS // HEADS_PER_CHAIN
    sinks = [_sink_column(sink_ref, BLK, p * HEADS_PER_CHAIN, HEADS_PER_CHAIN) for p in range(n_chains)]
    for bb in range(WIN_BLOCKS_PER_STEP):
        nb = pl.program_id(1) * WIN_BLOCKS_PER_STEP + bb
        start = pl.multiple_of(jnp.clip((nb - 1) * BLK, 0, n - WIN_KEYS), BLK)
        variant = jnp.where(nb == 0, 0, jnp.where(nb == n_blocks - 1, 2, 1))
        rows = slice(bb * BLK, (bb + 1) * BLK)
        for p in range(n_chains):
            sl = slice(p * CHAIN_W, (p + 1) * CHAIN_W)
            q = _stack_heads(q_ref[0, rows, sl], HEADS_PER_CHAIN)
            s_loc = _dot_nt(q, kge_ref[0, pl.ds(start, WIN_KEYS), sl]) + mask_ref[variant]
            s_ctx = _dot_nt(q, kce[:, sl])
            o = _softmax_pv([s_loc, s_ctx], [vge_ref[0, pl.ds(start, WIN_KEYS), sl], vce[:, sl]], sinks[p])
            o_ref[0, rows, sl] = _unstack_heads(o, HEADS_PER_CHAIN).astype(o_ref.dtype)


def _win_mask_table():
    qi = np.arange(BLK)[:, None]
    kj = np.arange(WIN_KEYS)[None, :]
    tabs = [np.where(np.abs(qi + off - kj) <= GQA_WINDOW, 0.0, NEG_INF) for off in (0, BLK, 2 * BLK)]
    return jnp.asarray(np.stack([np.tile(t, (HEADS_PER_CHAIN, 1)) for t in tabs]).astype(np.float32))


def _latent_attn_kernel(sink_ref, mask_ref, bm_ref, qn_ref, kn_ref, vn_ref, kcn_ref, vcn_ref,
                        qg_ref, kge_ref, vge_ref, kcg_ref, vcg_ref, yna_ref, ygq_ref):
    _na_attn_kernel(qn_ref, kn_ref, vn_ref, kcn_ref, vcn_ref, bm_ref, yna_ref)
    _win_attn_kernel(sink_ref, mask_ref, qg_ref, kge_ref, vge_ref, kcg_ref, vcg_ref, ygq_ref)


def _latent_attention(sink, bm2, qn, kn, vn, kc_na, vc_na, qg, kge, vge, kc_gq, vc_gq, layer):
    b, n, _ = qn.shape
    mask = _win_mask_table()
    tq = WIN_BLOCKS_PER_STEP * BLK
    assert tq == NA_ROWS_PER_STEP * GRID_W

    def q_spec(w):
        return pl.BlockSpec((1, tq, w), lambda i, j: (i, j, 0))

    def seq_spec(w):
        return pl.BlockSpec((1, n, w), lambda i, j: (i, 0, 0))

    def cache_spec(w):
        return pl.BlockSpec((1, 1, PAST_LEN, w), lambda i, j: (i, layer, 0, 0))

    return pl.pallas_call(
        _latent_attn_kernel,
        grid=(b, n // tq),
        in_specs=[
            pl.BlockSpec(memory_space=pltpu.SMEM),
            pl.BlockSpec(mask.shape, lambda i, j: (0, 0, 0)),
            pl.BlockSpec(bm2.shape, lambda i, j: (0, 0, 0)),
            q_spec(NA_W), seq_spec(NA_W), seq_spec(NA_W), cache_spec(NA_W), cache_spec(NA_W),
            q_spec(GQ_W), seq_spec(GQ_W), seq_spec(GQ_W), cache_spec(GKV_W), cache_spec(GKV_W),
        ],
        out_specs=[q_spec(NA_W), q_spec(GQ_W)],
        out_shape=[jax.ShapeDtypeStruct((b, n, NA_W), BF16), jax.ShapeDtypeStruct((b, n, GQ_W), BF16)],
        compiler_params=_cparams(("arbitrary", "arbitrary")),
        name="latent_attention",
    )(sink, mask, bm2, qn, kn, vn, kc_na, vc_na, qg, kge, vge, kc_gq, vc_gq)


@functools.lru_cache(maxsize=None)
def _hyena_consts_np(L):
    n = 2 * L
    k = np.arange(L, dtype=np.int64)
    ang = (2.0 * np.pi / n) * ((k[:, None] * k[None, :]) % n).astype(np.float64)

    cm = np.cos(ang).astype(np.float32)
    sm = (-np.sin(ang)).astype(np.float32)
    idx = np.arange(L, dtype=np.float32)
    t = idx / np.float32(L - 1)
    bands = np.linspace(1e-4, HY_BANDS - 1, HY_BANDS, dtype=np.float32)
    fang = np.float32(2.0 * math.pi / L) * idx[:, None] * bands[None, :]
    feats = np.zeros((L, LANES), np.float32)
    feats[:, 0] = t
    feats[:, 1:1 + HY_BANDS] = np.cos(fang)
    feats[:, 1 + HY_BANDS:HY_EMB] = -np.sin(fang)
    max_decay = math.log(HY_DECAY_TARGET) / HY_FAST_PCT
    min_decay = math.log(HY_DECAY_TARGET) / HY_SLOW_PCT
    deltas = np.abs(np.linspace(min_decay, max_decay, HY_CH, dtype=np.float32))
    decay = np.exp(-t[:, None] * deltas[None, :]).astype(np.float32)
    decay2 = np.concatenate([decay, decay], axis=1)
    return cm, sm, feats, decay2


def _hyena_consts(L):
    cm, sm, feats, decay2 = _hyena_consts_np(L)
    return jnp.asarray(cm), jnp.asarray(sm), jnp.asarray(feats), jnp.asarray(decay2)


def _hy_filter_kernel(feats_ref, w1_ref, b1_ref, w2_ref, b2_ref, w3_ref, freq_ref, decay_ref, wsum_ref, wdiff_ref):
    def dot_hi(a, b):
        a_hi, a_lo = _split(a)
        b_hi, b_lo = _split(b)
        return _dot(a_hi, b_hi) + (_dot(a_lo, b_hi) + _dot(a_hi, b_lo))

    half = feats_ref.shape[0] // 2
    feats2 = jnp.concatenate([feats_ref[0:half, :], feats_ref[half:2 * half, :]], axis=-1)
    hid = jnp.sin(freq_ref[0:1, :] * (dot_hi(feats2, w1_ref[...]) + b1_ref[...]))
    hid = jnp.sin(freq_ref[1:2, :] * (dot_hi(hid, w2_ref[...]) + b2_ref[...]))
    taps2 = dot_hi(hid, w3_ref[...])
    for part in range(2):
        rows = slice(part * half, (part + 1) * half)
        taps = taps2[:, part * 2 * HY_CH:(part + 1) * 2 * HY_CH] * decay_ref[rows, :]
        fwd = taps[:, 0:HY_CH]
        bwd = taps[:, HY_CH:2 * HY_CH]
        if part == 0:
            row = lax.broadcasted_iota(jnp.int32, bwd.shape, 0)
            bwd = jnp.where(row == 0, 0.0, bwd)
        wsum_ref[rows, :] = fwd + bwd
        wdiff_ref[rows, :] = fwd - bwd


def _block_diag2(w):
    r, c = w.shape
    return jnp.zeros((2 * r, 2 * c), w.dtype).at[:r, :c].set(w).at[r:, c:].set(w)


def _hy_filter(L, feats, decay2, w1, b1, w2, b2, w3, freq):
    w1p = jnp.zeros((LANES, HY_FILT_W), F32).at[:HY_EMB].set(w1)
    return pl.pallas_call(
        _hy_filter_kernel,
        out_shape=[jax.ShapeDtypeStruct((L, HY_CH), F32)] * 2,
        compiler_params=pltpu.CompilerParams(vmem_limit_bytes=VMEM_LIMIT),
        name="hyena_filter",
    )(feats, _block_diag2(w1p), jnp.tile(b1, 2).reshape(1, -1), _block_diag2(w2), jnp.tile(b2, 2).reshape(1, -1),
      _block_diag2(w3), jnp.tile(freq, (1, 2)), decay2)


def _hy_conv_kernel(z_ref, x0_ref, wsum_ref, wdiff_ref, skip_ref, cr_ref, sr_ref, cc_ref, sc_ref,
                    o_ref, z_scr, ws_scr, wd_scr, acc_scr, *, group):
    t = pl.program_id(1)
    L = z_ref.shape[1]
    n = 2 * L

    @pl.when(t == 0)
    def _():
        z_scr[...] = z_ref[...].astype(BF16)
        ws_scr[...] = wsum_ref[...].astype(BF16)
        wd_scr[...] = wdiff_ref[...].astype(BF16)
        acc_scr[...] = jnp.zeros_like(acc_scr)

    cr, sr = cr_ref[...].astype(BF16), sr_ref[...].astype(BF16)
    cc, sc = cc_ref[...].astype(BF16), sc_ref[...].astype(BF16)
    k_re = _dot(cr, ws_scr[...])
    k_im = _dot(sr, wd_scr[...])
    for b in range(group):
        zb = z_scr[b]
        z_re = _dot(cr, zb)
        z_im = _dot(sr, zb)
        y_re = (z_re * k_re - z_im * k_im).astype(BF16)
        y_im = (z_re * k_im + z_im * k_re).astype(BF16)
        acc_scr[b] += _dot(cc, y_re) + _dot(sc, y_im)

    @pl.when(t == pl.num_programs(1) - 1)
    def _():
        row = lax.broadcasted_iota(jnp.int32, (L, 1), 0)
        sgn = (1 - 2 * (row % 2)).astype(F32)
        w = wsum_ref[...]
        w_dc = w.sum(axis=0, keepdims=True)
        w_ny = (w * sgn).sum(axis=0, keepdims=True)
        for b in range(group):
            z = z_ref[b]
            z_dc = z.sum(axis=0, keepdims=True)
            z_ny = (z * sgn).sum(axis=0, keepdims=True)
            conv = (2.0 / n) * acc_scr[b] - (1.0 / n) * (z_dc * w_dc) + (1.0 / n) * (sgn * (z_ny * w_ny))
            o_ref[b] = (x0_ref[b] * (conv + z * skip_ref[...])).astype(o_ref.dtype)


def _hy_conv(z, x0, wsum, wdiff, skip, cm, sm, group):
    b, L, c = z.shape
    tkf = min(HY_TKF, L)
    row_spec = pl.BlockSpec((tkf, L), lambda g, t: (t, 0))
    col_spec = pl.BlockSpec((L, tkf), lambda g, t: (0, t))
    once = pl.Buffered(1)
    seq_spec = pl.BlockSpec((group, L, c), lambda g, t: (g, 0, 0), pipeline_mode=once)
    w_spec = pl.BlockSpec((L, c), lambda g, t: (0, 0), pipeline_mode=once)
    return pl.pallas_call(
        functools.partial(_hy_conv_kernel, group=group),
        grid=(b // group, L // tkf),
        in_specs=[seq_spec, seq_spec, w_spec, w_spec, pl.BlockSpec((1, c), lambda g, t: (0, 0)),
                  row_spec, row_spec, col_spec, col_spec],
        out_specs=pl.BlockSpec((group, L, c), lambda g, t: (g, 0, 0)),
        out_shape=jax.ShapeDtypeStruct((b, L, c), BF16),
        scratch_shapes=[pltpu.VMEM((group, L, c), BF16), pltpu.VMEM((L, c), BF16), pltpu.VMEM((L, c), BF16),
                        pltpu.VMEM((group, L, c), F32)],
        compiler_params=_cparams(("arbitrary", "arbitrary")),
        name="hyena_long_conv",
    )(z, x0, wsum, wdiff, skip.reshape(1, c), cm, sm, cm, sm)


def _merge_kernel(x_ref, yhy_ref, yna_ref, ygq_ref, mod_ref, w_ref, o_ref, w_scr):
    @pl.when(pl.program_id(0) == 0)
    def _():
        w_scr[...] = w_ref[...].astype(BF16)

    y = jnp.concatenate([yhy_ref[...], yna_ref[...], ygq_ref[...]], axis=-1)
    o_ref[...] = x_ref[...] + mod_ref[0, G2:G2 + 1, :] * _dot(y, w_scr[...])


def _merge(x, y_hy, y_na, y_gq, mod, w_out, layer, grp_fn):
    rows = x.shape[0]
    tm = MERGE_TM
    return pl.pallas_call(
        _merge_kernel,
        grid=(rows // tm,),
        in_specs=[
            pl.BlockSpec((tm, D_MODEL), lambda i: (i, 0)),
            pl.BlockSpec((tm, HY_CH), lambda i: (i, 0)),
            pl.BlockSpec((tm, NA_W), lambda i: (i, 0)),
            pl.BlockSpec((tm, GQ_W), lambda i: (i, 0)),
            pl.BlockSpec((1, N_ADA, D_MODEL), lambda i: (grp_fn(i, tm), 0, 0)),
            pl.BlockSpec((None, D_MODEL, D_MODEL), lambda i: (layer, 0, 0), pipeline_mode=pl.Buffered(1)),
        ],
        out_specs=pl.BlockSpec((tm, D_MODEL), lambda i: (i, 0)),
        out_shape=jax.ShapeDtypeStruct((rows, D_MODEL), F32),
        scratch_shapes=[pltpu.VMEM((D_MODEL, D_MODEL), BF16)],
        compiler_params=_cparams(("arbitrary",)),
        name="mixer_out_proj",
    )(x, y_hy, y_na, y_gq, mod, w_out)


def _rope_tables():
    pos = np.arange(DEC_SEQ)
    quarter = HEAD_DIM // 4
    inv = ROPE_BASE ** (-np.arange(quarter, dtype=np.float64) / quarter)
    ang_r = (pos // GRID_W)[:, None] * inv[None, :]
    ang_c = (pos % GRID_W)[:, None] * inv[None, :]
    cos_h = np.concatenate([np.cos(ang_r), np.cos(ang_r), np.cos(ang_c), np.cos(ang_c)], axis=1)
    sin_h = np.concatenate([-np.sin(ang_r), np.sin(ang_r), -np.sin(ang_c), np.sin(ang_c)], axis=1)
    reps = LANES // HEAD_DIM
    return (jnp.asarray(np.tile(cos_h, (1, reps)), F32), jnp.asarray(np.tile(sin_h, (1, reps)), F32))


def _head_sum_matrix():
    idx = np.arange(LANES) // HEAD_DIM
    return jnp.asarray((idx[:, None] == idx[None, :]).astype(np.float32) / HEAD_DIM, BF16)


def _grp_prompt(i, tm):
    return 0


def _grp_sample(i, tm):
    return 1 + (i * tm) // DEC_SEQ


def kernel(x_prompt, x_sample, cache_na_k, cache_na_v, cache_gqa_k, cache_gqa_v, c, c_ctx, ada_w, ada_b, norm_g, ffn_w1, ffn_w3, ffn_w2, w_in, w_out, hy_conv_w, hy_conv_b, hy_filt_w1, hy_filt_b1, hy_filt_w2, hy_filt_b2, hy_filt_w3, hy_freq, hy_skip, na_q_g, na_k_g, na_rpb, gqa_q_g, gqa_k_g, gqa_sink):
    xp = x_prompt.reshape(BATCH * SEQ, D_MODEL)
    xs = x_sample.reshape(DEC_BATCH * DEC_SEQ, D_MODEL)
    cc = jnp.zeros((8, D_MODEL), F32).at[0].set(c_ctx).at[1:1 + DEC_BATCH].set(c)
    kc_na = cache_na_k.reshape(DEC_BATCH, DEPTH, PAST_LEN, NA_W)
    vc_na = cache_na_v.reshape(DEC_BATCH, DEPTH, PAST_LEN, NA_W)
    kc_gq = cache_gqa_k.reshape(DEC_BATCH, DEPTH, PAST_LEN, GKV_W)
    vc_gq = cache_gqa_v.reshape(DEC_BATCH, DEPTH, PAST_LEN, GKV_W)

    rope_tabs = _rope_tables()
    bsum = _head_sum_matrix()
    hy_p = _hyena_consts(SEQ)
    hy_s = _hyena_consts(DEC_SEQ)

    cache = None
    for l in range(DEPTH):
        mod = _adaln_mod(cc, ada_w, ada_b, l)
        gains = jnp.stack([jnp.tile(na_q_g[l], NA_HEADS), jnp.tile(na_k_g[l], NA_HEADS),
                           jnp.tile(gqa_q_g[l], GQA_Q_HEADS), jnp.tile(gqa_k_g[l], GQA_Q_HEADS)])
        bm2 = _na_bias(na_rpb[l])
        ng = [norm_g[l, i].reshape(1, D_MODEL) for i in range(3)]
        filt = {}
        for L, consts in ((SEQ, hy_p), (DEC_SEQ, hy_s)):
            filt[L] = _hy_filter(L, consts[2], consts[3], hy_filt_w1[l], hy_filt_b1[l], hy_filt_w2[l],
                                 hy_filt_b2[l], hy_filt_w3[l], hy_freq[l])

        xp = _ffn(xp, mod, ng[0], ffn_w1, ffn_w3, ffn_w2, l, 0, _grp_prompt, SH1, SC1, G1)
        x0, z, qn, kn, vn, qg, kgd, vgd, kg, vg = _project(
            xp, mod, ng[1], w_in, l, bsum, gains, hy_conv_w[l], hy_conv_b[l], SEQ, _grp_prompt, None, F32,
            cache_layer=l, prev_cache=cache)
        cache = (kn, vn, kg, vg)
        shp = lambda a: a.reshape(BATCH, SEQ, a.shape[-1])
        y_hy = _hy_conv(shp(z), shp(x0), filt[SEQ][0], filt[SEQ][1], hy_skip[l], hy_p[0], hy_p[1], group=BATCH)
        y_na, y_gq = _ctx_attention(gqa_sink[l], shp(qn), kn, vn, shp(qg), shp(kgd), shp(vgd), l)
        xp = _merge(xp, y_hy.reshape(-1, HY_CH), y_na.reshape(-1, NA_W), y_gq.reshape(-1, GQ_W), mod, w_out, l,
                    _grp_prompt)
        xp = _ffn(xp, mod, ng[2], ffn_w1, ffn_w3, ffn_w2, l, 1, _grp_prompt, SH3, SC3, G3)

        xs = _ffn(xs, mod, ng[0], ffn_w1, ffn_w3, ffn_w2, l, 0, _grp_sample, SH1, SC1, G1)
        x0, z, qn, kn, vn, qg, kgd, vgd = _project(
            xs, mod, ng[1], w_in, l, bsum, gains, hy_conv_w[l], hy_conv_b[l], DEC_SEQ, _grp_sample, rope_tabs, BF16)
        shs = lambda a: a.reshape(DEC_BATCH, DEC_SEQ, a.shape[-1])
        y_hy = _hy_conv(shs(z), shs(x0), filt[DEC_SEQ][0], filt[DEC_SEQ][1], hy_skip[l], hy_s[0], hy_s[1],
                        group=DEC_BATCH)
        y_na, y_gq = _latent_attention(gqa_sink[l], bm2, shs(qn), shs(kn), shs(vn), kc_na, vc_na,
                                       shs(qg), shs(kgd), shs(vgd), kc_gq, vc_gq, l)
        xs = _merge(xs, y_hy.reshape(-1, HY_CH), y_na.reshape(-1, NA_W), y_gq.reshape(-1, GQ_W), mod, w_out, l,
                    _grp_sample)
        xs = _ffn(xs, mod, ng[2], ffn_w1, ffn_w3, ffn_w2, l, 1, _grp_sample, SH3, SC3, G3)

    kn, vn, kg, vg = cache
    return (xp.reshape(BATCH, SEQ, D_MODEL), xs.reshape(DEC_BATCH, DEC_SEQ, D_MODEL),
            kn.reshape(BATCH, DEPTH, SEQ, NA_HEADS, HEAD_DIM), vn.reshape(BATCH, DEPTH, SEQ, NA_HEADS, HEAD_DIM),
            kg.reshape(BATCH, DEPTH, SEQ, GQA_KV_HEADS, HEAD_DIM), vg.reshape(BATCH, DEPTH, SEQ, GQA_KV_HEADS, HEAD_DIM))
```

```python
import functools
import math

import numpy as np
import jax
import jax.numpy as jnp
from jax import lax
from jax.experimental import pallas as pl
from jax.experimental.pallas import tpu as pltpu

F32 = jnp.float32
BF16 = jnp.bfloat16

D_MODEL = 1024
BATCH = 16
SEQ = 256
DEPTH = 2
DEC_BATCH = 2
DEC_SEQ = 2048
PAST_LEN = 256
GRID_W = 64
HEAD_DIM = 64
HY_CH = 256
NA_HEADS = 6
GQA_Q_HEADS = 6
GQA_KV_HEADS = 2
NA_W = NA_HEADS * HEAD_DIM
GQ_W = GQA_Q_HEADS * HEAD_DIM
GKV_W = GQA_KV_HEADS * HEAD_DIM
IN_WIDTH = 3 * HY_CH + 3 * NA_W + GQ_W + 2 * GKV_W
D_FF = 2816
N_ADA = 9
HY_BANDS = 16
HY_EMB = 1 + 2 * HY_BANDS
HY_FILT_W = 64
HY_DECAY_TARGET = 1e-2
HY_FAST_PCT = 0.3
HY_SLOW_PCT = 1.5
NA_ROWS = 8
NA_COLS = 16
GQA_WINDOW = 128
BLK = 128
ROPE_BASE = 10000.0
EPS = 1e-6
NEG_INF = -1e30

LANES = 128
V7X_VMEM_BYTES = 64 * 1024 * 1024
VMEM_LIMIT = V7X_VMEM_BYTES * 15 // 16

OFF_NA = 3 * HY_CH
OFF_GQ = OFF_NA + 3 * NA_W

SH1, SC1, G1, SH2, SC2, G2, SH3, SC3, G3 = range(N_ADA)

FFN_TM = 2048
FFN_TF = 256
FFN_ROW_GROUPS = 4
ROW_TM = 512
MERGE_TM = 1024
HY_TKF = 256

_HIGHEST = lax.Precision.HIGHEST


def _cparams(sem):
    return pltpu.CompilerParams(dimension_semantics=sem, vmem_limit_bytes=VMEM_LIMIT)


def _silu(x):
    return x * (1.0 / (1.0 + jnp.exp(-x)))


def _dot(a, b):
    return jnp.dot(a, b, preferred_element_type=F32)


def _dot_nt(a, b):
    return lax.dot_general(a, b, (((1,), (1,)), ((), ())), preferred_element_type=F32)


def _split(x):
    hi = x.astype(BF16)
    lo = (x - hi.astype(F32)).astype(BF16)
    return hi, lo


def _mod_norm(x, g, shift, scale):
    ms = jnp.mean(x * x, axis=-1, keepdims=True)
    return (x * lax.rsqrt(ms + EPS) * g) * (1.0 + scale) + shift


def _mod_kernel(c_ref, w_ref, b_ref, o_ref):
    a = _silu(c_ref[...]).astype(BF16)
    o_ref[...] = _dot(a, w_ref[...].astype(BF16)) + b_ref[...]


def _adaln_mod(cc, ada_w, ada_b, layer):
    n_out = N_ADA * D_MODEL
    tn = n_out // 4
    out = pl.pallas_call(
        _mod_kernel,
        grid=(n_out // tn,),
        in_specs=[
            pl.BlockSpec((8, D_MODEL), lambda j: (0, 0)),
            pl.BlockSpec((None, D_MODEL, tn), lambda j: (layer, 0, j)),
            pl.BlockSpec((None, 1, tn), lambda j: (layer, 0, j)),
        ],
        out_specs=pl.BlockSpec((8, tn), lambda j: (0, j)),
        out_shape=jax.ShapeDtypeStruct((8, n_out), F32),
        compiler_params=_cparams(("arbitrary",)),
        name="adaln_mod",
    )(cc, ada_w, ada_b.reshape(DEPTH, 1, n_out))
    return out.reshape(8, N_ADA, D_MODEL)


def _ffn_kernel(x_ref, mod_ref, g_ref, w1_ref, w3_ref, w2_ref, o_ref, h_scr, acc_scr, *, sh, sc, gt):
    j = pl.program_id(1)
    last_j = pl.num_programs(1) - 1
    rows = h_scr.shape[0] // FFN_ROW_GROUPS

    def step(first, last):
        w1 = w1_ref[...].astype(BF16)
        w3 = w3_ref[...].astype(BF16)
        w2 = w2_ref[...].astype(BF16)
        for s in range(FFN_ROW_GROUPS):
            sl = slice(s * rows, (s + 1) * rows)
            if first:
                h = _mod_norm(x_ref[sl, :], g_ref[...], mod_ref[0, sh:sh + 1, :], mod_ref[0, sc:sc + 1, :]).astype(BF16)
                h_scr[sl] = h
            else:
                h = h_scr[sl]
            act = (_silu(_dot(h, w1)) * _dot(h, w3)).astype(BF16)
            part = _dot(act, w2)
            if first:
                acc_scr[sl] = part
            elif last:
                o_ref[sl, :] = x_ref[sl, :] + (0.5 * mod_ref[0, gt:gt + 1, :]) * (acc_scr[sl] + part)
            else:
                acc_scr[sl] += part

    pl.when(j == 0)(lambda: step(True, False))
    pl.when(jnp.logical_and(j > 0, j < last_j))(lambda: step(False, False))
    pl.when(j == last_j)(lambda: step(False, True))


def _ffn(x, mod, g, w1, w3, w2, layer, which, grp_fn, sh, sc, gt):
    rows = x.shape[0]
    tm, tf = FFN_TM, FFN_TF
    return pl.pallas_call(
        functools.partial(_ffn_kernel, sh=sh, sc=sc, gt=gt),
        grid=(rows // tm, D_FF // tf),
        in_specs=[
            pl.BlockSpec((tm, D_MODEL), lambda i, j: (i, 0)),
            pl.BlockSpec((1, N_ADA, D_MODEL), lambda i, j: (grp_fn(i, tm), 0, 0)),
            pl.BlockSpec((None, 1, D_MODEL), lambda i, j: (3 * layer + 2 * which, 0, 0)),
            pl.BlockSpec((None, None, D_MODEL, tf), lambda i, j: (layer, which, 0, j)),
            pl.BlockSpec((None, None, D_MODEL, tf), lambda i, j: (layer, which, 0, j)),
            pl.BlockSpec((None, None, tf, D_MODEL), lambda i, j: (layer, which, j, 0)),
        ],
        out_specs=pl.BlockSpec((tm, D_MODEL), lambda i, j: (i, 0)),
        out_shape=jax.ShapeDtypeStruct((rows, D_MODEL), F32),
        scratch_shapes=[pltpu.VMEM((tm, D_MODEL), BF16), pltpu.VMEM((tm, D_MODEL), F32)],
        compiler_params=_cparams(("arbitrary", "arbitrary")),
        name="ffn_half_step",
    )(x, mod, g, w1, w3, w2)


def _head_norm(xs, bsum, g):
    normed = []
    for p in range(xs.shape[1] // LANES):
        x = xs[:, p * LANES:(p + 1) * LANES]
        ms = _dot((x * x).astype(BF16), bsum)
        normed.append(x * lax.rsqrt(ms + EPS))
    return jnp.concatenate(normed, axis=-1) * g


def _rope(xs, cos, sin, first_half):
    swapped = jnp.where(first_half, pltpu.roll(xs, LANES - 16, 1), pltpu.roll(xs, 16, 1))
    return xs * cos + swapped * sin


def _dup_heads(x2, low_half):
    sw = pltpu.roll(x2, HEAD_DIM, 1)
    return jnp.where(low_half, x2, sw), jnp.where(low_half, sw, x2)


def _expand_kv(x2, low_half):
    d0, d1 = _dup_heads(x2, low_half)
    return jnp.concatenate([d0, x2, d1], axis=-1)


def _store_rows(ref, val, cache_layer):
    if len(ref.shape) == 2:
        ref[...] = val.astype(ref.dtype)
        return
    seq = ref.shape[-2]
    for s in range(ref.shape[0]):
        rows = val[s * seq:(s + 1) * seq].astype(ref.dtype)
        if len(ref.shape) == 3:
            ref[s] = rows
        else:
            for l in range(ref.shape[1]):
                ref[s, l] = rows if l == cache_layer else jnp.zeros_like(rows)


def _proj_kernel(*refs, rope, cache_layer, n_alias, seq_len):
    x_ref, mod_ref, g_ref, w_ref, bsum_ref, gains_ref, xprev_ref, xnext_ref, cw_ref, cb_ref = refs[:10]
    pos = 10
    if rope:
        cos_ref, sin_ref = refs[pos:pos + 2]
        pos += 2
    pos += n_alias
    x0_ref, z_ref, qn_ref, kn_ref, vn_ref, qg_ref, kge_ref, vge_ref = refs[pos:pos + 8]
    pos += 8
    emit_kv = cache_layer is not None
    if emit_kv:
        kg_ref, vg_ref = refs[pos:pos + 2]
        pos += 2
    u_scr, w_scr = refs[pos:pos + 2]
    scale = HEAD_DIM ** -0.5
    tm = x_ref.shape[0]

    @pl.when(pl.program_id(0) == 0)
    def _():
        w_scr[...] = w_ref[...].astype(BF16)

    def projected(x, cols):
        h = _mod_norm(x, g_ref[...], mod_ref[0, SH2:SH2 + 1, :], mod_ref[0, SC2:SC2 + 1, :])
        return _dot(h.astype(BF16), w_scr[:, cols])

    u_scr[...] = projected(x_ref[...], slice(0, IN_WIDTH))

    hy_cols = slice(0, OFF_NA)
    halo_prev = projected(xprev_ref[...], hy_cols)[7:8]
    halo_next = projected(xnext_ref[...], hy_cols)[0:1]
    row = lax.broadcasted_iota(jnp.int32, (tm, 1), 0)
    t_pos = (pl.program_id(0) * tm + row) & (seq_len - 1)

    def conv_chunk(ci):
        sl = slice(ci * HY_CH, (ci + 1) * HY_CH)
        u = u_scr[:, sl]
        prev = jnp.where(row == 0, halo_prev[:, sl], pltpu.roll(u, 1, 0))
        prev = jnp.where(t_pos == 0, 0.0, prev)
        nxt = jnp.where(row == tm - 1, halo_next[:, sl], pltpu.roll(u, tm - 1, 0))
        nxt = jnp.where(t_pos == seq_len - 1, 0.0, nxt)
        return prev * cw_ref[0:1, sl] + u * cw_ref[1:2, sl] + nxt * cw_ref[2:3, sl] + cb_ref[:, sl]

    x0_ref[...] = conv_chunk(0)
    z_ref[...] = conv_chunk(1) * conv_chunk(2)

    bsum = bsum_ref[...]
    qn = _head_norm(u_scr[:, OFF_NA:OFF_NA + NA_W], bsum, gains_ref[0:1, :])
    qn_ref[...] = (qn * scale).astype(qn_ref.dtype)
    kn = _head_norm(u_scr[:, OFF_NA + NA_W:OFF_NA + 2 * NA_W], bsum, gains_ref[1:2, :])
    _store_rows(kn_ref, kn, cache_layer)
    _store_rows(vn_ref, u_scr[:, OFF_NA + 2 * NA_W:OFF_GQ], cache_layer)

    qg = _head_norm(u_scr[:, OFF_GQ:OFF_GQ + GQ_W], bsum, gains_ref[2:3, :])
    kg = _head_norm(u_scr[:, OFF_GQ + GQ_W:OFF_GQ + GQ_W + GKV_W], bsum, gains_ref[3:4, 0:GKV_W])
    vg = u_scr[:, OFF_GQ + GQ_W + GKV_W:IN_WIDTH]
    if emit_kv:
        _store_rows(kg_ref, kg, cache_layer)
        _store_rows(vg_ref, vg, cache_layer)

    lane = lax.broadcasted_iota(jnp.int32, (1, LANES), 1)
    if rope:
        cos = cos_ref[...]
        sin = sin_ref[...]
        first_half = (lane % 32) < 16
        for p in range(GQ_W // LANES):
            sl = slice(p * LANES, (p + 1) * LANES)
            qg_ref[:, sl] = (_rope(qg[:, sl], cos, sin, first_half) * scale).astype(qg_ref.dtype)
        kg = _rope(kg, cos, sin, first_half)
    else:
        qg_ref[...] = (qg * scale).astype(qg_ref.dtype)

    low_half = lane < HEAD_DIM
    kge_ref[...] = _expand_kv(kg, low_half).astype(kge_ref.dtype)
    vge_ref[...] = _expand_kv(vg, low_half).astype(vge_ref.dtype)


def _project(x, mod, g, w_in, layer, bsum, gains, conv_w, conv_b, seq_len, grp_fn, rope_tabs, attn_dtype,
             cache_layer=None, prev_cache=None):
    rows = x.shape[0]
    tm = ROW_TM
    rope = rope_tabs is not None
    assert seq_len & (seq_len - 1) == 0 and (seq_len % tm == 0 or tm % seq_len == 0)
    sub = 8
    in_specs = [
        pl.BlockSpec((tm, D_MODEL), lambda i: (i, 0)),
        pl.BlockSpec((1, N_ADA, D_MODEL), lambda i: (grp_fn(i, tm), 0, 0)),
        pl.BlockSpec((None, 1, D_MODEL), lambda i: (3 * layer + 1, 0, 0)),
        pl.BlockSpec((None, D_MODEL, IN_WIDTH), lambda i: (layer, 0, 0), pipeline_mode=pl.Buffered(1)),
        pl.BlockSpec((LANES, LANES), lambda i: (0, 0)),
        pl.BlockSpec((None, 4, NA_W), lambda i: (layer, 0, 0)),
        pl.BlockSpec((sub, D_MODEL), lambda i: (jnp.maximum(i * (tm // sub) - 1, 0), 0)),
        pl.BlockSpec((sub, D_MODEL), lambda i: (jnp.minimum((i + 1) * (tm // sub), rows // sub - 1), 0)),
        pl.BlockSpec((3, OFF_NA), lambda i: (0, 0)),
        pl.BlockSpec((1, OFF_NA), lambda i: (0, 0)),
    ]
    args = [x, mod, g, w_in, bsum, gains, x, x, conv_w, conv_b.reshape(1, OFF_NA)]
    if rope:
        seq_tiles = DEC_SEQ // tm
        in_specs += [pl.BlockSpec((tm, LANES), lambda i: (i % seq_tiles, 0))] * 2
        args += list(rope_tabs)
    outs = [(HY_CH, F32)] * 2 + [(NA_W, attn_dtype)] * 3 + [(GQ_W, attn_dtype)] * 3
    if cache_layer is not None:
        outs += [(GKV_W, F32), (GKV_W, F32)]
    out_specs = [pl.BlockSpec((tm, w), lambda i: (i, 0)) for w, _ in outs]
    out_shape = [jax.ShapeDtypeStruct((rows, w), dt) for w, dt in outs]
    aliases = {}
    if cache_layer is not None:
        cache_outs = (3, 4, 8, 9)
        seqs = tm // SEQ
        for o in cache_outs:
            w = outs[o][0]
            out_shape[o] = jax.ShapeDtypeStruct((BATCH, DEPTH, SEQ, w), F32)
            if prev_cache is None:
                out_specs[o] = pl.BlockSpec((seqs, DEPTH, SEQ, w), lambda i: (i, 0, 0, 0))
            else:
                out_specs[o] = pl.BlockSpec((seqs, None, SEQ, w), lambda i: (i, cache_layer, 0, 0))
        if prev_cache is not None:
            aliases = {len(args) + n: o for n, o in enumerate(cache_outs)}
            in_specs += [pl.BlockSpec(memory_space=pl.ANY)] * len(cache_outs)
            args += list(prev_cache)
    return pl.pallas_call(
        functools.partial(_proj_kernel, rope=rope, cache_layer=cache_layer, n_alias=len(aliases), seq_len=seq_len),
        grid=(rows // tm,),
        in_specs=in_specs,
        out_specs=out_specs,
        out_shape=out_shape,
        input_output_aliases=aliases,
        scratch_shapes=[pltpu.VMEM((tm, IN_WIDTH), F32), pltpu.VMEM((D_MODEL, IN_WIDTH), BF16)],
        compiler_params=_cparams(("arbitrary",)),
        name="mixer_in_proj",
    )(*args)


HEADS_PER_CHAIN = 2
CHAIN_W = HEADS_PER_CHAIN * HEAD_DIM


def _log2(n):
    assert n & (n - 1) == 0
    return n.bit_length() - 1


def _stack_heads(q, n_heads):
    m, w = q.shape
    rows = lax.broadcasted_iota(jnp.int32, (n_heads * m, w), 0)
    lanes = lax.broadcasted_iota(jnp.int32, (n_heads * m, w), 1)
    own = jnp.right_shift(rows, _log2(m)) == jnp.right_shift(lanes, _log2(HEAD_DIM))
    return jnp.where(own, jnp.concatenate([q] * n_heads, axis=0), jnp.zeros((), q.dtype))


def _unstack_heads(o, n_heads):
    m = o.shape[0] // n_heads
    head = jnp.right_shift(lax.broadcasted_iota(jnp.int32, (m, o.shape[1]), 1), _log2(HEAD_DIM))
    out = o[0:m]
    for h in range(1, n_heads):
        out = jnp.where(head == h, o[h * m:(h + 1) * m], out)
    return out


def _sink_column(sink_ref, m, first_head, n_heads):
    block = jnp.right_shift(lax.broadcasted_iota(jnp.int32, (n_heads * m, 1), 0), _log2(m))
    col = jnp.full((n_heads * m, 1), sink_ref[first_head], F32)
    for h in range(1, n_heads):
        col = jnp.where(block == h, sink_ref[first_head + h], col)
    return col


def _softmax_pv(scores, values, sink):
    m = scores[0].max(axis=-1, keepdims=True)
    for s in scores[1:]:
        m = jnp.maximum(m, s.max(axis=-1, keepdims=True))
    if sink is not None:
        m = jnp.maximum(m, sink)
    den = None
    acc = None
    for s, v in zip(scores, values):
        p = jnp.exp(s - m)
        d = p.sum(axis=-1, keepdims=True)
        o = _dot(p.astype(BF16), v)
        den = d if den is None else den + d
        acc = o if acc is None else acc + o
    if sink is not None:
        den = den + jnp.exp(sink - m)
    return acc / den


CTX_SEQS_PER_STEP = 2


def _ctx_attn_kernel(sink_ref, qn_ref, kn_ref, vn_ref, qg_ref, kge_ref, vge_ref, yna_ref, ygq_ref):
    m = qn_ref.shape[1]
    low = lax.broadcasted_iota(jnp.int32, (m, LANES), 1) < HEAD_DIM
    for q_ref, k_ref, v_ref, y_ref, has_sink in ((qn_ref, kn_ref, vn_ref, yna_ref, False),
                                                  (qg_ref, kge_ref, vge_ref, ygq_ref, True)):
        for bi in range(CTX_SEQS_PER_STEP):
            for p in range(NA_W // LANES):
                sl = slice(p * LANES, (p + 1) * LANES)
                q2 = q_ref[bi, :, sl].astype(BF16)
                k2 = k_ref[bi, :, sl].astype(BF16)
                v2 = v_ref[bi, :, sl].astype(BF16)
                halves = []
                for half, keep in enumerate((low, ~low)):
                    qh = jnp.where(keep, q2, jnp.zeros((), BF16))
                    sink = sink_ref[2 * p + half] if has_sink else None
                    halves.append(_softmax_pv([_dot_nt(qh, k2)], [v2], sink))
                y_ref[bi, :, sl] = jnp.where(low, halves[0], halves[1]).astype(y_ref.dtype)


def _ctx_attention(sink, qn, kn_cache, vn_cache, qg, kge, vge, layer):
    b, l, _ = qn.shape

    nb = CTX_SEQS_PER_STEP

    def spec(w):
        return pl.BlockSpec((nb, l, w), lambda i: (i, 0, 0))

    cache_spec = pl.BlockSpec((nb, None, l, NA_W), lambda i: (i, layer, 0, 0))
    return pl.pallas_call(
        _ctx_attn_kernel,
        grid=(b // nb,),
        in_specs=[pl.BlockSpec(memory_space=pltpu.SMEM), spec(NA_W), cache_spec, cache_spec, spec(GQ_W),
                  spec(GQ_W), spec(GQ_W)],
        out_specs=[spec(NA_W), spec(GQ_W)],
        out_shape=[jax.ShapeDtypeStruct((b, l, NA_W), BF16), jax.ShapeDtypeStruct((b, l, GQ_W), BF16)],
        compiler_params=_cparams(("arbitrary",)),
        name="context_attention",
    )(sink, qn, kn_cache, vn_cache, qg, kge, vge)


NA_KEYS = NA_ROWS * GRID_W
NA_ROWS_PER_STEP = 8


def _na_bias_kernel(r_ref, oh_ref, mk_ref, o_ref):
    o_ref[...] = jnp.dot(r_ref[...], oh_ref[...], precision=_HIGHEST, preferred_element_type=F32) + mk_ref[...]


def _na_bias_tables():
    q = np.arange(GRID_W)[:, None]
    kc = np.arange(GRID_W)[None, :]
    win_lo = np.clip(q - NA_COLS // 2, 0, GRID_W - NA_COLS)
    ok = (kc >= win_lo) & (kc < win_lo + NA_COLS)
    j = kc - q + NA_COLS - 1
    onehot = np.zeros((LANES, GRID_W * GRID_W), np.float32)
    qq, kk = np.nonzero(ok)
    onehot[j[qq, kk], qq * GRID_W + kk] = 1.0
    mask = np.where(ok, 0.0, NEG_INF).astype(np.float32).reshape(1, GRID_W * GRID_W)
    return onehot, mask


def _na_bias(rpb_l):
    n_dr = 2 * NA_ROWS - 1
    rows = NA_HEADS * n_dr
    rows_pad = -(-rows // 8) * 8
    r = jnp.zeros((rows_pad, LANES), F32).at[:rows, :2 * NA_COLS - 1].set(rpb_l.reshape(rows, 2 * NA_COLS - 1))
    onehot, mask = _na_bias_tables()
    out = pl.pallas_call(
        _na_bias_kernel,
        out_shape=jax.ShapeDtypeStruct((rows_pad, GRID_W * GRID_W), F32),
        compiler_params=pltpu.CompilerParams(vmem_limit_bytes=VMEM_LIMIT),
        name="na_bias_expand",
    )(r, jnp.asarray(onehot), jnp.asarray(mask))
    bm = out[:rows].reshape(NA_HEADS, n_dr, GRID_W, GRID_W)
    bm2 = jnp.concatenate([bm[:, :-1], bm[:, 1:]], axis=-1)
    return bm2.transpose(1, 0, 2, 3).reshape(n_dr - 1, NA_HEADS * GRID_W, 2 * GRID_W)


def _na_attn_kernel(q_ref, k_ref, v_ref, kc_ref, vc_ref, bm_ref, o_ref):
    n_rows = DEC_SEQ // GRID_W
    kc = kc_ref[0, 0].astype(BF16)
    vc = vc_ref[0, 0].astype(BF16)
    for rr in range(NA_ROWS_PER_STEP):
        r = pl.program_id(1) * NA_ROWS_PER_STEP + rr
        start = jnp.clip(r - NA_ROWS // 2, 0, n_rows - NA_ROWS)
        shift = r - start
        row0 = pl.multiple_of(start * GRID_W, GRID_W)
        rows = slice(rr * GRID_W, (rr + 1) * GRID_W)
        q = _stack_heads(q_ref[0, rows, :], NA_HEADS)
        k = k_ref[0, pl.ds(row0, NA_KEYS), :]
        v = v_ref[0, pl.ds(row0, NA_KEYS), :]
        bias = jnp.concatenate([bm_ref[2 * jj - shift + NA_ROWS - 1] for jj in range(NA_ROWS // 2)], axis=-1)
        s_loc = _dot_nt(q, k) + bias
        s_ctx = _dot_nt(q, kc)
        o = _softmax_pv([s_loc, s_ctx], [v, vc], None)
        o_ref[0, rows, :] = _unstack_heads(o, NA_HEADS).astype(o_ref.dtype)


WIN_KEYS = 3 * BLK
WIN_BLOCKS_PER_STEP = 4


def _win_attn_kernel(sink_ref, mask_ref, q_ref, kge_ref, vge_ref, kc_ref, vc_ref, o_ref):
    n = kge_ref.shape[1]
    n_blocks = n // BLK
    low_c = lax.broadcasted_iota(jnp.int32, (1, LANES), 1) < HEAD_DIM
    kce = _expand_kv(kc_ref[0, 0], low_c).astype(BF16)
    vce = _expand_kv(vc_ref[0, 0], low_c).astype(BF16)
    n_chains = GQA_Q_HEADS // HEADS_PER_CHAIN
    sinks = [_sink_column(sink_ref, BLK, p * HEADS_PER_CHAIN, HEADS_PER_CHAIN) for p in range(n_chains)]
    for bb in range(WIN_BLOCKS_PER_STEP):
        nb = pl.program_id(1) * WIN_BLOCKS_PER_STEP + bb
        start = pl.multiple_of(jnp.clip((nb - 1) * BLK, 0, n - WIN_KEYS), BLK)
        variant = jnp.where(nb == 0, 0, jnp.where(nb == n_blocks - 1, 2, 1))
        rows = slice(bb * BLK, (bb + 1) * BLK)
        for p in range(n_chains):
            sl = slice(p * CHAIN_W, (p + 1) * CHAIN_W)
            q = _stack_heads(q_ref[0, rows, sl], HEADS_PER_CHAIN)
            s_loc = _dot_nt(q, kge_ref[0, pl.ds(start, WIN_KEYS), sl]) + mask_ref[variant]
            s_ctx = _dot_nt(q, kce[:, sl])
            o = _softmax_pv([s_loc, s_ctx], [vge_ref[0, pl.ds(start, WIN_KEYS), sl], vce[:, sl]], sinks[p])
            o_ref[0, rows, sl] = _unstack_heads(o, HEADS_PER_CHAIN).astype(o_ref.dtype)


def _win_mask_table():
    qi = np.arange(BLK)[:, None]
    kj = np.arange(WIN_KEYS)[None, :]
    tabs = [np.where(np.abs(qi + off - kj) <= GQA_WINDOW, 0.0, NEG_INF) for off in (0, BLK, 2 * BLK)]
    return jnp.asarray(np.stack([np.tile(t, (HEADS_PER_CHAIN, 1)) for t in tabs]).astype(np.float32))


def _latent_attn_kernel(sink_ref, mask_ref, bm_ref, qn_ref, kn_ref, vn_ref, kcn_ref, vcn_ref,
                        qg_ref, kge_ref, vge_ref, kcg_ref, vcg_ref, yna_ref, ygq_ref):
    _na_attn_kernel(qn_ref, kn_ref, vn_ref, kcn_ref, vcn_ref, bm_ref, yna_ref)
    _win_attn_kernel(sink_ref, mask_ref, qg_ref, kge_ref, vge_ref, kcg_ref, vcg_ref, ygq_ref)


def _latent_attention(sink, bm2, qn, kn, vn, kc_na, vc_na, qg, kge, vge, kc_gq, vc_gq, layer):
    b, n, _ = qn.shape
    mask = _win_mask_table()
    tq = WIN_BLOCKS_PER_STEP * BLK
    assert tq == NA_ROWS_PER_STEP * GRID_W

    def q_spec(w):
        return pl.BlockSpec((1, tq, w), lambda i, j: (i, j, 0))

    def seq_spec(w):
        return pl.BlockSpec((1, n, w), lambda i, j: (i, 0, 0))

    def cache_spec(w):
        return pl.BlockSpec((1, 1, PAST_LEN, w), lambda i, j: (i, layer, 0, 0))

    return pl.pallas_call(
        _latent_attn_kernel,
        grid=(b, n // tq),
        in_specs=[
            pl.BlockSpec(memory_space=pltpu.SMEM),
            pl.BlockSpec(mask.shape, lambda i, j: (0, 0, 0)),
            pl.BlockSpec(bm2.shape, lambda i, j: (0, 0, 0)),
            q_spec(NA_W), seq_spec(NA_W), seq_spec(NA_W), cache_spec(NA_W), cache_spec(NA_W),
            q_spec(GQ_W), seq_spec(GQ_W), seq_spec(GQ_W), cache_spec(GKV_W), cache_spec(GKV_W),
        ],
        out_specs=[q_spec(NA_W), q_spec(GQ_W)],
        out_shape=[jax.ShapeDtypeStruct((b, n, NA_W), BF16), jax.ShapeDtypeStruct((b, n, GQ_W), BF16)],
        compiler_params=_cparams(("arbitrary", "arbitrary")),
        name="latent_attention",
    )(sink, mask, bm2, qn, kn, vn, kc_na, vc_na, qg, kge, vge, kc_gq, vc_gq)


@functools.lru_cache(maxsize=None)
def _hyena_consts_np(L):
    n = 2 * L
    k = np.arange(L, dtype=np.int64)
    ang = (2.0 * np.pi / n) * ((k[:, None] * k[None, :]) % n).astype(np.float64)

    tkf = min(HY_TKF, L)
    cm = np.cos(ang).astype(np.float32).reshape(L // tkf, tkf, L)
    sm = (-np.sin(ang)).astype(np.float32).reshape(L // tkf, tkf, L)
    f_rows = np.concatenate([cm, sm], axis=1)
    f_cols = np.ascontiguousarray(f_rows.transpose(0, 2, 1))
    idx = np.arange(L, dtype=np.float32)
    t = idx / np.float32(L - 1)
    bands = np.linspace(1e-4, HY_BANDS - 1, HY_BANDS, dtype=np.float32)
    fang = np.float32(2.0 * math.pi / L) * idx[:, None] * bands[None, :]
    feats = np.zeros((L, LANES), np.float32)
    feats[:, 0] = t
    feats[:, 1:1 + HY_BANDS] = np.cos(fang)
    feats[:, 1 + HY_BANDS:HY_EMB] = -np.sin(fang)
    max_decay = math.log(HY_DECAY_TARGET) / HY_FAST_PCT
    min_decay = math.log(HY_DECAY_TARGET) / HY_SLOW_PCT
    deltas = np.abs(np.linspace(min_decay, max_decay, HY_CH, dtype=np.float32))
    decay = np.exp(-t[:, None] * deltas[None, :]).astype(np.float32)
    decay2 = np.concatenate([decay, decay], axis=1)
    return f_rows, f_cols, feats, decay2


def _hyena_consts(L):
    return tuple(jnp.asarray(a) for a in _hyena_consts_np(L))


def _hy_filter_kernel(feats_ref, w1_ref, b1_ref, w2_ref, b2_ref, w3_ref, freq_ref, decay_ref, wsum_ref, wdiff_ref):
    def dot_hi(a, b):
        a_hi, a_lo = _split(a)
        b_hi, b_lo = _split(b)
        return _dot(a_hi, b_hi) + (_dot(a_lo, b_hi) + _dot(a_hi, b_lo))

    half = feats_ref.shape[0] // 2
    feats2 = jnp.concatenate([feats_ref[0:half, :], feats_ref[half:2 * half, :]], axis=-1)
    hid = jnp.sin(freq_ref[0:1, :] * (dot_hi(feats2, w1_ref[...]) + b1_ref[...]))
    hid = jnp.sin(freq_ref[1:2, :] * (dot_hi(hid, w2_ref[...]) + b2_ref[...]))
    taps2 = dot_hi(hid, w3_ref[...])
    for part in range(2):
        rows = slice(part * half, (part + 1) * half)
        taps = taps2[:, part * 2 * HY_CH:(part + 1) * 2 * HY_CH] * decay_ref[rows, :]
        fwd = taps[:, 0:HY_CH]
        bwd = taps[:, HY_CH:2 * HY_CH]
        if part == 0:
            row = lax.broadcasted_iota(jnp.int32, bwd.shape, 0)
            bwd = jnp.where(row == 0, 0.0, bwd)
        wsum_ref[rows, :] = fwd + bwd
        wdiff_ref[rows, :] = fwd - bwd


def _block_diag2(w):
    d, r, c = w.shape
    return jnp.zeros((d, 2 * r, 2 * c), w.dtype).at[:, :r, :c].set(w).at[:, r:, c:].set(w)


def _hy_filter_params(w1, b1, w2, b2, w3, freq):
    w1p = jnp.pad(w1, ((0, 0), (0, LANES - HY_EMB), (0, 0)))
    return (_block_diag2(w1p), jnp.tile(b1, (1, 2))[:, None, :], _block_diag2(w2), jnp.tile(b2, (1, 2))[:, None, :],
            _block_diag2(w3), jnp.tile(freq, (1, 1, 2)))


def _hy_filter(L, feats, decay2, params, layer):
    def whole(a):
        return pl.BlockSpec(a.shape, lambda i: (0,) * a.ndim)

    def of_layer(a):
        return pl.BlockSpec((None,) + a.shape[1:], lambda i: (layer,) + (0,) * (a.ndim - 1))

    out_spec = pl.BlockSpec((L, HY_CH), lambda i: (0, 0))
    return pl.pallas_call(
        _hy_filter_kernel,
        grid=(1,),
        in_specs=[whole(feats)] + [of_layer(p) for p in params] + [whole(decay2)],
        out_specs=[out_spec, out_spec],
        out_shape=[jax.ShapeDtypeStruct((L, HY_CH), F32)] * 2,
        compiler_params=_cparams(("arbitrary",)),
        name="hyena_filter",
    )(feats, *params, decay2)


def _hy_conv_kernel(z_ref, x0_ref, wsum_ref, wdiff_ref, skip_ref, fr_ref, fc_ref,
                    o_ref, z_scr, ws_scr, wd_scr, acc_scr, *, group):
    t = pl.program_id(1)
    L = z_ref.shape[1]
    n = 2 * L

    @pl.when(t == 0)
    def _():
        z_scr[...] = z_ref[...].astype(BF16)
        ws_scr[...] = wsum_ref[...].astype(BF16)
        wd_scr[...] = wdiff_ref[...].astype(BF16)
        acc_scr[...] = jnp.zeros_like(acc_scr)

    fr = fr_ref[...].astype(BF16)
    fc = fc_ref[...].astype(BF16)
    tkf = fr.shape[0] // 2
    k_re = _dot(fr[0:tkf], ws_scr[...])
    k_im = _dot(fr[tkf:2 * tkf], wd_scr[...])
    for b in range(group):
        zf = _dot(fr, z_scr[b])
        z_re, z_im = zf[0:tkf], zf[tkf:2 * tkf]
        y = jnp.concatenate([z_re * k_re - z_im * k_im, z_re * k_im + z_im * k_re], axis=0).astype(BF16)
        acc_scr[b] += _dot(fc, y)

    @pl.when(t == pl.num_programs(1) - 1)
    def _():
        row = lax.broadcasted_iota(jnp.int32, (L, 1), 0)
        sgn = (1 - 2 * (row % 2)).astype(F32)
        w = wsum_ref[...]
        w_dc = w.sum(axis=0, keepdims=True)
        w_ny = (w * sgn).sum(axis=0, keepdims=True)
        for b in range(group):
            z = z_ref[b]
            z_dc = z.sum(axis=0, keepdims=True)
            z_ny = (z * sgn).sum(axis=0, keepdims=True)
            conv = (2.0 / n) * acc_scr[b] - (1.0 / n) * (z_dc * w_dc) + (1.0 / n) * (sgn * (z_ny * w_ny))
            o_ref[b] = (x0_ref[b] * (conv + z * skip_ref[...])).astype(o_ref.dtype)


def _hy_conv(z, x0, wsum, wdiff, skip, f_rows, f_cols, group):
    b, L, c = z.shape
    n_tiles, two_tkf, _ = f_rows.shape
    row_spec = pl.BlockSpec((None, two_tkf, L), lambda g, t: (t, 0, 0))
    col_spec = pl.BlockSpec((None, L, two_tkf), lambda g, t: (t, 0, 0))
    once = pl.Buffered(1)
    seq_spec = pl.BlockSpec((group, L, c), lambda g, t: (g, 0, 0), pipeline_mode=once)
    w_spec = pl.BlockSpec((L, c), lambda g, t: (0, 0), pipeline_mode=once)
    return pl.pallas_call(
        functools.partial(_hy_conv_kernel, group=group),
        grid=(b // group, n_tiles),
        in_specs=[seq_spec, seq_spec, w_spec, w_spec, pl.BlockSpec((1, c), lambda g, t: (0, 0)),
                  row_spec, col_spec],
        out_specs=pl.BlockSpec((group, L, c), lambda g, t: (g, 0, 0)),
        out_shape=jax.ShapeDtypeStruct((b, L, c), BF16),
        scratch_shapes=[pltpu.VMEM((group, L, c), BF16), pltpu.VMEM((L, c), BF16), pltpu.VMEM((L, c), BF16),
                        pltpu.VMEM((group, L, c), F32)],
        compiler_params=_cparams(("arbitrary", "arbitrary")),
        name="hyena_long_conv",
    )(z, x0, wsum, wdiff, skip.reshape(1, c), f_rows, f_cols)


def _merge_kernel(x_ref, yhy_ref, yna_ref, ygq_ref, mod_ref, w_ref, o_ref, w_scr):
    @pl.when(pl.program_id(0) == 0)
    def _():
        w_scr[...] = w_ref[...].astype(BF16)

    y = jnp.concatenate([yhy_ref[...], yna_ref[...], ygq_ref[...]], axis=-1)
    o_ref[...] = x_ref[...] + mod_ref[0, G2:G2 + 1, :] * _dot(y, w_scr[...])


def _merge(x, y_hy, y_na, y_gq, mod, w_out, layer, grp_fn):
    rows = x.shape[0]
    tm = MERGE_TM
    return pl.pallas_call(
        _merge_kernel,
        grid=(rows // tm,),
        in_specs=[
            pl.BlockSpec((tm, D_MODEL), lambda i: (i, 0)),
            pl.BlockSpec((tm, HY_CH), lambda i: (i, 0)),
            pl.BlockSpec((tm, NA_W), lambda i: (i, 0)),
            pl.BlockSpec((tm, GQ_W), lambda i: (i, 0)),
            pl.BlockSpec((1, N_ADA, D_MODEL), lambda i: (grp_fn(i, tm), 0, 0)),
            pl.BlockSpec((None, D_MODEL, D_MODEL), lambda i: (layer, 0, 0), pipeline_mode=pl.Buffered(1)),
        ],
        out_specs=pl.BlockSpec((tm, D_MODEL), lambda i: (i, 0)),
        out_shape=jax.ShapeDtypeStruct((rows, D_MODEL), F32),
        scratch_shapes=[pltpu.VMEM((D_MODEL, D_MODEL), BF16)],
        compiler_params=_cparams(("arbitrary",)),
        name="mixer_out_proj",
    )(x, y_hy, y_na, y_gq, mod, w_out)


def _rope_tables():
    pos = np.arange(DEC_SEQ)
    quarter = HEAD_DIM // 4
    inv = ROPE_BASE ** (-np.arange(quarter, dtype=np.float64) / quarter)
    ang_r = (pos // GRID_W)[:, None] * inv[None, :]
    ang_c = (pos % GRID_W)[:, None] * inv[None, :]
    cos_h = np.concatenate([np.cos(ang_r), np.cos(ang_r), np.cos(ang_c), np.cos(ang_c)], axis=1)
    sin_h = np.concatenate([-np.sin(ang_r), np.sin(ang_r), -np.sin(ang_c), np.sin(ang_c)], axis=1)
    reps = LANES // HEAD_DIM
    return (jnp.asarray(np.tile(cos_h, (1, reps)), F32), jnp.asarray(np.tile(sin_h, (1, reps)), F32))


def _head_sum_matrix():
    idx = np.arange(LANES) // HEAD_DIM
    return jnp.asarray((idx[:, None] == idx[None, :]).astype(np.float32) / HEAD_DIM, BF16)


def _grp_prompt(i, tm):
    return 0


def _grp_sample(i, tm):
    return 1 + (i * tm) // DEC_SEQ


def kernel(x_prompt, x_sample, cache_na_k, cache_na_v, cache_gqa_k, cache_gqa_v, c, c_ctx, ada_w, ada_b, norm_g, ffn_w1, ffn_w3, ffn_w2, w_in, w_out, hy_conv_w, hy_conv_b, hy_filt_w1, hy_filt_b1, hy_filt_w2, hy_filt_b2, hy_filt_w3, hy_freq, hy_skip, na_q_g, na_k_g, na_rpb, gqa_q_g, gqa_k_g, gqa_sink):
    xp = x_prompt.reshape(BATCH * SEQ, D_MODEL)
    xs = x_sample.reshape(DEC_BATCH * DEC_SEQ, D_MODEL)
    cc = jnp.zeros((8, D_MODEL), F32).at[0].set(c_ctx).at[1:1 + DEC_BATCH].set(c)
    kc_na = cache_na_k.reshape(DEC_BATCH, DEPTH, PAST_LEN, NA_W)
    vc_na = cache_na_v.reshape(DEC_BATCH, DEPTH, PAST_LEN, NA_W)
    kc_gq = cache_gqa_k.reshape(DEC_BATCH, DEPTH, PAST_LEN, GKV_W)
    vc_gq = cache_gqa_v.reshape(DEC_BATCH, DEPTH, PAST_LEN, GKV_W)

    rope_tabs = _rope_tables()
    bsum = _head_sum_matrix()
    hy_p = _hyena_consts(SEQ)
    hy_s = _hyena_consts(DEC_SEQ)

    gains = jnp.tile(jnp.stack([na_q_g, na_k_g, gqa_q_g, gqa_k_g], axis=1), (1, 1, NA_HEADS))
    norm_rows = norm_g.reshape(DEPTH * 3, 1, D_MODEL)
    ng = [norm_rows] * 3
    filt_params = _hy_filter_params(hy_filt_w1, hy_filt_b1, hy_filt_w2, hy_filt_b2, hy_filt_w3, hy_freq)

    cache = None
    for l in range(DEPTH):
        mod = _adaln_mod(cc, ada_w, ada_b, l)
        bm2 = _na_bias(na_rpb[l])
        filt = {}
        for L, consts in ((SEQ, hy_p), (DEC_SEQ, hy_s)):
            filt[L] = _hy_filter(L, consts[2], consts[3], filt_params, l)

        xp = _ffn(xp, mod, ng[0], ffn_w1, ffn_w3, ffn_w2, l, 0, _grp_prompt, SH1, SC1, G1)
        x0, z, qn, kn, vn, qg, kgd, vgd, kg, vg = _project(
            xp, mod, ng[1], w_in, l, bsum, gains, hy_conv_w[l], hy_conv_b[l], SEQ, _grp_prompt, None, F32,
            cache_layer=l, prev_cache=cache)
        cache = (kn, vn, kg, vg)
        shp = lambda a: a.reshape(BATCH, SEQ, a.shape[-1])
        y_hy = _hy_conv(shp(z), shp(x0), filt[SEQ][0], filt[SEQ][1], hy_skip[l], hy_p[0], hy_p[1], group=BATCH)
        y_na, y_gq = _ctx_attention(gqa_sink[l], shp(qn), kn, vn, shp(qg), shp(kgd), shp(vgd), l)
        xp = _merge(xp, y_hy.reshape(-1, HY_CH), y_na.reshape(-1, NA_W), y_gq.reshape(-1, GQ_W), mod, w_out, l,
                    _grp_prompt)
        xp = _ffn(xp, mod, ng[2], ffn_w1, ffn_w3, ffn_w2, l, 1, _grp_prompt, SH3, SC3, G3)

        xs = _ffn(xs, mod, ng[0], ffn_w1, ffn_w3, ffn_w2, l, 0, _grp_sample, SH1, SC1, G1)
        x0, z, qn, kn, vn, qg, kgd, vgd = _project(
            xs, mod, ng[1], w_in, l, bsum, gains, hy_conv_w[l], hy_conv_b[l], DEC_SEQ, _grp_sample, rope_tabs, BF16)
        shs = lambda a: a.reshape(DEC_BATCH, DEC_SEQ, a.shape[-1])
        y_hy = _hy_conv(shs(z), shs(x0), filt[DEC_SEQ][0], filt[DEC_SEQ][1], hy_skip[l], hy_s[0], hy_s[1],
                        group=DEC_BATCH)
        y_na, y_gq = _latent_attention(gqa_sink[l], bm2, shs(qn), shs(kn), shs(vn), kc_na, vc_na,
                                       shs(qg), shs(kgd), shs(vgd), kc_gq, vc_gq, l)
        xs = _merge(xs, y_hy.reshape(-1, HY_CH), y_na.reshape(-1, NA_W), y_gq.reshape(-1, GQ_W), mod, w_out, l,
                    _grp_sample)
        xs = _ffn(xs, mod, ng[2], ffn_w1, ffn_w3, ffn_w2, l, 1, _grp_sample, SH3, SC3, G3)

    kn, vn, kg, vg = cache
    return (xp.reshape(BATCH, SEQ, D_MODEL), xs.reshape(DEC_BATCH, DEC_SEQ, D_MODEL),
            kn.reshape(BATCH, DEPTH, SEQ, NA_HEADS, HEAD_DIM), vn.reshape(BATCH, DEPTH, SEQ, NA_HEADS, HEAD_DIM),
            kg.reshape(BATCH, DEPTH, SEQ, GQA_KV_HEADS, HEAD_DIM), vg.reshape(BATCH, DEPTH, SEQ, GQA_KV_HEADS, HEAD_DIM))
```

```python
import functools
import math

import numpy as np
import jax
import jax.numpy as jnp
from jax import lax
from jax.experimental import pallas as pl
from jax.experimental.pallas import tpu as pltpu

F32 = jnp.float32
BF16 = jnp.bfloat16

D_MODEL = 1024
BATCH = 16
SEQ = 256
DEPTH = 2
DEC_BATCH = 2
DEC_SEQ = 2048
PAST_LEN = 256
GRID_W = 64
HEAD_DIM = 64
HY_CH = 256
NA_HEADS = 6
GQA_Q_HEADS = 6
GQA_KV_HEADS = 2
NA_W = NA_HEADS * HEAD_DIM
GQ_W = GQA_Q_HEADS * HEAD_DIM
GKV_W = GQA_KV_HEADS * HEAD_DIM
IN_WIDTH = 3 * HY_CH + 3 * NA_W + GQ_W + 2 * GKV_W
D_FF = 2816
N_ADA = 9
HY_BANDS = 16
HY_EMB = 1 + 2 * HY_BANDS
HY_FILT_W = 64
HY_DECAY_TARGET = 1e-2
HY_FAST_PCT = 0.3
HY_SLOW_PCT = 1.5
NA_ROWS = 8
NA_COLS = 16
GQA_WINDOW = 128
BLK = 128
ROPE_BASE = 10000.0
EPS = 1e-6
NEG_INF = -1e30

LANES = 128
V7X_VMEM_BYTES = 64 * 1024 * 1024
VMEM_LIMIT = V7X_VMEM_BYTES * 15 // 16

OFF_NA = 3 * HY_CH
OFF_GQ = OFF_NA + 3 * NA_W

SH1, SC1, G1, SH2, SC2, G2, SH3, SC3, G3 = range(N_ADA)

FFN_TM = 2048
FFN_TF = 256
FFN_TM_MERGE = 1024
FFN_GROUP_ROWS = 512
ROW_TM = 512
MERGE_TM = 1024
HY_TKF = 256

_HIGHEST = lax.Precision.HIGHEST


def _cparams(sem):
    return pltpu.CompilerParams(dimension_semantics=sem, vmem_limit_bytes=VMEM_LIMIT)


def _silu(x):
    return x * (1.0 / (1.0 + jnp.exp(-x)))


def _dot(a, b):
    return jnp.dot(a, b, preferred_element_type=F32)


def _dot_nt(a, b):
    return lax.dot_general(a, b, (((1,), (1,)), ((), ())), preferred_element_type=F32)


def _split(x):
    hi = x.astype(BF16)
    lo = (x - hi.astype(F32)).astype(BF16)
    return hi, lo


def _mod_norm(x, g, shift, scale):
    ms = jnp.mean(x * x, axis=-1, keepdims=True)
    return (x * lax.rsqrt(ms + EPS) * g) * (1.0 + scale) + shift


def _mod_kernel(c_ref, w_ref, b_ref, o_ref):
    a = _silu(c_ref[...]).astype(BF16)
    o_ref[...] = _dot(a, w_ref[...].astype(BF16)) + b_ref[...]


def _adaln_mod(cc, ada_w, ada_b, layer):
    n_out = N_ADA * D_MODEL
    tn = n_out // 4
    out = pl.pallas_call(
        _mod_kernel,
        grid=(n_out // tn,),
        in_specs=[
            pl.BlockSpec((8, D_MODEL), lambda j: (0, 0)),
            pl.BlockSpec((None, D_MODEL, tn), lambda j: (layer, 0, j)),
            pl.BlockSpec((None, 1, tn), lambda j: (layer, 0, j)),
        ],
        out_specs=pl.BlockSpec((8, tn), lambda j: (0, j)),
        out_shape=jax.ShapeDtypeStruct((8, n_out), F32),
        compiler_params=_cparams(("arbitrary",)),
        name="adaln_mod",
    )(cc, ada_w, ada_b.reshape(DEPTH, 1, n_out))
    return out.reshape(8, N_ADA, D_MODEL)


def _ffn_kernel(*refs, sh, sc, gt, merge):
    x_ref, mod_ref, g_ref, w1_ref, w3_ref, w2_ref = refs[:6]
    if merge:
        yhy_ref, yna_ref, ygq_ref, wout_ref = refs[6:10]
    o_ref, h_scr, acc_scr = refs[-3:]
    j = pl.program_id(1)
    last_j = pl.num_programs(1) - 1
    n_groups = h_scr.shape[0] // FFN_GROUP_ROWS

    def step(first, last):
        w1 = w1_ref[...].astype(BF16)
        w3 = w3_ref[...].astype(BF16)
        w2 = w2_ref[...].astype(BF16)
        if first and merge:
            w_out = wout_ref[...].astype(BF16)
        for s in range(n_groups):
            sl = slice(s * FFN_GROUP_ROWS, (s + 1) * FFN_GROUP_ROWS)
            if first:
                x = x_ref[sl, :]
                if merge:
                    y = jnp.concatenate([yhy_ref[sl, :], yna_ref[sl, :], ygq_ref[sl, :]], axis=-1)
                    x = x + mod_ref[0, G2:G2 + 1, :] * _dot(y, w_out)
                    o_ref[sl, :] = x
                h = _mod_norm(x, g_ref[...], mod_ref[0, sh:sh + 1, :], mod_ref[0, sc:sc + 1, :]).astype(BF16)
                h_scr[sl] = h
            else:
                h = h_scr[sl]
            act = (_silu(_dot(h, w1)) * _dot(h, w3)).astype(BF16)
            part = _dot(act, w2)
            if first:
                acc_scr[sl] = part
            elif last:
                base = o_ref[sl, :] if merge else x_ref[sl, :]
                o_ref[sl, :] = base + (0.5 * mod_ref[0, gt:gt + 1, :]) * (acc_scr[sl] + part)
            else:
                acc_scr[sl] += part

    pl.when(j == 0)(lambda: step(True, False))
    pl.when(jnp.logical_and(j > 0, j < last_j))(lambda: step(False, False))
    pl.when(j == last_j)(lambda: step(False, True))


def _ffn(x, mod, g, w1, w3, w2, layer, which, grp_fn, sh, sc, gt, mixer_out=None):
    rows = x.shape[0]
    merge = mixer_out is not None
    tm, tf = (FFN_TM_MERGE if merge else FFN_TM), FFN_TF

    def row_spec(w):
        return pl.BlockSpec((tm, w), lambda i, j: (i, 0))

    in_specs = [
        row_spec(D_MODEL),
        pl.BlockSpec((1, N_ADA, D_MODEL), lambda i, j: (grp_fn(i, tm), 0, 0)),
        pl.BlockSpec((None, 1, D_MODEL), lambda i, j: (3 * layer + 2 * which, 0, 0)),
        pl.BlockSpec((None, None, D_MODEL, tf), lambda i, j: (layer, which, 0, j)),
        pl.BlockSpec((None, None, D_MODEL, tf), lambda i, j: (layer, which, 0, j)),
        pl.BlockSpec((None, None, tf, D_MODEL), lambda i, j: (layer, which, j, 0)),
    ]
    args = [x, mod, g, w1, w3, w2]
    if merge:
        in_specs += [row_spec(HY_CH), row_spec(NA_W), row_spec(GQ_W),
                     pl.BlockSpec((None, D_MODEL, D_MODEL), lambda i, j: (layer, 0, 0), pipeline_mode=pl.Buffered(1))]
        args += list(mixer_out)
    return pl.pallas_call(
        functools.partial(_ffn_kernel, sh=sh, sc=sc, gt=gt, merge=merge),
        grid=(rows // tm, D_FF // tf),
        in_specs=in_specs,
        out_specs=row_spec(D_MODEL),
        out_shape=jax.ShapeDtypeStruct((rows, D_MODEL), F32),
        scratch_shapes=[pltpu.VMEM((tm, D_MODEL), BF16), pltpu.VMEM((tm, D_MODEL), F32)],
        compiler_params=_cparams(("arbitrary", "arbitrary")),
        name="ffn_half_step",
    )(*args)


def _head_norm(xs, bsum, g):
    normed = []
    for p in range(xs.shape[1] // LANES):
        x = xs[:, p * LANES:(p + 1) * LANES]
        ms = _dot((x * x).astype(BF16), bsum)
        normed.append(x * lax.rsqrt(ms + EPS))
    return jnp.concatenate(normed, axis=-1) * g


def _rope(xs, cos, sin, first_half):
    swapped = jnp.where(first_half, pltpu.roll(xs, LANES - 16, 1), pltpu.roll(xs, 16, 1))
    return xs * cos + swapped * sin


def _dup_heads(x2, low_half):
    sw = pltpu.roll(x2, HEAD_DIM, 1)
    return jnp.where(low_half, x2, sw), jnp.where(low_half, sw, x2)


def _expand_kv(x2, low_half):
    d0, d1 = _dup_heads(x2, low_half)
    return jnp.concatenate([d0, x2, d1], axis=-1)


def _store_rows(ref, val, cache_layer):
    if len(ref.shape) == 2:
        ref[...] = val.astype(ref.dtype)
        return
    seq = ref.shape[-2]
    for s in range(ref.shape[0]):
        rows = val[s * seq:(s + 1) * seq].astype(ref.dtype)
        if len(ref.shape) == 3:
            ref[s] = rows
        else:
            for l in range(ref.shape[1]):
                ref[s, l] = rows if l == cache_layer else jnp.zeros_like(rows)


def _proj_kernel(*refs, rope, cache_layer, n_alias, seq_len):
    x_ref, mod_ref, g_ref, w_ref, bsum_ref, gains_ref, xprev_ref, xnext_ref, cw_ref, cb_ref = refs[:10]
    pos = 10
    if rope:
        cos_ref, sin_ref = refs[pos:pos + 2]
        pos += 2
    pos += n_alias
    x0_ref, z_ref, qn_ref, kn_ref, vn_ref, qg_ref, kge_ref, vge_ref = refs[pos:pos + 8]
    pos += 8
    emit_kv = cache_layer is not None
    if emit_kv:
        kg_ref, vg_ref = refs[pos:pos + 2]
        pos += 2
    u_scr, w_scr = refs[pos:pos + 2]
    scale = HEAD_DIM ** -0.5
    tm = x_ref.shape[0]

    @pl.when(pl.program_id(0) == 0)
    def _():
        w_scr[...] = w_ref[...].astype(BF16)

    def projected(x, cols):
        h = _mod_norm(x, g_ref[...], mod_ref[0, SH2:SH2 + 1, :], mod_ref[0, SC2:SC2 + 1, :])
        return _dot(h.astype(BF16), w_scr[:, cols])

    u_scr[...] = projected(x_ref[...], slice(0, IN_WIDTH))

    hy_cols = slice(0, OFF_NA)
    halo_prev = projected(xprev_ref[...], hy_cols)[7:8]
    halo_next = projected(xnext_ref[...], hy_cols)[0:1]
    row = lax.broadcasted_iota(jnp.int32, (tm, 1), 0)
    t_pos = (pl.program_id(0) * tm + row) & (seq_len - 1)

    def conv_chunk(ci):
        sl = slice(ci * HY_CH, (ci + 1) * HY_CH)
        u = u_scr[:, sl]
        prev = jnp.where(row == 0, halo_prev[:, sl], pltpu.roll(u, 1, 0))
        prev = jnp.where(t_pos == 0, 0.0, prev)
        nxt = jnp.where(row == tm - 1, halo_next[:, sl], pltpu.roll(u, tm - 1, 0))
        nxt = jnp.where(t_pos == seq_len - 1, 0.0, nxt)
        return prev * cw_ref[0:1, sl] + u * cw_ref[1:2, sl] + nxt * cw_ref[2:3, sl] + cb_ref[:, sl]

    x0_ref[...] = conv_chunk(0)
    z_ref[...] = conv_chunk(1) * conv_chunk(2)

    bsum = bsum_ref[...]
    qn = _head_norm(u_scr[:, OFF_NA:OFF_NA + NA_W], bsum, gains_ref[0:1, :])
    qn_ref[...] = (qn * scale).astype(qn_ref.dtype)
    kn = _head_norm(u_scr[:, OFF_NA + NA_W:OFF_NA + 2 * NA_W], bsum, gains_ref[1:2, :])
    _store_rows(kn_ref, kn, cache_layer)
    _store_rows(vn_ref, u_scr[:, OFF_NA + 2 * NA_W:OFF_GQ], cache_layer)

    qg = _head_norm(u_scr[:, OFF_GQ:OFF_GQ + GQ_W], bsum, gains_ref[2:3, :])
    kg = _head_norm(u_scr[:, OFF_GQ + GQ_W:OFF_GQ + GQ_W + GKV_W], bsum, gains_ref[3:4, 0:GKV_W])
    vg = u_scr[:, OFF_GQ + GQ_W + GKV_W:IN_WIDTH]
    if emit_kv:
        _store_rows(kg_ref, kg, cache_layer)
        _store_rows(vg_ref, vg, cache_layer)

    lane = lax.broadcasted_iota(jnp.int32, (1, LANES), 1)
    if rope:
        cos = cos_ref[...]
        sin = sin_ref[...]
        first_half = (lane % 32) < 16
        for p in range(GQ_W // LANES):
            sl = slice(p * LANES, (p + 1) * LANES)
            qg_ref[:, sl] = (_rope(qg[:, sl], cos, sin, first_half) * scale).astype(qg_ref.dtype)
        kg = _rope(kg, cos, sin, first_half)
    else:
        qg_ref[...] = (qg * scale).astype(qg_ref.dtype)

    low_half = lane < HEAD_DIM
    kge_ref[...] = _expand_kv(kg, low_half).astype(kge_ref.dtype)
    vge_ref[...] = _expand_kv(vg, low_half).astype(vge_ref.dtype)


def _project(x, mod, g, w_in, layer, bsum, gains, conv_w, conv_b, seq_len, grp_fn, rope_tabs, attn_dtype,
             cache_layer=None, prev_cache=None):
    rows = x.shape[0]
    tm = ROW_TM
    rope = rope_tabs is not None
    assert seq_len & (seq_len - 1) == 0 and (seq_len % tm == 0 or tm % seq_len == 0)
    sub = 8
    in_specs = [
        pl.BlockSpec((tm, D_MODEL), lambda i: (i, 0)),
        pl.BlockSpec((1, N_ADA, D_MODEL), lambda i: (grp_fn(i, tm), 0, 0)),
        pl.BlockSpec((None, 1, D_MODEL), lambda i: (3 * layer + 1, 0, 0)),
        pl.BlockSpec((None, D_MODEL, IN_WIDTH), lambda i: (layer, 0, 0), pipeline_mode=pl.Buffered(1)),
        pl.BlockSpec((LANES, LANES), lambda i: (0, 0)),
        pl.BlockSpec((None, 4, NA_W), lambda i: (layer, 0, 0)),
        pl.BlockSpec((sub, D_MODEL), lambda i: (jnp.maximum(i * (tm // sub) - 1, 0), 0)),
        pl.BlockSpec((sub, D_MODEL), lambda i: (jnp.minimum((i + 1) * (tm // sub), rows // sub - 1), 0)),
        pl.BlockSpec((3, OFF_NA), lambda i: (0, 0)),
        pl.BlockSpec((1, OFF_NA), lambda i: (0, 0)),
    ]
    args = [x, mod, g, w_in, bsum, gains, x, x, conv_w, conv_b.reshape(1, OFF_NA)]
    if rope:
        seq_tiles = DEC_SEQ // tm
        in_specs += [pl.BlockSpec((tm, LANES), lambda i: (i % seq_tiles, 0))] * 2
        args += list(rope_tabs)
    outs = [(HY_CH, F32)] * 2 + [(NA_W, attn_dtype)] * 3 + [(GQ_W, attn_dtype)] * 3
    if cache_layer is not None:
        outs += [(GKV_W, F32), (GKV_W, F32)]
    out_specs = [pl.BlockSpec((tm, w), lambda i: (i, 0)) for w, _ in outs]
    out_shape = [jax.ShapeDtypeStruct((rows, w), dt) for w, dt in outs]
    aliases = {}
    if cache_layer is not None:
        cache_outs = (3, 4, 8, 9)
        seqs = tm // SEQ
        for o in cache_outs:
            w = outs[o][0]
            out_shape[o] = jax.ShapeDtypeStruct((BATCH, DEPTH, SEQ, w), F32)
            if prev_cache is None:
                out_specs[o] = pl.BlockSpec((seqs, DEPTH, SEQ, w), lambda i: (i, 0, 0, 0))
            else:
                out_specs[o] = pl.BlockSpec((seqs, None, SEQ, w), lambda i: (i, cache_layer, 0, 0))
        if prev_cache is not None:
            aliases = {len(args) + n: o for n, o in enumerate(cache_outs)}
            in_specs += [pl.BlockSpec(memory_space=pl.ANY)] * len(cache_outs)
            args += list(prev_cache)
    return pl.pallas_call(
        functools.partial(_proj_kernel, rope=rope, cache_layer=cache_layer, n_alias=len(aliases), seq_len=seq_len),
        grid=(rows // tm,),
        in_specs=in_specs,
        out_specs=out_specs,
        out_shape=out_shape,
        input_output_aliases=aliases,
        scratch_shapes=[pltpu.VMEM((tm, IN_WIDTH), F32), pltpu.VMEM((D_MODEL, IN_WIDTH), BF16)],
        compiler_params=_cparams(("arbitrary",)),
        name="mixer_in_proj",
    )(*args)


HEADS_PER_CHAIN = 2
CHAIN_W = HEADS_PER_CHAIN * HEAD_DIM


def _log2(n):
    assert n & (n - 1) == 0
    return n.bit_length() - 1


def _stack_heads(q, n_heads):
    m, w = q.shape
    rows = lax.broadcasted_iota(jnp.int32, (n_heads * m, w), 0)
    lanes = lax.broadcasted_iota(jnp.int32, (n_heads * m, w), 1)
    own = jnp.right_shift(rows, _log2(m)) == jnp.right_shift(lanes, _log2(HEAD_DIM))
    return jnp.where(own, jnp.concatenate([q] * n_heads, axis=0), jnp.zeros((), q.dtype))


def _unstack_heads(o, n_heads):
    m = o.shape[0] // n_heads
    head = jnp.right_shift(lax.broadcasted_iota(jnp.int32, (m, o.shape[1]), 1), _log2(HEAD_DIM))
    out = o[0:m]
    for h in range(1, n_heads):
        out = jnp.where(head == h, o[h * m:(h + 1) * m], out)
    return out


def _sink_column(sink_ref, m, first_head, n_heads):
    block = jnp.right_shift(lax.broadcasted_iota(jnp.int32, (n_heads * m, 1), 0), _log2(m))
    col = jnp.full((n_heads * m, 1), sink_ref[first_head], F32)
    for h in range(1, n_heads):
        col = jnp.where(block == h, sink_ref[first_head + h], col)
    return col


def _softmax_pv(scores, values, sink):
    m = scores[0].max(axis=-1, keepdims=True)
    for s in scores[1:]:
        m = jnp.maximum(m, s.max(axis=-1, keepdims=True))
    if sink is not None:
        m = jnp.maximum(m, sink)
    den = None
    acc = None
    for s, v in zip(scores, values):
        p = jnp.exp(s - m)
        d = p.sum(axis=-1, keepdims=True)
        o = _dot(p.astype(BF16), v)
        den = d if den is None else den + d
        acc = o if acc is None else acc + o
    if sink is not None:
        den = den + jnp.exp(sink - m)
    return acc / den


CTX_SEQS_PER_STEP = 2


def _ctx_attn_kernel(sink_ref, qn_ref, kn_ref, vn_ref, qg_ref, kge_ref, vge_ref, yna_ref, ygq_ref):
    m = qn_ref.shape[1]
    low = lax.broadcasted_iota(jnp.int32, (m, LANES), 1) < HEAD_DIM
    for q_ref, k_ref, v_ref, y_ref, has_sink in ((qn_ref, kn_ref, vn_ref, yna_ref, False),
                                                  (qg_ref, kge_ref, vge_ref, ygq_ref, True)):
        for bi in range(CTX_SEQS_PER_STEP):
            for p in range(NA_W // LANES):
                sl = slice(p * LANES, (p + 1) * LANES)
                q2 = q_ref[bi, :, sl].astype(BF16)
                k2 = k_ref[bi, :, sl].astype(BF16)
                v2 = v_ref[bi, :, sl].astype(BF16)
                halves = []
                for half, keep in enumerate((low, ~low)):
                    qh = jnp.where(keep, q2, jnp.zeros((), BF16))
                    sink = sink_ref[2 * p + half] if has_sink else None
                    halves.append(_softmax_pv([_dot_nt(qh, k2)], [v2], sink))
                y_ref[bi, :, sl] = jnp.where(low, halves[0], halves[1]).astype(y_ref.dtype)


def _ctx_attention(sink, qn, kn_cache, vn_cache, qg, kge, vge, layer):
    b, l, _ = qn.shape

    nb = CTX_SEQS_PER_STEP

    def spec(w):
        return pl.BlockSpec((nb, l, w), lambda i: (i, 0, 0))

    cache_spec = pl.BlockSpec((nb, None, l, NA_W), lambda i: (i, layer, 0, 0))
    return pl.pallas_call(
        _ctx_attn_kernel,
        grid=(b // nb,),
        in_specs=[pl.BlockSpec(memory_space=pltpu.SMEM), spec(NA_W), cache_spec, cache_spec, spec(GQ_W),
                  spec(GQ_W), spec(GQ_W)],
        out_specs=[spec(NA_W), spec(GQ_W)],
        out_shape=[jax.ShapeDtypeStruct((b, l, NA_W), BF16), jax.ShapeDtypeStruct((b, l, GQ_W), BF16)],
        compiler_params=_cparams(("arbitrary",)),
        name="context_attention",
    )(sink, qn, kn_cache, vn_cache, qg, kge, vge)


NA_KEYS = NA_ROWS * GRID_W
NA_ROWS_PER_STEP = 8


def _na_bias_kernel(r_ref, oh_ref, mk_ref, o_ref):
    o_ref[...] = jnp.dot(r_ref[...], oh_ref[...], precision=_HIGHEST, preferred_element_type=F32) + mk_ref[...]


def _na_bias_tables():
    q = np.arange(GRID_W)[:, None]
    kc = np.arange(GRID_W)[None, :]
    win_lo = np.clip(q - NA_COLS // 2, 0, GRID_W - NA_COLS)
    ok = (kc >= win_lo) & (kc < win_lo + NA_COLS)
    j = kc - q + NA_COLS - 1
    onehot = np.zeros((LANES, GRID_W * GRID_W), np.float32)
    qq, kk = np.nonzero(ok)
    onehot[j[qq, kk], qq * GRID_W + kk] = 1.0
    mask = np.where(ok, 0.0, NEG_INF).astype(np.float32).reshape(1, GRID_W * GRID_W)
    return onehot, mask


def _na_bias(rpb_l):
    n_dr = 2 * NA_ROWS - 1
    rows = NA_HEADS * n_dr
    rows_pad = -(-rows // 8) * 8
    r = jnp.zeros((rows_pad, LANES), F32).at[:rows, :2 * NA_COLS - 1].set(rpb_l.reshape(rows, 2 * NA_COLS - 1))
    onehot, mask = _na_bias_tables()
    out = pl.pallas_call(
        _na_bias_kernel,
        out_shape=jax.ShapeDtypeStruct((rows_pad, GRID_W * GRID_W), F32),
        compiler_params=pltpu.CompilerParams(vmem_limit_bytes=VMEM_LIMIT),
        name="na_bias_expand",
    )(r, jnp.asarray(onehot), jnp.asarray(mask))
    bm = out[:rows].reshape(NA_HEADS, n_dr, GRID_W, GRID_W)
    bm2 = jnp.concatenate([bm[:, :-1], bm[:, 1:]], axis=-1)
    return bm2.transpose(1, 0, 2, 3).reshape(n_dr - 1, NA_HEADS * GRID_W, 2 * GRID_W)


def _na_attn_kernel(q_ref, k_ref, v_ref, kc_ref, vc_ref, bm_ref, o_ref):
    n_rows = DEC_SEQ // GRID_W
    kc = kc_ref[0, 0].astype(BF16)
    vc = vc_ref[0, 0].astype(BF16)
    for rr in range(NA_ROWS_PER_STEP):
        r = pl.program_id(1) * NA_ROWS_PER_STEP + rr
        start = jnp.clip(r - NA_ROWS // 2, 0, n_rows - NA_ROWS)
        shift = r - start
        row0 = pl.multiple_of(start * GRID_W, GRID_W)
        rows = slice(rr * GRID_W, (rr + 1) * GRID_W)
        q = _stack_heads(q_ref[0, rows, :], NA_HEADS)
        k = k_ref[0, pl.ds(row0, NA_KEYS), :]
        v = v_ref[0, pl.ds(row0, NA_KEYS), :]
        bias = jnp.concatenate([bm_ref[2 * jj - shift + NA_ROWS - 1] for jj in range(NA_ROWS // 2)], axis=-1)
        s_loc = _dot_nt(q, k) + bias
        s_ctx = _dot_nt(q, kc)
        o = _softmax_pv([s_loc, s_ctx], [v, vc], None)
        o_ref[0, rows, :] = _unstack_heads(o, NA_HEADS).astype(o_ref.dtype)


WIN_KEYS = 3 * BLK
WIN_BLOCKS_PER_STEP = 4


def _win_attn_kernel(sink_ref, mask_ref, q_ref, kge_ref, vge_ref, kc_ref, vc_ref, o_ref):
    n = kge_ref.shape[1]
    n_blocks = n // BLK
    low_c = lax.broadcasted_iota(jnp.int32, (1, LANES), 1) < HEAD_DIM
    kce = _expand_kv(kc_ref[0, 0], low_c).astype(BF16)
    vce = _expand_kv(vc_ref[0, 0], low_c).astype(BF16)
    n_chains = GQA_Q_HEADS // HEADS_PER_CHAIN
    sinks = [_sink_column(sink_ref, BLK, p * HEADS_PER_CHAIN, HEADS_PER_CHAIN) for p in range(n_chains)]
    for bb in range(WIN_BLOCKS_PER_STEP):
        nb = pl.program_id(1) * WIN_BLOCKS_PER_STEP + bb
        start = pl.multiple_of(jnp.clip((nb - 1) * BLK, 0, n - WIN_KEYS), BLK)
        variant = jnp.where(nb == 0, 0, jnp.where(nb == n_blocks - 1, 2, 1))
        rows = slice(bb * BLK, (bb + 1) * BLK)
        for p in range(n_chains):
            sl = slice(p * CHAIN_W, (p + 1) * CHAIN_W)
            q = _stack_heads(q_ref[0, rows, sl], HEADS_PER_CHAIN)
            s_loc = _dot_nt(q, kge_ref[0, pl.ds(start, WIN_KEYS), sl]) + mask_ref[variant]
            s_ctx = _dot_nt(q, kce[:, sl])
            o = _softmax_pv([s_loc, s_ctx], [vge_ref[0, pl.ds(start, WIN_KEYS), sl], vce[:, sl]], sinks[p])
            o_ref[0, rows, sl] = _unstack_heads(o, HEADS_PER_CHAIN).astype(o_ref.dtype)


def _win_mask_table():
    qi = np.arange(BLK)[:, None]
    kj = np.arange(WIN_KEYS)[None, :]
    tabs = [np.where(np.abs(qi + off - kj) <= GQA_WINDOW, 0.0, NEG_INF) for off in (0, BLK, 2 * BLK)]
    return jnp.asarray(np.stack([np.tile(t, (HEADS_PER_CHAIN, 1)) for t in tabs]).astype(np.float32))


def _latent_attn_kernel(sink_ref, mask_ref, bm_ref, qn_ref, kn_ref, vn_ref, kcn_ref, vcn_ref,
                        qg_ref, kge_ref, vge_ref, kcg_ref, vcg_ref, yna_ref, ygq_ref):
    _na_attn_kernel(qn_ref, kn_ref, vn_ref, kcn_ref, vcn_ref, bm_ref, yna_ref)
    _win_attn_kernel(sink_ref, mask_ref, qg_ref, kge_ref, vge_ref, kcg_ref, vcg_ref, ygq_ref)


def _latent_attention(sink, bm2, qn, kn, vn, kc_na, vc_na, qg, kge, vge, kc_gq, vc_gq, layer):
    b, n, _ = qn.shape
    mask = _win_mask_table()
    tq = WIN_BLOCKS_PER_STEP * BLK
    assert tq == NA_ROWS_PER_STEP * GRID_W

    def q_spec(w):
        return pl.BlockSpec((1, tq, w), lambda i, j: (i, j, 0))

    def seq_spec(w):
        return pl.BlockSpec((1, n, w), lambda i, j: (i, 0, 0))

    def cache_spec(w):
        return pl.BlockSpec((1, 1, PAST_LEN, w), lambda i, j: (i, layer, 0, 0))

    return pl.pallas_call(
        _latent_attn_kernel,
        grid=(b, n // tq),
        in_specs=[
            pl.BlockSpec(memory_space=pltpu.SMEM),
            pl.BlockSpec(mask.shape, lambda i, j: (0, 0, 0)),
            pl.BlockSpec(bm2.shape, lambda i, j: (0, 0, 0)),
            q_spec(NA_W), seq_spec(NA_W), seq_spec(NA_W), cache_spec(NA_W), cache_spec(NA_W),
            q_spec(GQ_W), seq_spec(GQ_W), seq_spec(GQ_W), cache_spec(GKV_W), cache_spec(GKV_W),
        ],
        out_specs=[q_spec(NA_W), q_spec(GQ_W)],
        out_shape=[jax.ShapeDtypeStruct((b, n, NA_W), BF16), jax.ShapeDtypeStruct((b, n, GQ_W), BF16)],
        compiler_params=_cparams(("arbitrary", "arbitrary")),
        name="latent_attention",
    )(sink, mask, bm2, qn, kn, vn, kc_na, vc_na, qg, kge, vge, kc_gq, vc_gq)


@functools.lru_cache(maxsize=None)
def _hyena_consts_np(L):
    n = 2 * L
    k = np.arange(L, dtype=np.int64)
    ang = (2.0 * np.pi / n) * ((k[:, None] * k[None, :]) % n).astype(np.float64)

    tkf = min(HY_TKF, L)
    cm = np.cos(ang).astype(np.float32).reshape(L // tkf, tkf, L)
    sm = (-np.sin(ang)).astype(np.float32).reshape(L // tkf, tkf, L)
    f_rows = np.concatenate([cm, sm], axis=1)
    f_cols = np.ascontiguousarray(f_rows.transpose(0, 2, 1))
    idx = np.arange(L, dtype=np.float32)
    t = idx / np.float32(L - 1)
    bands = np.linspace(1e-4, HY_BANDS - 1, HY_BANDS, dtype=np.float32)
    fang = np.float32(2.0 * math.pi / L) * idx[:, None] * bands[None, :]
    feats = np.zeros((L, LANES), np.float32)
    feats[:, 0] = t
    feats[:, 1:1 + HY_BANDS] = np.cos(fang)
    feats[:, 1 + HY_BANDS:HY_EMB] = -np.sin(fang)
    max_decay = math.log(HY_DECAY_TARGET) / HY_FAST_PCT
    min_decay = math.log(HY_DECAY_TARGET) / HY_SLOW_PCT
    deltas = np.abs(np.linspace(min_decay, max_decay, HY_CH, dtype=np.float32))
    decay = np.exp(-t[:, None] * deltas[None, :]).astype(np.float32)
    decay2 = np.concatenate([decay, decay], axis=1)
    return f_rows, f_cols, feats, decay2


def _hyena_consts(L):
    return tuple(jnp.asarray(a) for a in _hyena_consts_np(L))


def _hy_filter_kernel(feats_ref, w1_ref, b1_ref, w2_ref, b2_ref, w3_ref, freq_ref, decay_ref, wsum_ref, wdiff_ref):
    def dot_hi(a, b):
        a_hi, a_lo = _split(a)
        b_hi, b_lo = _split(b)
        return _dot(a_hi, b_hi) + (_dot(a_lo, b_hi) + _dot(a_hi, b_lo))

    half = feats_ref.shape[0] // 2
    feats2 = jnp.concatenate([feats_ref[0:half, :], feats_ref[half:2 * half, :]], axis=-1)
    hid = jnp.sin(freq_ref[0:1, :] * (dot_hi(feats2, w1_ref[...]) + b1_ref[...]))
    hid = jnp.sin(freq_ref[1:2, :] * (dot_hi(hid, w2_ref[...]) + b2_ref[...]))
    taps2 = dot_hi(hid, w3_ref[...])
    for part in range(2):
        rows = slice(part * half, (part + 1) * half)
        taps = taps2[:, part * 2 * HY_CH:(part + 1) * 2 * HY_CH] * decay_ref[rows, :]
        fwd = taps[:, 0:HY_CH]
        bwd = taps[:, HY_CH:2 * HY_CH]
        if part == 0:
            row = lax.broadcasted_iota(jnp.int32, bwd.shape, 0)
            bwd = jnp.where(row == 0, 0.0, bwd)
        wsum_ref[rows, :] = fwd + bwd
        wdiff_ref[rows, :] = fwd - bwd


def _block_diag2(w):
    d, r, c = w.shape
    return jnp.zeros((d, 2 * r, 2 * c), w.dtype).at[:, :r, :c].set(w).at[:, r:, c:].set(w)


def _hy_filter_params(w1, b1, w2, b2, w3, freq):
    w1p = jnp.pad(w1, ((0, 0), (0, LANES - HY_EMB), (0, 0)))
    return (_block_diag2(w1p), jnp.tile(b1, (1, 2))[:, None, :], _block_diag2(w2), jnp.tile(b2, (1, 2))[:, None, :],
            _block_diag2(w3), jnp.tile(freq, (1, 1, 2)))


def _hy_filter(L, feats, decay2, params, layer):
    def whole(a):
        return pl.BlockSpec(a.shape, lambda i: (0,) * a.ndim)

    def of_layer(a):
        return pl.BlockSpec((None,) + a.shape[1:], lambda i: (layer,) + (0,) * (a.ndim - 1))

    out_spec = pl.BlockSpec((L, HY_CH), lambda i: (0, 0))
    return pl.pallas_call(
        _hy_filter_kernel,
        grid=(1,),
        in_specs=[whole(feats)] + [of_layer(p) for p in params] + [whole(decay2)],
        out_specs=[out_spec, out_spec],
        out_shape=[jax.ShapeDtypeStruct((L, HY_CH), F32)] * 2,
        compiler_params=_cparams(("arbitrary",)),
        name="hyena_filter",
    )(feats, *params, decay2)


def _hy_conv_kernel(z_ref, x0_ref, wsum_ref, wdiff_ref, skip_ref, fr_ref, fc_ref,
                    o_ref, z_scr, ws_scr, wd_scr, acc_scr, *, group):
    t = pl.program_id(1)
    L = z_ref.shape[1]
    n = 2 * L

    @pl.when(t == 0)
    def _():
        z_scr[...] = z_ref[...].astype(BF16)
        ws_scr[...] = wsum_ref[...].astype(BF16)
        wd_scr[...] = wdiff_ref[...].astype(BF16)
        acc_scr[...] = jnp.zeros_like(acc_scr)

    fr = fr_ref[...].astype(BF16)
    fc = fc_ref[...].astype(BF16)
    tkf = fr.shape[0] // 2
    k_re = _dot(fr[0:tkf], ws_scr[...])
    k_im = _dot(fr[tkf:2 * tkf], wd_scr[...])
    for b in range(group):
        zf = _dot(fr, z_scr[b])
        z_re, z_im = zf[0:tkf], zf[tkf:2 * tkf]
        y = jnp.concatenate([z_re * k_re - z_im * k_im, z_re * k_im + z_im * k_re], axis=0).astype(BF16)
        acc_scr[b] += _dot(fc, y)

    @pl.when(t == pl.num_programs(1) - 1)
    def _():
        row = lax.broadcasted_iota(jnp.int32, (L, 1), 0)
        sgn = (1 - 2 * (row % 2)).astype(F32)
        w = wsum_ref[...]
        w_dc = w.sum(axis=0, keepdims=True)
        w_ny = (w * sgn).sum(axis=0, keepdims=True)
        for b in range(group):
            z = z_ref[b]
            z_dc = z.sum(axis=0, keepdims=True)
            z_ny = (z * sgn).sum(axis=0, keepdims=True)
            conv = (2.0 / n) * acc_scr[b] - (1.0 / n) * (z_dc * w_dc) + (1.0 / n) * (sgn * (z_ny * w_ny))
            o_ref[b] = (x0_ref[b] * (conv + z * skip_ref[...])).astype(o_ref.dtype)


def _hy_conv(z, x0, wsum, wdiff, skip, f_rows, f_cols, group):
    b, L, c = z.shape
    n_tiles, two_tkf, _ = f_rows.shape
    row_spec = pl.BlockSpec((None, two_tkf, L), lambda g, t: (t, 0, 0))
    col_spec = pl.BlockSpec((None, L, two_tkf), lambda g, t: (t, 0, 0))
    once = pl.Buffered(1)
    seq_spec = pl.BlockSpec((group, L, c), lambda g, t: (g, 0, 0), pipeline_mode=once)
    w_spec = pl.BlockSpec((L, c), lambda g, t: (0, 0), pipeline_mode=once)
    return pl.pallas_call(
        functools.partial(_hy_conv_kernel, group=group),
        grid=(b // group, n_tiles),
        in_specs=[seq_spec, seq_spec, w_spec, w_spec, pl.BlockSpec((1, c), lambda g, t: (0, 0)),
                  row_spec, col_spec],
        out_specs=pl.BlockSpec((group, L, c), lambda g, t: (g, 0, 0)),
        out_shape=jax.ShapeDtypeStruct((b, L, c), BF16),
        scratch_shapes=[pltpu.VMEM((group, L, c), BF16), pltpu.VMEM((L, c), BF16), pltpu.VMEM((L, c), BF16),
                        pltpu.VMEM((group, L, c), F32)],
        compiler_params=_cparams(("arbitrary", "arbitrary")),
        name="hyena_long_conv",
    )(z, x0, wsum, wdiff, skip.reshape(1, c), f_rows, f_cols)


def _merge_kernel(x_ref, yhy_ref, yna_ref, ygq_ref, mod_ref, w_ref, o_ref, w_scr):
    @pl.when(pl.program_id(0) == 0)
    def _():
        w_scr[...] = w_ref[...].astype(BF16)

    y = jnp.concatenate([yhy_ref[...], yna_ref[...], ygq_ref[...]], axis=-1)
    o_ref[...] = x_ref[...] + mod_ref[0, G2:G2 + 1, :] * _dot(y, w_scr[...])


def _merge(x, y_hy, y_na, y_gq, mod, w_out, layer, grp_fn):
    rows = x.shape[0]
    tm = MERGE_TM
    return pl.pallas_call(
        _merge_kernel,
        grid=(rows // tm,),
        in_specs=[
            pl.BlockSpec((tm, D_MODEL), lambda i: (i, 0)),
            pl.BlockSpec((tm, HY_CH), lambda i: (i, 0)),
            pl.BlockSpec((tm, NA_W), lambda i: (i, 0)),
            pl.BlockSpec((tm, GQ_W), lambda i: (i, 0)),
            pl.BlockSpec((1, N_ADA, D_MODEL), lambda i: (grp_fn(i, tm), 0, 0)),
            pl.BlockSpec((None, D_MODEL, D_MODEL), lambda i: (layer, 0, 0), pipeline_mode=pl.Buffered(1)),
        ],
        out_specs=pl.BlockSpec((tm, D_MODEL), lambda i: (i, 0)),
        out_shape=jax.ShapeDtypeStruct((rows, D_MODEL), F32),
        scratch_shapes=[pltpu.VMEM((D_MODEL, D_MODEL), BF16)],
        compiler_params=_cparams(("arbitrary",)),
        name="mixer_out_proj",
    )(x, y_hy, y_na, y_gq, mod, w_out)


def _rope_tables():
    pos = np.arange(DEC_SEQ)
    quarter = HEAD_DIM // 4
    inv = ROPE_BASE ** (-np.arange(quarter, dtype=np.float64) / quarter)
    ang_r = (pos // GRID_W)[:, None] * inv[None, :]
    ang_c = (pos % GRID_W)[:, None] * inv[None, :]
    cos_h = np.concatenate([np.cos(ang_r), np.cos(ang_r), np.cos(ang_c), np.cos(ang_c)], axis=1)
    sin_h = np.concatenate([-np.sin(ang_r), np.sin(ang_r), -np.sin(ang_c), np.sin(ang_c)], axis=1)
    reps = LANES // HEAD_DIM
    return (jnp.asarray(np.tile(cos_h, (1, reps)), F32), jnp.asarray(np.tile(sin_h, (1, reps)), F32))


def _head_sum_matrix():
    idx = np.arange(LANES) // HEAD_DIM
    return jnp.asarray((idx[:, None] == idx[None, :]).astype(np.float32) / HEAD_DIM, BF16)


def _grp_prompt(i, tm):
    return 0


def _grp_sample(i, tm):
    return 1 + (i * tm) // DEC_SEQ


def kernel(x_prompt, x_sample, cache_na_k, cache_na_v, cache_gqa_k, cache_gqa_v, c, c_ctx, ada_w, ada_b, norm_g, ffn_w1, ffn_w3, ffn_w2, w_in, w_out, hy_conv_w, hy_conv_b, hy_filt_w1, hy_filt_b1, hy_filt_w2, hy_filt_b2, hy_filt_w3, hy_freq, hy_skip, na_q_g, na_k_g, na_rpb, gqa_q_g, gqa_k_g, gqa_sink):
    xp = x_prompt.reshape(BATCH * SEQ, D_MODEL)
    xs = x_sample.reshape(DEC_BATCH * DEC_SEQ, D_MODEL)
    cc = jnp.zeros((8, D_MODEL), F32).at[0].set(c_ctx).at[1:1 + DEC_BATCH].set(c)
    kc_na = cache_na_k.reshape(DEC_BATCH, DEPTH, PAST_LEN, NA_W)
    vc_na = cache_na_v.reshape(DEC_BATCH, DEPTH, PAST_LEN, NA_W)
    kc_gq = cache_gqa_k.reshape(DEC_BATCH, DEPTH, PAST_LEN, GKV_W)
    vc_gq = cache_gqa_v.reshape(DEC_BATCH, DEPTH, PAST_LEN, GKV_W)

    rope_tabs = _rope_tables()
    bsum = _head_sum_matrix()
    hy_p = _hyena_consts(SEQ)
    hy_s = _hyena_consts(DEC_SEQ)

    gains = jnp.tile(jnp.stack([na_q_g, na_k_g, gqa_q_g, gqa_k_g], axis=1), (1, 1, NA_HEADS))
    norm_rows = norm_g.reshape(DEPTH * 3, 1, D_MODEL)
    ng = [norm_rows] * 3
    filt_params = _hy_filter_params(hy_filt_w1, hy_filt_b1, hy_filt_w2, hy_filt_b2, hy_filt_w3, hy_freq)

    cache = None
    for l in range(DEPTH):
        mod = _adaln_mod(cc, ada_w, ada_b, l)
        bm2 = _na_bias(na_rpb[l])
        filt = {}
        for L, consts in ((SEQ, hy_p), (DEC_SEQ, hy_s)):
            filt[L] = _hy_filter(L, consts[2], consts[3], filt_params, l)

        xp = _ffn(xp, mod, ng[0], ffn_w1, ffn_w3, ffn_w2, l, 0, _grp_prompt, SH1, SC1, G1)
        x0, z, qn, kn, vn, qg, kgd, vgd, kg, vg = _project(
            xp, mod, ng[1], w_in, l, bsum, gains, hy_conv_w[l], hy_conv_b[l], SEQ, _grp_prompt, None, F32,
            cache_layer=l, prev_cache=cache)
        cache = (kn, vn, kg, vg)
        shp = lambda a: a.reshape(BATCH, SEQ, a.shape[-1])
        y_hy = _hy_conv(shp(z), shp(x0), filt[SEQ][0], filt[SEQ][1], hy_skip[l], hy_p[0], hy_p[1], group=BATCH)
        y_na, y_gq = _ctx_attention(gqa_sink[l], shp(qn), kn, vn, shp(qg), shp(kgd), shp(vgd), l)
        xp = _ffn(xp, mod, ng[2], ffn_w1, ffn_w3, ffn_w2, l, 1, _grp_prompt, SH3, SC3, G3,
                  mixer_out=(y_hy.reshape(-1, HY_CH), y_na.reshape(-1, NA_W), y_gq.reshape(-1, GQ_W), w_out))

        xs = _ffn(xs, mod, ng[0], ffn_w1, ffn_w3, ffn_w2, l, 0, _grp_sample, SH1, SC1, G1)
        x0, z, qn, kn, vn, qg, kgd, vgd = _project(
            xs, mod, ng[1], w_in, l, bsum, gains, hy_conv_w[l], hy_conv_b[l], DEC_SEQ, _grp_sample, rope_tabs, BF16)
        shs = lambda a: a.reshape(DEC_BATCH, DEC_SEQ, a.shape[-1])
        y_hy = _hy_conv(shs(z), shs(x0), filt[DEC_SEQ][0], filt[DEC_SEQ][1], hy_skip[l], hy_s[0], hy_s[1],
                        group=DEC_BATCH)
        y_na, y_gq = _latent_attention(gqa_sink[l], bm2, shs(qn), shs(kn), shs(vn), kc_na, vc_na,
                                       shs(qg), shs(kgd), shs(vgd), kc_gq, vc_gq, l)
        xs = _ffn(xs, mod, ng[2], ffn_w1, ffn_w3, ffn_w2, l, 1, _grp_sample, SH3, SC3, G3,
                  mixer_out=(y_hy.reshape(-1, HY_CH), y_na.reshape(-1, NA_W), y_gq.reshape(-1, GQ_W), w_out))

    kn, vn, kg, vg = cache
    return (xp.reshape(BATCH, SEQ, D_MODEL), xs.reshape(DEC_BATCH, DEC_SEQ, D_MODEL),
            kn.reshape(BATCH, DEPTH, SEQ, NA_HEADS, HEAD_DIM), vn.reshape(BATCH, DEPTH, SEQ, NA_HEADS, HEAD_DIM),
            kg.reshape(BATCH, DEPTH, SEQ, GQA_KV_HEADS, HEAD_DIM), vg.reshape(BATCH, DEPTH, SEQ, GQA_KV_HEADS, HEAD_DIM))
```

```python
import functools
import math

import numpy as np
import jax
import jax.numpy as jnp
from jax import lax
from jax.experimental import pallas as pl
from jax.experimental.pallas import tpu as pltpu

F32 = jnp.float32
BF16 = jnp.bfloat16

D_MODEL = 1024
BATCH = 16
SEQ = 256
DEPTH = 2
DEC_BATCH = 2
DEC_SEQ = 2048
PAST_LEN = 256
GRID_W = 64
HEAD_DIM = 64
HY_CH = 256
NA_HEADS = 6
GQA_Q_HEADS = 6
GQA_KV_HEADS = 2
NA_W = NA_HEADS * HEAD_DIM
GQ_W = GQA_Q_HEADS * HEAD_DIM
GKV_W = GQA_KV_HEADS * HEAD_DIM
IN_WIDTH = 3 * HY_CH + 3 * NA_W + GQ_W + 2 * GKV_W
D_FF = 2816
N_ADA = 9
HY_BANDS = 16
HY_EMB = 1 + 2 * HY_BANDS
HY_FILT_W = 64
HY_DECAY_TARGET = 1e-2
HY_FAST_PCT = 0.3
HY_SLOW_PCT = 1.5
NA_ROWS = 8
NA_COLS = 16
GQA_WINDOW = 128
BLK = 128
ROPE_BASE = 10000.0
EPS = 1e-6
NEG_INF = -1e30

LANES = 128
V7X_VMEM_BYTES = 64 * 1024 * 1024
VMEM_LIMIT = V7X_VMEM_BYTES * 15 // 16

OFF_NA = 3 * HY_CH
OFF_GQ = OFF_NA + 3 * NA_W

SH1, SC1, G1, SH2, SC2, G2, SH3, SC3, G3 = range(N_ADA)

FFN_TM = 2048
FFN_TF = 256
FFN_ROW_GROUPS = 4
ROW_TM = 512
MERGE_TM = 1024
HY_TKF = 256

_HIGHEST = lax.Precision.HIGHEST


def _cparams(sem):
    return pltpu.CompilerParams(dimension_semantics=sem, vmem_limit_bytes=VMEM_LIMIT)


def _silu(x):
    return x * (1.0 / (1.0 + jnp.exp(-x)))


def _dot(a, b):
    return jnp.dot(a, b, preferred_element_type=F32)


def _dot_nt(a, b):
    return lax.dot_general(a, b, (((1,), (1,)), ((), ())), preferred_element_type=F32)


def _split(x):
    hi = x.astype(BF16)
    lo = (x - hi.astype(F32)).astype(BF16)
    return hi, lo


def _mod_norm(x, g, shift, scale):
    ms = jnp.mean(x * x, axis=-1, keepdims=True)
    return (x * lax.rsqrt(ms + EPS) * g) * (1.0 + scale) + shift


def _mod_kernel(c_ref, w_ref, b_ref, o_ref):
    a = _silu(c_ref[...]).astype(BF16)
    o_ref[...] = _dot(a, w_ref[...].astype(BF16)) + b_ref[...]


def _adaln_mod(cc, ada_w, ada_b, layer):
    n_out = N_ADA * D_MODEL
    tn = n_out // 4
    out = pl.pallas_call(
        _mod_kernel,
        grid=(n_out // tn,),
        in_specs=[
            pl.BlockSpec((8, D_MODEL), lambda j: (0, 0)),
            pl.BlockSpec((None, D_MODEL, tn), lambda j: (layer, 0, j)),
            pl.BlockSpec((None, 1, tn), lambda j: (layer, 0, j)),
        ],
        out_specs=pl.BlockSpec((8, tn), lambda j: (0, j)),
        out_shape=jax.ShapeDtypeStruct((8, n_out), F32),
        compiler_params=_cparams(("arbitrary",)),
        name="adaln_mod",
    )(cc, ada_w, ada_b.reshape(DEPTH, 1, n_out))
    return out.reshape(8, N_ADA, D_MODEL)


def _ffn_kernel(x_ref, mod_ref, g_ref, w1_ref, w3_ref, w2_ref, o_ref, h_scr, acc_scr, *, sh, sc, gt):
    j = pl.program_id(1)
    last_j = pl.num_programs(1) - 1
    rows = h_scr.shape[0] // FFN_ROW_GROUPS

    def step(first, last):
        w1 = w1_ref[...].astype(BF16)
        w3 = w3_ref[...].astype(BF16)
        w2 = w2_ref[...].astype(BF16)
        for s in range(FFN_ROW_GROUPS):
            sl = slice(s * rows, (s + 1) * rows)
            if first:
                h = _mod_norm(x_ref[sl, :], g_ref[...], mod_ref[0, sh:sh + 1, :], mod_ref[0, sc:sc + 1, :]).astype(BF16)
                h_scr[sl] = h
            else:
                h = h_scr[sl]
            act = (_silu(_dot(h, w1)) * _dot(h, w3)).astype(BF16)
            part = _dot(act, w2)
            if first:
                acc_scr[sl] = part
            elif last:
                o_ref[sl, :] = x_ref[sl, :] + (0.5 * mod_ref[0, gt:gt + 1, :]) * (acc_scr[sl] + part)
            else:
                acc_scr[sl] += part

    pl.when(j == 0)(lambda: step(True, False))
    pl.when(jnp.logical_and(j > 0, j < last_j))(lambda: step(False, False))
    pl.when(j == last_j)(lambda: step(False, True))


def _ffn(x, mod, g, w1, w3, w2, layer, which, grp_fn, sh, sc, gt):
    rows = x.shape[0]
    tm, tf = FFN_TM, FFN_TF
    return pl.pallas_call(
        functools.partial(_ffn_kernel, sh=sh, sc=sc, gt=gt),
        grid=(rows // tm, D_FF // tf),
        in_specs=[
            pl.BlockSpec((tm, D_MODEL), lambda i, j: (i, 0)),
            pl.BlockSpec((1, N_ADA, D_MODEL), lambda i, j: (grp_fn(i, tm), 0, 0)),
            pl.BlockSpec((None, 1, D_MODEL), lambda i, j: (3 * layer + 2 * which, 0, 0)),
            pl.BlockSpec((None, None, D_MODEL, tf), lambda i, j: (layer, which, 0, j)),
            pl.BlockSpec((None, None, D_MODEL, tf), lambda i, j: (layer, which, 0, j)),
            pl.BlockSpec((None, None, tf, D_MODEL), lambda i, j: (layer, which, j, 0)),
        ],
        out_specs=pl.BlockSpec((tm, D_MODEL), lambda i, j: (i, 0)),
        out_shape=jax.ShapeDtypeStruct((rows, D_MODEL), F32),
        scratch_shapes=[pltpu.VMEM((tm, D_MODEL), BF16), pltpu.VMEM((tm, D_MODEL), F32)],
        compiler_params=_cparams(("arbitrary", "arbitrary")),
        name="ffn_half_step",
    )(x, mod, g, w1, w3, w2)


def _head_norm(xs, bsum, g):
    normed = []
    for p in range(xs.shape[1] // LANES):
        x = xs[:, p * LANES:(p + 1) * LANES]
        ms = _dot((x * x).astype(BF16), bsum)
        normed.append(x * lax.rsqrt(ms + EPS))
    return jnp.concatenate(normed, axis=-1) * g


def _rope(xs, cos, sin, first_half):
    swapped = jnp.where(first_half, pltpu.roll(xs, LANES - 16, 1), pltpu.roll(xs, 16, 1))
    return xs * cos + swapped * sin


def _dup_heads(x2, low_half):
    sw = pltpu.roll(x2, HEAD_DIM, 1)
    return jnp.where(low_half, x2, sw), jnp.where(low_half, sw, x2)


def _expand_kv(x2, low_half):
    d0, d1 = _dup_heads(x2, low_half)
    return jnp.concatenate([d0, x2, d1], axis=-1)


PROJ_GROUP_ROWS = SEQ


def _store_group(ref, val, s, rs, cache_layer):
    val = val.astype(ref.dtype)
    if len(ref.shape) == 2:
        ref[rs, :] = val
    elif len(ref.shape) == 3:
        ref[s] = val
    else:
        for l in range(ref.shape[1]):
            ref[s, l] = val if l == cache_layer else jnp.zeros_like(val)


def _proj_kernel(*refs, rope, cache_layer, n_alias, seq_len):
    x_ref, mod_ref, g_ref, w_ref, bsum_ref, gains_ref, xprev_ref, xnext_ref, cw_ref, cb_ref = refs[:10]
    pos = 10
    if rope:
        cos_ref, sin_ref = refs[pos:pos + 2]
        pos += 2
    pos += n_alias
    x0_ref, z_ref, qn_ref, kn_ref, vn_ref, qg_ref, kge_ref, vge_ref = refs[pos:pos + 8]
    pos += 8
    emit_kv = cache_layer is not None
    if emit_kv:
        kg_ref, vg_ref = refs[pos:pos + 2]
        pos += 2
    u_scr, w_scr = refs[pos:pos + 2]
    scale = HEAD_DIM ** -0.5
    tm = x_ref.shape[0]

    @pl.when(pl.program_id(0) == 0)
    def _():
        w_scr[...] = w_ref[...].astype(BF16)

    def projected(x, cols):
        h = _mod_norm(x, g_ref[...], mod_ref[0, SH2:SH2 + 1, :], mod_ref[0, SC2:SC2 + 1, :])
        return _dot(h.astype(BF16), w_scr[:, cols])

    bsum = bsum_ref[...]
    lane = lax.broadcasted_iota(jnp.int32, (1, LANES), 1)
    first_half = (lane % 32) < 16
    low_half = lane < HEAD_DIM
    for s in range(tm // PROJ_GROUP_ROWS):
        rs = slice(s * PROJ_GROUP_ROWS, (s + 1) * PROJ_GROUP_ROWS)
        u_scr[rs, :] = projected(x_ref[rs, :], slice(0, IN_WIDTH))

        qn = _head_norm(u_scr[rs, OFF_NA:OFF_NA + NA_W], bsum, gains_ref[0:1, :])
        qn_ref[rs, :] = (qn * scale).astype(qn_ref.dtype)
        kn = _head_norm(u_scr[rs, OFF_NA + NA_W:OFF_NA + 2 * NA_W], bsum, gains_ref[1:2, :])
        _store_group(kn_ref, kn, s, rs, cache_layer)
        _store_group(vn_ref, u_scr[rs, OFF_NA + 2 * NA_W:OFF_GQ], s, rs, cache_layer)

        qg = _head_norm(u_scr[rs, OFF_GQ:OFF_GQ + GQ_W], bsum, gains_ref[2:3, :])
        kg = _head_norm(u_scr[rs, OFF_GQ + GQ_W:OFF_GQ + GQ_W + GKV_W], bsum, gains_ref[3:4, 0:GKV_W])
        vg = u_scr[rs, OFF_GQ + GQ_W + GKV_W:IN_WIDTH]
        if emit_kv:
            _store_group(kg_ref, kg, s, rs, cache_layer)
            _store_group(vg_ref, vg, s, rs, cache_layer)
        if rope:
            cos = cos_ref[rs, :]
            sin = sin_ref[rs, :]
            for p in range(GQ_W // LANES):
                sl = slice(p * LANES, (p + 1) * LANES)
                qg_ref[rs, sl] = (_rope(qg[:, sl], cos, sin, first_half) * scale).astype(qg_ref.dtype)
            kg = _rope(kg, cos, sin, first_half)
        else:
            qg_ref[rs, :] = (qg * scale).astype(qg_ref.dtype)
        kge_ref[rs, :] = _expand_kv(kg, low_half).astype(kge_ref.dtype)
        vge_ref[rs, :] = _expand_kv(vg, low_half).astype(vge_ref.dtype)

    hy_cols = slice(0, OFF_NA)
    halo_prev = projected(xprev_ref[...], hy_cols)[7:8]
    halo_next = projected(xnext_ref[...], hy_cols)[0:1]
    row = lax.broadcasted_iota(jnp.int32, (tm, 1), 0)
    t_pos = (pl.program_id(0) * tm + row) & (seq_len - 1)

    def conv_chunk(ci):
        sl = slice(ci * HY_CH, (ci + 1) * HY_CH)
        u = u_scr[:, sl]
        prev = jnp.where(row == 0, halo_prev[:, sl], pltpu.roll(u, 1, 0))
        prev = jnp.where(t_pos == 0, 0.0, prev)
        nxt = jnp.where(row == tm - 1, halo_next[:, sl], pltpu.roll(u, tm - 1, 0))
        nxt = jnp.where(t_pos == seq_len - 1, 0.0, nxt)
        return prev * cw_ref[0:1, sl] + u * cw_ref[1:2, sl] + nxt * cw_ref[2:3, sl] + cb_ref[:, sl]

    x0_ref[...] = conv_chunk(0)
    z_ref[...] = conv_chunk(1) * conv_chunk(2)


def _project(x, mod, g, w_in, layer, bsum, gains, conv_w, conv_b, seq_len, grp_fn, rope_tabs, attn_dtype,
             cache_layer=None, prev_cache=None):
    rows = x.shape[0]
    tm = ROW_TM
    rope = rope_tabs is not None
    assert seq_len & (seq_len - 1) == 0 and (seq_len % tm == 0 or tm % seq_len == 0)
    sub = 8
    in_specs = [
        pl.BlockSpec((tm, D_MODEL), lambda i: (i, 0)),
        pl.BlockSpec((1, N_ADA, D_MODEL), lambda i: (grp_fn(i, tm), 0, 0)),
        pl.BlockSpec((None, 1, D_MODEL), lambda i: (3 * layer + 1, 0, 0)),
        pl.BlockSpec((None, D_MODEL, IN_WIDTH), lambda i: (layer, 0, 0), pipeline_mode=pl.Buffered(1)),
        pl.BlockSpec((LANES, LANES), lambda i: (0, 0)),
        pl.BlockSpec((None, 4, NA_W), lambda i: (layer, 0, 0)),
        pl.BlockSpec((sub, D_MODEL), lambda i: (jnp.maximum(i * (tm // sub) - 1, 0), 0)),
        pl.BlockSpec((sub, D_MODEL), lambda i: (jnp.minimum((i + 1) * (tm // sub), rows // sub - 1), 0)),
        pl.BlockSpec((3, OFF_NA), lambda i: (0, 0)),
        pl.BlockSpec((1, OFF_NA), lambda i: (0, 0)),
    ]
    args = [x, mod, g, w_in, bsum, gains, x, x, conv_w, conv_b.reshape(1, OFF_NA)]
    if rope:
        seq_tiles = DEC_SEQ // tm
        in_specs += [pl.BlockSpec((tm, LANES), lambda i: (i % seq_tiles, 0))] * 2
        args += list(rope_tabs)
    outs = [(HY_CH, F32)] * 2 + [(NA_W, attn_dtype)] * 3 + [(GQ_W, attn_dtype)] * 3
    if cache_layer is not None:
        outs += [(GKV_W, F32), (GKV_W, F32)]
    out_specs = [pl.BlockSpec((tm, w), lambda i: (i, 0)) for w, _ in outs]
    out_shape = [jax.ShapeDtypeStruct((rows, w), dt) for w, dt in outs]
    aliases = {}
    if cache_layer is not None:
        cache_outs = (3, 4, 8, 9)
        seqs = tm // SEQ
        for o in cache_outs:
            w = outs[o][0]
            out_shape[o] = jax.ShapeDtypeStruct((BATCH, DEPTH, SEQ, w), F32)
            if prev_cache is None:
                out_specs[o] = pl.BlockSpec((seqs, DEPTH, SEQ, w), lambda i: (i, 0, 0, 0))
            else:
                out_specs[o] = pl.BlockSpec((seqs, None, SEQ, w), lambda i: (i, cache_layer, 0, 0))
        if prev_cache is not None:
            aliases = {len(args) + n: o for n, o in enumerate(cache_outs)}
            in_specs += [pl.BlockSpec(memory_space=pl.ANY)] * len(cache_outs)
            args += list(prev_cache)
    return pl.pallas_call(
        functools.partial(_proj_kernel, rope=rope, cache_layer=cache_layer, n_alias=len(aliases), seq_len=seq_len),
        grid=(rows // tm,),
        in_specs=in_specs,
        out_specs=out_specs,
        out_shape=out_shape,
        input_output_aliases=aliases,
        scratch_shapes=[pltpu.VMEM((tm, IN_WIDTH), F32), pltpu.VMEM((D_MODEL, IN_WIDTH), BF16)],
        compiler_params=_cparams(("arbitrary",)),
        name="mixer_in_proj",
    )(*args)


HEADS_PER_CHAIN = 2
CHAIN_W = HEADS_PER_CHAIN * HEAD_DIM


def _log2(n):
    assert n & (n - 1) == 0
    return n.bit_length() - 1


def _stack_heads(q, n_heads):
    m, w = q.shape
    rows = lax.broadcasted_iota(jnp.int32, (n_heads * m, w), 0)
    lanes = lax.broadcasted_iota(jnp.int32, (n_heads * m, w), 1)
    own = jnp.right_shift(rows, _log2(m)) == jnp.right_shift(lanes, _log2(HEAD_DIM))
    return jnp.where(own, jnp.concatenate([q] * n_heads, axis=0), jnp.zeros((), q.dtype))


def _unstack_heads(o, n_heads):
    m = o.shape[0] // n_heads
    head = jnp.right_shift(lax.broadcasted_iota(jnp.int32, (m, o.shape[1]), 1), _log2(HEAD_DIM))
    out = o[0:m]
    for h in range(1, n_heads):
        out = jnp.where(head == h, o[h * m:(h + 1) * m], out)
    return out


def _sink_column(sink_ref, m, first_head, n_heads):
    block = jnp.right_shift(lax.broadcasted_iota(jnp.int32, (n_heads * m, 1), 0), _log2(m))
    col = jnp.full((n_heads * m, 1), sink_ref[first_head], F32)
    for h in range(1, n_heads):
        col = jnp.where(block == h, sink_ref[first_head + h], col)
    return col


def _softmax_pv(scores, values, sink):
    m = scores[0].max(axis=-1, keepdims=True)
    for s in scores[1:]:
        m = jnp.maximum(m, s.max(axis=-1, keepdims=True))
    if sink is not None:
        m = jnp.maximum(m, sink)
    den = None
    acc = None
    for s, v in zip(scores, values):
        p = jnp.exp(s - m)
        d = p.sum(axis=-1, keepdims=True)
        o = _dot(p.astype(BF16), v)
        den = d if den is None else den + d
        acc = o if acc is None else acc + o
    if sink is not None:
        den = den + jnp.exp(sink - m)
    return acc / den


CTX_SEQS_PER_STEP = 2


def _ctx_attn_kernel(sink_ref, qn_ref, kn_ref, vn_ref, qg_ref, kge_ref, vge_ref, yna_ref, ygq_ref):
    m = qn_ref.shape[1]
    low = lax.broadcasted_iota(jnp.int32, (m, LANES), 1) < HEAD_DIM
    for q_ref, k_ref, v_ref, y_ref, has_sink in ((qn_ref, kn_ref, vn_ref, yna_ref, False),
                                                  (qg_ref, kge_ref, vge_ref, ygq_ref, True)):
        for bi in range(CTX_SEQS_PER_STEP):
            for p in range(NA_W // LANES):
                sl = slice(p * LANES, (p + 1) * LANES)
                q2 = q_ref[bi, :, sl].astype(BF16)
                k2 = k_ref[bi, :, sl].astype(BF16)
                v2 = v_ref[bi, :, sl].astype(BF16)
                halves = []
                for half, keep in enumerate((low, ~low)):
                    qh = jnp.where(keep, q2, jnp.zeros((), BF16))
                    sink = sink_ref[2 * p + half] if has_sink else None
                    halves.append(_softmax_pv([_dot_nt(qh, k2)], [v2], sink))
                y_ref[bi, :, sl] = jnp.where(low, halves[0], halves[1]).astype(y_ref.dtype)


def _ctx_attention(sink, qn, kn_cache, vn_cache, qg, kge, vge, layer):
    b, l, _ = qn.shape

    nb = CTX_SEQS_PER_STEP

    def spec(w):
        return pl.BlockSpec((nb, l, w), lambda i: (i, 0, 0))

    cache_spec = pl.BlockSpec((nb, None, l, NA_W), lambda i: (i, layer, 0, 0))
    return pl.pallas_call(
        _ctx_attn_kernel,
        grid=(b // nb,),
        in_specs=[pl.BlockSpec(memory_space=pltpu.SMEM), spec(NA_W), cache_spec, cache_spec, spec(GQ_W),
                  spec(GQ_W), spec(GQ_W)],
        out_specs=[spec(NA_W), spec(GQ_W)],
        out_shape=[jax.ShapeDtypeStruct((b, l, NA_W), BF16), jax.ShapeDtypeStruct((b, l, GQ_W), BF16)],
        compiler_params=_cparams(("arbitrary",)),
        name="context_attention",
    )(sink, qn, kn_cache, vn_cache, qg, kge, vge)


NA_KEYS = NA_ROWS * GRID_W
NA_ROWS_PER_STEP = 8


def _na_bias_kernel(r_ref, oh_ref, mk_ref, o_ref):
    o_ref[...] = jnp.dot(r_ref[...], oh_ref[...], precision=_HIGHEST, preferred_element_type=F32) + mk_ref[...]


def _na_bias_tables():
    q = np.arange(GRID_W)[:, None]
    kc = np.arange(GRID_W)[None, :]
    win_lo = np.clip(q - NA_COLS // 2, 0, GRID_W - NA_COLS)
    ok = (kc >= win_lo) & (kc < win_lo + NA_COLS)
    j = kc - q + NA_COLS - 1
    onehot = np.zeros((LANES, GRID_W * GRID_W), np.float32)
    qq, kk = np.nonzero(ok)
    onehot[j[qq, kk], qq * GRID_W + kk] = 1.0
    mask = np.where(ok, 0.0, NEG_INF).astype(np.float32).reshape(1, GRID_W * GRID_W)
    return onehot, mask


def _na_bias(rpb_l):
    n_dr = 2 * NA_ROWS - 1
    rows = NA_HEADS * n_dr
    rows_pad = -(-rows // 8) * 8
    r = jnp.zeros((rows_pad, LANES), F32).at[:rows, :2 * NA_COLS - 1].set(rpb_l.reshape(rows, 2 * NA_COLS - 1))
    onehot, mask = _na_bias_tables()
    out = pl.pallas_call(
        _na_bias_kernel,
        out_shape=jax.ShapeDtypeStruct((rows_pad, GRID_W * GRID_W), F32),
        compiler_params=pltpu.CompilerParams(vmem_limit_bytes=VMEM_LIMIT),
        name="na_bias_expand",
    )(r, jnp.asarray(onehot), jnp.asarray(mask))
    bm = out[:rows].reshape(NA_HEADS, n_dr, GRID_W, GRID_W)
    bm2 = jnp.concatenate([bm[:, :-1], bm[:, 1:]], axis=-1)
    return bm2.transpose(1, 0, 2, 3).reshape(n_dr - 1, NA_HEADS * GRID_W, 2 * GRID_W)


def _na_attn_kernel(q_ref, k_ref, v_ref, kc_ref, vc_ref, bm_ref, o_ref):
    n_rows = DEC_SEQ // GRID_W
    kc = kc_ref[0, 0].astype(BF16)
    vc = vc_ref[0, 0].astype(BF16)
    for rr in range(NA_ROWS_PER_STEP):
        r = pl.program_id(1) * NA_ROWS_PER_STEP + rr
        start = jnp.clip(r - NA_ROWS // 2, 0, n_rows - NA_ROWS)
        shift = r - start
        row0 = pl.multiple_of(start * GRID_W, GRID_W)
        rows = slice(rr * GRID_W, (rr + 1) * GRID_W)
        q = _stack_heads(q_ref[0, rows, :], NA_HEADS)
        k = k_ref[0, pl.ds(row0, NA_KEYS), :]
        v = v_ref[0, pl.ds(row0, NA_KEYS), :]
        bias = jnp.concatenate([bm_ref[2 * jj - shift + NA_ROWS - 1] for jj in range(NA_ROWS // 2)], axis=-1)
        s_loc = _dot_nt(q, k) + bias
        s_ctx = _dot_nt(q, kc)
        o = _softmax_pv([s_loc, s_ctx], [v, vc], None)
        o_ref[0, rows, :] = _unstack_heads(o, NA_HEADS).astype(o_ref.dtype)


WIN_KEYS = 3 * BLK
WIN_BLOCKS_PER_STEP = 4


def _win_attn_kernel(sink_ref, mask_ref, q_ref, kge_ref, vge_ref, kc_ref, vc_ref, o_ref):
    n = kge_ref.shape[1]
    n_blocks = n // BLK
    low_c = lax.broadcasted_iota(jnp.int32, (1, LANES), 1) < HEAD_DIM
    kce = _expand_kv(kc_ref[0, 0], low_c).astype(BF16)
    vce = _expand_kv(vc_ref[0, 0], low_c).astype(BF16)
    n_chains = GQA_Q_HEADS // HEADS_PER_CHAIN
    sinks = [_sink_column(sink_ref, BLK, p * HEADS_PER_CHAIN, HEADS_PER_CHAIN) for p in range(n_chains)]
    for bb in range(WIN_BLOCKS_PER_STEP):
        nb = pl.program_id(1) * WIN_BLOCKS_PER_STEP + bb
        start = pl.multiple_of(jnp.clip((nb - 1) * BLK, 0, n - WIN_KEYS), BLK)
        variant = jnp.where(nb == 0, 0, jnp.where(nb == n_blocks - 1, 2, 1))
        rows = slice(bb * BLK, (bb + 1) * BLK)
        for p in range(n_chains):
            sl = slice(p * CHAIN_W, (p + 1) * CHAIN_W)
            q = _stack_heads(q_ref[0, rows, sl], HEADS_PER_CHAIN)
            s_loc = _dot_nt(q, kge_ref[0, pl.ds(start, WIN_KEYS), sl]) + mask_ref[variant]
            s_ctx = _dot_nt(q, kce[:, sl])
            o = _softmax_pv([s_loc, s_ctx], [vge_ref[0, pl.ds(start, WIN_KEYS), sl], vce[:, sl]], sinks[p])
            o_ref[0, rows, sl] = _unstack_heads(o, HEADS_PER_CHAIN).astype(o_ref.dtype)


def _win_mask_table():
    qi = np.arange(BLK)[:, None]
    kj = np.arange(WIN_KEYS)[None, :]
    tabs = [np.where(np.abs(qi + off - kj) <= GQA_WINDOW, 0.0, NEG_INF) for off in (0, BLK, 2 * BLK)]
    return jnp.asarray(np.stack([np.tile(t, (HEADS_PER_CHAIN, 1)) for t in tabs]).astype(np.float32))


def _latent_attn_kernel(sink_ref, mask_ref, bm_ref, qn_ref, kn_ref, vn_ref, kcn_ref, vcn_ref,
                        qg_ref, kge_ref, vge_ref, kcg_ref, vcg_ref, yna_ref, ygq_ref):
    _na_attn_kernel(qn_ref, kn_ref, vn_ref, kcn_ref, vcn_ref, bm_ref, yna_ref)
    _win_attn_kernel(sink_ref, mask_ref, qg_ref, kge_ref, vge_ref, kcg_ref, vcg_ref, ygq_ref)


def _latent_attention(sink, bm2, qn, kn, vn, kc_na, vc_na, qg, kge, vge, kc_gq, vc_gq, layer):
    b, n, _ = qn.shape
    mask = _win_mask_table()
    tq = WIN_BLOCKS_PER_STEP * BLK
    assert tq == NA_ROWS_PER_STEP * GRID_W

    def q_spec(w):
        return pl.BlockSpec((1, tq, w), lambda i, j: (i, j, 0))

    def seq_spec(w):
        return pl.BlockSpec((1, n, w), lambda i, j: (i, 0, 0))

    def cache_spec(w):
        return pl.BlockSpec((1, 1, PAST_LEN, w), lambda i, j: (i, layer, 0, 0))

    return pl.pallas_call(
        _latent_attn_kernel,
        grid=(b, n // tq),
        in_specs=[
            pl.BlockSpec(memory_space=pltpu.SMEM),
            pl.BlockSpec(mask.shape, lambda i, j: (0, 0, 0)),
            pl.BlockSpec(bm2.shape, lambda i, j: (0, 0, 0)),
            q_spec(NA_W), seq_spec(NA_W), seq_spec(NA_W), cache_spec(NA_W), cache_spec(NA_W),
            q_spec(GQ_W), seq_spec(GQ_W), seq_spec(GQ_W), cache_spec(GKV_W), cache_spec(GKV_W),
        ],
        out_specs=[q_spec(NA_W), q_spec(GQ_W)],
        out_shape=[jax.ShapeDtypeStruct((b, n, NA_W), BF16), jax.ShapeDtypeStruct((b, n, GQ_W), BF16)],
        compiler_params=_cparams(("arbitrary", "arbitrary")),
        name="latent_attention",
    )(sink, mask, bm2, qn, kn, vn, kc_na, vc_na, qg, kge, vge, kc_gq, vc_gq)


@functools.lru_cache(maxsize=None)
def _hyena_consts_np(L):
    n = 2 * L
    k = np.arange(L, dtype=np.int64)
    ang = (2.0 * np.pi / n) * ((k[:, None] * k[None, :]) % n).astype(np.float64)

    tkf = min(HY_TKF, L)
    cm = np.cos(ang).astype(np.float32).reshape(L // tkf, tkf, L)
    sm = (-np.sin(ang)).astype(np.float32).reshape(L // tkf, tkf, L)
    f_rows = np.concatenate([cm, sm], axis=1)
    f_cols = np.ascontiguousarray(f_rows.transpose(0, 2, 1))
    idx = np.arange(L, dtype=np.float32)
    t = idx / np.float32(L - 1)
    bands = np.linspace(1e-4, HY_BANDS - 1, HY_BANDS, dtype=np.float32)
    fang = np.float32(2.0 * math.pi / L) * idx[:, None] * bands[None, :]
    feats = np.zeros((L, LANES), np.float32)
    feats[:, 0] = t
    feats[:, 1:1 + HY_BANDS] = np.cos(fang)
    feats[:, 1 + HY_BANDS:HY_EMB] = -np.sin(fang)
    max_decay = math.log(HY_DECAY_TARGET) / HY_FAST_PCT
    min_decay = math.log(HY_DECAY_TARGET) / HY_SLOW_PCT
    deltas = np.abs(np.linspace(min_decay, max_decay, HY_CH, dtype=np.float32))
    decay = np.exp(-t[:, None] * deltas[None, :]).astype(np.float32)
    decay2 = np.concatenate([decay, decay], axis=1)
    return f_rows, f_cols, feats, decay2


def _hyena_consts(L):
    return tuple(jnp.asarray(a) for a in _hyena_consts_np(L))


def _hy_filter_kernel(feats_ref, w1_ref, b1_ref, w2_ref, b2_ref, w3_ref, freq_ref, decay_ref, wsum_ref, wdiff_ref):
    def dot_hi(a, b):
        a_hi, a_lo = _split(a)
        b_hi, b_lo = _split(b)
        return _dot(a_hi, b_hi) + (_dot(a_lo, b_hi) + _dot(a_hi, b_lo))

    half = feats_ref.shape[0] // 2
    feats2 = jnp.concatenate([feats_ref[0:half, :], feats_ref[half:2 * half, :]], axis=-1)
    hid = jnp.sin(freq_ref[0:1, :] * (dot_hi(feats2, w1_ref[...]) + b1_ref[...]))
    hid = jnp.sin(freq_ref[1:2, :] * (dot_hi(hid, w2_ref[...]) + b2_ref[...]))
    taps2 = dot_hi(hid, w3_ref[...])
    for part in range(2):
        rows = slice(part * half, (part + 1) * half)
        taps = taps2[:, part * 2 * HY_CH:(part + 1) * 2 * HY_CH] * decay_ref[rows, :]
        fwd = taps[:, 0:HY_CH]
        bwd = taps[:, HY_CH:2 * HY_CH]
        if part == 0:
            row = lax.broadcasted_iota(jnp.int32, bwd.shape, 0)
            bwd = jnp.where(row == 0, 0.0, bwd)
        wsum_ref[rows, :] = fwd + bwd
        wdiff_ref[rows, :] = fwd - bwd


def _block_diag2(w):
    d, r, c = w.shape
    return jnp.zeros((d, 2 * r, 2 * c), w.dtype).at[:, :r, :c].set(w).at[:, r:, c:].set(w)


def _hy_filter_params(w1, b1, w2, b2, w3, freq):
    w1p = jnp.pad(w1, ((0, 0), (0, LANES - HY_EMB), (0, 0)))
    return (_block_diag2(w1p), jnp.tile(b1, (1, 2))[:, None, :], _block_diag2(w2), jnp.tile(b2, (1, 2))[:, None, :],
            _block_diag2(w3), jnp.tile(freq, (1, 1, 2)))


def _hy_filter(L, feats, decay2, params, layer):
    def whole(a):
        return pl.BlockSpec(a.shape, lambda i: (0,) * a.ndim)

    def of_layer(a):
        return pl.BlockSpec((None,) + a.shape[1:], lambda i: (layer,) + (0,) * (a.ndim - 1))

    out_spec = pl.BlockSpec((L, HY_CH), lambda i: (0, 0))
    return pl.pallas_call(
        _hy_filter_kernel,
        grid=(1,),
        in_specs=[whole(feats)] + [of_layer(p) for p in params] + [whole(decay2)],
        out_specs=[out_spec, out_spec],
        out_shape=[jax.ShapeDtypeStruct((L, HY_CH), F32)] * 2,
        compiler_params=_cparams(("arbitrary",)),
        name="hyena_filter",
    )(feats, *params, decay2)


def _hy_conv_kernel(z_ref, x0_ref, wsum_ref, wdiff_ref, skip_ref, fr_ref, fc_ref,
                    o_ref, z_scr, ws_scr, wd_scr, acc_scr, *, group):
    t = pl.program_id(1)
    L = z_ref.shape[1]
    n = 2 * L

    @pl.when(t == 0)
    def _():
        z_scr[...] = z_ref[...].astype(BF16)
        ws_scr[...] = wsum_ref[...].astype(BF16)
        wd_scr[...] = wdiff_ref[...].astype(BF16)
        acc_scr[...] = jnp.zeros_like(acc_scr)

    fr = fr_ref[...].astype(BF16)
    fc = fc_ref[...].astype(BF16)
    tkf = fr.shape[0] // 2
    k_re = _dot(fr[0:tkf], ws_scr[...])
    k_im = _dot(fr[tkf:2 * tkf], wd_scr[...])
    for b in range(group):
        zf = _dot(fr, z_scr[b])
        z_re, z_im = zf[0:tkf], zf[tkf:2 * tkf]
        y = jnp.concatenate([z_re * k_re - z_im * k_im, z_re * k_im + z_im * k_re], axis=0).astype(BF16)
        acc_scr[b] += _dot(fc, y)

    @pl.when(t == pl.num_programs(1) - 1)
    def _():
        row = lax.broadcasted_iota(jnp.int32, (L, 1), 0)
        sgn = (1 - 2 * (row % 2)).astype(F32)
        w = wsum_ref[...]
        w_dc = w.sum(axis=0, keepdims=True)
        w_ny = (w * sgn).sum(axis=0, keepdims=True)
        for b in range(group):
            z = z_ref[b]
            z_dc = z.sum(axis=0, keepdims=True)
            z_ny = (z * sgn).sum(axis=0, keepdims=True)
            conv = (2.0 / n) * acc_scr[b] - (1.0 / n) * (z_dc * w_dc) + (1.0 / n) * (sgn * (z_ny * w_ny))
            o_ref[b] = (x0_ref[b] * (conv + z * skip_ref[...])).astype(o_ref.dtype)


def _hy_conv(z, x0, wsum, wdiff, skip, f_rows, f_cols, group):
    b, L, c = z.shape
    n_tiles, two_tkf, _ = f_rows.shape
    row_spec = pl.BlockSpec((None, two_tkf, L), lambda g, t: (t, 0, 0))
    col_spec = pl.BlockSpec((None, L, two_tkf), lambda g, t: (t, 0, 0))
    once = pl.Buffered(1)
    seq_spec = pl.BlockSpec((group, L, c), lambda g, t: (g, 0, 0), pipeline_mode=once)
    w_spec = pl.BlockSpec((L, c), lambda g, t: (0, 0), pipeline_mode=once)
    return pl.pallas_call(
        functools.partial(_hy_conv_kernel, group=group),
        grid=(b // group, n_tiles),
        in_specs=[seq_spec, seq_spec, w_spec, w_spec, pl.BlockSpec((1, c), lambda g, t: (0, 0)),
                  row_spec, col_spec],
        out_specs=pl.BlockSpec((group, L, c), lambda g, t: (g, 0, 0)),
        out_shape=jax.ShapeDtypeStruct((b, L, c), BF16),
        scratch_shapes=[pltpu.VMEM((group, L, c), BF16), pltpu.VMEM((L, c), BF16), pltpu.VMEM((L, c), BF16),
                        pltpu.VMEM((group, L, c), F32)],
        compiler_params=_cparams(("arbitrary", "arbitrary")),
        name="hyena_long_conv",
    )(z, x0, wsum, wdiff, skip.reshape(1, c), f_rows, f_cols)


def _merge_kernel(x_ref, yhy_ref, yna_ref, ygq_ref, mod_ref, w_ref, o_ref, w_scr):
    @pl.when(pl.program_id(0) == 0)
    def _():
        w_scr[...] = w_ref[...].astype(BF16)

    y = jnp.concatenate([yhy_ref[...], yna_ref[...], ygq_ref[...]], axis=-1)
    o_ref[...] = x_ref[...] + mod_ref[0, G2:G2 + 1, :] * _dot(y, w_scr[...])


def _merge(x, y_hy, y_na, y_gq, mod, w_out, layer, grp_fn):
    rows = x.shape[0]
    tm = MERGE_TM
    return pl.pallas_call(
        _merge_kernel,
        grid=(rows // tm,),
        in_specs=[
            pl.BlockSpec((tm, D_MODEL), lambda i: (i, 0)),
            pl.BlockSpec((tm, HY_CH), lambda i: (i, 0)),
            pl.BlockSpec((tm, NA_W), lambda i: (i, 0)),
            pl.BlockSpec((tm, GQ_W), lambda i: (i, 0)),
            pl.BlockSpec((1, N_ADA, D_MODEL), lambda i: (grp_fn(i, tm), 0, 0)),
            pl.BlockSpec((None, D_MODEL, D_MODEL), lambda i: (layer, 0, 0), pipeline_mode=pl.Buffered(1)),
        ],
        out_specs=pl.BlockSpec((tm, D_MODEL), lambda i: (i, 0)),
        out_shape=jax.ShapeDtypeStruct((rows, D_MODEL), F32),
        scratch_shapes=[pltpu.VMEM((D_MODEL, D_MODEL), BF16)],
        compiler_params=_cparams(("arbitrary",)),
        name="mixer_out_proj",
    )(x, y_hy, y_na, y_gq, mod, w_out)


def _rope_tables():
    pos = np.arange(DEC_SEQ)
    quarter = HEAD_DIM // 4
    inv = ROPE_BASE ** (-np.arange(quarter, dtype=np.float64) / quarter)
    ang_r = (pos // GRID_W)[:, None] * inv[None, :]
    ang_c = (pos % GRID_W)[:, None] * inv[None, :]
    cos_h = np.concatenate([np.cos(ang_r), np.cos(ang_r), np.cos(ang_c), np.cos(ang_c)], axis=1)
    sin_h = np.concatenate([-np.sin(ang_r), np.sin(ang_r), -np.sin(ang_c), np.sin(ang_c)], axis=1)
    reps = LANES // HEAD_DIM
    return (jnp.asarray(np.tile(cos_h, (1, reps)), F32), jnp.asarray(np.tile(sin_h, (1, reps)), F32))


def _head_sum_matrix():
    idx = np.arange(LANES) // HEAD_DIM
    return jnp.asarray((idx[:, None] == idx[None, :]).astype(np.float32) / HEAD_DIM, BF16)


def _grp_prompt(i, tm):
    return 0


def _grp_sample(i, tm):
    return 1 + (i * tm) // DEC_SEQ


def kernel(x_prompt, x_sample, cache_na_k, cache_na_v, cache_gqa_k, cache_gqa_v, c, c_ctx, ada_w, ada_b, norm_g, ffn_w1, ffn_w3, ffn_w2, w_in, w_out, hy_conv_w, hy_conv_b, hy_filt_w1, hy_filt_b1, hy_filt_w2, hy_filt_b2, hy_filt_w3, hy_freq, hy_skip, na_q_g, na_k_g, na_rpb, gqa_q_g, gqa_k_g, gqa_sink):
    xp = x_prompt.reshape(BATCH * SEQ, D_MODEL)
    xs = x_sample.reshape(DEC_BATCH * DEC_SEQ, D_MODEL)
    cc = jnp.zeros((8, D_MODEL), F32).at[0].set(c_ctx).at[1:1 + DEC_BATCH].set(c)
    kc_na = cache_na_k.reshape(DEC_BATCH, DEPTH, PAST_LEN, NA_W)
    vc_na = cache_na_v.reshape(DEC_BATCH, DEPTH, PAST_LEN, NA_W)
    kc_gq = cache_gqa_k.reshape(DEC_BATCH, DEPTH, PAST_LEN, GKV_W)
    vc_gq = cache_gqa_v.reshape(DEC_BATCH, DEPTH, PAST_LEN, GKV_W)

    rope_tabs = _rope_tables()
    bsum = _head_sum_matrix()
    hy_p = _hyena_consts(SEQ)
    hy_s = _hyena_consts(DEC_SEQ)

    gains = jnp.tile(jnp.stack([na_q_g, na_k_g, gqa_q_g, gqa_k_g], axis=1), (1, 1, NA_HEADS))
    norm_rows = norm_g.reshape(DEPTH * 3, 1, D_MODEL)
    ng = [norm_rows] * 3
    filt_params = _hy_filter_params(hy_filt_w1, hy_filt_b1, hy_filt_w2, hy_filt_b2, hy_filt_w3, hy_freq)

    cache = None
    for l in range(DEPTH):
        mod = _adaln_mod(cc, ada_w, ada_b, l)
        bm2 = _na_bias(na_rpb[l])
        filt = {}
        for L, consts in ((SEQ, hy_p), (DEC_SEQ, hy_s)):
            filt[L] = _hy_filter(L, consts[2], consts[3], filt_params, l)

        xp = _ffn(xp, mod, ng[0], ffn_w1, ffn_w3, ffn_w2, l, 0, _grp_prompt, SH1, SC1, G1)
        x0, z, qn, kn, vn, qg, kgd, vgd, kg, vg = _project(
            xp, mod, ng[1], w_in, l, bsum, gains, hy_conv_w[l], hy_conv_b[l], SEQ, _grp_prompt, None, F32,
            cache_layer=l, prev_cache=cache)
        cache = (kn, vn, kg, vg)
        shp = lambda a: a.reshape(BATCH, SEQ, a.shape[-1])
        y_hy = _hy_conv(shp(z), shp(x0), filt[SEQ][0], filt[SEQ][1], hy_skip[l], hy_p[0], hy_p[1], group=BATCH)
        y_na, y_gq = _ctx_attention(gqa_sink[l], shp(qn), kn, vn, shp(qg), shp(kgd), shp(vgd), l)
        xp = _merge(xp, y_hy.reshape(-1, HY_CH), y_na.reshape(-1, NA_W), y_gq.reshape(-1, GQ_W), mod, w_out, l,
                    _grp_prompt)
        xp = _ffn(xp, mod, ng[2], ffn_w1, ffn_w3, ffn_w2, l, 1, _grp_prompt, SH3, SC3, G3)

        xs = _ffn(xs, mod, ng[0], ffn_w1, ffn_w3, ffn_w2, l, 0, _grp_sample, SH1, SC1, G1)
        x0, z, qn, kn, vn, qg, kgd, vgd = _project(
            xs, mod, ng[1], w_in, l, bsum, gains, hy_conv_w[l], hy_conv_b[l], DEC_SEQ, _grp_sample, rope_tabs, BF16)
        shs = lambda a: a.reshape(DEC_BATCH, DEC_SEQ, a.shape[-1])
        y_hy = _hy_conv(shs(z), shs(x0), filt[DEC_SEQ][0], filt[DEC_SEQ][1], hy_skip[l], hy_s[0], hy_s[1],
                        group=DEC_BATCH)
        y_na, y_gq = _latent_attention(gqa_sink[l], bm2, shs(qn), shs(kn), shs(vn), kc_na, vc_na,
                                       shs(qg), shs(kgd), shs(vgd), kc_gq, vc_gq, l)
        xs = _merge(xs, y_hy.reshape(-1, HY_CH), y_na.reshape(-1, NA_W), y_gq.reshape(-1, GQ_W), mod, w_out, l,
                    _grp_sample)
        xs = _ffn(xs, mod, ng[2], ffn_w1, ffn_w3, ffn_w2, l, 1, _grp_sample, SH3, SC3, G3)

    kn, vn, kg, vg = cache
    return (xp.reshape(BATCH, SEQ, D_MODEL), xs.reshape(DEC_BATCH, DEC_SEQ, D_MODEL),
            kn.reshape(BATCH, DEPTH, SEQ, NA_HEADS, HEAD_DIM), vn.reshape(BATCH, DEPTH, SEQ, NA_HEADS, HEAD_DIM),
            kg.reshape(BATCH, DEPTH, SEQ, GQA_KV_HEADS, HEAD_DIM), vg.reshape(BATCH, DEPTH, SEQ, GQA_KV_HEADS, HEAD_DIM))
```

```python
import functools
import math

import numpy as np
import jax
import jax.numpy as jnp
from jax import lax
from jax.experimental import pallas as pl
from jax.experimental.pallas import tpu as pltpu

F32 = jnp.float32
BF16 = jnp.bfloat16

D_MODEL = 1024
BATCH = 16
SEQ = 256
DEPTH = 2
DEC_BATCH = 2
DEC_SEQ = 2048
PAST_LEN = 256
GRID_W = 64
HEAD_DIM = 64
HY_CH = 256
NA_HEADS = 6
GQA_Q_HEADS = 6
GQA_KV_HEADS = 2
NA_W = NA_HEADS * HEAD_DIM
GQ_W = GQA_Q_HEADS * HEAD_DIM
GKV_W = GQA_KV_HEADS * HEAD_DIM
IN_WIDTH = 3 * HY_CH + 3 * NA_W + GQ_W + 2 * GKV_W
D_FF = 2816
N_ADA = 9
HY_BANDS = 16
HY_EMB = 1 + 2 * HY_BANDS
HY_FILT_W = 64
HY_DECAY_TARGET = 1e-2
HY_FAST_PCT = 0.3
HY_SLOW_PCT = 1.5
NA_ROWS = 8
NA_COLS = 16
GQA_WINDOW = 128
BLK = 128
ROPE_BASE = 10000.0
EPS = 1e-6
NEG_INF = -1e30

LANES = 128
V7X_VMEM_BYTES = 64 * 1024 * 1024
VMEM_LIMIT = V7X_VMEM_BYTES * 15 // 16

OFF_NA = 3 * HY_CH
OFF_GQ = OFF_NA + 3 * NA_W

SH1, SC1, G1, SH2, SC2, G2, SH3, SC3, G3 = range(N_ADA)

FFN_TM = 2048
FFN_TF = 256
FFN_ROW_GROUPS = 4
ROW_TM = 512
MERGE_TM = 1024
HY_TKF = 256
HY_FILT_TL = 256

_HIGHEST = lax.Precision.HIGHEST


def _cparams(sem):
    return pltpu.CompilerParams(dimension_semantics=sem, vmem_limit_bytes=VMEM_LIMIT)


def _silu(x):
    return x * (1.0 / (1.0 + jnp.exp(-x)))


def _dot(a, b):
    return jnp.dot(a, b, preferred_element_type=F32)


def _dot_nt(a, b):
    return lax.dot_general(a, b, (((1,), (1,)), ((), ())), preferred_element_type=F32)


def _split(x):
    hi = x.astype(BF16)
    lo = (x - hi.astype(F32)).astype(BF16)
    return hi, lo


def _mod_norm(x, g, shift, scale):
    ms = jnp.mean(x * x, axis=-1, keepdims=True)
    return (x * lax.rsqrt(ms + EPS) * g) * (1.0 + scale) + shift


def _mod_kernel(c_ref, w_ref, b_ref, o_ref):
    a = _silu(c_ref[...]).astype(BF16)
    o_ref[...] = _dot(a, w_ref[...].astype(BF16)) + b_ref[...]


def _adaln_mod(cc, ada_w, ada_b, layer):
    n_out = N_ADA * D_MODEL
    tn = n_out // 4
    out = pl.pallas_call(
        _mod_kernel,
        grid=(n_out // tn,),
        in_specs=[
            pl.BlockSpec((8, D_MODEL), lambda j: (0, 0)),
            pl.BlockSpec((None, D_MODEL, tn), lambda j: (layer, 0, j)),
            pl.BlockSpec((None, 1, tn), lambda j: (layer, 0, j)),
        ],
        out_specs=pl.BlockSpec((8, tn), lambda j: (0, j)),
        out_shape=jax.ShapeDtypeStruct((8, n_out), F32),
        compiler_params=_cparams(("arbitrary",)),
        name="adaln_mod",
    )(cc, ada_w, ada_b.reshape(DEPTH, 1, n_out))
    return out.reshape(8, N_ADA, D_MODEL)


def _ffn_kernel(x_ref, mod_ref, g_ref, w1_ref, w3_ref, w2_ref, o_ref, h_scr, acc_scr, *, sh, sc, gt):
    j = pl.program_id(1)
    last_j = pl.num_programs(1) - 1
    rows = h_scr.shape[0] // FFN_ROW_GROUPS

    def step(first, last):
        w1 = w1_ref[...].astype(BF16)
        w3 = w3_ref[...].astype(BF16)
        w2 = w2_ref[...].astype(BF16)
        for s in range(FFN_ROW_GROUPS):
            sl = slice(s * rows, (s + 1) * rows)
            if first:
                h = _mod_norm(x_ref[sl, :], g_ref[...], mod_ref[0, sh:sh + 1, :], mod_ref[0, sc:sc + 1, :]).astype(BF16)
                h_scr[sl] = h
            else:
                h = h_scr[sl]
            act = (_silu(_dot(h, w1)) * _dot(h, w3)).astype(BF16)
            part = _dot(act, w2)
            if first:
                acc_scr[sl] = part
            elif last:
                o_ref[sl, :] = x_ref[sl, :] + (0.5 * mod_ref[0, gt:gt + 1, :]) * (acc_scr[sl] + part)
            else:
                acc_scr[sl] += part

    pl.when(j == 0)(lambda: step(True, False))
    pl.when(jnp.logical_and(j > 0, j < last_j))(lambda: step(False, False))
    pl.when(j == last_j)(lambda: step(False, True))


def _ffn(x, mod, g, w1, w3, w2, layer, which, grp_fn, sh, sc, gt):
    rows = x.shape[0]
    tm, tf = FFN_TM, FFN_TF
    return pl.pallas_call(
        functools.partial(_ffn_kernel, sh=sh, sc=sc, gt=gt),
        grid=(rows // tm, D_FF // tf),
        in_specs=[
            pl.BlockSpec((tm, D_MODEL), lambda i, j: (i, 0)),
            pl.BlockSpec((1, N_ADA, D_MODEL), lambda i, j: (grp_fn(i, tm), 0, 0)),
            pl.BlockSpec((None, 1, D_MODEL), lambda i, j: (3 * layer + 2 * which, 0, 0)),
            pl.BlockSpec((None, None, D_MODEL, tf), lambda i, j: (layer, which, 0, j)),
            pl.BlockSpec((None, None, D_MODEL, tf), lambda i, j: (layer, which, 0, j)),
            pl.BlockSpec((None, None, tf, D_MODEL), lambda i, j: (layer, which, j, 0)),
        ],
        out_specs=pl.BlockSpec((tm, D_MODEL), lambda i, j: (i, 0)),
        out_shape=jax.ShapeDtypeStruct((rows, D_MODEL), F32),
        scratch_shapes=[pltpu.VMEM((tm, D_MODEL), BF16), pltpu.VMEM((tm, D_MODEL), F32)],
        compiler_params=_cparams(("arbitrary", "arbitrary")),
        name="ffn_half_step",
    )(x, mod, g, w1, w3, w2)


def _head_norm(xs, bsum, g):
    normed = []
    for p in range(xs.shape[1] // LANES):
        x = xs[:, p * LANES:(p + 1) * LANES]
        ms = _dot((x * x).astype(BF16), bsum)
        normed.append(x * lax.rsqrt(ms + EPS))
    return jnp.concatenate(normed, axis=-1) * g


def _rope(xs, cos, sin, first_half):
    swapped = jnp.where(first_half, pltpu.roll(xs, LANES - 16, 1), pltpu.roll(xs, 16, 1))
    return xs * cos + swapped * sin


def _dup_heads(x2, low_half):
    sw = pltpu.roll(x2, HEAD_DIM, 1)
    return jnp.where(low_half, x2, sw), jnp.where(low_half, sw, x2)


def _expand_kv(x2, low_half):
    d0, d1 = _dup_heads(x2, low_half)
    return jnp.concatenate([d0, x2, d1], axis=-1)


PROJ_GROUP_ROWS = SEQ


def _store_group(ref, val, s, rs, cache_layer):
    val = val.astype(ref.dtype)
    if len(ref.shape) == 2:
        ref[rs, :] = val
    elif len(ref.shape) == 3:
        ref[s] = val
    else:
        for l in range(ref.shape[1]):
            ref[s, l] = val if l == cache_layer else jnp.zeros_like(val)


def _proj_kernel(*refs, rope, cache_layer, n_alias, seq_len):
    x_ref, mod_ref, g_ref, w_ref, bsum_ref, gains_ref, xprev_ref, xnext_ref, cw_ref, cb_ref = refs[:10]
    pos = 10
    if rope:
        cos_ref, sin_ref = refs[pos:pos + 2]
        pos += 2
    pos += n_alias
    x0_ref, z_ref, qn_ref, kn_ref, vn_ref, qg_ref, kge_ref, vge_ref = refs[pos:pos + 8]
    pos += 8
    emit_kv = cache_layer is not None
    if emit_kv:
        kg_ref, vg_ref = refs[pos:pos + 2]
        pos += 2
    u_scr, w_scr = refs[pos:pos + 2]
    scale = HEAD_DIM ** -0.5
    tm = x_ref.shape[0]

    @pl.when(pl.program_id(0) == 0)
    def _():
        w_scr[...] = w_ref[...].astype(BF16)

    def projected(x, cols):
        h = _mod_norm(x, g_ref[...], mod_ref[0, SH2:SH2 + 1, :], mod_ref[0, SC2:SC2 + 1, :])
        return _dot(h.astype(BF16), w_scr[:, cols])

    bsum = bsum_ref[...]
    lane = lax.broadcasted_iota(jnp.int32, (1, LANES), 1)
    first_half = (lane % 32) < 16
    low_half = lane < HEAD_DIM
    for s in range(tm // PROJ_GROUP_ROWS):
        rs = slice(s * PROJ_GROUP_ROWS, (s + 1) * PROJ_GROUP_ROWS)
        u_scr[rs, :] = projected(x_ref[rs, :], slice(0, IN_WIDTH))

        qn = _head_norm(u_scr[rs, OFF_NA:OFF_NA + NA_W], bsum, gains_ref[0:1, :])
        qn_ref[rs, :] = (qn * scale).astype(qn_ref.dtype)
        kn = _head_norm(u_scr[rs, OFF_NA + NA_W:OFF_NA + 2 * NA_W], bsum, gains_ref[1:2, :])
        _store_group(kn_ref, kn, s, rs, cache_layer)
        _store_group(vn_ref, u_scr[rs, OFF_NA + 2 * NA_W:OFF_GQ], s, rs, cache_layer)

        qg = _head_norm(u_scr[rs, OFF_GQ:OFF_GQ + GQ_W], bsum, gains_ref[2:3, :])
        kg = _head_norm(u_scr[rs, OFF_GQ + GQ_W:OFF_GQ + GQ_W + GKV_W], bsum, gains_ref[3:4, 0:GKV_W])
        vg = u_scr[rs, OFF_GQ + GQ_W + GKV_W:IN_WIDTH]
        if emit_kv:
            _store_group(kg_ref, kg, s, rs, cache_layer)
            _store_group(vg_ref, vg, s, rs, cache_layer)
        if rope:
            cos = cos_ref[rs, :]
            sin = sin_ref[rs, :]
            for p in range(GQ_W // LANES):
                sl = slice(p * LANES, (p + 1) * LANES)
                qg_ref[rs, sl] = (_rope(qg[:, sl], cos, sin, first_half) * scale).astype(qg_ref.dtype)
            kg = _rope(kg, cos, sin, first_half)
        else:
            qg_ref[rs, :] = (qg * scale).astype(qg_ref.dtype)
        kge_ref[rs, :] = _expand_kv(kg, low_half).astype(kge_ref.dtype)
        vge_ref[rs, :] = _expand_kv(vg, low_half).astype(vge_ref.dtype)

    hy_cols = slice(0, OFF_NA)
    halo_prev = projected(xprev_ref[...], hy_cols)[7:8]
    halo_next = projected(xnext_ref[...], hy_cols)[0:1]
    row = lax.broadcasted_iota(jnp.int32, (tm, 1), 0)
    t_pos = (pl.program_id(0) * tm + row) & (seq_len - 1)

    def conv_chunk(ci):
        sl = slice(ci * HY_CH, (ci + 1) * HY_CH)
        u = u_scr[:, sl]
        prev = jnp.where(row == 0, halo_prev[:, sl], pltpu.roll(u, 1, 0))
        prev = jnp.where(t_pos == 0, 0.0, prev)
        nxt = jnp.where(row == tm - 1, halo_next[:, sl], pltpu.roll(u, tm - 1, 0))
        nxt = jnp.where(t_pos == seq_len - 1, 0.0, nxt)
        return prev * cw_ref[0:1, sl] + u * cw_ref[1:2, sl] + nxt * cw_ref[2:3, sl] + cb_ref[:, sl]

    x0_ref[...] = conv_chunk(0)
    z_ref[...] = conv_chunk(1) * conv_chunk(2)


def _project(x, mod, g, w_in, layer, bsum, gains, conv_w, conv_b, seq_len, grp_fn, rope_tabs, attn_dtype,
             cache_layer=None, prev_cache=None):
    rows = x.shape[0]
    tm = ROW_TM
    rope = rope_tabs is not None
    assert seq_len & (seq_len - 1) == 0 and (seq_len % tm == 0 or tm % seq_len == 0)
    sub = 8
    in_specs = [
        pl.BlockSpec((tm, D_MODEL), lambda i: (i, 0)),
        pl.BlockSpec((1, N_ADA, D_MODEL), lambda i: (grp_fn(i, tm), 0, 0)),
        pl.BlockSpec((None, 1, D_MODEL), lambda i: (3 * layer + 1, 0, 0)),
        pl.BlockSpec((None, D_MODEL, IN_WIDTH), lambda i: (layer, 0, 0), pipeline_mode=pl.Buffered(1)),
        pl.BlockSpec((LANES, LANES), lambda i: (0, 0)),
        pl.BlockSpec((None, 4, NA_W), lambda i: (layer, 0, 0)),
        pl.BlockSpec((sub, D_MODEL), lambda i: (jnp.maximum(i * (tm // sub) - 1, 0), 0)),
        pl.BlockSpec((sub, D_MODEL), lambda i: (jnp.minimum((i + 1) * (tm // sub), rows // sub - 1), 0)),
        pl.BlockSpec((3, OFF_NA), lambda i: (0, 0)),
        pl.BlockSpec((1, OFF_NA), lambda i: (0, 0)),
    ]
    args = [x, mod, g, w_in, bsum, gains, x, x, conv_w, conv_b.reshape(1, OFF_NA)]
    if rope:
        seq_tiles = DEC_SEQ // tm
        in_specs += [pl.BlockSpec((tm, LANES), lambda i: (i % seq_tiles, 0))] * 2
        args += list(rope_tabs)
    outs = [(HY_CH, F32)] * 2 + [(NA_W, attn_dtype)] * 3 + [(GQ_W, attn_dtype)] * 3
    if cache_layer is not None:
        outs += [(GKV_W, F32), (GKV_W, F32)]
    out_specs = [pl.BlockSpec((tm, w), lambda i: (i, 0)) for w, _ in outs]
    out_shape = [jax.ShapeDtypeStruct((rows, w), dt) for w, dt in outs]
    aliases = {}
    if cache_layer is not None:
        cache_outs = (3, 4, 8, 9)
        seqs = tm // SEQ
        for o in cache_outs:
            w = outs[o][0]
            out_shape[o] = jax.ShapeDtypeStruct((BATCH, DEPTH, SEQ, w), F32)
            if prev_cache is None:
                out_specs[o] = pl.BlockSpec((seqs, DEPTH, SEQ, w), lambda i: (i, 0, 0, 0))
            else:
                out_specs[o] = pl.BlockSpec((seqs, None, SEQ, w), lambda i: (i, cache_layer, 0, 0))
        if prev_cache is not None:
            aliases = {len(args) + n: o for n, o in enumerate(cache_outs)}
            in_specs += [pl.BlockSpec(memory_space=pl.ANY)] * len(cache_outs)
            args += list(prev_cache)
    return pl.pallas_call(
        functools.partial(_proj_kernel, rope=rope, cache_layer=cache_layer, n_alias=len(aliases), seq_len=seq_len),
        grid=(rows // tm,),
        in_specs=in_specs,
        out_specs=out_specs,
        out_shape=out_shape,
        input_output_aliases=aliases,
        scratch_shapes=[pltpu.VMEM((tm, IN_WIDTH), F32), pltpu.VMEM((D_MODEL, IN_WIDTH), BF16)],
        compiler_params=_cparams(("arbitrary",)),
        name="mixer_in_proj",
    )(*args)


HEADS_PER_CHAIN = 2
CHAIN_W = HEADS_PER_CHAIN * HEAD_DIM


def _log2(n):
    assert n & (n - 1) == 0
    return n.bit_length() - 1


def _stack_heads(q, n_heads):
    m, w = q.shape
    rows = lax.broadcasted_iota(jnp.int32, (n_heads * m, w), 0)
    lanes = lax.broadcasted_iota(jnp.int32, (n_heads * m, w), 1)
    own = jnp.right_shift(rows, _log2(m)) == jnp.right_shift(lanes, _log2(HEAD_DIM))
    return jnp.where(own, jnp.concatenate([q] * n_heads, axis=0), jnp.zeros((), q.dtype))


def _unstack_heads(o, n_heads):
    m = o.shape[0] // n_heads
    head = jnp.right_shift(lax.broadcasted_iota(jnp.int32, (m, o.shape[1]), 1), _log2(HEAD_DIM))
    out = o[0:m]
    for h in range(1, n_heads):
        out = jnp.where(head == h, o[h * m:(h + 1) * m], out)
    return out


def _sink_column(sink_ref, m, first_head, n_heads):
    block = jnp.right_shift(lax.broadcasted_iota(jnp.int32, (n_heads * m, 1), 0), _log2(m))
    col = jnp.full((n_heads * m, 1), sink_ref[first_head], F32)
    for h in range(1, n_heads):
        col = jnp.where(block == h, sink_ref[first_head + h], col)
    return col


def _softmax_pv(scores, values, sink):
    m = scores[0].max(axis=-1, keepdims=True)
    for s in scores[1:]:
        m = jnp.maximum(m, s.max(axis=-1, keepdims=True))
    if sink is not None:
        m = jnp.maximum(m, sink)
    den = None
    acc = None
    for s, v in zip(scores, values):
        p = jnp.exp(s - m)
        d = p.sum(axis=-1, keepdims=True)
        o = _dot(p.astype(BF16), v)
        den = d if den is None else den + d
        acc = o if acc is None else acc + o
    if sink is not None:
        den = den + jnp.exp(sink - m)
    return acc / den


CTX_SEQS_PER_STEP = 2


def _ctx_attn_kernel(sink_ref, qn_ref, kn_ref, vn_ref, qg_ref, kge_ref, vge_ref, yna_ref, ygq_ref):
    m = qn_ref.shape[1]
    low = lax.broadcasted_iota(jnp.int32, (m, LANES), 1) < HEAD_DIM
    for q_ref, k_ref, v_ref, y_ref, has_sink in ((qn_ref, kn_ref, vn_ref, yna_ref, False),
                                                  (qg_ref, kge_ref, vge_ref, ygq_ref, True)):
        for bi in range(CTX_SEQS_PER_STEP):
            for p in range(NA_W // LANES):
                sl = slice(p * LANES, (p + 1) * LANES)
                q2 = q_ref[bi, :, sl].astype(BF16)
                k2 = k_ref[bi, :, sl].astype(BF16)
                v2 = v_ref[bi, :, sl].astype(BF16)
                halves = []
                for half, keep in enumerate((low, ~low)):
                    qh = jnp.where(keep, q2, jnp.zeros((), BF16))
                    sink = sink_ref[2 * p + half] if has_sink else None
                    halves.append(_softmax_pv([_dot_nt(qh, k2)], [v2], sink))
                y_ref[bi, :, sl] = jnp.where(low, halves[0], halves[1]).astype(y_ref.dtype)


def _ctx_attention(sink, qn, kn_cache, vn_cache, qg, kge, vge, layer):
    b, l, _ = qn.shape

    nb = CTX_SEQS_PER_STEP

    def spec(w):
        return pl.BlockSpec((nb, l, w), lambda i: (i, 0, 0))

    cache_spec = pl.BlockSpec((nb, None, l, NA_W), lambda i: (i, layer, 0, 0))
    return pl.pallas_call(
        _ctx_attn_kernel,
        grid=(b // nb,),
        in_specs=[pl.BlockSpec(memory_space=pltpu.SMEM), spec(NA_W), cache_spec, cache_spec, spec(GQ_W),
                  spec(GQ_W), spec(GQ_W)],
        out_specs=[spec(NA_W), spec(GQ_W)],
        out_shape=[jax.ShapeDtypeStruct((b, l, NA_W), BF16), jax.ShapeDtypeStruct((b, l, GQ_W), BF16)],
        compiler_params=_cparams(("arbitrary",)),
        name="context_attention",
    )(sink, qn, kn_cache, vn_cache, qg, kge, vge)


NA_KEYS = NA_ROWS * GRID_W
NA_ROWS_PER_STEP = 8


def _na_bias_kernel(r_ref, oh_ref, mk_ref, o_ref):
    o_ref[...] = jnp.dot(r_ref[...], oh_ref[...], precision=_HIGHEST, preferred_element_type=F32) + mk_ref[...]


def _na_bias_tables():
    q = np.arange(GRID_W)[:, None]
    kc = np.arange(GRID_W)[None, :]
    win_lo = np.clip(q - NA_COLS // 2, 0, GRID_W - NA_COLS)
    ok = (kc >= win_lo) & (kc < win_lo + NA_COLS)
    j = kc - q + NA_COLS - 1
    onehot = np.zeros((LANES, GRID_W * GRID_W), np.float32)
    qq, kk = np.nonzero(ok)
    onehot[j[qq, kk], qq * GRID_W + kk] = 1.0
    mask = np.where(ok, 0.0, NEG_INF).astype(np.float32).reshape(1, GRID_W * GRID_W)
    return onehot, mask


def _na_bias(rpb_l):
    n_dr = 2 * NA_ROWS - 1
    rows = NA_HEADS * n_dr
    rows_pad = -(-rows // 8) * 8
    r = jnp.zeros((rows_pad, LANES), F32).at[:rows, :2 * NA_COLS - 1].set(rpb_l.reshape(rows, 2 * NA_COLS - 1))
    onehot, mask = _na_bias_tables()
    out = pl.pallas_call(
        _na_bias_kernel,
        out_shape=jax.ShapeDtypeStruct((rows_pad, GRID_W * GRID_W), F32),
        compiler_params=pltpu.CompilerParams(vmem_limit_bytes=VMEM_LIMIT),
        name="na_bias_expand",
    )(r, jnp.asarray(onehot), jnp.asarray(mask))
    bm = out[:rows].reshape(NA_HEADS, n_dr, GRID_W, GRID_W)
    bm2 = jnp.concatenate([bm[:, :-1], bm[:, 1:]], axis=-1)
    return bm2.transpose(1, 0, 2, 3).reshape(n_dr - 1, NA_HEADS * GRID_W, 2 * GRID_W)


def _na_attn_kernel(q_ref, k_ref, v_ref, kc_ref, vc_ref, bm_ref, o_ref):
    n_rows = DEC_SEQ // GRID_W
    kc = kc_ref[0, 0].astype(BF16)
    vc = vc_ref[0, 0].astype(BF16)
    for rr in range(NA_ROWS_PER_STEP):
        r = pl.program_id(1) * NA_ROWS_PER_STEP + rr
        start = jnp.clip(r - NA_ROWS // 2, 0, n_rows - NA_ROWS)
        shift = r - start
        row0 = pl.multiple_of(start * GRID_W, GRID_W)
        rows = slice(rr * GRID_W, (rr + 1) * GRID_W)
        q = _stack_heads(q_ref[0, rows, :], NA_HEADS)
        k = k_ref[0, pl.ds(row0, NA_KEYS), :]
        v = v_ref[0, pl.ds(row0, NA_KEYS), :]
        bias = jnp.concatenate([bm_ref[2 * jj - shift + NA_ROWS - 1] for jj in range(NA_ROWS // 2)], axis=-1)
        s_loc = _dot_nt(q, k) + bias
        s_ctx = _dot_nt(q, kc)
        o = _softmax_pv([s_loc, s_ctx], [v, vc], None)
        o_ref[0, rows, :] = _unstack_heads(o, NA_HEADS).astype(o_ref.dtype)


WIN_KEYS = 3 * BLK
WIN_BLOCKS_PER_STEP = 4


def _win_attn_kernel(sink_ref, mask_ref, q_ref, kge_ref, vge_ref, kc_ref, vc_ref, o_ref):
    n = kge_ref.shape[1]
    n_blocks = n // BLK
    low_c = lax.broadcasted_iota(jnp.int32, (1, LANES), 1) < HEAD_DIM
    kce = _expand_kv(kc_ref[0, 0], low_c).astype(BF16)
    vce = _expand_kv(vc_ref[0, 0], low_c).astype(BF16)
    n_chains = GQA_Q_HEADS // HEADS_PER_CHAIN
    sinks = [_sink_column(sink_ref, BLK, p * HEADS_PER_CHAIN, HEADS_PER_CHAIN) for p in range(n_chains)]
    for bb in range(WIN_BLOCKS_PER_STEP):
        nb = pl.program_id(1) * WIN_BLOCKS_PER_STEP + bb
        start = pl.multiple_of(jnp.clip((nb - 1) * BLK, 0, n - WIN_KEYS), BLK)
        variant = jnp.where(nb == 0, 0, jnp.where(nb == n_blocks - 1, 2, 1))
        rows = slice(bb * BLK, (bb + 1) * BLK)
        for p in range(n_chains):
            sl = slice(p * CHAIN_W, (p + 1) * CHAIN_W)
            q = _stack_heads(q_ref[0, rows, sl], HEADS_PER_CHAIN)
            s_loc = _dot_nt(q, kge_ref[0, pl.ds(start, WIN_KEYS), sl]) + mask_ref[variant]
            s_ctx = _dot_nt(q, kce[:, sl])
            o = _softmax_pv([s_loc, s_ctx], [vge_ref[0, pl.ds(start, WIN_KEYS), sl], vce[:, sl]], sinks[p])
            o_ref[0, rows, sl] = _unstack_heads(o, HEADS_PER_CHAIN).astype(o_ref.dtype)


def _win_mask_table():
    qi = np.arange(BLK)[:, None]
    kj = np.arange(WIN_KEYS)[None, :]
    tabs = [np.where(np.abs(qi + off - kj) <= GQA_WINDOW, 0.0, NEG_INF) for off in (0, BLK, 2 * BLK)]
    return jnp.asarray(np.stack([np.tile(t, (HEADS_PER_CHAIN, 1)) for t in tabs]).astype(np.float32))


def _latent_attn_kernel(sink_ref, mask_ref, bm_ref, qn_ref, kn_ref, vn_ref, kcn_ref, vcn_ref,
                        qg_ref, kge_ref, vge_ref, kcg_ref, vcg_ref, yna_ref, ygq_ref):
    _na_attn_kernel(qn_ref, kn_ref, vn_ref, kcn_ref, vcn_ref, bm_ref, yna_ref)
    _win_attn_kernel(sink_ref, mask_ref, qg_ref, kge_ref, vge_ref, kcg_ref, vcg_ref, ygq_ref)


def _latent_attention(sink, bm2, qn, kn, vn, kc_na, vc_na, qg, kge, vge, kc_gq, vc_gq, layer):
    b, n, _ = qn.shape
    mask = _win_mask_table()
    tq = WIN_BLOCKS_PER_STEP * BLK
    assert tq == NA_ROWS_PER_STEP * GRID_W

    def q_spec(w):
        return pl.BlockSpec((1, tq, w), lambda i, j: (i, j, 0))

    def seq_spec(w):
        return pl.BlockSpec((1, n, w), lambda i, j: (i, 0, 0))

    def cache_spec(w):
        return pl.BlockSpec((1, 1, PAST_LEN, w), lambda i, j: (i, layer, 0, 0))

    return pl.pallas_call(
        _latent_attn_kernel,
        grid=(b, n // tq),
        in_specs=[
            pl.BlockSpec(memory_space=pltpu.SMEM),
            pl.BlockSpec(mask.shape, lambda i, j: (0, 0, 0)),
            pl.BlockSpec(bm2.shape, lambda i, j: (0, 0, 0)),
            q_spec(NA_W), seq_spec(NA_W), seq_spec(NA_W), cache_spec(NA_W), cache_spec(NA_W),
            q_spec(GQ_W), seq_spec(GQ_W), seq_spec(GQ_W), cache_spec(GKV_W), cache_spec(GKV_W),
        ],
        out_specs=[q_spec(NA_W), q_spec(GQ_W)],
        out_shape=[jax.ShapeDtypeStruct((b, n, NA_W), BF16), jax.ShapeDtypeStruct((b, n, GQ_W), BF16)],
        compiler_params=_cparams(("arbitrary", "arbitrary")),
        name="latent_attention",
    )(sink, mask, bm2, qn, kn, vn, kc_na, vc_na, qg, kge, vge, kc_gq, vc_gq)


@functools.lru_cache(maxsize=None)
def _hyena_consts_np(L):
    n = 2 * L
    k = np.arange(L, dtype=np.int64)
    ang = (2.0 * np.pi / n) * ((k[:, None] * k[None, :]) % n).astype(np.float64)

    tkf = min(HY_TKF, L)
    cm = np.cos(ang).astype(np.float32).reshape(L // tkf, tkf, L)
    sm = (-np.sin(ang)).astype(np.float32).reshape(L // tkf, tkf, L)
    f_rows = np.concatenate([cm, sm], axis=1)
    f_cols = np.ascontiguousarray(f_rows.transpose(0, 2, 1))
    idx = np.arange(L, dtype=np.float32)
    t = idx / np.float32(L - 1)
    bands = np.linspace(1e-4, HY_BANDS - 1, HY_BANDS, dtype=np.float32)
    fang = np.float32(2.0 * math.pi / L) * idx[:, None] * bands[None, :]
    feats = np.zeros((L, LANES), np.float32)
    feats[:, 0] = t
    feats[:, 1:1 + HY_BANDS] = np.cos(fang)
    feats[:, 1 + HY_BANDS:HY_EMB] = -np.sin(fang)
    max_decay = math.log(HY_DECAY_TARGET) / HY_FAST_PCT
    min_decay = math.log(HY_DECAY_TARGET) / HY_SLOW_PCT
    deltas = np.abs(np.linspace(min_decay, max_decay, HY_CH, dtype=np.float32))
    decay = np.exp(-t[:, None] * deltas[None, :]).astype(np.float32)
    decay2 = np.concatenate([decay, decay], axis=1)
    half = L // 2
    feats = np.concatenate([feats[:half], feats[half:]], axis=1)
    decay2 = np.concatenate([decay2[:half], decay2[half:]], axis=1)
    return f_rows, f_cols, feats, decay2


def _hyena_consts(L):
    return tuple(jnp.asarray(a) for a in _hyena_consts_np(L))


def _hy_filter_kernel(feats_ref, w1_ref, b1_ref, w2_ref, b2_ref, w3_ref, freq_ref, decay_ref, wsum_ref, wdiff_ref):
    def dot_hi(a, b):
        a_hi, a_lo = _split(a)
        b_hi, b_lo = _split(b)
        return _dot(a_hi, b_hi) + (_dot(a_lo, b_hi) + _dot(a_hi, b_lo))

    hid = jnp.sin(freq_ref[0:1, :] * (dot_hi(feats_ref[...], w1_ref[...]) + b1_ref[...]))
    hid = jnp.sin(freq_ref[1:2, :] * (dot_hi(hid, w2_ref[...]) + b2_ref[...]))
    taps2 = dot_hi(hid, w3_ref[...]) * decay_ref[...]
    tl = feats_ref.shape[0]
    row = pl.program_id(0) * tl + lax.broadcasted_iota(jnp.int32, (tl, HY_CH), 0)
    for part in range(2):
        fwd = taps2[:, part * 2 * HY_CH:part * 2 * HY_CH + HY_CH]
        bwd = taps2[:, part * 2 * HY_CH + HY_CH:(part + 1) * 2 * HY_CH]
        if part == 0:
            bwd = jnp.where(row == 0, 0.0, bwd)
        lanes = slice(part * HY_CH, (part + 1) * HY_CH)
        wsum_ref[:, lanes] = fwd + bwd
        wdiff_ref[:, lanes] = fwd - bwd


def _block_diag2(w):
    d, r, c = w.shape
    return jnp.zeros((d, 2 * r, 2 * c), w.dtype).at[:, :r, :c].set(w).at[:, r:, c:].set(w)


def _hy_filter_params(w1, b1, w2, b2, w3, freq):
    w1p = jnp.pad(w1, ((0, 0), (0, LANES - HY_EMB), (0, 0)))
    return (_block_diag2(w1p), jnp.tile(b1, (1, 2))[:, None, :], _block_diag2(w2), jnp.tile(b2, (1, 2))[:, None, :],
            _block_diag2(w3), jnp.tile(freq, (1, 1, 2)))


def _hy_filter(L, feats, decay2, params, layer):
    half = L // 2
    tl = min(HY_FILT_TL, half)

    def rows_of(a):
        return pl.BlockSpec((tl, a.shape[1]), lambda i: (i, 0))

    def of_layer(a):
        return pl.BlockSpec((None,) + a.shape[1:], lambda i: (layer,) + (0,) * (a.ndim - 1))

    out_spec = pl.BlockSpec((tl, 2 * HY_CH), lambda i: (i, 0))
    return pl.pallas_call(
        _hy_filter_kernel,
        grid=(half // tl,),
        in_specs=[rows_of(feats)] + [of_layer(p) for p in params] + [rows_of(decay2)],
        out_specs=[out_spec, out_spec],
        out_shape=[jax.ShapeDtypeStruct((half, 2 * HY_CH), F32)] * 2,
        compiler_params=_cparams(("arbitrary",)),
        name="hyena_filter",
    )(feats, *params, decay2)


def _hy_conv_kernel(z_ref, x0_ref, wsum_ref, wdiff_ref, skip_ref, fr_ref, fc_ref,
                    o_ref, z_scr, ws_scr, wd_scr, acc_scr, *, group):
    t = pl.program_id(1)
    L = z_ref.shape[1]
    n = 2 * L
    half = L // 2

    @pl.when(t == 0)
    def _():
        z_scr[...] = z_ref[...].astype(BF16)
        for part in range(2):
            rows = slice(part * half, (part + 1) * half)
            lanes = slice(part * HY_CH, (part + 1) * HY_CH)
            ws_scr[rows, :] = wsum_ref[:, lanes].astype(BF16)
            wd_scr[rows, :] = wdiff_ref[:, lanes].astype(BF16)
        acc_scr[...] = jnp.zeros_like(acc_scr)

    fr = fr_ref[...].astype(BF16)
    fc = fc_ref[...].astype(BF16)
    tkf = fr.shape[0] // 2
    k_re = _dot(fr[0:tkf], ws_scr[...])
    k_im = _dot(fr[tkf:2 * tkf], wd_scr[...])
    for b in range(group):
        zf = _dot(fr, z_scr[b])
        z_re, z_im = zf[0:tkf], zf[tkf:2 * tkf]
        y = jnp.concatenate([z_re * k_re - z_im * k_im, z_re * k_im + z_im * k_re], axis=0).astype(BF16)
        acc_scr[b] += _dot(fc, y)

    @pl.when(t == pl.num_programs(1) - 1)
    def _():
        row = lax.broadcasted_iota(jnp.int32, (L, 1), 0)
        sgn = (1 - 2 * (row % 2)).astype(F32)
        w_a, w_b = wsum_ref[:, 0:HY_CH], wsum_ref[:, HY_CH:2 * HY_CH]
        sgn_h = sgn[0:half]
        w_dc = w_a.sum(axis=0, keepdims=True) + w_b.sum(axis=0, keepdims=True)
        w_ny = (w_a * sgn_h).sum(axis=0, keepdims=True) + (w_b * sgn_h).sum(axis=0, keepdims=True)
        for b in range(group):
            z = z_ref[b]
            z_dc = z.sum(axis=0, keepdims=True)
            z_ny = (z * sgn).sum(axis=0, keepdims=True)
            conv = (2.0 / n) * acc_scr[b] - (1.0 / n) * (z_dc * w_dc) + (1.0 / n) * (sgn * (z_ny * w_ny))
            o_ref[b] = (x0_ref[b] * (conv + z * skip_ref[...])).astype(o_ref.dtype)


def _hy_conv(z, x0, wsum, wdiff, skip, f_rows, f_cols, group):
    b, L, c = z.shape
    n_tiles, two_tkf, _ = f_rows.shape
    row_spec = pl.BlockSpec((None, two_tkf, L), lambda g, t: (t, 0, 0))
    col_spec = pl.BlockSpec((None, L, two_tkf), lambda g, t: (t, 0, 0))
    once = pl.Buffered(1)
    seq_spec = pl.BlockSpec((group, L, c), lambda g, t: (g, 0, 0), pipeline_mode=once)
    w_spec = pl.BlockSpec((L // 2, 2 * c), lambda g, t: (0, 0), pipeline_mode=once)
    return pl.pallas_call(
        functools.partial(_hy_conv_kernel, group=group),
        grid=(b // group, n_tiles),
        in_specs=[seq_spec, seq_spec, w_spec, w_spec, pl.BlockSpec((1, c), lambda g, t: (0, 0)),
                  row_spec, col_spec],
        out_specs=pl.BlockSpec((group, L, c), lambda g, t: (g, 0, 0)),
        out_shape=jax.ShapeDtypeStruct((b, L, c), BF16),
        scratch_shapes=[pltpu.VMEM((group, L, c), BF16), pltpu.VMEM((L, c), BF16), pltpu.VMEM((L, c), BF16),
                        pltpu.VMEM((group, L, c), F32)],
        compiler_params=_cparams(("arbitrary", "arbitrary")),
        name="hyena_long_conv",
    )(z, x0, wsum, wdiff, skip.reshape(1, c), f_rows, f_cols)


def _merge_kernel(x_ref, yhy_ref, yna_ref, ygq_ref, mod_ref, w_ref, o_ref, w_scr):
    @pl.when(pl.program_id(0) == 0)
    def _():
        w_scr[...] = w_ref[...].astype(BF16)

    y = jnp.concatenate([yhy_ref[...], yna_ref[...], ygq_ref[...]], axis=-1)
    o_ref[...] = x_ref[...] + mod_ref[0, G2:G2 + 1, :] * _dot(y, w_scr[...])


def _merge(x, y_hy, y_na, y_gq, mod, w_out, layer, grp_fn):
    rows = x.shape[0]
    tm = MERGE_TM
    return pl.pallas_call(
        _merge_kernel,
        grid=(rows // tm,),
        in_specs=[
            pl.BlockSpec((tm, D_MODEL), lambda i: (i, 0)),
            pl.BlockSpec((tm, HY_CH), lambda i: (i, 0)),
            pl.BlockSpec((tm, NA_W), lambda i: (i, 0)),
            pl.BlockSpec((tm, GQ_W), lambda i: (i, 0)),
            pl.BlockSpec((1, N_ADA, D_MODEL), lambda i: (grp_fn(i, tm), 0, 0)),
            pl.BlockSpec((None, D_MODEL, D_MODEL), lambda i: (layer, 0, 0), pipeline_mode=pl.Buffered(1)),
        ],
        out_specs=pl.BlockSpec((tm, D_MODEL), lambda i: (i, 0)),
        out_shape=jax.ShapeDtypeStruct((rows, D_MODEL), F32),
        scratch_shapes=[pltpu.VMEM((D_MODEL, D_MODEL), BF16)],
        compiler_params=_cparams(("arbitrary",)),
        name="mixer_out_proj",
    )(x, y_hy, y_na, y_gq, mod, w_out)


def _rope_tables():
    pos = np.arange(DEC_SEQ)
    quarter = HEAD_DIM // 4
    inv = ROPE_BASE ** (-np.arange(quarter, dtype=np.float64) / quarter)
    ang_r = (pos // GRID_W)[:, None] * inv[None, :]
    ang_c = (pos % GRID_W)[:, None] * inv[None, :]
    cos_h = np.concatenate([np.cos(ang_r), np.cos(ang_r), np.cos(ang_c), np.cos(ang_c)], axis=1)
    sin_h = np.concatenate([-np.sin(ang_r), np.sin(ang_r), -np.sin(ang_c), np.sin(ang_c)], axis=1)
    reps = LANES // HEAD_DIM
    return (jnp.asarray(np.tile(cos_h, (1, reps)), F32), jnp.asarray(np.tile(sin_h, (1, reps)), F32))


def _head_sum_matrix():
    idx = np.arange(LANES) // HEAD_DIM
    return jnp.asarray((idx[:, None] == idx[None, :]).astype(np.float32) / HEAD_DIM, BF16)


def _grp_prompt(i, tm):
    return 0


def _grp_sample(i, tm):
    return 1 + (i * tm) // DEC_SEQ


def kernel(x_prompt, x_sample, cache_na_k, cache_na_v, cache_gqa_k, cache_gqa_v, c, c_ctx, ada_w, ada_b, norm_g, ffn_w1, ffn_w3, ffn_w2, w_in, w_out, hy_conv_w, hy_conv_b, hy_filt_w1, hy_filt_b1, hy_filt_w2, hy_filt_b2, hy_filt_w3, hy_freq, hy_skip, na_q_g, na_k_g, na_rpb, gqa_q_g, gqa_k_g, gqa_sink):
    xp = x_prompt.reshape(BATCH * SEQ, D_MODEL)
    xs = x_sample.reshape(DEC_BATCH * DEC_SEQ, D_MODEL)
    cc = jnp.zeros((8, D_MODEL), F32).at[0].set(c_ctx).at[1:1 + DEC_BATCH].set(c)
    kc_na = cache_na_k.reshape(DEC_BATCH, DEPTH, PAST_LEN, NA_W)
    vc_na = cache_na_v.reshape(DEC_BATCH, DEPTH, PAST_LEN, NA_W)
    kc_gq = cache_gqa_k.reshape(DEC_BATCH, DEPTH, PAST_LEN, GKV_W)
    vc_gq = cache_gqa_v.reshape(DEC_BATCH, DEPTH, PAST_LEN, GKV_W)

    rope_tabs = _rope_tables()
    bsum = _head_sum_matrix()
    hy_p = _hyena_consts(SEQ)
    hy_s = _hyena_consts(DEC_SEQ)

    gains = jnp.tile(jnp.stack([na_q_g, na_k_g, gqa_q_g, gqa_k_g], axis=1), (1, 1, NA_HEADS))
    norm_rows = norm_g.reshape(DEPTH * 3, 1, D_MODEL)
    ng = [norm_rows] * 3
    filt_params = _hy_filter_params(hy_filt_w1, hy_filt_b1, hy_filt_w2, hy_filt_b2, hy_filt_w3, hy_freq)

    cache = None
    for l in range(DEPTH):
        mod = _adaln_mod(cc, ada_w, ada_b, l)
        bm2 = _na_bias(na_rpb[l])
        filt = {}
        for L, consts in ((SEQ, hy_p), (DEC_SEQ, hy_s)):
            filt[L] = _hy_filter(L, consts[2], consts[3], filt_params, l)

        xp = _ffn(xp, mod, ng[0], ffn_w1, ffn_w3, ffn_w2, l, 0, _grp_prompt, SH1, SC1, G1)
        x0, z, qn, kn, vn, qg, kgd, vgd, kg, vg = _project(
            xp, mod, ng[1], w_in, l, bsum, gains, hy_conv_w[l], hy_conv_b[l], SEQ, _grp_prompt, None, F32,
            cache_layer=l, prev_cache=cache)
        cache = (kn, vn, kg, vg)
        shp = lambda a: a.reshape(BATCH, SEQ, a.shape[-1])
        y_hy = _hy_conv(shp(z), shp(x0), filt[SEQ][0], filt[SEQ][1], hy_skip[l], hy_p[0], hy_p[1], group=BATCH)
        y_na, y_gq = _ctx_attention(gqa_sink[l], shp(qn), kn, vn, shp(qg), shp(kgd), shp(vgd), l)
        xp = _merge(xp, y_hy.reshape(-1, HY_CH), y_na.reshape(-1, NA_W), y_gq.reshape(-1, GQ_W), mod, w_out, l,
                    _grp_prompt)
        xp = _ffn(xp, mod, ng[2], ffn_w1, ffn_w3, ffn_w2, l, 1, _grp_prompt, SH3, SC3, G3)

        xs = _ffn(xs, mod, ng[0], ffn_w1, ffn_w3, ffn_w2, l, 0, _grp_sample, SH1, SC1, G1)
        x0, z, qn, kn, vn, qg, kgd, vgd = _project(
            xs, mod, ng[1], w_in, l, bsum, gains, hy_conv_w[l], hy_conv_b[l], DEC_SEQ, _grp_sample, rope_tabs, BF16)
        shs = lambda a: a.reshape(DEC_BATCH, DEC_SEQ, a.shape[-1])
        y_hy = _hy_conv(shs(z), shs(x0), filt[DEC_SEQ][0], filt[DEC_SEQ][1], hy_skip[l], hy_s[0], hy_s[1],
                        group=DEC_BATCH)
        y_na, y_gq = _latent_attention(gqa_sink[l], bm2, shs(qn), shs(kn), shs(vn), kc_na, vc_na,
                                       shs(qg), shs(kgd), shs(vgd), kc_gq, vc_gq, l)
        xs = _merge(xs, y_hy.reshape(-1, HY_CH), y_na.reshape(-1, NA_W), y_gq.reshape(-1, GQ_W), mod, w_out, l,
                    _grp_sample)
        xs = _ffn(xs, mod, ng[2], ffn_w1, ffn_w3, ffn_w2, l, 1, _grp_sample, SH3, SC3, G3)

    kn, vn, kg, vg = cache
    return (xp.reshape(BATCH, SEQ, D_MODEL), xs.reshape(DEC_BATCH, DEC_SEQ, D_MODEL),
            kn.reshape(BATCH, DEPTH, SEQ, NA_HEADS, HEAD_DIM), vn.reshape(BATCH, DEPTH, SEQ, NA_HEADS, HEAD_DIM),
            kg.reshape(BATCH, DEPTH, SEQ, GQA_KV_HEADS, HEAD_DIM), vg.reshape(BATCH, DEPTH, SEQ, GQA_KV_HEADS, HEAD_DIM))
```

```python
import functools
import math

import numpy as np
import jax
import jax.numpy as jnp
from jax import lax
from jax.experimental import pallas as pl
from jax.experimental.pallas import tpu as pltpu

F32 = jnp.float32
BF16 = jnp.bfloat16

D_MODEL = 1024
BATCH = 16
SEQ = 256
DEPTH = 2
DEC_BATCH = 2
DEC_SEQ = 2048
PAST_LEN = 256
GRID_W = 64
HEAD_DIM = 64
HY_CH = 256
NA_HEADS = 6
GQA_Q_HEADS = 6
GQA_KV_HEADS = 2
NA_W = NA_HEADS * HEAD_DIM
GQ_W = GQA_Q_HEADS * HEAD_DIM
GKV_W = GQA_KV_HEADS * HEAD_DIM
IN_WIDTH = 3 * HY_CH + 3 * NA_W + GQ_W + 2 * GKV_W
D_FF = 2816
N_ADA = 9
HY_BANDS = 16
HY_EMB = 1 + 2 * HY_BANDS
HY_FILT_W = 64
HY_DECAY_TARGET = 1e-2
HY_FAST_PCT = 0.3
HY_SLOW_PCT = 1.5
NA_ROWS = 8
NA_COLS = 16
GQA_WINDOW = 128
BLK = 128
ROPE_BASE = 10000.0
EPS = 1e-6
NEG_INF = -1e30

LANES = 128
V7X_VMEM_BYTES = 64 * 1024 * 1024
VMEM_LIMIT = V7X_VMEM_BYTES * 15 // 16

OFF_NA = 3 * HY_CH
OFF_GQ = OFF_NA + 3 * NA_W

SH1, SC1, G1, SH2, SC2, G2, SH3, SC3, G3 = range(N_ADA)

FFN_TM = 2048
FFN_TF = 256
FFN_ROW_GROUPS = 4
ROW_TM = 512
MERGE_TM = 1024
HY_TKF = 256
HY_FILT_TL = 256

_HIGHEST = lax.Precision.HIGHEST


def _cparams(sem):
    return pltpu.CompilerParams(dimension_semantics=sem, vmem_limit_bytes=VMEM_LIMIT)


def _silu(x):
    return x * (1.0 / (1.0 + jnp.exp(-x)))


def _dot(a, b):
    return jnp.dot(a, b, preferred_element_type=F32)


def _dot_nt(a, b):
    return lax.dot_general(a, b, (((1,), (1,)), ((), ())), preferred_element_type=F32)


def _split(x):
    hi = x.astype(BF16)
    lo = (x - hi.astype(F32)).astype(BF16)
    return hi, lo


def _mod_norm(x, g, shift, scale):
    ms = jnp.mean(x * x, axis=-1, keepdims=True)
    return (x * lax.rsqrt(ms + EPS) * g) * (1.0 + scale) + shift


def _mod_kernel(c_ref, w_ref, b_ref, o_ref):
    a = _silu(c_ref[...]).astype(BF16)
    o_ref[...] = _dot(a, w_ref[...].astype(BF16)) + b_ref[...]


def _adaln_mod(cc, ada_w, ada_b, layer):
    n_out = N_ADA * D_MODEL
    tn = n_out // 4
    out = pl.pallas_call(
        _mod_kernel,
        grid=(n_out // tn,),
        in_specs=[
            pl.BlockSpec((8, D_MODEL), lambda j: (0, 0)),
            pl.BlockSpec((None, D_MODEL, tn), lambda j: (layer, 0, j)),
            pl.BlockSpec((None, 1, tn), lambda j: (layer, 0, j)),
        ],
        out_specs=pl.BlockSpec((8, tn), lambda j: (0, j)),
        out_shape=jax.ShapeDtypeStruct((8, n_out), F32),
        compiler_params=_cparams(("arbitrary",)),
        name="adaln_mod",
    )(cc, ada_w, ada_b.reshape(DEPTH, 1, n_out))
    return out.reshape(8, N_ADA, D_MODEL)


def _ffn_kernel(x_ref, mod_ref, g_ref, w1_ref, w3_ref, w2_ref, o_ref, h_scr, acc_scr, *, sh, sc, gt):
    j = pl.program_id(1)
    last_j = pl.num_programs(1) - 1
    rows = h_scr.shape[0] // FFN_ROW_GROUPS

    def step(first, last):
        w1 = w1_ref[...].astype(BF16)
        w3 = w3_ref[...].astype(BF16)
        w2 = w2_ref[...].astype(BF16)
        for s in range(FFN_ROW_GROUPS):
            sl = slice(s * rows, (s + 1) * rows)
            if first:
                h = _mod_norm(x_ref[sl, :], g_ref[...], mod_ref[0, sh:sh + 1, :], mod_ref[0, sc:sc + 1, :]).astype(BF16)
                h_scr[sl] = h
            else:
                h = h_scr[sl]
            act = (_silu(_dot(h, w1)) * _dot(h, w3)).astype(BF16)
            part = _dot(act, w2)
            if first:
                acc_scr[sl] = part
            elif last:
                o_ref[sl, :] = x_ref[sl, :] + (0.5 * mod_ref[0, gt:gt + 1, :]) * (acc_scr[sl] + part)
            else:
                acc_scr[sl] += part

    pl.when(j == 0)(lambda: step(True, False))
    pl.when(jnp.logical_and(j > 0, j < last_j))(lambda: step(False, False))
    pl.when(j == last_j)(lambda: step(False, True))


def _ffn(x, mod, g, w1, w3, w2, layer, which, grp_fn, sh, sc, gt):
    rows = x.shape[0]
    tm, tf = FFN_TM, FFN_TF
    return pl.pallas_call(
        functools.partial(_ffn_kernel, sh=sh, sc=sc, gt=gt),
        grid=(rows // tm, D_FF // tf),
        in_specs=[
            pl.BlockSpec((tm, D_MODEL), lambda i, j: (i, 0)),
            pl.BlockSpec((1, N_ADA, D_MODEL), lambda i, j: (grp_fn(i, tm), 0, 0)),
            pl.BlockSpec((None, 1, D_MODEL), lambda i, j: (3 * layer + 2 * which, 0, 0)),
            pl.BlockSpec((None, None, D_MODEL, tf), lambda i, j: (layer, which, 0, j)),
            pl.BlockSpec((None, None, D_MODEL, tf), lambda i, j: (layer, which, 0, j)),
            pl.BlockSpec((None, None, tf, D_MODEL), lambda i, j: (layer, which, j, 0)),
        ],
        out_specs=pl.BlockSpec((tm, D_MODEL), lambda i, j: (i, 0)),
        out_shape=jax.ShapeDtypeStruct((rows, D_MODEL), F32),
        scratch_shapes=[pltpu.VMEM((tm, D_MODEL), BF16), pltpu.VMEM((tm, D_MODEL), F32)],
        compiler_params=_cparams(("arbitrary", "arbitrary")),
        name="ffn_half_step",
    )(x, mod, g, w1, w3, w2)


def _head_norm(xs, bsum, g):
    normed = []
    for p in range(xs.shape[1] // LANES):
        x = xs[:, p * LANES:(p + 1) * LANES]
        ms = _dot((x * x).astype(BF16), bsum)
        normed.append(x * lax.rsqrt(ms + EPS))
    return jnp.concatenate(normed, axis=-1) * g


def _rope(xs, cos, sin, first_half):
    swapped = jnp.where(first_half, pltpu.roll(xs, LANES - 16, 1), pltpu.roll(xs, 16, 1))
    return xs * cos + swapped * sin


def _dup_heads(x2, low_half):
    sw = pltpu.roll(x2, HEAD_DIM, 1)
    return jnp.where(low_half, x2, sw), jnp.where(low_half, sw, x2)


def _expand_kv(x2, low_half):
    d0, d1 = _dup_heads(x2, low_half)
    return jnp.concatenate([d0, x2, d1], axis=-1)


PROJ_GROUP_ROWS = SEQ


def _store_group(ref, val, s, rs, cache_layer):
    val = val.astype(ref.dtype)
    if len(ref.shape) == 2:
        ref[rs, :] = val
    elif len(ref.shape) == 3:
        ref[s] = val
    else:
        for l in range(ref.shape[1]):
            ref[s, l] = val if l == cache_layer else jnp.zeros_like(val)


def _proj_kernel(*refs, rope, cache_layer, n_alias, seq_len):
    x_ref, mod_ref, g_ref, w_ref, bsum_ref, gains_ref, xprev_ref, xnext_ref, cw_ref, cb_ref = refs[:10]
    pos = 10
    if rope:
        cos_ref, sin_ref = refs[pos:pos + 2]
        pos += 2
    pos += n_alias
    x0_ref, z_ref, qn_ref, kn_ref, vn_ref, qg_ref, kge_ref, vge_ref = refs[pos:pos + 8]
    pos += 8
    emit_kv = cache_layer is not None
    if emit_kv:
        kg_ref, vg_ref = refs[pos:pos + 2]
        pos += 2
    u_scr, w_scr = refs[pos:pos + 2]
    scale = HEAD_DIM ** -0.5
    tm = x_ref.shape[0]

    @pl.when(pl.program_id(0) == 0)
    def _():
        w_scr[...] = w_ref[...].astype(BF16)

    def projected(x, cols):
        h = _mod_norm(x, g_ref[...], mod_ref[0, SH2:SH2 + 1, :], mod_ref[0, SC2:SC2 + 1, :])
        return _dot(h.astype(BF16), w_scr[:, cols])

    bsum = bsum_ref[...]
    lane = lax.broadcasted_iota(jnp.int32, (1, LANES), 1)
    first_half = (lane % 32) < 16
    low_half = lane < HEAD_DIM
    for s in range(tm // PROJ_GROUP_ROWS):
        rs = slice(s * PROJ_GROUP_ROWS, (s + 1) * PROJ_GROUP_ROWS)
        u_scr[rs, :] = projected(x_ref[rs, :], slice(0, IN_WIDTH))

        qn = _head_norm(u_scr[rs, OFF_NA:OFF_NA + NA_W], bsum, gains_ref[0:1, :])
        qn_ref[rs, :] = (qn * scale).astype(qn_ref.dtype)
        kn = _head_norm(u_scr[rs, OFF_NA + NA_W:OFF_NA + 2 * NA_W], bsum, gains_ref[1:2, :])
        _store_group(kn_ref, kn, s, rs, cache_layer)
        _store_group(vn_ref, u_scr[rs, OFF_NA + 2 * NA_W:OFF_GQ], s, rs, cache_layer)

        qg = _head_norm(u_scr[rs, OFF_GQ:OFF_GQ + GQ_W], bsum, gains_ref[2:3, :])
        kg = _head_norm(u_scr[rs, OFF_GQ + GQ_W:OFF_GQ + GQ_W + GKV_W], bsum, gains_ref[3:4, 0:GKV_W])
        vg = u_scr[rs, OFF_GQ + GQ_W + GKV_W:IN_WIDTH]
        if emit_kv:
            _store_group(kg_ref, kg, s, rs, cache_layer)
            _store_group(vg_ref, vg, s, rs, cache_layer)
        if rope:
            cos = cos_ref[rs, :]
            sin = sin_ref[rs, :]
            for p in range(GQ_W // LANES):
                sl = slice(p * LANES, (p + 1) * LANES)
                qg_ref[rs, sl] = (_rope(qg[:, sl], cos, sin, first_half) * scale).astype(qg_ref.dtype)
            kg = _rope(kg, cos, sin, first_half)
        else:
            qg_ref[rs, :] = (qg * scale).astype(qg_ref.dtype)
        kge_ref[rs, :] = _expand_kv(kg, low_half).astype(kge_ref.dtype)
        vge_ref[rs, :] = _expand_kv(vg, low_half).astype(vge_ref.dtype)

    hy_cols = slice(0, OFF_NA)
    halo_prev = projected(xprev_ref[...], hy_cols)[7:8]
    halo_next = projected(xnext_ref[...], hy_cols)[0:1]
    row = lax.broadcasted_iota(jnp.int32, (tm, 1), 0)
    t_pos = (pl.program_id(0) * tm + row) & (seq_len - 1)

    def conv_chunk(ci):
        sl = slice(ci * HY_CH, (ci + 1) * HY_CH)
        u = u_scr[:, sl]
        prev = jnp.where(row == 0, halo_prev[:, sl], pltpu.roll(u, 1, 0))
        prev = jnp.where(t_pos == 0, 0.0, prev)
        nxt = jnp.where(row == tm - 1, halo_next[:, sl], pltpu.roll(u, tm - 1, 0))
        nxt = jnp.where(t_pos == seq_len - 1, 0.0, nxt)
        return prev * cw_ref[0:1, sl] + u * cw_ref[1:2, sl] + nxt * cw_ref[2:3, sl] + cb_ref[:, sl]

    x0_ref[...] = conv_chunk(0)
    z_ref[...] = conv_chunk(1) * conv_chunk(2)


def _project(x, mod, g, w_in, layer, bsum, gains, conv_w, conv_b, seq_len, grp_fn, rope_tabs, attn_dtype,
             cache_layer=None, prev_cache=None):
    rows = x.shape[0]
    tm = ROW_TM
    rope = rope_tabs is not None
    assert seq_len & (seq_len - 1) == 0 and (seq_len % tm == 0 or tm % seq_len == 0)
    sub = 8
    in_specs = [
        pl.BlockSpec((tm, D_MODEL), lambda i: (i, 0)),
        pl.BlockSpec((1, N_ADA, D_MODEL), lambda i: (grp_fn(i, tm), 0, 0)),
        pl.BlockSpec((None, 1, D_MODEL), lambda i: (3 * layer + 1, 0, 0)),
        pl.BlockSpec((None, D_MODEL, IN_WIDTH), lambda i: (layer, 0, 0), pipeline_mode=pl.Buffered(1)),
        pl.BlockSpec((LANES, LANES), lambda i: (0, 0)),
        pl.BlockSpec((None, 4, NA_W), lambda i: (layer, 0, 0)),
        pl.BlockSpec((sub, D_MODEL), lambda i: (jnp.maximum(i * (tm // sub) - 1, 0), 0)),
        pl.BlockSpec((sub, D_MODEL), lambda i: (jnp.minimum((i + 1) * (tm // sub), rows // sub - 1), 0)),
        pl.BlockSpec((3, OFF_NA), lambda i: (0, 0)),
        pl.BlockSpec((1, OFF_NA), lambda i: (0, 0)),
    ]
    args = [x, mod, g, w_in, bsum, gains, x, x, conv_w, conv_b.reshape(1, OFF_NA)]
    if rope:
        seq_tiles = DEC_SEQ // tm
        in_specs += [pl.BlockSpec((tm, LANES), lambda i: (i % seq_tiles, 0))] * 2
        args += list(rope_tabs)
    outs = [(HY_CH, F32)] * 2 + [(NA_W, attn_dtype)] * 3 + [(GQ_W, attn_dtype)] * 3
    if cache_layer is not None:
        outs += [(GKV_W, F32), (GKV_W, F32)]
    out_specs = [pl.BlockSpec((tm, w), lambda i: (i, 0)) for w, _ in outs]
    out_shape = [jax.ShapeDtypeStruct((rows, w), dt) for w, dt in outs]
    aliases = {}
    if cache_layer is not None:
        cache_outs = (3, 4, 8, 9)
        seqs = tm // SEQ
        for o in cache_outs:
            w = outs[o][0]
            out_shape[o] = jax.ShapeDtypeStruct((BATCH, DEPTH, SEQ, w), F32)
            if prev_cache is None:
                out_specs[o] = pl.BlockSpec((seqs, DEPTH, SEQ, w), lambda i: (i, 0, 0, 0))
            else:
                out_specs[o] = pl.BlockSpec((seqs, None, SEQ, w), lambda i: (i, cache_layer, 0, 0))
        if prev_cache is not None:
            aliases = {len(args) + n: o for n, o in enumerate(cache_outs)}
            in_specs += [pl.BlockSpec(memory_space=pl.ANY)] * len(cache_outs)
            args += list(prev_cache)
    return pl.pallas_call(
        functools.partial(_proj_kernel, rope=rope, cache_layer=cache_layer, n_alias=len(aliases), seq_len=seq_len),
        grid=(rows // tm,),
        in_specs=in_specs,
        out_specs=out_specs,
        out_shape=out_shape,
        input_output_aliases=aliases,
        scratch_shapes=[pltpu.VMEM((tm, IN_WIDTH), F32), pltpu.VMEM((D_MODEL, IN_WIDTH), BF16)],
        compiler_params=_cparams(("arbitrary",)),
        name="mixer_in_proj",
    )(*args)


HEADS_PER_CHAIN = 2
CHAIN_W = HEADS_PER_CHAIN * HEAD_DIM


def _log2(n):
    assert n & (n - 1) == 0
    return n.bit_length() - 1


def _stack_heads(q, n_heads):
    m, w = q.shape
    rows = lax.broadcasted_iota(jnp.int32, (n_heads * m, w), 0)
    lanes = lax.broadcasted_iota(jnp.int32, (n_heads * m, w), 1)
    own = jnp.right_shift(rows, _log2(m)) == jnp.right_shift(lanes, _log2(HEAD_DIM))
    return jnp.where(own, jnp.concatenate([q] * n_heads, axis=0), jnp.zeros((), q.dtype))


def _unstack_heads(o, n_heads):
    m = o.shape[0] // n_heads
    head = jnp.right_shift(lax.broadcasted_iota(jnp.int32, (m, o.shape[1]), 1), _log2(HEAD_DIM))
    out = o[0:m]
    for h in range(1, n_heads):
        out = jnp.where(head == h, o[h * m:(h + 1) * m], out)
    return out


def _sink_column(sink_ref, m, first_head, n_heads):
    block = jnp.right_shift(lax.broadcasted_iota(jnp.int32, (n_heads * m, 1), 0), _log2(m))
    col = jnp.full((n_heads * m, 1), sink_ref[first_head], F32)
    for h in range(1, n_heads):
        col = jnp.where(block == h, sink_ref[first_head + h], col)
    return col


def _softmax_pv(scores, values, sink):
    m = scores[0].max(axis=-1, keepdims=True)
    for s in scores[1:]:
        m = jnp.maximum(m, s.max(axis=-1, keepdims=True))
    if sink is not None:
        m = jnp.maximum(m, sink)
    den = None
    acc = None
    for s, v in zip(scores, values):
        p = jnp.exp(s - m)
        d = p.sum(axis=-1, keepdims=True)
        o = _dot(p.astype(BF16), v)
        den = d if den is None else den + d
        acc = o if acc is None else acc + o
    if sink is not None:
        den = den + jnp.exp(sink - m)
    return acc / den


CTX_SEQS_PER_STEP = 4


def _ctx_attn_kernel(sink_ref, qn_ref, kn_ref, vn_ref, qg_ref, kge_ref, vge_ref, yna_ref, ygq_ref):
    m = qn_ref.shape[1]
    low = lax.broadcasted_iota(jnp.int32, (m, LANES), 1) < HEAD_DIM
    for q_ref, k_ref, v_ref, y_ref, has_sink in ((qn_ref, kn_ref, vn_ref, yna_ref, False),
                                                  (qg_ref, kge_ref, vge_ref, ygq_ref, True)):
        for bi in range(CTX_SEQS_PER_STEP):
            for p in range(NA_W // LANES):
                sl = slice(p * LANES, (p + 1) * LANES)
                q2 = q_ref[bi, :, sl].astype(BF16)
                k2 = k_ref[bi, :, sl].astype(BF16)
                v2 = v_ref[bi, :, sl].astype(BF16)
                halves = []
                for half, keep in enumerate((low, ~low)):
                    qh = jnp.where(keep, q2, jnp.zeros((), BF16))
                    sink = sink_ref[2 * p + half] if has_sink else None
                    halves.append(_softmax_pv([_dot_nt(qh, k2)], [v2], sink))
                y_ref[bi, :, sl] = jnp.where(low, halves[0], halves[1]).astype(y_ref.dtype)


def _ctx_attention(sink, qn, kn_cache, vn_cache, qg, kge, vge, layer):
    b, l, _ = qn.shape

    nb = CTX_SEQS_PER_STEP

    def spec(w):
        return pl.BlockSpec((nb, l, w), lambda i: (i, 0, 0))

    cache_spec = pl.BlockSpec((nb, None, l, NA_W), lambda i: (i, layer, 0, 0))
    return pl.pallas_call(
        _ctx_attn_kernel,
        grid=(b // nb,),
        in_specs=[pl.BlockSpec(memory_space=pltpu.SMEM), spec(NA_W), cache_spec, cache_spec, spec(GQ_W),
                  spec(GQ_W), spec(GQ_W)],
        out_specs=[spec(NA_W), spec(GQ_W)],
        out_shape=[jax.ShapeDtypeStruct((b, l, NA_W), BF16), jax.ShapeDtypeStruct((b, l, GQ_W), BF16)],
        compiler_params=_cparams(("arbitrary",)),
        name="context_attention",
    )(sink, qn, kn_cache, vn_cache, qg, kge, vge)


NA_KEYS = NA_ROWS * GRID_W
NA_ROWS_PER_STEP = 16


def _na_bias_kernel(r_ref, oh_ref, mk_ref, o_ref):
    o_ref[...] = jnp.dot(r_ref[...], oh_ref[...], precision=_HIGHEST, preferred_element_type=F32) + mk_ref[...]


def _na_bias_tables():
    q = np.arange(GRID_W)[:, None]
    kc = np.arange(GRID_W)[None, :]
    win_lo = np.clip(q - NA_COLS // 2, 0, GRID_W - NA_COLS)
    ok = (kc >= win_lo) & (kc < win_lo + NA_COLS)
    j = kc - q + NA_COLS - 1
    onehot = np.zeros((LANES, GRID_W * GRID_W), np.float32)
    qq, kk = np.nonzero(ok)
    onehot[j[qq, kk], qq * GRID_W + kk] = 1.0
    mask = np.where(ok, 0.0, NEG_INF).astype(np.float32).reshape(1, GRID_W * GRID_W)
    return onehot, mask


def _na_bias(rpb_l):
    n_dr = 2 * NA_ROWS - 1
    rows = NA_HEADS * n_dr
    rows_pad = -(-rows // 8) * 8
    r = jnp.zeros((rows_pad, LANES), F32).at[:rows, :2 * NA_COLS - 1].set(rpb_l.reshape(rows, 2 * NA_COLS - 1))
    onehot, mask = _na_bias_tables()
    out = pl.pallas_call(
        _na_bias_kernel,
        out_shape=jax.ShapeDtypeStruct((rows_pad, GRID_W * GRID_W), F32),
        compiler_params=pltpu.CompilerParams(vmem_limit_bytes=VMEM_LIMIT),
        name="na_bias_expand",
    )(r, jnp.asarray(onehot), jnp.asarray(mask))
    bm = out[:rows].reshape(NA_HEADS, n_dr, GRID_W, GRID_W)
    bm2 = jnp.concatenate([bm[:, :-1], bm[:, 1:]], axis=-1)
    return bm2.transpose(1, 0, 2, 3).reshape(n_dr - 1, NA_HEADS * GRID_W, 2 * GRID_W)


def _na_attn_kernel(q_ref, k_ref, v_ref, kc_ref, vc_ref, bm_ref, o_ref):
    n_rows = DEC_SEQ // GRID_W
    kc = kc_ref[0, 0].astype(BF16)
    vc = vc_ref[0, 0].astype(BF16)
    for rr in range(NA_ROWS_PER_STEP):
        r = pl.program_id(1) * NA_ROWS_PER_STEP + rr
        start = jnp.clip(r - NA_ROWS // 2, 0, n_rows - NA_ROWS)
        shift = r - start
        row0 = pl.multiple_of(start * GRID_W, GRID_W)
        rows = slice(rr * GRID_W, (rr + 1) * GRID_W)
        q = _stack_heads(q_ref[0, rows, :], NA_HEADS)
        k = k_ref[0, pl.ds(row0, NA_KEYS), :]
        v = v_ref[0, pl.ds(row0, NA_KEYS), :]
        bias = jnp.concatenate([bm_ref[2 * jj - shift + NA_ROWS - 1] for jj in range(NA_ROWS // 2)], axis=-1)
        s_loc = _dot_nt(q, k) + bias
        s_ctx = _dot_nt(q, kc)
        o = _softmax_pv([s_loc, s_ctx], [v, vc], None)
        o_ref[0, rows, :] = _unstack_heads(o, NA_HEADS).astype(o_ref.dtype)


WIN_KEYS = 3 * BLK
WIN_BLOCKS_PER_STEP = 8


def _win_attn_kernel(sink_ref, mask_ref, q_ref, kge_ref, vge_ref, kc_ref, vc_ref, o_ref):
    n = kge_ref.shape[1]
    n_blocks = n // BLK
    low_c = lax.broadcasted_iota(jnp.int32, (1, LANES), 1) < HEAD_DIM
    kce = _expand_kv(kc_ref[0, 0], low_c).astype(BF16)
    vce = _expand_kv(vc_ref[0, 0], low_c).astype(BF16)
    n_chains = GQA_Q_HEADS // HEADS_PER_CHAIN
    sinks = [_sink_column(sink_ref, BLK, p * HEADS_PER_CHAIN, HEADS_PER_CHAIN) for p in range(n_chains)]
    for bb in range(WIN_BLOCKS_PER_STEP):
        nb = pl.program_id(1) * WIN_BLOCKS_PER_STEP + bb
        start = pl.multiple_of(jnp.clip((nb - 1) * BLK, 0, n - WIN_KEYS), BLK)
        variant = jnp.where(nb == 0, 0, jnp.where(nb == n_blocks - 1, 2, 1))
        rows = slice(bb * BLK, (bb + 1) * BLK)
        for p in range(n_chains):
            sl = slice(p * CHAIN_W, (p + 1) * CHAIN_W)
            q = _stack_heads(q_ref[0, rows, sl], HEADS_PER_CHAIN)
            s_loc = _dot_nt(q, kge_ref[0, pl.ds(start, WIN_KEYS), sl]) + mask_ref[variant]
            s_ctx = _dot_nt(q, kce[:, sl])
            o = _softmax_pv([s_loc, s_ctx], [vge_ref[0, pl.ds(start, WIN_KEYS), sl], vce[:, sl]], sinks[p])
            o_ref[0, rows, sl] = _unstack_heads(o, HEADS_PER_CHAIN).astype(o_ref.dtype)


def _win_mask_table():
    qi = np.arange(BLK)[:, None]
    kj = np.arange(WIN_KEYS)[None, :]
    tabs = [np.where(np.abs(qi + off - kj) <= GQA_WINDOW, 0.0, NEG_INF) for off in (0, BLK, 2 * BLK)]
    return jnp.asarray(np.stack([np.tile(t, (HEADS_PER_CHAIN, 1)) for t in tabs]).astype(np.float32))


def _latent_attn_kernel(sink_ref, mask_ref, bm_ref, qn_ref, kn_ref, vn_ref, kcn_ref, vcn_ref,
                        qg_ref, kge_ref, vge_ref, kcg_ref, vcg_ref, yna_ref, ygq_ref):
    _na_attn_kernel(qn_ref, kn_ref, vn_ref, kcn_ref, vcn_ref, bm_ref, yna_ref)
    _win_attn_kernel(sink_ref, mask_ref, qg_ref, kge_ref, vge_ref, kcg_ref, vcg_ref, ygq_ref)


def _latent_attention(sink, bm2, qn, kn, vn, kc_na, vc_na, qg, kge, vge, kc_gq, vc_gq, layer):
    b, n, _ = qn.shape
    mask = _win_mask_table()
    tq = WIN_BLOCKS_PER_STEP * BLK
    assert tq == NA_ROWS_PER_STEP * GRID_W

    def q_spec(w):
        return pl.BlockSpec((1, tq, w), lambda i, j: (i, j, 0))

    def seq_spec(w):
        return pl.BlockSpec((1, n, w), lambda i, j: (i, 0, 0))

    def cache_spec(w):
        return pl.BlockSpec((1, 1, PAST_LEN, w), lambda i, j: (i, layer, 0, 0))

    return pl.pallas_call(
        _latent_attn_kernel,
        grid=(b, n // tq),
        in_specs=[
            pl.BlockSpec(memory_space=pltpu.SMEM),
            pl.BlockSpec(mask.shape, lambda i, j: (0, 0, 0)),
            pl.BlockSpec(bm2.shape, lambda i, j: (0, 0, 0)),
            q_spec(NA_W), seq_spec(NA_W), seq_spec(NA_W), cache_spec(NA_W), cache_spec(NA_W),
            q_spec(GQ_W), seq_spec(GQ_W), seq_spec(GQ_W), cache_spec(GKV_W), cache_spec(GKV_W),
        ],
        out_specs=[q_spec(NA_W), q_spec(GQ_W)],
        out_shape=[jax.ShapeDtypeStruct((b, n, NA_W), BF16), jax.ShapeDtypeStruct((b, n, GQ_W), BF16)],
        compiler_params=_cparams(("arbitrary", "arbitrary")),
        name="latent_attention",
    )(sink, mask, bm2, qn, kn, vn, kc_na, vc_na, qg, kge, vge, kc_gq, vc_gq)


@functools.lru_cache(maxsize=None)
def _hyena_consts_np(L):
    n = 2 * L
    k = np.arange(L, dtype=np.int64)
    ang = (2.0 * np.pi / n) * ((k[:, None] * k[None, :]) % n).astype(np.float64)

    tkf = min(HY_TKF, L)
    cm = np.cos(ang).astype(np.float32).reshape(L // tkf, tkf, L)
    sm = (-np.sin(ang)).astype(np.float32).reshape(L // tkf, tkf, L)
    f_rows = np.concatenate([cm, sm], axis=1)
    f_cols = np.ascontiguousarray(f_rows.transpose(0, 2, 1))
    idx = np.arange(L, dtype=np.float32)
    t = idx / np.float32(L - 1)
    bands = np.linspace(1e-4, HY_BANDS - 1, HY_BANDS, dtype=np.float32)
    fang = np.float32(2.0 * math.pi / L) * idx[:, None] * bands[None, :]
    feats = np.zeros((L, LANES), np.float32)
    feats[:, 0] = t
    feats[:, 1:1 + HY_BANDS] = np.cos(fang)
    feats[:, 1 + HY_BANDS:HY_EMB] = -np.sin(fang)
    max_decay = math.log(HY_DECAY_TARGET) / HY_FAST_PCT
    min_decay = math.log(HY_DECAY_TARGET) / HY_SLOW_PCT
    deltas = np.abs(np.linspace(min_decay, max_decay, HY_CH, dtype=np.float32))
    decay = np.exp(-t[:, None] * deltas[None, :]).astype(np.float32)
    decay2 = np.concatenate([decay, decay], axis=1)
    half = L // 2
    feats = np.concatenate([feats[:half], feats[half:]], axis=1)
    decay2 = np.concatenate([decay2[:half], decay2[half:]], axis=1)
    return f_rows, f_cols, feats, decay2


def _hyena_consts(L):
    return tuple(jnp.asarray(a) for a in _hyena_consts_np(L))


def _hy_filter_kernel(feats_ref, w1_ref, b1_ref, w2_ref, b2_ref, w3_ref, freq_ref, decay_ref, wsum_ref, wdiff_ref):
    def dot_hi(a, b):
        a_hi, a_lo = _split(a)
        b_hi, b_lo = _split(b)
        return _dot(a_hi, b_hi) + (_dot(a_lo, b_hi) + _dot(a_hi, b_lo))

    hid = jnp.sin(freq_ref[0:1, :] * (dot_hi(feats_ref[...], w1_ref[...]) + b1_ref[...]))
    hid = jnp.sin(freq_ref[1:2, :] * (dot_hi(hid, w2_ref[...]) + b2_ref[...]))
    taps2 = dot_hi(hid, w3_ref[...]) * decay_ref[...]
    tl = feats_ref.shape[0]
    row = pl.program_id(0) * tl + lax.broadcasted_iota(jnp.int32, (tl, HY_CH), 0)
    for part in range(2):
        fwd = taps2[:, part * 2 * HY_CH:part * 2 * HY_CH + HY_CH]
        bwd = taps2[:, part * 2 * HY_CH + HY_CH:(part + 1) * 2 * HY_CH]
        if part == 0:
            bwd = jnp.where(row == 0, 0.0, bwd)
        lanes = slice(part * HY_CH, (part + 1) * HY_CH)
        wsum_ref[:, lanes] = fwd + bwd
        wdiff_ref[:, lanes] = fwd - bwd


def _block_diag2(w):
    d, r, c = w.shape
    return jnp.zeros((d, 2 * r, 2 * c), w.dtype).at[:, :r, :c].set(w).at[:, r:, c:].set(w)


def _hy_filter_params(w1, b1, w2, b2, w3, freq):
    w1p = jnp.pad(w1, ((0, 0), (0, LANES - HY_EMB), (0, 0)))
    return (_block_diag2(w1p), jnp.tile(b1, (1, 2))[:, None, :], _block_diag2(w2), jnp.tile(b2, (1, 2))[:, None, :],
            _block_diag2(w3), jnp.tile(freq, (1, 1, 2)))


def _hy_filter(L, feats, decay2, params, layer):
    half = L // 2
    tl = min(HY_FILT_TL, half)

    def rows_of(a):
        return pl.BlockSpec((tl, a.shape[1]), lambda i: (i, 0))

    def of_layer(a):
        return pl.BlockSpec((None,) + a.shape[1:], lambda i: (layer,) + (0,) * (a.ndim - 1))

    out_spec = pl.BlockSpec((tl, 2 * HY_CH), lambda i: (i, 0))
    return pl.pallas_call(
        _hy_filter_kernel,
        grid=(half // tl,),
        in_specs=[rows_of(feats)] + [of_layer(p) for p in params] + [rows_of(decay2)],
        out_specs=[out_spec, out_spec],
        out_shape=[jax.ShapeDtypeStruct((half, 2 * HY_CH), F32)] * 2,
        compiler_params=_cparams(("arbitrary",)),
        name="hyena_filter",
    )(feats, *params, decay2)


def _hy_conv_kernel(z_ref, x0_ref, wsum_ref, wdiff_ref, skip_ref, fr_ref, fc_ref,
                    o_ref, z_scr, ws_scr, wd_scr, acc_scr, *, group):
    t = pl.program_id(1)
    L = z_ref.shape[1]
    n = 2 * L
    half = L // 2

    @pl.when(t == 0)
    def _():
        z_scr[...] = z_ref[...].astype(BF16)
        for part in range(2):
            rows = slice(part * half, (part + 1) * half)
            lanes = slice(part * HY_CH, (part + 1) * HY_CH)
            ws_scr[rows, :] = wsum_ref[:, lanes].astype(BF16)
            wd_scr[rows, :] = wdiff_ref[:, lanes].astype(BF16)
        acc_scr[...] = jnp.zeros_like(acc_scr)

    fr = fr_ref[...].astype(BF16)
    fc = fc_ref[...].astype(BF16)
    tkf = fr.shape[0] // 2
    k_re = _dot(fr[0:tkf], ws_scr[...])
    k_im = _dot(fr[tkf:2 * tkf], wd_scr[...])
    for b in range(group):
        zf = _dot(fr, z_scr[b])
        z_re, z_im = zf[0:tkf], zf[tkf:2 * tkf]
        y = jnp.concatenate([z_re * k_re - z_im * k_im, z_re * k_im + z_im * k_re], axis=0).astype(BF16)
        acc_scr[b] += _dot(fc, y)

    @pl.when(t == pl.num_programs(1) - 1)
    def _():
        row = lax.broadcasted_iota(jnp.int32, (L, 1), 0)
        sgn = (1 - 2 * (row % 2)).astype(F32)
        w_a, w_b = wsum_ref[:, 0:HY_CH], wsum_ref[:, HY_CH:2 * HY_CH]
        sgn_h = sgn[0:half]
        w_dc = w_a.sum(axis=0, keepdims=True) + w_b.sum(axis=0, keepdims=True)
        w_ny = (w_a * sgn_h).sum(axis=0, keepdims=True) + (w_b * sgn_h).sum(axis=0, keepdims=True)
        for b in range(group):
            z = z_ref[b]
            z_dc = z.sum(axis=0, keepdims=True)
            z_ny = (z * sgn).sum(axis=0, keepdims=True)
            conv = (2.0 / n) * acc_scr[b] - (1.0 / n) * (z_dc * w_dc) + (1.0 / n) * (sgn * (z_ny * w_ny))
            o_ref[b] = (x0_ref[b] * (conv + z * skip_ref[...])).astype(o_ref.dtype)


def _hy_conv(z, x0, wsum, wdiff, skip, f_rows, f_cols, group):
    b, L, c = z.shape
    n_tiles, two_tkf, _ = f_rows.shape
    row_spec = pl.BlockSpec((None, two_tkf, L), lambda g, t: (t, 0, 0))
    col_spec = pl.BlockSpec((None, L, two_tkf), lambda g, t: (t, 0, 0))
    once = pl.Buffered(1)
    seq_spec = pl.BlockSpec((group, L, c), lambda g, t: (g, 0, 0), pipeline_mode=once)
    w_spec = pl.BlockSpec((L // 2, 2 * c), lambda g, t: (0, 0), pipeline_mode=once)
    return pl.pallas_call(
        functools.partial(_hy_conv_kernel, group=group),
        grid=(b // group, n_tiles),
        in_specs=[seq_spec, seq_spec, w_spec, w_spec, pl.BlockSpec((1, c), lambda g, t: (0, 0)),
                  row_spec, col_spec],
        out_specs=pl.BlockSpec((group, L, c), lambda g, t: (g, 0, 0)),
        out_shape=jax.ShapeDtypeStruct((b, L, c), BF16),
        scratch_shapes=[pltpu.VMEM((group, L, c), BF16), pltpu.VMEM((L, c), BF16), pltpu.VMEM((L, c), BF16),
                        pltpu.VMEM((group, L, c), F32)],
        compiler_params=_cparams(("arbitrary", "arbitrary")),
        name="hyena_long_conv",
    )(z, x0, wsum, wdiff, skip.reshape(1, c), f_rows, f_cols)


def _merge_kernel(x_ref, yhy_ref, yna_ref, ygq_ref, mod_ref, w_ref, o_ref, w_scr):
    @pl.when(pl.program_id(0) == 0)
    def _():
        w_scr[...] = w_ref[...].astype(BF16)

    y = jnp.concatenate([yhy_ref[...], yna_ref[...], ygq_ref[...]], axis=-1)
    o_ref[...] = x_ref[...] + mod_ref[0, G2:G2 + 1, :] * _dot(y, w_scr[...])


def _merge(x, y_hy, y_na, y_gq, mod, w_out, layer, grp_fn):
    rows = x.shape[0]
    tm = MERGE_TM
    return pl.pallas_call(
        _merge_kernel,
        grid=(rows // tm,),
        in_specs=[
            pl.BlockSpec((tm, D_MODEL), lambda i: (i, 0)),
            pl.BlockSpec((tm, HY_CH), lambda i: (i, 0)),
            pl.BlockSpec((tm, NA_W), lambda i: (i, 0)),
            pl.BlockSpec((tm, GQ_W), lambda i: (i, 0)),
            pl.BlockSpec((1, N_ADA, D_MODEL), lambda i: (grp_fn(i, tm), 0, 0)),
            pl.BlockSpec((None, D_MODEL, D_MODEL), lambda i: (layer, 0, 0), pipeline_mode=pl.Buffered(1)),
        ],
        out_specs=pl.BlockSpec((tm, D_MODEL), lambda i: (i, 0)),
        out_shape=jax.ShapeDtypeStruct((rows, D_MODEL), F32),
        scratch_shapes=[pltpu.VMEM((D_MODEL, D_MODEL), BF16)],
        compiler_params=_cparams(("arbitrary",)),
        name="mixer_out_proj",
    )(x, y_hy, y_na, y_gq, mod, w_out)


def _rope_tables():
    pos = np.arange(DEC_SEQ)
    quarter = HEAD_DIM // 4
    inv = ROPE_BASE ** (-np.arange(quarter, dtype=np.float64) / quarter)
    ang_r = (pos // GRID_W)[:, None] * inv[None, :]
    ang_c = (pos % GRID_W)[:, None] * inv[None, :]
    cos_h = np.concatenate([np.cos(ang_r), np.cos(ang_r), np.cos(ang_c), np.cos(ang_c)], axis=1)
    sin_h = np.concatenate([-np.sin(ang_r), np.sin(ang_r), -np.sin(ang_c), np.sin(ang_c)], axis=1)
    reps = LANES // HEAD_DIM
    return (jnp.asarray(np.tile(cos_h, (1, reps)), F32), jnp.asarray(np.tile(sin_h, (1, reps)), F32))


def _head_sum_matrix():
    idx = np.arange(LANES) // HEAD_DIM
    return jnp.asarray((idx[:, None] == idx[None, :]).astype(np.float32) / HEAD_DIM, BF16)


def _grp_prompt(i, tm):
    return 0


def _grp_sample(i, tm):
    return 1 + (i * tm) // DEC_SEQ


def kernel(x_prompt, x_sample, cache_na_k, cache_na_v, cache_gqa_k, cache_gqa_v, c, c_ctx, ada_w, ada_b, norm_g, ffn_w1, ffn_w3, ffn_w2, w_in, w_out, hy_conv_w, hy_conv_b, hy_filt_w1, hy_filt_b1, hy_filt_w2, hy_filt_b2, hy_filt_w3, hy_freq, hy_skip, na_q_g, na_k_g, na_rpb, gqa_q_g, gqa_k_g, gqa_sink):
    xp = x_prompt.reshape(BATCH * SEQ, D_MODEL)
    xs = x_sample.reshape(DEC_BATCH * DEC_SEQ, D_MODEL)
    cc = jnp.zeros((8, D_MODEL), F32).at[0].set(c_ctx).at[1:1 + DEC_BATCH].set(c)
    kc_na = cache_na_k.reshape(DEC_BATCH, DEPTH, PAST_LEN, NA_W)
    vc_na = cache_na_v.reshape(DEC_BATCH, DEPTH, PAST_LEN, NA_W)
    kc_gq = cache_gqa_k.reshape(DEC_BATCH, DEPTH, PAST_LEN, GKV_W)
    vc_gq = cache_gqa_v.reshape(DEC_BATCH, DEPTH, PAST_LEN, GKV_W)

    rope_tabs = _rope_tables()
    bsum = _head_sum_matrix()
    hy_p = _hyena_consts(SEQ)
    hy_s = _hyena_consts(DEC_SEQ)

    gains = jnp.tile(jnp.stack([na_q_g, na_k_g, gqa_q_g, gqa_k_g], axis=1), (1, 1, NA_HEADS))
    norm_rows = norm_g.reshape(DEPTH * 3, 1, D_MODEL)
    ng = [norm_rows] * 3
    filt_params = _hy_filter_params(hy_filt_w1, hy_filt_b1, hy_filt_w2, hy_filt_b2, hy_filt_w3, hy_freq)

    cache = None
    for l in range(DEPTH):
        mod = _adaln_mod(cc, ada_w, ada_b, l)
        bm2 = _na_bias(na_rpb[l])
        filt = {}
        for L, consts in ((SEQ, hy_p), (DEC_SEQ, hy_s)):
            filt[L] = _hy_filter(L, consts[2], consts[3], filt_params, l)

        xp = _ffn(xp, mod, ng[0], ffn_w1, ffn_w3, ffn_w2, l, 0, _grp_prompt, SH1, SC1, G1)
        x0, z, qn, kn, vn, qg, kgd, vgd, kg, vg = _project(
            xp, mod, ng[1], w_in, l, bsum, gains, hy_conv_w[l], hy_conv_b[l], SEQ, _grp_prompt, None, F32,
            cache_layer=l, prev_cache=cache)
        cache = (kn, vn, kg, vg)
        shp = lambda a: a.reshape(BATCH, SEQ, a.shape[-1])
        y_hy = _hy_conv(shp(z), shp(x0), filt[SEQ][0], filt[SEQ][1], hy_skip[l], hy_p[0], hy_p[1], group=BATCH)
        y_na, y_gq = _ctx_attention(gqa_sink[l], shp(qn), kn, vn, shp(qg), shp(kgd), shp(vgd), l)
        xp = _merge(xp, y_hy.reshape(-1, HY_CH), y_na.reshape(-1, NA_W), y_gq.reshape(-1, GQ_W), mod, w_out, l,
                    _grp_prompt)
        xp = _ffn(xp, mod, ng[2], ffn_w1, ffn_w3, ffn_w2, l, 1, _grp_prompt, SH3, SC3, G3)

        xs = _ffn(xs, mod, ng[0], ffn_w1, ffn_w3, ffn_w2, l, 0, _grp_sample, SH1, SC1, G1)
        x0, z, qn, kn, vn, qg, kgd, vgd = _project(
            xs, mod, ng[1], w_in, l, bsum, gains, hy_conv_w[l], hy_conv_b[l], DEC_SEQ, _grp_sample, rope_tabs, BF16)
        shs = lambda a: a.reshape(DEC_BATCH, DEC_SEQ, a.shape[-1])
        y_hy = _hy_conv(shs(z), shs(x0), filt[DEC_SEQ][0], filt[DEC_SEQ][1], hy_skip[l], hy_s[0], hy_s[1],
                        group=DEC_BATCH)
        y_na, y_gq = _latent_attention(gqa_sink[l], bm2, shs(qn), shs(kn), shs(vn), kc_na, vc_na,
                                       shs(qg), shs(kgd), shs(vgd), kc_gq, vc_gq, l)
        xs = _merge(xs, y_hy.reshape(-1, HY_CH), y_na.reshape(-1, NA_W), y_gq.reshape(-1, GQ_W), mod, w_out, l,
                    _grp_sample)
        xs = _ffn(xs, mod, ng[2], ffn_w1, ffn_w3, ffn_w2, l, 1, _grp_sample, SH3, SC3, G3)

    kn, vn, kg, vg = cache
    return (xp.reshape(BATCH, SEQ, D_MODEL), xs.reshape(DEC_BATCH, DEC_SEQ, D_MODEL),
            kn.reshape(BATCH, DEPTH, SEQ, NA_HEADS, HEAD_DIM), vn.reshape(BATCH, DEPTH, SEQ, NA_HEADS, HEAD_DIM),
            kg.reshape(BATCH, DEPTH, SEQ, GQA_KV_HEADS, HEAD_DIM), vg.reshape(BATCH, DEPTH, SEQ, GQA_KV_HEADS, HEAD_DIM))
```

```python
import functools
import math

import numpy as np
import jax
import jax.numpy as jnp
from jax import lax
from jax.experimental import pallas as pl
from jax.experimental.pallas import tpu as pltpu

F32 = jnp.float32
BF16 = jnp.bfloat16

D_MODEL = 1024
BATCH = 16
SEQ = 256
DEPTH = 2
DEC_BATCH = 2
DEC_SEQ = 2048
PAST_LEN = 256
GRID_W = 64
HEAD_DIM = 64
HY_CH = 256
NA_HEADS = 6
GQA_Q_HEADS = 6
GQA_KV_HEADS = 2
NA_W = NA_HEADS * HEAD_DIM
GQ_W = GQA_Q_HEADS * HEAD_DIM
GKV_W = GQA_KV_HEADS * HEAD_DIM
IN_WIDTH = 3 * HY_CH + 3 * NA_W + GQ_W + 2 * GKV_W
D_FF = 2816
N_ADA = 9
HY_BANDS = 16
HY_EMB = 1 + 2 * HY_BANDS
HY_FILT_W = 64
HY_DECAY_TARGET = 1e-2
HY_FAST_PCT = 0.3
HY_SLOW_PCT = 1.5
NA_ROWS = 8
NA_COLS = 16
GQA_WINDOW = 128
BLK = 128
ROPE_BASE = 10000.0
EPS = 1e-6
NEG_INF = -1e30

LANES = 128
V7X_VMEM_BYTES = 64 * 1024 * 1024
VMEM_LIMIT = V7X_VMEM_BYTES * 15 // 16

OFF_NA = 3 * HY_CH
OFF_GQ = OFF_NA + 3 * NA_W

SH1, SC1, G1, SH2, SC2, G2, SH3, SC3, G3 = range(N_ADA)

FFN_TM = 2048
FFN_TF = 256
FFN_ROW_GROUPS = 4
ROW_TM = 512
MERGE_TM = 1024
HY_TKF = 256
HY_FILT_TL = 256

_HIGHEST = lax.Precision.HIGHEST


def _cparams(sem):
    return pltpu.CompilerParams(dimension_semantics=sem, vmem_limit_bytes=VMEM_LIMIT)


def _silu(x):
    return x * (1.0 / (1.0 + jnp.exp(-x)))


def _dot(a, b):
    return jnp.dot(a, b, preferred_element_type=F32)


def _dot_nt(a, b):
    return lax.dot_general(a, b, (((1,), (1,)), ((), ())), preferred_element_type=F32)


def _split(x):
    hi = x.astype(BF16)
    lo = (x - hi.astype(F32)).astype(BF16)
    return hi, lo


def _mod_norm(x, g, shift, scale):
    ms = jnp.mean(x * x, axis=-1, keepdims=True)
    return (x * lax.rsqrt(ms + EPS) * g) * (1.0 + scale) + shift


def _mod_kernel(c_ref, w_ref, b_ref, o_ref):
    a = _silu(c_ref[...]).astype(BF16)
    o_ref[...] = _dot(a, w_ref[...].astype(BF16)) + b_ref[...]


def _adaln_mod(cc, ada_w, ada_b):
    n_out = N_ADA * D_MODEL
    tn = n_out // 4
    out = pl.pallas_call(
        _mod_kernel,
        grid=(DEPTH, n_out // tn),
        in_specs=[
            pl.BlockSpec((8, D_MODEL), lambda l, j: (0, 0)),
            pl.BlockSpec((None, D_MODEL, tn), lambda l, j: (l, 0, j)),
            pl.BlockSpec((None, 1, tn), lambda l, j: (l, 0, j)),
        ],
        out_specs=pl.BlockSpec((None, 8, tn), lambda l, j: (l, 0, j)),
        out_shape=jax.ShapeDtypeStruct((DEPTH, 8, n_out), F32),
        compiler_params=_cparams(("arbitrary", "arbitrary")),
        name="adaln_mod",
    )(cc, ada_w, ada_b.reshape(DEPTH, 1, n_out))
    return out.reshape(DEPTH * 8, N_ADA, D_MODEL)


def _ffn_kernel(x_ref, mod_ref, g_ref, w1_ref, w3_ref, w2_ref, o_ref, h_scr, acc_scr, *, sh, sc, gt):
    j = pl.program_id(1)
    last_j = pl.num_programs(1) - 1
    rows = h_scr.shape[0] // FFN_ROW_GROUPS

    def step(first, last):
        w1 = w1_ref[...].astype(BF16)
        w3 = w3_ref[...].astype(BF16)
        w2 = w2_ref[...].astype(BF16)
        for s in range(FFN_ROW_GROUPS):
            sl = slice(s * rows, (s + 1) * rows)
            if first:
                h = _mod_norm(x_ref[sl, :], g_ref[...], mod_ref[0, sh:sh + 1, :], mod_ref[0, sc:sc + 1, :]).astype(BF16)
                h_scr[sl] = h
            else:
                h = h_scr[sl]
            act = (_silu(_dot(h, w1)) * _dot(h, w3)).astype(BF16)
            part = _dot(act, w2)
            if first:
                acc_scr[sl] = part
            elif last:
                o_ref[sl, :] = x_ref[sl, :] + (0.5 * mod_ref[0, gt:gt + 1, :]) * (acc_scr[sl] + part)
            else:
                acc_scr[sl] += part

    pl.when(j == 0)(lambda: step(True, False))
    pl.when(jnp.logical_and(j > 0, j < last_j))(lambda: step(False, False))
    pl.when(j == last_j)(lambda: step(False, True))


def _ffn(x, mod, g, w1, w3, w2, layer, which, grp_fn, sh, sc, gt):
    rows = x.shape[0]
    tm, tf = FFN_TM, FFN_TF
    return pl.pallas_call(
        functools.partial(_ffn_kernel, sh=sh, sc=sc, gt=gt),
        grid=(rows // tm, D_FF // tf),
        in_specs=[
            pl.BlockSpec((tm, D_MODEL), lambda i, j: (i, 0)),
            pl.BlockSpec((1, N_ADA, D_MODEL), lambda i, j: (grp_fn(i, tm), 0, 0)),
            pl.BlockSpec((None, 1, D_MODEL), lambda i, j: (3 * layer + 2 * which, 0, 0)),
            pl.BlockSpec((None, None, D_MODEL, tf), lambda i, j: (layer, which, 0, j)),
            pl.BlockSpec((None, None, D_MODEL, tf), lambda i, j: (layer, which, 0, j)),
            pl.BlockSpec((None, None, tf, D_MODEL), lambda i, j: (layer, which, j, 0)),
        ],
        out_specs=pl.BlockSpec((tm, D_MODEL), lambda i, j: (i, 0)),
        out_shape=jax.ShapeDtypeStruct((rows, D_MODEL), F32),
        scratch_shapes=[pltpu.VMEM((tm, D_MODEL), BF16), pltpu.VMEM((tm, D_MODEL), F32)],
        compiler_params=_cparams(("arbitrary", "arbitrary")),
        name="ffn_half_step",
    )(x, mod, g, w1, w3, w2)


def _head_norm(xs, bsum, g):
    normed = []
    for p in range(xs.shape[1] // LANES):
        x = xs[:, p * LANES:(p + 1) * LANES]
        ms = _dot((x * x).astype(BF16), bsum)
        normed.append(x * lax.rsqrt(ms + EPS))
    return jnp.concatenate(normed, axis=-1) * g


def _rope(xs, cos, sin, first_half):
    swapped = jnp.where(first_half, pltpu.roll(xs, LANES - 16, 1), pltpu.roll(xs, 16, 1))
    return xs * cos + swapped * sin


def _dup_heads(x2, low_half):
    sw = pltpu.roll(x2, HEAD_DIM, 1)
    return jnp.where(low_half, x2, sw), jnp.where(low_half, sw, x2)


def _expand_kv(x2, low_half):
    d0, d1 = _dup_heads(x2, low_half)
    return jnp.concatenate([d0, x2, d1], axis=-1)


PROJ_GROUP_ROWS = SEQ


def _store_group(ref, val, s, rs, cache_layer):
    val = val.astype(ref.dtype)
    if len(ref.shape) == 2:
        ref[rs, :] = val
    elif len(ref.shape) == 3:
        ref[s] = val
    else:
        for l in range(ref.shape[1]):
            ref[s, l] = val if l == cache_layer else jnp.zeros_like(val)


def _proj_kernel(*refs, rope, cache_layer, n_alias, seq_len):
    x_ref, mod_ref, g_ref, w_ref, bsum_ref, gains_ref, xprev_ref, xnext_ref, cw_ref, cb_ref = refs[:10]
    pos = 10
    if rope:
        cos_ref, sin_ref = refs[pos:pos + 2]
        pos += 2
    pos += n_alias
    x0_ref, z_ref, qn_ref, kn_ref, vn_ref, qg_ref, kge_ref, vge_ref = refs[pos:pos + 8]
    pos += 8
    emit_kv = cache_layer is not None
    if emit_kv:
        kg_ref, vg_ref = refs[pos:pos + 2]
        pos += 2
    u_scr, w_scr = refs[pos:pos + 2]
    scale = HEAD_DIM ** -0.5
    tm = x_ref.shape[0]

    @pl.when(pl.program_id(0) == 0)
    def _():
        w_scr[...] = w_ref[...].astype(BF16)

    def projected(x, cols):
        h = _mod_norm(x, g_ref[...], mod_ref[0, SH2:SH2 + 1, :], mod_ref[0, SC2:SC2 + 1, :])
        return _dot(h.astype(BF16), w_scr[:, cols])

    bsum = bsum_ref[...]
    lane = lax.broadcasted_iota(jnp.int32, (1, LANES), 1)
    first_half = (lane % 32) < 16
    low_half = lane < HEAD_DIM
    for s in range(tm // PROJ_GROUP_ROWS):
        rs = slice(s * PROJ_GROUP_ROWS, (s + 1) * PROJ_GROUP_ROWS)
        u_scr[rs, :] = projected(x_ref[rs, :], slice(0, IN_WIDTH))

        qn = _head_norm(u_scr[rs, OFF_NA:OFF_NA + NA_W], bsum, gains_ref[0:1, :])
        qn_ref[rs, :] = (qn * scale).astype(qn_ref.dtype)
        kn = _head_norm(u_scr[rs, OFF_NA + NA_W:OFF_NA + 2 * NA_W], bsum, gains_ref[1:2, :])
        _store_group(kn_ref, kn, s, rs, cache_layer)
        _store_group(vn_ref, u_scr[rs, OFF_NA + 2 * NA_W:OFF_GQ], s, rs, cache_layer)

        qg = _head_norm(u_scr[rs, OFF_GQ:OFF_GQ + GQ_W], bsum, gains_ref[2:3, :])
        kg = _head_norm(u_scr[rs, OFF_GQ + GQ_W:OFF_GQ + GQ_W + GKV_W], bsum, gains_ref[3:4, 0:GKV_W])
        vg = u_scr[rs, OFF_GQ + GQ_W + GKV_W:IN_WIDTH]
        if emit_kv:
            _store_group(kg_ref, kg, s, rs, cache_layer)
            _store_group(vg_ref, vg, s, rs, cache_layer)
        if rope:
            cos = cos_ref[rs, :]
            sin = sin_ref[rs, :]
            for p in range(GQ_W // LANES):
                sl = slice(p * LANES, (p + 1) * LANES)
                qg_ref[rs, sl] = (_rope(qg[:, sl], cos, sin, first_half) * scale).astype(qg_ref.dtype)
            kg = _rope(kg, cos, sin, first_half)
        else:
            qg_ref[rs, :] = (qg * scale).astype(qg_ref.dtype)
        kge_ref[rs, :] = _expand_kv(kg, low_half).astype(kge_ref.dtype)
        vge_ref[rs, :] = _expand_kv(vg, low_half).astype(vge_ref.dtype)

    hy_cols = slice(0, OFF_NA)
    halo_prev = projected(xprev_ref[...], hy_cols)[7:8]
    halo_next = projected(xnext_ref[...], hy_cols)[0:1]
    row = lax.broadcasted_iota(jnp.int32, (tm, 1), 0)
    t_pos = (pl.program_id(0) * tm + row) & (seq_len - 1)

    def conv_chunk(ci):
        sl = slice(ci * HY_CH, (ci + 1) * HY_CH)
        u = u_scr[:, sl]
        prev = jnp.where(row == 0, halo_prev[:, sl], pltpu.roll(u, 1, 0))
        prev = jnp.where(t_pos == 0, 0.0, prev)
        nxt = jnp.where(row == tm - 1, halo_next[:, sl], pltpu.roll(u, tm - 1, 0))
        nxt = jnp.where(t_pos == seq_len - 1, 0.0, nxt)
        return prev * cw_ref[0:1, sl] + u * cw_ref[1:2, sl] + nxt * cw_ref[2:3, sl] + cb_ref[:, sl]

    x0_ref[...] = conv_chunk(0)
    z_ref[...] = conv_chunk(1) * conv_chunk(2)


def _project(x, mod, g, w_in, layer, bsum, gains, conv_w, conv_b, seq_len, grp_fn, rope_tabs, attn_dtype,
             cache_layer=None, prev_cache=None):
    rows = x.shape[0]
    tm = ROW_TM
    rope = rope_tabs is not None
    assert seq_len & (seq_len - 1) == 0 and (seq_len % tm == 0 or tm % seq_len == 0)
    sub = 8
    in_specs = [
        pl.BlockSpec((tm, D_MODEL), lambda i: (i, 0)),
        pl.BlockSpec((1, N_ADA, D_MODEL), lambda i: (grp_fn(i, tm), 0, 0)),
        pl.BlockSpec((None, 1, D_MODEL), lambda i: (3 * layer + 1, 0, 0)),
        pl.BlockSpec((None, D_MODEL, IN_WIDTH), lambda i: (layer, 0, 0), pipeline_mode=pl.Buffered(1)),
        pl.BlockSpec((LANES, LANES), lambda i: (0, 0)),
        pl.BlockSpec((None, 4, NA_W), lambda i: (layer, 0, 0)),
        pl.BlockSpec((sub, D_MODEL), lambda i: (jnp.maximum(i * (tm // sub) - 1, 0), 0)),
        pl.BlockSpec((sub, D_MODEL), lambda i: (jnp.minimum((i + 1) * (tm // sub), rows // sub - 1), 0)),
        pl.BlockSpec((3, OFF_NA), lambda i: (0, 0)),
        pl.BlockSpec((1, OFF_NA), lambda i: (0, 0)),
    ]
    args = [x, mod, g, w_in, bsum, gains, x, x, conv_w, conv_b.reshape(1, OFF_NA)]
    if rope:
        seq_tiles = DEC_SEQ // tm
        in_specs += [pl.BlockSpec((tm, LANES), lambda i: (i % seq_tiles, 0))] * 2
        args += list(rope_tabs)
    outs = [(HY_CH, F32)] * 2 + [(NA_W, attn_dtype)] * 3 + [(GQ_W, attn_dtype)] * 3
    if cache_layer is not None:
        outs += [(GKV_W, F32), (GKV_W, F32)]
    out_specs = [pl.BlockSpec((tm, w), lambda i: (i, 0)) for w, _ in outs]
    out_shape = [jax.ShapeDtypeStruct((rows, w), dt) for w, dt in outs]
    aliases = {}
    if cache_layer is not None:
        cache_outs = (3, 4, 8, 9)
        seqs = tm // SEQ
        for o in cache_outs:
            w = outs[o][0]
            out_shape[o] = jax.ShapeDtypeStruct((BATCH, DEPTH, SEQ, w), F32)
            if prev_cache is None:
                out_specs[o] = pl.BlockSpec((seqs, DEPTH, SEQ, w), lambda i: (i, 0, 0, 0))
            else:
                out_specs[o] = pl.BlockSpec((seqs, None, SEQ, w), lambda i: (i, cache_layer, 0, 0))
        if prev_cache is not None:
            aliases = {len(args) + n: o for n, o in enumerate(cache_outs)}
            in_specs += [pl.BlockSpec(memory_space=pl.ANY)] * len(cache_outs)
            args += list(prev_cache)
    return pl.pallas_call(
        functools.partial(_proj_kernel, rope=rope, cache_layer=cache_layer, n_alias=len(aliases), seq_len=seq_len),
        grid=(rows // tm,),
        in_specs=in_specs,
        out_specs=out_specs,
        out_shape=out_shape,
        input_output_aliases=aliases,
        scratch_shapes=[pltpu.VMEM((tm, IN_WIDTH), F32), pltpu.VMEM((D_MODEL, IN_WIDTH), BF16)],
        compiler_params=_cparams(("arbitrary",)),
        name="mixer_in_proj",
    )(*args)


HEADS_PER_CHAIN = 2
CHAIN_W = HEADS_PER_CHAIN * HEAD_DIM


def _log2(n):
    assert n & (n - 1) == 0
    return n.bit_length() - 1


def _stack_heads(q, n_heads):
    m, w = q.shape
    rows = lax.broadcasted_iota(jnp.int32, (n_heads * m, w), 0)
    lanes = lax.broadcasted_iota(jnp.int32, (n_heads * m, w), 1)
    own = jnp.right_shift(rows, _log2(m)) == jnp.right_shift(lanes, _log2(HEAD_DIM))
    return jnp.where(own, jnp.concatenate([q] * n_heads, axis=0), jnp.zeros((), q.dtype))


def _unstack_heads(o, n_heads):
    m = o.shape[0] // n_heads
    head = jnp.right_shift(lax.broadcasted_iota(jnp.int32, (m, o.shape[1]), 1), _log2(HEAD_DIM))
    out = o[0:m]
    for h in range(1, n_heads):
        out = jnp.where(head == h, o[h * m:(h + 1) * m], out)
    return out


def _sink_column(sink_ref, m, first_head, n_heads):
    block = jnp.right_shift(lax.broadcasted_iota(jnp.int32, (n_heads * m, 1), 0), _log2(m))
    col = jnp.full((n_heads * m, 1), sink_ref[first_head], F32)
    for h in range(1, n_heads):
        col = jnp.where(block == h, sink_ref[first_head + h], col)
    return col


def _softmax_pv(scores, values, sink):
    m = scores[0].max(axis=-1, keepdims=True)
    for s in scores[1:]:
        m = jnp.maximum(m, s.max(axis=-1, keepdims=True))
    if sink is not None:
        m = jnp.maximum(m, sink)
    den = None
    acc = None
    for s, v in zip(scores, values):
        p = jnp.exp(s - m)
        d = p.sum(axis=-1, keepdims=True)
        o = _dot(p.astype(BF16), v)
        den = d if den is None else den + d
        acc = o if acc is None else acc + o
    if sink is not None:
        den = den + jnp.exp(sink - m)
    return acc / den


CTX_SEQS_PER_STEP = 4


def _ctx_attn_kernel(sink_ref, qn_ref, kn_ref, vn_ref, qg_ref, kge_ref, vge_ref, yna_ref, ygq_ref):
    m = qn_ref.shape[1]
    low = lax.broadcasted_iota(jnp.int32, (m, LANES), 1) < HEAD_DIM
    for q_ref, k_ref, v_ref, y_ref, has_sink in ((qn_ref, kn_ref, vn_ref, yna_ref, False),
                                                  (qg_ref, kge_ref, vge_ref, ygq_ref, True)):
        for bi in range(CTX_SEQS_PER_STEP):
            for p in range(NA_W // LANES):
                sl = slice(p * LANES, (p + 1) * LANES)
                q2 = q_ref[bi, :, sl].astype(BF16)
                k2 = k_ref[bi, :, sl].astype(BF16)
                v2 = v_ref[bi, :, sl].astype(BF16)
                halves = []
                for half, keep in enumerate((low, ~low)):
                    qh = jnp.where(keep, q2, jnp.zeros((), BF16))
                    sink = sink_ref[2 * p + half] if has_sink else None
                    halves.append(_softmax_pv([_dot_nt(qh, k2)], [v2], sink))
                y_ref[bi, :, sl] = jnp.where(low, halves[0], halves[1]).astype(y_ref.dtype)


def _ctx_attention(sink, qn, kn_cache, vn_cache, qg, kge, vge, layer):
    b, l, _ = qn.shape

    nb = CTX_SEQS_PER_STEP

    def spec(w):
        return pl.BlockSpec((nb, l, w), lambda i: (i, 0, 0))

    cache_spec = pl.BlockSpec((nb, None, l, NA_W), lambda i: (i, layer, 0, 0))
    return pl.pallas_call(
        _ctx_attn_kernel,
        grid=(b // nb,),
        in_specs=[pl.BlockSpec(memory_space=pltpu.SMEM), spec(NA_W), cache_spec, cache_spec, spec(GQ_W),
                  spec(GQ_W), spec(GQ_W)],
        out_specs=[spec(NA_W), spec(GQ_W)],
        out_shape=[jax.ShapeDtypeStruct((b, l, NA_W), BF16), jax.ShapeDtypeStruct((b, l, GQ_W), BF16)],
        compiler_params=_cparams(("arbitrary",)),
        name="context_attention",
    )(sink, qn, kn_cache, vn_cache, qg, kge, vge)


NA_KEYS = NA_ROWS * GRID_W
NA_ROWS_PER_STEP = 16


def _na_bias_kernel(r_ref, oh_ref, mk_ref, o_ref):
    o_ref[...] = jnp.dot(r_ref[...], oh_ref[...], precision=_HIGHEST, preferred_element_type=F32) + mk_ref[...]


def _na_bias_tables():
    q = np.arange(GRID_W)[:, None]
    kc = np.arange(GRID_W)[None, :]
    win_lo = np.clip(q - NA_COLS // 2, 0, GRID_W - NA_COLS)
    ok = (kc >= win_lo) & (kc < win_lo + NA_COLS)
    j = kc - q + NA_COLS - 1
    onehot = np.zeros((LANES, GRID_W * GRID_W), np.float32)
    qq, kk = np.nonzero(ok)
    onehot[j[qq, kk], qq * GRID_W + kk] = 1.0
    mask = np.where(ok, 0.0, NEG_INF).astype(np.float32).reshape(1, GRID_W * GRID_W)
    return onehot, mask


def _na_bias(rpb_l):
    n_dr = 2 * NA_ROWS - 1
    rows = NA_HEADS * n_dr
    rows_pad = -(-rows // 8) * 8
    r = jnp.zeros((rows_pad, LANES), F32).at[:rows, :2 * NA_COLS - 1].set(rpb_l.reshape(rows, 2 * NA_COLS - 1))
    onehot, mask = _na_bias_tables()
    out = pl.pallas_call(
        _na_bias_kernel,
        out_shape=jax.ShapeDtypeStruct((rows_pad, GRID_W * GRID_W), F32),
        compiler_params=pltpu.CompilerParams(vmem_limit_bytes=VMEM_LIMIT),
        name="na_bias_expand",
    )(r, jnp.asarray(onehot), jnp.asarray(mask))
    bm = out[:rows].reshape(NA_HEADS, n_dr, GRID_W, GRID_W)
    bm2 = jnp.concatenate([bm[:, :-1], bm[:, 1:]], axis=-1)
    return bm2.transpose(1, 0, 2, 3).reshape(n_dr - 1, NA_HEADS * GRID_W, 2 * GRID_W)


def _na_attn_kernel(q_ref, k_ref, v_ref, kc_ref, vc_ref, bm_ref, o_ref):
    n_rows = DEC_SEQ // GRID_W
    kc = kc_ref[0, 0].astype(BF16)
    vc = vc_ref[0, 0].astype(BF16)
    for rr in range(NA_ROWS_PER_STEP):
        r = pl.program_id(1) * NA_ROWS_PER_STEP + rr
        start = jnp.clip(r - NA_ROWS // 2, 0, n_rows - NA_ROWS)
        shift = r - start
        row0 = pl.multiple_of(start * GRID_W, GRID_W)
        rows = slice(rr * GRID_W, (rr + 1) * GRID_W)
        q = _stack_heads(q_ref[0, rows, :], NA_HEADS)
        k = k_ref[0, pl.ds(row0, NA_KEYS), :]
        v = v_ref[0, pl.ds(row0, NA_KEYS), :]
        bias = jnp.concatenate([bm_ref[2 * jj - shift + NA_ROWS - 1] for jj in range(NA_ROWS // 2)], axis=-1)
        s_loc = _dot_nt(q, k) + bias
        s_ctx = _dot_nt(q, kc)
        o = _softmax_pv([s_loc, s_ctx], [v, vc], None)
        o_ref[0, rows, :] = _unstack_heads(o, NA_HEADS).astype(o_ref.dtype)


WIN_KEYS = 3 * BLK
WIN_BLOCKS_PER_STEP = 8


def _win_attn_kernel(sink_ref, mask_ref, q_ref, kge_ref, vge_ref, kc_ref, vc_ref, o_ref):
    n = kge_ref.shape[1]
    n_blocks = n // BLK
    low_c = lax.broadcasted_iota(jnp.int32, (1, LANES), 1) < HEAD_DIM
    kce = _expand_kv(kc_ref[0, 0], low_c).astype(BF16)
    vce = _expand_kv(vc_ref[0, 0], low_c).astype(BF16)
    n_chains = GQA_Q_HEADS // HEADS_PER_CHAIN
    sinks = [_sink_column(sink_ref, BLK, p * HEADS_PER_CHAIN, HEADS_PER_CHAIN) for p in range(n_chains)]
    for bb in range(WIN_BLOCKS_PER_STEP):
        nb = pl.program_id(1) * WIN_BLOCKS_PER_STEP + bb
        start = pl.multiple_of(jnp.clip((nb - 1) * BLK, 0, n - WIN_KEYS), BLK)
        variant = jnp.where(nb == 0, 0, jnp.where(nb == n_blocks - 1, 2, 1))
        rows = slice(bb * BLK, (bb + 1) * BLK)
        for p in range(n_chains):
            sl = slice(p * CHAIN_W, (p + 1) * CHAIN_W)
            q = _stack_heads(q_ref[0, rows, sl], HEADS_PER_CHAIN)
            s_loc = _dot_nt(q, kge_ref[0, pl.ds(start, WIN_KEYS), sl]) + mask_ref[variant]
            s_ctx = _dot_nt(q, kce[:, sl])
            o = _softmax_pv([s_loc, s_ctx], [vge_ref[0, pl.ds(start, WIN_KEYS), sl], vce[:, sl]], sinks[p])
            o_ref[0, rows, sl] = _unstack_heads(o, HEADS_PER_CHAIN).astype(o_ref.dtype)


def _win_mask_table():
    qi = np.arange(BLK)[:, None]
    kj = np.arange(WIN_KEYS)[None, :]
    tabs = [np.where(np.abs(qi + off - kj) <= GQA_WINDOW, 0.0, NEG_INF) for off in (0, BLK, 2 * BLK)]
    return jnp.asarray(np.stack([np.tile(t, (HEADS_PER_CHAIN, 1)) for t in tabs]).astype(np.float32))


def _latent_attn_kernel(sink_ref, mask_ref, bm_ref, qn_ref, kn_ref, vn_ref, kcn_ref, vcn_ref,
                        qg_ref, kge_ref, vge_ref, kcg_ref, vcg_ref, yna_ref, ygq_ref):
    _na_attn_kernel(qn_ref, kn_ref, vn_ref, kcn_ref, vcn_ref, bm_ref, yna_ref)
    _win_attn_kernel(sink_ref, mask_ref, qg_ref, kge_ref, vge_ref, kcg_ref, vcg_ref, ygq_ref)


def _latent_attention(sink, bm2, qn, kn, vn, kc_na, vc_na, qg, kge, vge, kc_gq, vc_gq, layer):
    b, n, _ = qn.shape
    mask = _win_mask_table()
    tq = WIN_BLOCKS_PER_STEP * BLK
    assert tq == NA_ROWS_PER_STEP * GRID_W

    def q_spec(w):
        return pl.BlockSpec((1, tq, w), lambda i, j: (i, j, 0))

    def seq_spec(w):
        return pl.BlockSpec((1, n, w), lambda i, j: (i, 0, 0))

    def cache_spec(w):
        return pl.BlockSpec((1, 1, PAST_LEN, w), lambda i, j: (i, layer, 0, 0))

    return pl.pallas_call(
        _latent_attn_kernel,
        grid=(b, n // tq),
        in_specs=[
            pl.BlockSpec(memory_space=pltpu.SMEM),
            pl.BlockSpec(mask.shape, lambda i, j: (0, 0, 0)),
            pl.BlockSpec(bm2.shape, lambda i, j: (0, 0, 0)),
            q_spec(NA_W), seq_spec(NA_W), seq_spec(NA_W), cache_spec(NA_W), cache_spec(NA_W),
            q_spec(GQ_W), seq_spec(GQ_W), seq_spec(GQ_W), cache_spec(GKV_W), cache_spec(GKV_W),
        ],
        out_specs=[q_spec(NA_W), q_spec(GQ_W)],
        out_shape=[jax.ShapeDtypeStruct((b, n, NA_W), BF16), jax.ShapeDtypeStruct((b, n, GQ_W), BF16)],
        compiler_params=_cparams(("arbitrary", "arbitrary")),
        name="latent_attention",
    )(sink, mask, bm2, qn, kn, vn, kc_na, vc_na, qg, kge, vge, kc_gq, vc_gq)


@functools.lru_cache(maxsize=None)
def _hyena_consts_np(L):
    n = 2 * L
    k = np.arange(L, dtype=np.int64)
    ang = (2.0 * np.pi / n) * ((k[:, None] * k[None, :]) % n).astype(np.float64)

    tkf = min(HY_TKF, L)
    cm = np.cos(ang).astype(np.float32).reshape(L // tkf, tkf, L)
    sm = (-np.sin(ang)).astype(np.float32).reshape(L // tkf, tkf, L)
    f_rows = np.concatenate([cm, sm], axis=1)
    f_cols = np.ascontiguousarray(f_rows.transpose(0, 2, 1))
    idx = np.arange(L, dtype=np.float32)
    t = idx / np.float32(L - 1)
    bands = np.linspace(1e-4, HY_BANDS - 1, HY_BANDS, dtype=np.float32)
    fang = np.float32(2.0 * math.pi / L) * idx[:, None] * bands[None, :]
    feats = np.zeros((L, LANES), np.float32)
    feats[:, 0] = t
    feats[:, 1:1 + HY_BANDS] = np.cos(fang)
    feats[:, 1 + HY_BANDS:HY_EMB] = -np.sin(fang)
    max_decay = math.log(HY_DECAY_TARGET) / HY_FAST_PCT
    min_decay = math.log(HY_DECAY_TARGET) / HY_SLOW_PCT
    deltas = np.abs(np.linspace(min_decay, max_decay, HY_CH, dtype=np.float32))
    decay = np.exp(-t[:, None] * deltas[None, :]).astype(np.float32)
    decay2 = np.concatenate([decay, decay], axis=1)
    half = L // 2
    feats = np.concatenate([feats[:half], feats[half:]], axis=1)
    decay2 = np.concatenate([decay2[:half], decay2[half:]], axis=1)
    return f_rows, f_cols, feats, decay2


def _hyena_consts(L):
    return tuple(jnp.asarray(a) for a in _hyena_consts_np(L))


def _hy_filter_kernel(feats_ref, w1_ref, b1_ref, w2_ref, b2_ref, w3_ref, freq_ref, decay_ref, wsum_ref, wdiff_ref):
    def dot_hi(a, b):
        a_hi, a_lo = _split(a)
        b_hi, b_lo = _split(b)
        return _dot(a_hi, b_hi) + (_dot(a_lo, b_hi) + _dot(a_hi, b_lo))

    hid = jnp.sin(freq_ref[0:1, :] * (dot_hi(feats_ref[...], w1_ref[...]) + b1_ref[...]))
    hid = jnp.sin(freq_ref[1:2, :] * (dot_hi(hid, w2_ref[...]) + b2_ref[...]))
    taps2 = dot_hi(hid, w3_ref[...]) * decay_ref[...]
    tl = feats_ref.shape[0]
    row = pl.program_id(0) * tl + lax.broadcasted_iota(jnp.int32, (tl, HY_CH), 0)
    for part in range(2):
        fwd = taps2[:, part * 2 * HY_CH:part * 2 * HY_CH + HY_CH]
        bwd = taps2[:, part * 2 * HY_CH + HY_CH:(part + 1) * 2 * HY_CH]
        if part == 0:
            bwd = jnp.where(row == 0, 0.0, bwd)
        lanes = slice(part * HY_CH, (part + 1) * HY_CH)
        wsum_ref[:, lanes] = fwd + bwd
        wdiff_ref[:, lanes] = fwd - bwd


def _block_diag2(w):
    d, r, c = w.shape
    return jnp.zeros((d, 2 * r, 2 * c), w.dtype).at[:, :r, :c].set(w).at[:, r:, c:].set(w)


def _hy_filter_params(w1, b1, w2, b2, w3, freq):
    w1p = jnp.pad(w1, ((0, 0), (0, LANES - HY_EMB), (0, 0)))
    return (_block_diag2(w1p), jnp.tile(b1, (1, 2))[:, None, :], _block_diag2(w2), jnp.tile(b2, (1, 2))[:, None, :],
            _block_diag2(w3), jnp.tile(freq, (1, 1, 2)))


def _hy_filter(L, feats, decay2, params, layer):
    half = L // 2
    tl = min(HY_FILT_TL, half)

    def rows_of(a):
        return pl.BlockSpec((tl, a.shape[1]), lambda i: (i, 0))

    def of_layer(a):
        return pl.BlockSpec((None,) + a.shape[1:], lambda i: (layer,) + (0,) * (a.ndim - 1))

    out_spec = pl.BlockSpec((tl, 2 * HY_CH), lambda i: (i, 0))
    return pl.pallas_call(
        _hy_filter_kernel,
        grid=(half // tl,),
        in_specs=[rows_of(feats)] + [of_layer(p) for p in params] + [rows_of(decay2)],
        out_specs=[out_spec, out_spec],
        out_shape=[jax.ShapeDtypeStruct((half, 2 * HY_CH), F32)] * 2,
        compiler_params=_cparams(("arbitrary",)),
        name="hyena_filter",
    )(feats, *params, decay2)


def _hy_conv_kernel(z_ref, x0_ref, wsum_ref, wdiff_ref, skip_ref, fr_ref, fc_ref,
                    o_ref, z_scr, ws_scr, wd_scr, acc_scr, *, group):
    t = pl.program_id(1)
    L = z_ref.shape[1]
    n = 2 * L
    half = L // 2

    @pl.when(t == 0)
    def _():
        z_scr[...] = z_ref[...].astype(BF16)
        for part in range(2):
            rows = slice(part * half, (part + 1) * half)
            lanes = slice(part * HY_CH, (part + 1) * HY_CH)
            ws_scr[rows, :] = wsum_ref[:, lanes].astype(BF16)
            wd_scr[rows, :] = wdiff_ref[:, lanes].astype(BF16)
        acc_scr[...] = jnp.zeros_like(acc_scr)

    fr = fr_ref[...].astype(BF16)
    fc = fc_ref[...].astype(BF16)
    tkf = fr.shape[0] // 2
    k_re = _dot(fr[0:tkf], ws_scr[...])
    k_im = _dot(fr[tkf:2 * tkf], wd_scr[...])
    for b in range(group):
        zf = _dot(fr, z_scr[b])
        z_re, z_im = zf[0:tkf], zf[tkf:2 * tkf]
        y = jnp.concatenate([z_re * k_re - z_im * k_im, z_re * k_im + z_im * k_re], axis=0).astype(BF16)
        acc_scr[b] += _dot(fc, y)

    @pl.when(t == pl.num_programs(1) - 1)
    def _():
        row = lax.broadcasted_iota(jnp.int32, (L, 1), 0)
        sgn = (1 - 2 * (row % 2)).astype(F32)
        w_a, w_b = wsum_ref[:, 0:HY_CH], wsum_ref[:, HY_CH:2 * HY_CH]
        sgn_h = sgn[0:half]
        w_dc = w_a.sum(axis=0, keepdims=True) + w_b.sum(axis=0, keepdims=True)
        w_ny = (w_a * sgn_h).sum(axis=0, keepdims=True) + (w_b * sgn_h).sum(axis=0, keepdims=True)
        for b in range(group):
            z = z_ref[b]
            z_dc = z.sum(axis=0, keepdims=True)
            z_ny = (z * sgn).sum(axis=0, keepdims=True)
            conv = (2.0 / n) * acc_scr[b] - (1.0 / n) * (z_dc * w_dc) + (1.0 / n) * (sgn * (z_ny * w_ny))
            o_ref[b] = (x0_ref[b] * (conv + z * skip_ref[...])).astype(o_ref.dtype)


def _hy_conv(z, x0, wsum, wdiff, skip, f_rows, f_cols, group):
    b, L, c = z.shape
    n_tiles, two_tkf, _ = f_rows.shape
    row_spec = pl.BlockSpec((None, two_tkf, L), lambda g, t: (t, 0, 0))
    col_spec = pl.BlockSpec((None, L, two_tkf), lambda g, t: (t, 0, 0))
    once = pl.Buffered(1)
    seq_spec = pl.BlockSpec((group, L, c), lambda g, t: (g, 0, 0), pipeline_mode=once)
    w_spec = pl.BlockSpec((L // 2, 2 * c), lambda g, t: (0, 0), pipeline_mode=once)
    return pl.pallas_call(
        functools.partial(_hy_conv_kernel, group=group),
        grid=(b // group, n_tiles),
        in_specs=[seq_spec, seq_spec, w_spec, w_spec, pl.BlockSpec((1, c), lambda g, t: (0, 0)),
                  row_spec, col_spec],
        out_specs=pl.BlockSpec((group, L, c), lambda g, t: (g, 0, 0)),
        out_shape=jax.ShapeDtypeStruct((b, L, c), BF16),
        scratch_shapes=[pltpu.VMEM((group, L, c), BF16), pltpu.VMEM((L, c), BF16), pltpu.VMEM((L, c), BF16),
                        pltpu.VMEM((group, L, c), F32)],
        compiler_params=_cparams(("arbitrary", "arbitrary")),
        name="hyena_long_conv",
    )(z, x0, wsum, wdiff, skip.reshape(1, c), f_rows, f_cols)


def _merge_kernel(x_ref, yhy_ref, yna_ref, ygq_ref, mod_ref, w_ref, o_ref, w_scr):
    @pl.when(pl.program_id(0) == 0)
    def _():
        w_scr[...] = w_ref[...].astype(BF16)

    y = jnp.concatenate([yhy_ref[...], yna_ref[...], ygq_ref[...]], axis=-1)
    o_ref[...] = x_ref[...] + mod_ref[0, G2:G2 + 1, :] * _dot(y, w_scr[...])


def _merge(x, y_hy, y_na, y_gq, mod, w_out, layer, grp_fn):
    rows = x.shape[0]
    tm = MERGE_TM
    return pl.pallas_call(
        _merge_kernel,
        grid=(rows // tm,),
        in_specs=[
            pl.BlockSpec((tm, D_MODEL), lambda i: (i, 0)),
            pl.BlockSpec((tm, HY_CH), lambda i: (i, 0)),
            pl.BlockSpec((tm, NA_W), lambda i: (i, 0)),
            pl.BlockSpec((tm, GQ_W), lambda i: (i, 0)),
            pl.BlockSpec((1, N_ADA, D_MODEL), lambda i: (grp_fn(i, tm), 0, 0)),
            pl.BlockSpec((None, D_MODEL, D_MODEL), lambda i: (layer, 0, 0), pipeline_mode=pl.Buffered(1)),
        ],
        out_specs=pl.BlockSpec((tm, D_MODEL), lambda i: (i, 0)),
        out_shape=jax.ShapeDtypeStruct((rows, D_MODEL), F32),
        scratch_shapes=[pltpu.VMEM((D_MODEL, D_MODEL), BF16)],
        compiler_params=_cparams(("arbitrary",)),
        name="mixer_out_proj",
    )(x, y_hy, y_na, y_gq, mod, w_out)


def _rope_tables():
    pos = np.arange(DEC_SEQ)
    quarter = HEAD_DIM // 4
    inv = ROPE_BASE ** (-np.arange(quarter, dtype=np.float64) / quarter)
    ang_r = (pos // GRID_W)[:, None] * inv[None, :]
    ang_c = (pos % GRID_W)[:, None] * inv[None, :]
    cos_h = np.concatenate([np.cos(ang_r), np.cos(ang_r), np.cos(ang_c), np.cos(ang_c)], axis=1)
    sin_h = np.concatenate([-np.sin(ang_r), np.sin(ang_r), -np.sin(ang_c), np.sin(ang_c)], axis=1)
    reps = LANES // HEAD_DIM
    return (jnp.asarray(np.tile(cos_h, (1, reps)), F32), jnp.asarray(np.tile(sin_h, (1, reps)), F32))


def _head_sum_matrix():
    idx = np.arange(LANES) // HEAD_DIM
    return jnp.asarray((idx[:, None] == idx[None, :]).astype(np.float32) / HEAD_DIM, BF16)


def _grp_prompt(i, tm, base):
    return base


def _grp_sample(i, tm, base):
    return base + 1 + (i * tm) // DEC_SEQ


def kernel(x_prompt, x_sample, cache_na_k, cache_na_v, cache_gqa_k, cache_gqa_v, c, c_ctx, ada_w, ada_b, norm_g, ffn_w1, ffn_w3, ffn_w2, w_in, w_out, hy_conv_w, hy_conv_b, hy_filt_w1, hy_filt_b1, hy_filt_w2, hy_filt_b2, hy_filt_w3, hy_freq, hy_skip, na_q_g, na_k_g, na_rpb, gqa_q_g, gqa_k_g, gqa_sink):
    xp = x_prompt.reshape(BATCH * SEQ, D_MODEL)
    xs = x_sample.reshape(DEC_BATCH * DEC_SEQ, D_MODEL)
    cc = jnp.zeros((8, D_MODEL), F32).at[0].set(c_ctx).at[1:1 + DEC_BATCH].set(c)
    kc_na = cache_na_k.reshape(DEC_BATCH, DEPTH, PAST_LEN, NA_W)
    vc_na = cache_na_v.reshape(DEC_BATCH, DEPTH, PAST_LEN, NA_W)
    kc_gq = cache_gqa_k.reshape(DEC_BATCH, DEPTH, PAST_LEN, GKV_W)
    vc_gq = cache_gqa_v.reshape(DEC_BATCH, DEPTH, PAST_LEN, GKV_W)

    rope_tabs = _rope_tables()
    bsum = _head_sum_matrix()
    hy_p = _hyena_consts(SEQ)
    hy_s = _hyena_consts(DEC_SEQ)

    gains = jnp.tile(jnp.stack([na_q_g, na_k_g, gqa_q_g, gqa_k_g], axis=1), (1, 1, NA_HEADS))
    norm_rows = norm_g.reshape(DEPTH * 3, 1, D_MODEL)
    ng = [norm_rows] * 3
    filt_params = _hy_filter_params(hy_filt_w1, hy_filt_b1, hy_filt_w2, hy_filt_b2, hy_filt_w3, hy_freq)

    mod = _adaln_mod(cc, ada_w, ada_b)
    cache = None
    for l in range(DEPTH):
        grp_p = functools.partial(_grp_prompt, base=8 * l)
        grp_s = functools.partial(_grp_sample, base=8 * l)
        bm2 = _na_bias(na_rpb[l])
        filt = {}
        for L, consts in ((SEQ, hy_p), (DEC_SEQ, hy_s)):
            filt[L] = _hy_filter(L, consts[2], consts[3], filt_params, l)

        xp = _ffn(xp, mod, ng[0], ffn_w1, ffn_w3, ffn_w2, l, 0, grp_p, SH1, SC1, G1)
        x0, z, qn, kn, vn, qg, kgd, vgd, kg, vg = _project(
            xp, mod, ng[1], w_in, l, bsum, gains, hy_conv_w[l], hy_conv_b[l], SEQ, grp_p, None, F32,
            cache_layer=l, prev_cache=cache)
        cache = (kn, vn, kg, vg)
        shp = lambda a: a.reshape(BATCH, SEQ, a.shape[-1])
        y_hy = _hy_conv(shp(z), shp(x0), filt[SEQ][0], filt[SEQ][1], hy_skip[l], hy_p[0], hy_p[1], group=BATCH)
        y_na, y_gq = _ctx_attention(gqa_sink[l], shp(qn), kn, vn, shp(qg), shp(kgd), shp(vgd), l)
        xp = _merge(xp, y_hy.reshape(-1, HY_CH), y_na.reshape(-1, NA_W), y_gq.reshape(-1, GQ_W), mod, w_out, l,
                    grp_p)
        xp = _ffn(xp, mod, ng[2], ffn_w1, ffn_w3, ffn_w2, l, 1, grp_p, SH3, SC3, G3)

        xs = _ffn(xs, mod, ng[0], ffn_w1, ffn_w3, ffn_w2, l, 0, grp_s, SH1, SC1, G1)
        x0, z, qn, kn, vn, qg, kgd, vgd = _project(
            xs, mod, ng[1], w_in, l, bsum, gains, hy_conv_w[l], hy_conv_b[l], DEC_SEQ, grp_s, rope_tabs, BF16)
        shs = lambda a: a.reshape(DEC_BATCH, DEC_SEQ, a.shape[-1])
        y_hy = _hy_conv(shs(z), shs(x0), filt[DEC_SEQ][0], filt[DEC_SEQ][1], hy_skip[l], hy_s[0], hy_s[1],
                        group=DEC_BATCH)
        y_na, y_gq = _latent_attention(gqa_sink[l], bm2, shs(qn), shs(kn), shs(vn), kc_na, vc_na,
                                       shs(qg), shs(kgd), shs(vgd), kc_gq, vc_gq, l)
        xs = _merge(xs, y_hy.reshape(-1, HY_CH), y_na.reshape(-1, NA_W), y_gq.reshape(-1, GQ_W), mod, w_out, l,
                    grp_s)
        xs = _ffn(xs, mod, ng[2], ffn_w1, ffn_w3, ffn_w2, l, 1, grp_s, SH3, SC3, G3)

    kn, vn, kg, vg = cache
    return (xp.reshape(BATCH, SEQ, D_MODEL), xs.reshape(DEC_BATCH, DEC_SEQ, D_MODEL),
            kn.reshape(BATCH, DEPTH, SEQ, NA_HEADS, HEAD_DIM), vn.reshape(BATCH, DEPTH, SEQ, NA_HEADS, HEAD_DIM),
            kg.reshape(BATCH, DEPTH, SEQ, GQA_KV_HEADS, HEAD_DIM), vg.reshape(BATCH, DEPTH, SEQ, GQA_KV_HEADS, HEAD_DIM))
```
